```python
import jax, jax.numpy as jnp
from jax import lax
import numpy as np

D_MODEL = 1024
BATCH = 8
SEQ = 4096
DEPTH = 1

CONV_CH = 512
CONV_K = 3
N_HEADS = 8
N_KV_GROUPS = 2
HEADS_PER_GROUP = N_HEADS // N_KV_GROUPS
HEAD_DIM = 64
NSA_WIDTH = N_HEADS * HEAD_DIM
KV_WIDTH = N_KV_GROUPS * HEAD_DIM
ROPE_DIM = HEAD_DIM // 4
ROPE_THETA = 500000.0
CMP_BLOCK = 32
CMP_STRIDE = 16
CMP_HIDDEN = 256
SEL_BLOCK = 64
N_SELECT = 16
WINDOW = 512
Q_BLOCK = 64
N_EXPERTS = 32
TOP_K = 4
D_FF = 1024
SWIGLU_LIMIT = 7.0
SWIGLU_ALPHA = 1.702
MOE_BLOCK = 128
DN_ALPHA = (2 * DEPTH) ** 0.25
DN_BETA = (8 * DEPTH) ** -0.25
LN_EPS = 1e-5
NEG_INF = -1e30
FORCE_SCORE = 1e4
IN_COLS = 3 * CONV_CH + NSA_WIDTH + 6 * KV_WIDTH + 3 * N_HEADS + 2 * D_MODEL

kernel_name = 'hybrid_shortconv_nsa_moe_deepnorm'


def layer_norm(x, g, b):
    xf = x.astype(jnp.float32)
    mu = jnp.mean(xf, axis=-1, keepdims=True)
    var = jnp.mean(jnp.square(xf - mu), axis=-1, keepdims=True)
    y = (xf - mu) * lax.rsqrt(var + LN_EPS)
    return (y * g.astype(jnp.float32) + b.astype(jnp.float32)).astype(x.dtype)


def partial_rope(t, positions):
    half = ROPE_DIM // 2
    inv_freq = ROPE_THETA ** (-jnp.arange(half, dtype=jnp.float32) * (2.0 / ROPE_DIM))
    ang = positions.astype(jnp.float32)[..., None] * inv_freq
    cos = jnp.cos(ang)[:, :, None, :].astype(t.dtype)
    sin = jnp.sin(ang)[:, :, None, :].astype(t.dtype)
    t1 = t[..., :half]
    t2 = t[..., half:ROPE_DIM]
    return jnp.concatenate([t1 * cos - t2 * sin, t2 * cos + t1 * sin, t[..., ROPE_DIM:]], axis=-1)


def compress_blocks(k, pos_emb, w1, w2):
    T = k.shape[2]
    n_cmp = (T - CMP_BLOCK) // CMP_STRIDE + 1
    idx = np.arange(n_cmp)[:, None] * CMP_STRIDE + np.arange(CMP_BLOCK)[None, :]
    blocks = k[:, :, idx] + pos_emb
    flat = blocks.reshape(blocks.shape[0], blocks.shape[1], n_cmp, CMP_BLOCK * HEAD_DIM)
    return jax.nn.gelu(flat @ w1) @ w2


def nsa_attention(q, kc_raw, vc_raw, ks, vs, kw, vw, gate_logits,
                  cmp_pos_k, cmp_w1_k, cmp_w2_k, cmp_pos_v, cmp_w1_v, cmp_w2_v):
    B, G, R, T, DH = q.shape
    f32 = jnp.float32
    t_idx = np.arange(T)

    kc = compress_blocks(kc_raw, cmp_pos_k, cmp_w1_k, cmp_w2_k)
    vc = compress_blocks(vc_raw, cmp_pos_v, cmp_w1_v, cmp_w2_v)
    n_cmp = kc.shape[2]
    c_start = np.arange(n_cmp) * CMP_STRIDE
    mask_c = (c_start + CMP_BLOCK - 1)[None, :] <= t_idx[:, None]
    s_c = jnp.einsum('bgrtd,bgcd->bgrtc', q, kc).astype(f32)
    p_c = jnp.where(mask_c, jax.nn.softmax(jnp.where(mask_c, s_c, NEG_INF), axis=-1), 0.0)
    o_c = jnp.einsum('bgrtc,bgcd->bgrtd', p_c.astype(vc.dtype), vc)

    n_sel = T // SEL_BLOCK
    j_start = np.arange(n_sel) * SEL_BLOCK
    overlap = ((c_start[:, None] < j_start[None, :] + SEL_BLOCK)
               & (c_start[:, None] + CMP_BLOCK > j_start[None, :])).astype(np.float32)
    imp = jnp.einsum('bgrtc,cj->bgtj', p_c, overlap)
    cur = t_idx // SEL_BLOCK
    jj = np.arange(n_sel)
    valid = jj[None, :] <= cur[:, None]
    forced = (jj[None, :] == 0) | (jj[None, :] == cur[:, None]) | (jj[None, :] == cur[:, None] - 1)
    sel_score = jnp.where(valid, jnp.where(forced, FORCE_SCORE, imp), NEG_INF)
    k_eff = min(N_SELECT, n_sel)
    _, sel_idx = lax.top_k(sel_score, k_eff)

    ks_blk = ks.reshape(B, G, n_sel, SEL_BLOCK, DH)
    vs_blk = vs.reshape(B, G, n_sel, SEL_BLOCK, DH)
    kw_pad = jnp.pad(kw, ((0, 0), (0, 0), (WINDOW, 0), (0, 0)))
    vw_pad = jnp.pad(vw, ((0, 0), (0, 0), (WINDOW, 0), (0, 0)))
    b_i = jnp.arange(B)[:, None, None, None]
    g_i = jnp.arange(G)[None, :, None, None]
    in_blk = jnp.arange(SEL_BLOCK)
    win_off = jnp.arange(WINDOW + Q_BLOCK)

    def query_block(i):
        q0 = i * Q_BLOCK
        qb = lax.dynamic_slice_in_dim(q, q0, Q_BLOCK, axis=3)
        tq = q0 + jnp.arange(Q_BLOCK)
        idx = lax.dynamic_slice_in_dim(sel_idx, q0, Q_BLOCK, axis=2)
        k_g = ks_blk[b_i, g_i, idx]
        v_g = vs_blk[b_i, g_i, idx]
        s = jnp.einsum('bgrqd,bgqnld->bgrqnl', qb, k_g).astype(f32)
        kpos = idx[..., None] * SEL_BLOCK + in_blk
        m = (kpos <= tq[:, None, None])[:, :, None]
        s = jnp.where(m, s, NEG_INF).reshape(B, G, R, Q_BLOCK, k_eff * SEL_BLOCK)
        p = jax.nn.softmax(s, axis=-1).reshape(B, G, R, Q_BLOCK, k_eff, SEL_BLOCK)
        o_s = jnp.einsum('bgrqnl,bgqnld->bgrqd', p.astype(v_g.dtype), v_g)
        k_w = lax.dynamic_slice_in_dim(kw_pad, q0, WINDOW + Q_BLOCK, axis=2)
        v_w = lax.dynamic_slice_in_dim(vw_pad, q0, WINDOW + Q_BLOCK, axis=2)
        kp = q0 - WINDOW + win_off
        diff = tq[:, None] - kp[None, :]
        mw = (diff >= 0) & (diff < WINDOW) & (kp[None, :] >= 0)
        s_w = jnp.einsum('bgrqd,bgkd->bgrqk', qb, k_w).astype(f32)
        p_w = jax.nn.softmax(jnp.where(mw, s_w, NEG_INF), axis=-1)
        o_w = jnp.einsum('bgrqk,bgkd->bgrqd', p_w.astype(v_w.dtype), v_w)
        return o_s, o_w

    o_s, o_w = lax.map(query_block, jnp.arange(T // Q_BLOCK))
    o_s = jnp.moveaxis(o_s, 0, 3).reshape(B, G, R, T, DH)
    o_w = jnp.moveaxis(o_w, 0, 3).reshape(B, G, R, T, DH)

    gates = jax.nn.sigmoid(gate_logits.astype(f32)).reshape(B, T, 3, G, R)
    gates = jnp.transpose(gates, (2, 0, 3, 4, 1))[..., None].astype(q.dtype)
    o = gates[0] * o_c + gates[1] * o_s + gates[2] * o_w
    return jnp.transpose(o, (0, 3, 1, 2, 4)).reshape(B, T, G * R * DH)


def moe_ffn(x, w_router, b_router, w_gate_up, b_gate_up, w_down, b_down):
    B, T, D = x.shape
    xf = x.reshape(B * T, D)
    N = B * T
    logits = (xf @ w_router + b_router).astype(jnp.float32)
    top_val, top_idx = lax.top_k(logits, TOP_K)
    gate = jax.nn.softmax(top_val, axis=-1)
    M = N * TOP_K
    e_flat = top_idx.reshape(M)
    tok_flat = jnp.repeat(jnp.arange(N, dtype=jnp.int32), TOP_K)
    g_flat = gate.reshape(M).astype(x.dtype)
    order = jnp.argsort(e_flat)
    e_sorted = e_flat[order]
    tok_sorted = tok_flat[order]
    g_sorted = g_flat[order]
    counts = jnp.zeros((N_EXPERTS,), jnp.int32).at[e_flat].add(1)
    start = jnp.cumsum(counts) - counts
    padded = (counts + MOE_BLOCK - 1) // MOE_BLOCK * MOE_BLOCK
    pend = jnp.cumsum(padded)
    pstart = pend - padded
    dest = pstart[e_sorted] + (jnp.arange(M, dtype=jnp.int32) - start[e_sorted])
    n_blocks = -(-M // MOE_BLOCK) + N_EXPERTS
    P = n_blocks * MOE_BLOCK
    row_tok = jnp.full((P,), N, jnp.int32).at[dest].set(tok_sorted)
    row_gate = jnp.zeros((P,), x.dtype).at[dest].set(g_sorted)
    block_expert = jnp.minimum(
        jnp.searchsorted(pend, jnp.arange(n_blocks, dtype=jnp.int32) * MOE_BLOCK, side='right'),
        N_EXPERTS - 1)
    x_pad = jnp.concatenate([xf, jnp.zeros((1, D), xf.dtype)], axis=0)

    def expert_block(args):
        toks, e = args
        xb = x_pad[toks]
        h = xb @ w_gate_up[e] + b_gate_up[e]
        x_glu = jnp.minimum(h[:, ::2], SWIGLU_LIMIT)
        x_lin = jnp.clip(h[:, 1::2], -SWIGLU_LIMIT, SWIGLU_LIMIT)
        act = x_glu * jax.nn.sigmoid(SWIGLU_ALPHA * x_glu) * (x_lin + 1.0)
        return act @ w_down[e] + b_down[e]

    y = lax.map(expert_block, (row_tok.reshape(n_blocks, MOE_BLOCK), block_expert))
    y = y.reshape(P, D) * row_gate[:, None]
    out = jnp.zeros((N + 1, D), y.dtype).at[row_tok].add(y)[:N]
    return out.reshape(B, T, D)


def hybrid_layer(x, positions, w_in, conv_w, cmp_pos_k, cmp_w1_k, cmp_w2_k, cmp_pos_v, cmp_w1_v, cmp_w2_v,
                 w_up_conv, w_up_nsa, w_o, ln1_g, ln1_b, w_router, b_router, w_gate_up, b_gate_up,
                 w_down, b_down, ln2_g, ln2_b):
    B, T, _ = x.shape
    G, R = N_KV_GROUPS, HEADS_PER_GROUP
    proj = x @ w_in
    sizes = [CONV_CH] * 3 + [NSA_WIDTH] + [KV_WIDTH] * 6 + [3 * N_HEADS, D_MODEL, D_MODEL]
    cuts = [int(c) for c in np.cumsum(sizes)[:-1]]
    (xv, b_gate, c_gate, q, kc, vc, ks, vs, kw, vw, nsa_gate, mg_a, mg_b) = jnp.split(proj, cuts, axis=-1)

    u = c_gate * xv
    conv = lax.conv_general_dilated(u, conv_w, window_strides=(1,), padding=[(CONV_K - 1, 0)],
                                    dimension_numbers=('NWC', 'WIO', 'NWC'), feature_group_count=CONV_CH)
    y_a = (b_gate * conv) @ w_up_conv

    qh = partial_rope(q.reshape(B, T, N_HEADS, HEAD_DIM), positions) * (HEAD_DIM ** -0.5)
    qh = qh.reshape(B, T, G, R, HEAD_DIM).transpose(0, 2, 3, 1, 4)

    def kv_heads(t, rotate):
        t = t.reshape(B, T, G, HEAD_DIM)
        if rotate:
            t = partial_rope(t, positions)
        return t.transpose(0, 2, 1, 3)

    o_nsa = nsa_attention(qh, kv_heads(kc, True), kv_heads(vc, False), kv_heads(ks, True), kv_heads(vs, False),
                          kv_heads(kw, True), kv_heads(vw, False), nsa_gate,
                          cmp_pos_k, cmp_w1_k, cmp_w2_k, cmp_pos_v, cmp_w1_v, cmp_w2_v)
    y_b = o_nsa @ w_up_nsa

    merged = jax.nn.sigmoid(mg_a) * y_a + jax.nn.sigmoid(mg_b) * y_b
    x1 = layer_norm(DN_ALPHA * x + merged @ w_o, ln1_g, ln1_b)
    x2 = layer_norm(DN_ALPHA * x1 + moe_ffn(x1, w_router, b_router, w_gate_up, b_gate_up, w_down, b_down),
                    ln2_g, ln2_b)
    return x2


def setup_inputs(seed: int = 0) -> dict:
    key = jax.random.key(seed)
    ks = jax.random.split(key, 24)
    f32 = jnp.float32

    def nrm(k, shape, scale):
        return jax.random.normal(k, (DEPTH,) + shape, f32) * scale

    x = jax.random.normal(ks[0], (BATCH, SEQ, D_MODEL), f32)
    offset = jax.random.randint(ks[1], (BATCH, 1), 0, 1024, dtype=jnp.int32)
    positions = offset + jnp.arange(SEQ, dtype=jnp.int32)[None, :]
    flat_cmp = CMP_BLOCK * HEAD_DIM
    return {
        'x': x,
        'positions': positions,
        'w_in': nrm(ks[2], (D_MODEL, IN_COLS), D_MODEL ** -0.5),
        'conv_w': nrm(ks[3], (CONV_K, 1, CONV_CH), CONV_K ** -0.5),
        'cmp_pos_k': nrm(ks[4], (CMP_BLOCK, HEAD_DIM), 0.1),
        'cmp_w1_k': nrm(ks[5], (flat_cmp, CMP_HIDDEN), flat_cmp ** -0.5),
        'cmp_w2_k': nrm(ks[6], (CMP_HIDDEN, HEAD_DIM), CMP_HIDDEN ** -0.5),
        'cmp_pos_v': nrm(ks[7], (CMP_BLOCK, HEAD_DIM), 0.1),
        'cmp_w1_v': nrm(ks[8], (flat_cmp, CMP_HIDDEN), flat_cmp ** -0.5),
        'cmp_w2_v': nrm(ks[9], (CMP_HIDDEN, HEAD_DIM), CMP_HIDDEN ** -0.5),
        'w_up_conv': nrm(ks[10], (CONV_CH, D_MODEL), CONV_CH ** -0.5),
        'w_up_nsa': nrm(ks[11], (NSA_WIDTH, D_MODEL), NSA_WIDTH ** -0.5),
        'w_o': nrm(ks[12], (D_MODEL, D_MODEL), D_MODEL ** -0.5 * DN_BETA),
        'ln1_g': 1.0 + nrm(ks[13], (D_MODEL,), 0.02),
        'ln1_b': nrm(ks[14], (D_MODEL,), 0.02),
        'w_router': nrm(ks[15], (D_MODEL, N_EXPERTS), D_MODEL ** -0.5),
        'b_router': nrm(ks[16], (N_EXPERTS,), 0.01),
        'w_gate_up': nrm(ks[17], (N_EXPERTS, D_MODEL, 2 * D_FF), D_MODEL ** -0.5),
        'b_gate_up': nrm(ks[18], (N_EXPERTS, 2 * D_FF), 0.02),
        'w_down': nrm(ks[19], (N_EXPERTS, D_FF, D_MODEL), D_FF ** -0.5 * DN_BETA),
        'b_down': nrm(ks[20], (N_EXPERTS, D_MODEL), 0.02),
        'ln2_g': 1.0 + nrm(ks[21], (D_MODEL,), 0.02),
        'ln2_b': nrm(ks[22], (D_MODEL,), 0.02),
    }


def reference(x, positions, w_in, conv_w, cmp_pos_k, cmp_w1_k, cmp_w2_k, cmp_pos_v, cmp_w1_v, cmp_w2_v,
              w_up_conv, w_up_nsa, w_o, ln1_g, ln1_b, w_router, b_router, w_gate_up, b_gate_up,
              w_down, b_down, ln2_g, ln2_b):
    h = x
    for l in range(DEPTH):
        h = hybrid_layer(h, positions, w_in[l], conv_w[l], cmp_pos_k[l], cmp_w1_k[l], cmp_w2_k[l],
                         cmp_pos_v[l], cmp_w1_v[l], cmp_w2_v[l], w_up_conv[l], w_up_nsa[l], w_o[l],
                         ln1_g[l], ln1_b[l], w_router[l], b_router[l], w_gate_up[l], b_gate_up[l],
                         w_down[l], b_down[l], ln2_g[l], ln2_b[l])
    return h
```

```python
import functools

import numpy as np
import jax
import jax.numpy as jnp
from jax import lax
from jax.experimental import pallas as pl
from jax.experimental.pallas import tpu as pltpu

F32 = jnp.float32
BF16 = jnp.bfloat16
I32 = jnp.int32

D_MODEL = 1024
CONV_CH = 512
CONV_K = 3
N_HEADS = 8
N_KV_GROUPS = 2
HEADS_PER_GROUP = N_HEADS // N_KV_GROUPS
HEAD_DIM = 64
NSA_WIDTH = N_HEADS * HEAD_DIM
KV_WIDTH = N_KV_GROUPS * HEAD_DIM
ROPE_DIM = HEAD_DIM // 4
ROPE_THETA = 500000.0
CMP_BLOCK = 32
CMP_STRIDE = 16
CMP_HIDDEN = 256
SEL_BLOCK = 64
N_SELECT = 16
WINDOW = 512
N_EXPERTS = 32
TOP_K = 4
D_FF = 1024
SWIGLU_LIMIT = 7.0
SWIGLU_ALPHA = 1.702
LN_EPS = 1e-5
NEG_INF = -1e30
FORCE_SCORE = 1e4

LANES = 128
SEL_LANES = 64
SEL_MASK_BIAS = -32768.0
GROUP_W = HEADS_PER_GROUP * HEAD_DIM

PROJ_TM = 512
CMP_TQ = 256
SEL_TQ = 128
SEL_KC = 512
WIN_TQ = 128
MERGE_TM = 256
MOE_BLK = 256
DISP_TM = 512
COMB_TM = 256


def _dot(a, b):
    return jnp.dot(a, b, preferred_element_type=F32)


def _dot_nt(a, b, precision=None):
    return lax.dot_general(a, b, (((1,), (1,)), ((), ())), precision=precision, preferred_element_type=F32)


def _proj_kernel(x_ref, pos_ref, wc_ref, wq_ref, wkv_ref, wg_ref, cw_ref, freq_ref,
                 ya_ref, q_ref, kv_ref, g_ref, carry_ref):
    ti = pl.program_id(1)
    tm = x_ref.shape[1]
    xb = x_ref[0].astype(BF16)

    pc = _dot(xb, wc_ref[...])
    xv = pc[:, :CONV_CH]
    bg = pc[:, CONV_CH:2 * CONV_CH]
    cg = pc[:, 2 * CONV_CH:]
    u = cg * xv

    @pl.when(ti == 0)
    def _():
        carry_ref[...] = jnp.zeros_like(carry_ref)

    prev = carry_ref[...]
    row = lax.broadcasted_iota(I32, u.shape, 0)
    u1 = jnp.where(row == 0, prev[7:8], pltpu.roll(u, 1, 0))
    u2 = jnp.where(row == 0, prev[6:7], jnp.where(row == 1, prev[7:8], pltpu.roll(u, 2, 0)))
    cw = cw_ref[...]
    conv = cw[2:3] * u + cw[1:2] * u1 + cw[0:1] * u2
    carry_ref[...] = u[tm - 8:]
    ya_ref[0] = (bg * conv).astype(BF16)

    ang = pos_ref[0] * freq_ref[...]
    cos = jnp.cos(ang)
    sin = jnp.sin(ang)
    d = lax.broadcasted_iota(I32, (1, LANES), 1) % HEAD_DIM
    half = ROPE_DIM // 2
    cm = jnp.where(d < ROPE_DIM, cos, 1.0)
    s_lo = jnp.where(d < half, -sin, 0.0)
    s_hi = jnp.where((d >= half) & (d < ROPE_DIM), sin, 0.0)

    def rope(t):
        return t * cm + pltpu.roll(t, LANES - half, 1) * s_lo + pltpu.roll(t, half, 1) * s_hi

    pq = _dot(xb, wq_ref[...])
    scale = HEAD_DIM ** -0.5
    q_ref[0] = jnp.concatenate(
        [rope(pq[:, j * LANES:(j + 1) * LANES]) * scale for j in range(NSA_WIDTH // LANES)], axis=1).astype(BF16)

    pkv = _dot(xb, wkv_ref[...])
    chunks = []
    for j in range(6):
        t = pkv[:, j * KV_WIDTH:(j + 1) * KV_WIDTH]
        chunks.append(rope(t) if j % 2 == 0 else t)
    kv_ref[0] = jnp.concatenate(chunks, axis=1).astype(BF16)

    g_ref[0] = jax.nn.sigmoid(_dot(xb, wg_ref[...]))


def _proj(x, pos_f, wc, wq, wkv, wg, cw, freq):
    B, T, D = x.shape
    tm = min(PROJ_TM, T)
    grid = (B, T // tm)
    full = lambda a: pl.BlockSpec(a.shape, lambda b, t: (0,) * a.ndim)
    tok = lambda w: pl.BlockSpec((1, tm, w), lambda b, t: (b, t, 0))
    return pl.pallas_call(
        _proj_kernel,
        grid=grid,
        in_specs=[tok(D), tok(1), full(wc), full(wq), full(wkv), full(wg), full(cw), full(freq)],
        out_specs=[tok(CONV_CH), tok(NSA_WIDTH), tok(6 * KV_WIDTH), tok(N_KV_GROUPS * LANES)],
        out_shape=[jax.ShapeDtypeStruct((B, T, CONV_CH), BF16),
                   jax.ShapeDtypeStruct((B, T, NSA_WIDTH), BF16),
                   jax.ShapeDtypeStruct((B, T, 6 * KV_WIDTH), BF16),
                   jax.ShapeDtypeStruct((B, T, N_KV_GROUPS * LANES), F32)],
        scratch_shapes=[pltpu.VMEM((8, CONV_CH), F32)],
        compiler_params=pltpu.CompilerParams(dimension_semantics=("arbitrary", "arbitrary")),
        name="proj",
    )(x, pos_f, wc, wq, wkv, wg, cw, freq)


def _compress_kernel(x_ref, w1_ref, w2_ref, pos_ref, o_ref):
    xb = x_ref[0, 0]
    nc = xb.shape[0]
    w1 = w1_ref[0]
    w1b = w1.astype(BF16)
    half = CMP_STRIDE * HEAD_DIM
    a = _dot(xb, w1b[:half])
    b = _dot(xb, w1b[half:])
    b_next = pltpu.roll(b, nc - 1, 0)
    pb = jnp.dot(pos_ref[0], w1, precision=lax.Precision.HIGHEST, preferred_element_type=F32)[0:1]
    h = a + b_next + pb
    g = 0.5 * h * (1.0 + jnp.tanh(np.sqrt(2.0 / np.pi) * (h + 0.044715 * (h * h * h))))
    o_ref[0, 0] = _dot(g.astype(BF16), w2_ref[0].astype(BF16)).astype(BF16)


def _compress(xin, w1, w2, pos):
    _, BG, nc, W = xin.shape
    return pl.pallas_call(
        _compress_kernel,
        grid=(2, BG),
        in_specs=[pl.BlockSpec((1, 1, nc, W), lambda s, i: (s, i, 0, 0)),
                  pl.BlockSpec((1,) + w1.shape[1:], lambda s, i: (s, 0, 0)),
                  pl.BlockSpec((1,) + w2.shape[1:], lambda s, i: (s, 0, 0)),
                  pl.BlockSpec((1,) + pos.shape[1:], lambda s, i: (s, 0, 0))],
        out_specs=pl.BlockSpec((1, 1, nc, HEAD_DIM), lambda s, i: (s, i, 0, 0)),
        out_shape=jax.ShapeDtypeStruct((2, BG, nc, HEAD_DIM), BF16),
        name="compress",
    )(xin, w1, w2, pos)


def _head_rows(qb):
    return jnp.concatenate([qb[:, r * HEAD_DIM:(r + 1) * HEAD_DIM] for r in range(HEADS_PER_GROUP)], axis=0)


def _gated_out(o, gates, branch, tq):
    cols = []
    for r in range(HEADS_PER_GROUP):
        c = branch * HEADS_PER_GROUP + r
        cols.append(o[r * tq:(r + 1) * tq] * gates[:, c:c + 1])
    return jnp.concatenate(cols, axis=1).astype(BF16)


def _cmp_attn_kernel(q_ref, kc_ref, vc_ref, g_ref, ovt_ref, o_ref, sel_ref):
    qi = pl.program_id(2)
    tq = q_ref.shape[1]
    q0 = qi * tq
    qrows = _head_rows(q_ref[0])
    kc = kc_ref[0, 0]
    vc = vc_ref[0, 0]
    nc = kc.shape[0]
    s = _dot_nt(qrows, kc)
    t = q0 + (lax.broadcasted_iota(I32, s.shape, 0) & (tq - 1))
    c = lax.broadcasted_iota(I32, s.shape, 1)
    valid = c * CMP_STRIDE + (CMP_BLOCK - 1) <= t
    sm = jnp.where(valid, s, NEG_INF)
    m = jnp.max(sm, axis=-1, keepdims=True)
    p = jnp.where(valid, jnp.exp(sm - m), 0.0)
    l = jnp.sum(p, axis=-1, keepdims=True)
    p = p / jnp.where(l > 0.0, l, 1.0)
    o = _dot(p.astype(BF16), vc)
    o_ref[0] = _gated_out(o, g_ref[0], 0, tq)

    psum = p[0:tq]
    for r in range(1, HEADS_PER_GROUP):
        psum = psum + p[r * tq:(r + 1) * tq]
    imp_t = _dot_nt(ovt_ref[...], psum, precision=lax.Precision.HIGHEST)
    j = lax.broadcasted_iota(I32, imp_t.shape, 0)
    cur = (q0 + lax.broadcasted_iota(I32, imp_t.shape, 1)) // SEL_BLOCK
    valid_b = j <= cur
    forced = (j == 0) | (j == cur) | (j == cur - 1)
    score = jnp.where(valid_b, jnp.where(forced, FORCE_SCORE, imp_t), NEG_INF)
    rank = jnp.zeros(score.shape, I32)
    for i in range(SEL_LANES):
        si = score[i:i + 1, :]
        rank = rank + jnp.where(j > i, jnp.where(si >= score, 1, 0), jnp.where(si > score, 1, 0))
    selected = (rank < N_SELECT) & valid_b
    bias = jnp.where(selected, 0.0, SEL_MASK_BIAS)
    sel_ref[0] = jnp.concatenate([bias, bias], axis=0).T.astype(BF16)


def _cmp_attn(q, kcmp, gates, ovt):
    B, T, _ = q.shape
    tq = min(CMP_TQ, T)
    nc = kcmp.shape[2]
    G = N_KV_GROUPS
    tokg = lambda w: pl.BlockSpec((1, tq, w), lambda b, g, i: (b, i, g))
    return pl.pallas_call(
        _cmp_attn_kernel,
        grid=(B, G, T // tq),
        in_specs=[tokg(GROUP_W),
                  pl.BlockSpec((1, 1, nc, HEAD_DIM), lambda b, g, i: (0, b * G + g, 0, 0)),
                  pl.BlockSpec((1, 1, nc, HEAD_DIM), lambda b, g, i: (1, b * G + g, 0, 0)),
                  tokg(LANES),
                  pl.BlockSpec(ovt.shape, lambda b, g, i: (0, 0))],
        out_specs=[tokg(GROUP_W), tokg(LANES)],
        out_shape=[jax.ShapeDtypeStruct((B, T, NSA_WIDTH), BF16),
                   jax.ShapeDtypeStruct((B, T, G * LANES), BF16)],
        name="cmp_attn",
    )(q, kcmp, kcmp, gates, ovt)


def _sel_attn_kernel(q_ref, sel_ref, k_ref, v_ref, g_ref, o_ref, *, kc):
    qi = pl.program_id(2)
    tq = q_ref.shape[1]
    q0 = qi * tq
    qb = q_ref[0]
    sb = sel_ref[0][:, :SEL_LANES]
    qa = jnp.concatenate(
        [jnp.concatenate([qb[:, r * HEAD_DIM:(r + 1) * HEAD_DIM], sb], axis=1) for r in range(HEADS_PER_GROUP)],
        axis=0)
    rows = qa.shape[0]

    def step(kstart, carry, causal):
        m, l, acc = carry
        kblk = k_ref[0, 0, pl.ds(kstart, kc), :]
        vblk = v_ref[0, 0, pl.ds(kstart, kc), :]
        s = _dot_nt(qa, kblk)
        if causal:
            t = q0 + (lax.broadcasted_iota(I32, s.shape, 0) & (tq - 1))
            kp = kstart + lax.broadcasted_iota(I32, s.shape, 1)
            s = jnp.where(kp <= t, s, NEG_INF)
        mn = jnp.maximum(m, jnp.max(s, axis=-1, keepdims=True))
        alpha = jnp.exp(m - mn)
        p = jnp.exp(s - mn)
        l = alpha * l + jnp.sum(p, axis=-1, keepdims=True)
        acc = alpha * acc + _dot(p.astype(BF16), vblk)
        return mn, l, acc

    init = (jnp.full((rows, 1), NEG_INF, F32), jnp.zeros((rows, 1), F32), jnp.zeros((rows, HEAD_DIM), F32))
    n_full = q0 // kc
    carry = lax.fori_loop(0, n_full, lambda i, cr: step(pl.multiple_of(i * kc, kc), cr, False), init)
    _, l, acc = step(pl.multiple_of(n_full * kc, kc), carry, True)
    o_ref[0] = _gated_out(acc / l, g_ref[0], 1, tq)


def _sel_attn(q, sel, kaug, v, gates):
    B, T, _ = q.shape
    tq = min(SEL_TQ, T)
    kc = min(SEL_KC, T)
    G = N_KV_GROUPS
    tokg = lambda w: pl.BlockSpec((1, tq, w), lambda b, g, i: (b, i, g))
    kvspec = lambda a: pl.BlockSpec((1, 1) + a.shape[2:], lambda b, g, i: (b, g, 0, 0))
    return pl.pallas_call(
        functools.partial(_sel_attn_kernel, kc=kc),
        grid=(B, G, T // tq),
        in_specs=[tokg(GROUP_W), tokg(LANES), kvspec(kaug), kvspec(v), tokg(LANES)],
        out_specs=tokg(GROUP_W),
        out_shape=jax.ShapeDtypeStruct((B, T, NSA_WIDTH), BF16),
        name="sel_attn",
    )(q, sel, kaug, v, gates)


def _win_attn_kernel(q_ref, k_ref, v_ref, g_ref, o_ref, *, span):
    qi = pl.program_id(2)
    tq = q_ref.shape[1]
    T = k_ref.shape[2]
    q0 = qi * tq
    start = pl.multiple_of(jnp.clip(q0 + tq - span, 0, T - span), tq)
    qrows = _head_rows(q_ref[0])
    k = k_ref[0, 0, pl.ds(start, span), :]
    v = v_ref[0, 0, pl.ds(start, span), :]
    s = _dot_nt(qrows, k)
    t = q0 + (lax.broadcasted_iota(I32, s.shape, 0) & (tq - 1))
    diff = t - (start + lax.broadcasted_iota(I32, s.shape, 1))
    s = jnp.where((diff >= 0) & (diff < WINDOW), s, NEG_INF)
    m = jnp.max(s, axis=-1, keepdims=True)
    p = jnp.exp(s - m)
    l = jnp.sum(p, axis=-1, keepdims=True)
    o = _dot(p.astype(BF16), v) / l
    o_ref[0] = _gated_out(o, g_ref[0], 2, tq)


def _win_attn(q, k, v, gates):
    B, T, _ = q.shape
    tq = min(WIN_TQ, T)
    span = min(WINDOW + tq, T)
    G = N_KV_GROUPS
    tokg = lambda w: pl.BlockSpec((1, tq, w), lambda b, g, i: (b, i, g))
    kvspec = lambda a: pl.BlockSpec((1, 1) + a.shape[2:], lambda b, g, i: (b, g, 0, 0))
    return pl.pallas_call(
        functools.partial(_win_attn_kernel, span=span),
        grid=(B, G, T // tq),
        in_specs=[tokg(GROUP_W), kvspec(k), kvspec(v), tokg(LANES)],
        out_specs=tokg(GROUP_W),
        out_shape=jax.ShapeDtypeStruct((B, T, NSA_WIDTH), BF16),
        name="win_attn",
    )(q, k, v, gates)


def _layer_norm(h, g, b):
    mu = jnp.mean(h, axis=-1, keepdims=True)
    c = h - mu
    var = jnp.mean(c * c, axis=-1, keepdims=True)
    return c * lax.rsqrt(var + LN_EPS) * g + b


def _merge_kernel(x_ref, ya_ref, oc_ref, os_ref, ow_ref, wmg_ref, wuc_ref, wun_ref, wo_ref, g1_ref, b1_ref,
                  wr_ref, br_ref, tri_ref, x1_ref, ri_ref, rg_ref, cnt_ref, carry_ref, *, alpha):
    i = pl.program_id(0)
    x = x_ref[...]
    xb = x.astype(BF16)
    mg = _dot(xb, wmg_ref[...])
    y_a = _dot(ya_ref[...], wuc_ref[...])
    o_nsa = (oc_ref[...].astype(F32) + os_ref[...].astype(F32) + ow_ref[...].astype(F32)).astype(BF16)
    y_b = _dot(o_nsa, wun_ref[...])
    merged = jax.nn.sigmoid(mg[:, :D_MODEL]) * y_a + jax.nn.sigmoid(mg[:, D_MODEL:]) * y_b
    h = alpha * x + _dot(merged.astype(BF16), wo_ref[...])
    x1 = _layer_norm(h, g1_ref[...], b1_ref[...])
    x1_ref[...] = x1

    logits = jnp.dot(x1, wr_ref[...], precision=lax.Precision.HIGHEST, preferred_element_type=F32) + br_ref[...]
    lane = lax.broadcasted_iota(I32, logits.shape, 1).astype(F32)
    rem = logits
    vals, idxs, hots = [], [], []
    for _ in range(TOP_K):
        m = jnp.max(rem, axis=-1, keepdims=True)
        idx = jnp.min(jnp.where(rem == m, lane, float(LANES)), axis=-1, keepdims=True)
        hot = lane == idx
        vals.append(m)
        idxs.append(idx)
        hots.append(hot)
        rem = jnp.where(hot, -3.0e38, rem)
    es = [jnp.exp(v - vals[0]) for v in vals]
    den = es[0]
    for e in es[1:]:
        den = den + e
    chosen = hots[0]
    for hot in hots[1:]:
        chosen = chosen | hot
    chosen_f = jnp.where(chosen, 1.0, 0.0)

    @pl.when(i == 0)
    def _():
        carry_ref[...] = jnp.zeros_like(carry_ref)

    before = _dot(tri_ref[...], chosen_f.astype(BF16)) + carry_ref[...]
    total = carry_ref[...] + jnp.sum(chosen_f, axis=0, keepdims=True)
    carry_ref[...] = total
    cnt_ref[...] = total.astype(I32)

    ri = jnp.zeros(logits.shape, F32)
    rg = jnp.zeros(logits.shape, F32)
    for k in range(TOP_K):
        rank_k = jnp.sum(jnp.where(hots[k], before, 0.0), axis=-1, keepdims=True)
        ri = jnp.where(lane == float(k), idxs[k], ri)
        ri = jnp.where(lane == float(TOP_K + k), rank_k, ri)
        rg = jnp.where(lane == float(k), es[k] / den, rg)
    ri_ref[...] = ri.astype(I32)
    rg_ref[...] = rg


def _merge(x, ya, oc, os_, ow, wmg, wuc, wun, wo, g1, b1, wr, br, tri, alpha):
    N, D = x.shape
    tm = tri.shape[0]
    tok = lambda w: pl.BlockSpec((tm, w), lambda i: (i, 0))
    full = lambda a: pl.BlockSpec(a.shape, lambda i: (0,) * a.ndim)
    return pl.pallas_call(
        functools.partial(_merge_kernel, alpha=alpha),
        grid=(N // tm,),
        in_specs=[tok(D), tok(CONV_CH), tok(NSA_WIDTH), tok(NSA_WIDTH), tok(NSA_WIDTH),
                  full(wmg), full(wuc), full(wun), full(wo), full(g1), full(b1), full(wr), full(br), full(tri)],
        out_specs=[tok(D), tok(LANES), tok(LANES), pl.BlockSpec((1, LANES), lambda i: (0, 0))],
        out_shape=[jax.ShapeDtypeStruct((N, D), F32),
                   jax.ShapeDtypeStruct((N, LANES), I32),
                   jax.ShapeDtypeStruct((N, LANES), F32),
                   jax.ShapeDtypeStruct((1, LANES), I32)],
        scratch_shapes=[pltpu.VMEM((1, LANES), F32)],
        compiler_params=pltpu.CompilerParams(dimension_semantics=("arbitrary",)),
        name="merge",
    )(x, ya, oc, os_, ow, wmg, wuc, wun, wo, g1, b1, wr, br, tri)


def _row_copy(src, si, dst, di, sem):
    return pltpu.make_async_copy(src.at[pl.ds(si, 1), :], dst.at[pl.ds(di, 1), :], sem)


def _dispatch_kernel(dest_ref, cnt_ref, pstart_ref, pend_ref, x_hbm, xs_hbm, zero_ref, sem, zsem):
    i = pl.program_id(0)
    n = dest_ref.shape[0]
    tok0 = i * (n // TOP_K)

    @pl.when(i == 0)
    def _():
        zero_ref[...] = jnp.zeros_like(zero_ref)
        for e in range(N_EXPERTS):
            lo = pstart_ref[e] + cnt_ref[e]
            hi = pend_ref[e]

            def zstart(r, c):
                _row_copy(zero_ref, 0, xs_hbm, r, zsem).start()
                return c

            def zwait(r, c):
                _row_copy(zero_ref, 0, xs_hbm, r, zsem).wait()
                return c

            lax.fori_loop(lo, hi, zstart, 0)
            lax.fori_loop(lo, hi, zwait, 0)

    def start(j, c):
        _row_copy(x_hbm, tok0 + j // TOP_K, xs_hbm, dest_ref[j], sem).start()
        return c

    def wait(j, c):
        _row_copy(x_hbm, 0, xs_hbm, 0, sem).wait()
        return c

    lax.fori_loop(0, n, start, 0)
    lax.fori_loop(0, n, wait, 0)


def _dispatch(dest, cnt, pstart, pend, x1, n_rows):
    N, D = x1.shape
    tm = min(DISP_TM, N)
    smem = pl.BlockSpec(memory_space=pltpu.SMEM)
    return pl.pallas_call(
        _dispatch_kernel,
        grid=(N // tm,),
        in_specs=[pl.BlockSpec((tm * TOP_K,), lambda i: (i,), memory_space=pltpu.SMEM), smem, smem, smem,
                  pl.BlockSpec(memory_space=pl.ANY)],
        out_specs=pl.BlockSpec(memory_space=pl.ANY),
        out_shape=jax.ShapeDtypeStruct((n_rows, D), F32),
        scratch_shapes=[pltpu.VMEM((8, D), F32), pltpu.SemaphoreType.DMA(()), pltpu.SemaphoreType.DMA(())],
        compiler_params=pltpu.CompilerParams(dimension_semantics=("arbitrary",)),
        name="dispatch",
    )(dest, cnt, pstart, pend, x1)


def _experts_kernel(be_ref, nu_ref, xs_ref, wg_ref, wl_ref, bg_ref, bl_ref, wd_ref, bd_ref, ys_ref):
    i = pl.program_id(0)

    @pl.when(i < nu_ref[0])
    def _():
        xb = xs_ref[...].astype(BF16)
        x_glu = jnp.minimum(_dot(xb, wg_ref[0]) + bg_ref[0], SWIGLU_LIMIT)
        x_lin = jnp.clip(_dot(xb, wl_ref[0]) + bl_ref[0], -SWIGLU_LIMIT, SWIGLU_LIMIT)
        act = x_glu * jax.nn.sigmoid(SWIGLU_ALPHA * x_glu) * (x_lin + 1.0)
        ys_ref[...] = _dot(act.astype(BF16), wd_ref[0]) + bd_ref[0]


def _experts(block_expert, n_used, xs, wg, wl, bg, bl, wd, bd):
    P, D = xs.shape
    n_blocks = P // MOE_BLK
    rows = lambda i, be, nu: (jnp.minimum(i, nu[0] - 1), 0)
    wsel = lambda i, be, nu: (be[i], 0, 0)
    return pl.pallas_call(
        _experts_kernel,
        grid_spec=pltpu.PrefetchScalarGridSpec(
            num_scalar_prefetch=2,
            grid=(n_blocks,),
            in_specs=[pl.BlockSpec((MOE_BLK, D), rows),
                      pl.BlockSpec((1, D, D_FF), wsel), pl.BlockSpec((1, D, D_FF), wsel),
                      pl.BlockSpec((1, 1, D_FF), wsel), pl.BlockSpec((1, 1, D_FF), wsel),
                      pl.BlockSpec((1, D_FF, D), wsel), pl.BlockSpec((1, 1, D), wsel)],
            out_specs=pl.BlockSpec((MOE_BLK, D), rows)),
        out_shape=jax.ShapeDtypeStruct((P, D), F32),
        compiler_params=pltpu.CompilerParams(dimension_semantics=("arbitrary",)),
        name="experts",
    )(block_expert, n_used, xs, wg, wl, bg, bl, wd, bd)


def _combine_kernel(dest_ref, ys_hbm, x1_ref, rg_ref, g2_ref, b2_ref, o_ref, buf, sem, *, alpha):
    tm = o_ref.shape[0]

    def start(t, c):
        for k in range(TOP_K):
            pltpu.make_async_copy(ys_hbm.at[pl.ds(dest_ref[t * TOP_K + k], 1), :], buf.at[k, pl.ds(t, 1), :], sem).start()
        return c

    def wait(t, c):
        for k in range(TOP_K):
            pltpu.make_async_copy(ys_hbm.at[pl.ds(0, 1), :], buf.at[k, pl.ds(t, 1), :], sem).wait()
        return c

    lax.fori_loop(0, tm, start, 0)
    lax.fori_loop(0, tm, wait, 0)
    gate = rg_ref[...]
    y = gate[:, 0:1] * buf[0]
    for k in range(1, TOP_K):
        y = y + gate[:, k:k + 1] * buf[k]
    o_ref[...] = _layer_norm(alpha * x1_ref[...] + y, g2_ref[...], b2_ref[...])


def _combine(dest, ys, x1, rg, g2, b2, alpha):
    N, D = x1.shape
    tm = min(COMB_TM, N)
    tok = lambda w: pl.BlockSpec((tm, w), lambda i: (i, 0))
    full = lambda a: pl.BlockSpec(a.shape, lambda i: (0,) * a.ndim)
    return pl.pallas_call(
        functools.partial(_combine_kernel, alpha=alpha),
        grid=(N // tm,),
        in_specs=[pl.BlockSpec((tm * TOP_K,), lambda i: (i,), memory_space=pltpu.SMEM),
                  pl.BlockSpec(memory_space=pl.ANY), tok(D), tok(LANES), full(g2), full(b2)],
        out_specs=tok(D),
        out_shape=jax.ShapeDtypeStruct((N, D), F32),
        scratch_shapes=[pltpu.VMEM((TOP_K, tm, D), F32), pltpu.SemaphoreType.DMA(())],
        compiler_params=pltpu.CompilerParams(dimension_semantics=("arbitrary",)),
        name="combine",
    )(dest, ys, x1, rg, g2, b2)


def _rope_freq_lanes():
    half = ROPE_DIM // 2
    inv = (np.float32(ROPE_THETA) ** (-np.arange(half, dtype=np.float32) * np.float32(2.0 / ROPE_DIM))).astype(np.float32)
    return np.tile(inv, LANES // half)[None, :]


def _overlap_t(T):
    nc = T // CMP_STRIDE
    c0 = np.arange(nc) * CMP_STRIDE
    j0 = np.arange(SEL_LANES) * SEL_BLOCK
    ov = (c0[None, :] < j0[:, None] + SEL_BLOCK) & (c0[None, :] + CMP_BLOCK > j0[:, None])
    ov &= (np.arange(nc) < nc - 1)[None, :] & (np.arange(SEL_LANES) < T // SEL_BLOCK)[:, None]
    return ov.astype(np.float32)


def _group_heads(t, B, T):
    return t.reshape(B, T, N_KV_GROUPS, HEAD_DIM).transpose(0, 2, 1, 3)


def _layer(x, positions, w_in, conv_w, cmp_pos_k, cmp_w1_k, cmp_w2_k, cmp_pos_v, cmp_w1_v, cmp_w2_v,
           w_up_conv, w_up_nsa, w_o, ln1_g, ln1_b, w_router, b_router, w_gate_up, b_gate_up,
           w_down, b_down, ln2_g, ln2_b, alpha):
    B, T, D = x.shape
    G, R = N_KV_GROUPS, HEADS_PER_GROUP
    N = B * T
    assert D == D_MODEL and T % SEL_BLOCK == 0 and T // SEL_BLOCK <= SEL_LANES

    c0 = 3 * CONV_CH
    c1 = c0 + NSA_WIDTH
    c2 = c1 + 6 * KV_WIDTH
    c3 = c2 + 3 * N_HEADS
    wc = w_in[:, :c0].astype(BF16)
    wq = w_in[:, c0:c1].astype(BF16)
    wkv = w_in[:, c1:c2].astype(BF16)
    w_nsa_gate = w_in[:, c2:c3]
    gcols = np.zeros((G, LANES), np.int64)
    gmask = np.zeros((G, LANES), np.float32)
    for g in range(G):
        for br in range(3):
            for r in range(R):
                gcols[g, br * R + r] = br * N_HEADS + g * R + r
                gmask[g, br * R + r] = 1.0
    wg = (w_nsa_gate[:, gcols.reshape(-1)] * gmask.reshape(-1)).astype(BF16)
    wmg = w_in[:, c3:].astype(BF16)
    cw = conv_w.reshape(CONV_K, CONV_CH)
    freq = jnp.asarray(_rope_freq_lanes())
    pos_f = positions.astype(F32)[..., None]

    ya_pre, q, kv, gates = _proj(x, pos_f, wc, wq, wkv, wg, cw, freq)

    nc = T // CMP_STRIDE
    to_chunks = lambda t: _group_heads(t, B, T).reshape(B * G, nc, CMP_STRIDE * HEAD_DIM)
    xin = jnp.stack([to_chunks(kv[..., 0:KV_WIDTH]), to_chunks(kv[..., KV_WIDTH:2 * KV_WIDTH])])
    w1 = jnp.stack([cmp_w1_k, cmp_w1_v])
    w2 = jnp.stack([cmp_w2_k, cmp_w2_v])
    pos_flat = jnp.stack([cmp_pos_k.reshape(1, -1), cmp_pos_v.reshape(1, -1)])
    pos_flat = jnp.pad(pos_flat, ((0, 0), (0, 7), (0, 0)))
    kcmp = _compress(xin, w1, w2, pos_flat)

    o_c, sel = _cmp_attn(q, kcmp, gates, jnp.asarray(_overlap_t(T)))

    ks = _group_heads(kv[..., 2 * KV_WIDTH:3 * KV_WIDTH], B, T)
    vs = _group_heads(kv[..., 3 * KV_WIDTH:4 * KV_WIDTH], B, T)
    kw = _group_heads(kv[..., 4 * KV_WIDTH:5 * KV_WIDTH], B, T)
    vw = _group_heads(kv[..., 5 * KV_WIDTH:6 * KV_WIDTH], B, T)
    onehot = (np.arange(T)[:, None] // SEL_BLOCK == np.arange(SEL_LANES)[None, :]).astype(np.float32)
    kaug = jnp.concatenate([ks, jnp.broadcast_to(jnp.asarray(onehot, BF16), (B, G, T, SEL_LANES))], axis=-1)
    o_s = _sel_attn(q, sel, kaug, vs, gates)
    o_w = _win_attn(q, kw, vw, gates)

    tm = min(MERGE_TM, N)
    tri = jnp.asarray(np.tril(np.ones((tm, tm), np.float32), -1), BF16)
    wr = jnp.pad(w_router, ((0, 0), (0, LANES - N_EXPERTS)))
    br = jnp.pad(b_router, (0, LANES - N_EXPERTS), constant_values=NEG_INF)[None, :]
    x1, ri, rg, cnt = _merge(x.reshape(N, D), ya_pre.reshape(N, CONV_CH), o_c.reshape(N, NSA_WIDTH),
                             o_s.reshape(N, NSA_WIDTH), o_w.reshape(N, NSA_WIDTH), wmg,
                             w_up_conv.astype(BF16), w_up_nsa.astype(BF16), w_o.astype(BF16),
                             ln1_g[None, :], ln1_b[None, :], wr, br, tri, alpha)

    counts = cnt[0, :N_EXPERTS]
    padded = (counts + MOE_BLK - 1) // MOE_BLK * MOE_BLK
    pend = jnp.cumsum(padded).astype(I32)
    pstart = pend - padded
    idx4 = ri[:, :TOP_K]
    rank4 = ri[:, TOP_K:2 * TOP_K]
    dest = (pstart[idx4] + rank4).reshape(N * TOP_K)
    n_blocks = (N * TOP_K) // MOE_BLK + N_EXPERTS
    block_expert = jnp.minimum(
        jnp.searchsorted(pend, jnp.arange(n_blocks, dtype=I32) * MOE_BLK, side='right'), N_EXPERTS - 1).astype(I32)
    n_used = (pend[-1:] // MOE_BLK).astype(I32)

    xs = _dispatch(dest, counts, pstart, pend, x1, n_blocks * MOE_BLK)
    ys = _experts(block_expert, n_used, xs,
                  w_gate_up[:, :, 0::2].astype(BF16), w_gate_up[:, :, 1::2].astype(BF16),
                  b_gate_up[:, None, 0::2], b_gate_up[:, None, 1::2],
                  w_down.astype(BF16), b_down[:, None, :])
    out = _combine(dest, ys, x1, rg, ln2_g[None, :], ln2_b[None, :], alpha)
    return out.reshape(B, T, D)


def kernel(x, positions, w_in, conv_w, cmp_pos_k, cmp_w1_k, cmp_w2_k, cmp_pos_v, cmp_w1_v, cmp_w2_v, w_up_conv, w_up_nsa, w_o, ln1_g, ln1_b, w_router, b_router, w_gate_up, b_gate_up, w_down, b_down, ln2_g, ln2_b):
    depth = w_in.shape[0]
    alpha = float((2 * depth) ** 0.25)
    h = x
    for l in range(depth):
        h = _layer(h, positions, w_in[l], conv_w[l], cmp_pos_k[l], cmp_w1_k[l], cmp_w2_k[l],
                   cmp_pos_v[l], cmp_w1_v[l], cmp_w2_v[l], w_up_conv[l], w_up_nsa[l], w_o[l],
                   ln1_g[l], ln1_b[l], w_router[l], b_router[l], w_gate_up[l], b_gate_up[l],
                   w_down[l], b_down[l], ln2_g[l], ln2_b[l], alpha)
    return h
```

```python
import functools

import numpy as np
import jax
import jax.numpy as jnp
from jax import lax
from jax.experimental import pallas as pl
from jax.experimental.pallas import tpu as pltpu

F32 = jnp.float32
BF16 = jnp.bfloat16
I32 = jnp.int32

D_MODEL = 1024
CONV_CH = 512
CONV_K = 3
N_HEADS = 8
N_KV_GROUPS = 2
HEADS_PER_GROUP = N_HEADS // N_KV_GROUPS
HEAD_DIM = 64
NSA_WIDTH = N_HEADS * HEAD_DIM
KV_WIDTH = N_KV_GROUPS * HEAD_DIM
ROPE_DIM = HEAD_DIM // 4
ROPE_THETA = 500000.0
CMP_BLOCK = 32
CMP_STRIDE = 16
CMP_HIDDEN = 256
SEL_BLOCK = 64
N_SELECT = 16
WINDOW = 512
N_EXPERTS = 32
TOP_K = 4
D_FF = 1024
SWIGLU_LIMIT = 7.0
SWIGLU_ALPHA = 1.702
LN_EPS = 1e-5
NEG_INF = -1e30
FORCE_SCORE = 1e4

LANES = 128
SEL_LANES = 64
SEL_MASK_BIAS = -32768.0
GROUP_W = HEADS_PER_GROUP * HEAD_DIM

PROJ_TM = 512
CMP_TQ = 256
SEL_TQ = 128
SEL_KC = 512
WIN_TQ = 128
MERGE_TM = 256
MOE_BLK = 256
DISP_TM = 512
COMB_TM = 256


def _dot(a, b):
    return jnp.dot(a, b, preferred_element_type=F32)


def _dot_nt(a, b, precision=None):
    return lax.dot_general(a, b, (((1,), (1,)), ((), ())), precision=precision, preferred_element_type=F32)


def _proj_kernel(x_ref, pos_ref, wc_ref, wq_ref, wkv_ref, wg_ref, cw_ref, freq_ref,
                 ya_ref, q_ref, kv_ref, g_ref, carry_ref):
    ti = pl.program_id(1)
    tm = x_ref.shape[1]
    xb = x_ref[0].astype(BF16)

    pc = _dot(xb, wc_ref[...])
    xv = pc[:, :CONV_CH]
    bg = pc[:, CONV_CH:2 * CONV_CH]
    cg = pc[:, 2 * CONV_CH:]
    u = cg * xv

    @pl.when(ti == 0)
    def _():
        carry_ref[...] = jnp.zeros_like(carry_ref)

    prev = carry_ref[...]
    row = lax.broadcasted_iota(I32, u.shape, 0)
    u1 = jnp.where(row == 0, prev[7:8], pltpu.roll(u, 1, 0))
    u2 = jnp.where(row == 0, prev[6:7], jnp.where(row == 1, prev[7:8], pltpu.roll(u, 2, 0)))
    cw = cw_ref[...]
    conv = cw[2:3] * u + cw[1:2] * u1 + cw[0:1] * u2
    carry_ref[...] = u[tm - 8:]
    ya_ref[0] = (bg * conv).astype(BF16)

    ang = pos_ref[0] * freq_ref[...]
    cos = jnp.cos(ang)
    sin = jnp.sin(ang)
    d = lax.broadcasted_iota(I32, (1, LANES), 1) % HEAD_DIM
    half = ROPE_DIM // 2
    cm = jnp.where(d < ROPE_DIM, cos, 1.0)
    s_lo = jnp.where(d < half, -sin, 0.0)
    s_hi = jnp.where((d >= half) & (d < ROPE_DIM), sin, 0.0)

    def rope(t):
        return t * cm + pltpu.roll(t, LANES - half, 1) * s_lo + pltpu.roll(t, half, 1) * s_hi

    pq = _dot(xb, wq_ref[...])
    scale = HEAD_DIM ** -0.5
    q_ref[0] = jnp.concatenate(
        [rope(pq[:, j * LANES:(j + 1) * LANES]) * scale for j in range(NSA_WIDTH // LANES)], axis=1).astype(BF16)

    pkv = _dot(xb, wkv_ref[...])
    chunks = []
    for j in range(6):
        t = pkv[:, j * KV_WIDTH:(j + 1) * KV_WIDTH]
        chunks.append(rope(t) if j % 2 == 0 else t)
    kv_ref[0] = jnp.concatenate(chunks, axis=1).astype(BF16)

    g_ref[0] = jax.nn.sigmoid(_dot(xb, wg_ref[...]))


def _proj(x, pos_f, wc, wq, wkv, wg, cw, freq):
    B, T, D = x.shape
    tm = min(PROJ_TM, T)
    grid = (B, T // tm)
    full = lambda a: pl.BlockSpec(a.shape, lambda b, t: (0,) * a.ndim)
    tok = lambda w: pl.BlockSpec((1, tm, w), lambda b, t: (b, t, 0))
    return pl.pallas_call(
        _proj_kernel,
        grid=grid,
        in_specs=[tok(D), tok(1), full(wc), full(wq), full(wkv), full(wg), full(cw), full(freq)],
        out_specs=[tok(CONV_CH), tok(NSA_WIDTH), tok(6 * KV_WIDTH), tok(N_KV_GROUPS * LANES)],
        out_shape=[jax.ShapeDtypeStruct((B, T, CONV_CH), BF16),
                   jax.ShapeDtypeStruct((B, T, NSA_WIDTH), BF16),
                   jax.ShapeDtypeStruct((B, T, 6 * KV_WIDTH), BF16),
                   jax.ShapeDtypeStruct((B, T, N_KV_GROUPS * LANES), F32)],
        scratch_shapes=[pltpu.VMEM((8, CONV_CH), F32)],
        compiler_params=pltpu.CompilerParams(dimension_semantics=("arbitrary", "arbitrary")),
        name="proj",
    )(x, pos_f, wc, wq, wkv, wg, cw, freq)


def _compress_kernel(x_ref, w1_ref, w2_ref, pos_ref, o_ref):
    xb = x_ref[0, 0]
    nc = xb.shape[0]
    w1 = w1_ref[0]
    w1b = w1.astype(BF16)
    half = CMP_STRIDE * HEAD_DIM
    a = _dot(xb, w1b[:half])
    b = _dot(xb, w1b[half:])
    b_next = pltpu.roll(b, nc - 1, 0)
    pb = jnp.dot(pos_ref[0], w1, precision=lax.Precision.HIGHEST, preferred_element_type=F32)[0:1]
    h = a + b_next + pb
    g = 0.5 * h * (1.0 + jnp.tanh(np.sqrt(2.0 / np.pi) * (h + 0.044715 * (h * h * h))))
    o_ref[0, 0] = _dot(g.astype(BF16), w2_ref[0].astype(BF16)).astype(BF16)


def _compress(xin, w1, w2, pos):
    _, BG, nc, W = xin.shape
    return pl.pallas_call(
        _compress_kernel,
        grid=(2, BG),
        in_specs=[pl.BlockSpec((1, 1, nc, W), lambda s, i: (s, i, 0, 0)),
                  pl.BlockSpec((1,) + w1.shape[1:], lambda s, i: (s, 0, 0)),
                  pl.BlockSpec((1,) + w2.shape[1:], lambda s, i: (s, 0, 0)),
                  pl.BlockSpec((1,) + pos.shape[1:], lambda s, i: (s, 0, 0))],
        out_specs=pl.BlockSpec((1, 1, nc, HEAD_DIM), lambda s, i: (s, i, 0, 0)),
        out_shape=jax.ShapeDtypeStruct((2, BG, nc, HEAD_DIM), BF16),
        name="compress",
    )(xin, w1, w2, pos)


def _head_rows(qb):
    return jnp.concatenate([qb[:, r * HEAD_DIM:(r + 1) * HEAD_DIM] for r in range(HEADS_PER_GROUP)], axis=0)


def _gated_out(o, gates, branch, tq):
    cols = []
    for r in range(HEADS_PER_GROUP):
        c = branch * HEADS_PER_GROUP + r
        cols.append(o[r * tq:(r + 1) * tq] * gates[:, c:c + 1])
    return jnp.concatenate(cols, axis=1).astype(BF16)


def _cmp_attn_kernel(q_ref, kc_ref, vc_ref, g_ref, ovt_ref, o_ref, sel_ref):
    qi = pl.program_id(2)
    tq = q_ref.shape[1]
    q0 = qi * tq
    qrows = _head_rows(q_ref[0])
    kc = kc_ref[0, 0]
    vc = vc_ref[0, 0]
    nc = kc.shape[0]
    s = _dot_nt(qrows, kc)
    t = q0 + (lax.broadcasted_iota(I32, s.shape, 0) & (tq - 1))
    c = lax.broadcasted_iota(I32, s.shape, 1)
    valid = c * CMP_STRIDE + (CMP_BLOCK - 1) <= t
    sm = jnp.where(valid, s, NEG_INF)
    m = jnp.max(sm, axis=-1, keepdims=True)
    p = jnp.where(valid, jnp.exp(sm - m), 0.0)
    l = jnp.sum(p, axis=-1, keepdims=True)
    p = p / jnp.where(l > 0.0, l, 1.0)
    o = _dot(p.astype(BF16), vc)
    o_ref[0] = _gated_out(o, g_ref[0], 0, tq)

    psum = p[0:tq]
    for r in range(1, HEADS_PER_GROUP):
        psum = psum + p[r * tq:(r + 1) * tq]
    imp_t = _dot_nt(ovt_ref[...], psum, precision=lax.Precision.HIGHEST)
    j = lax.broadcasted_iota(I32, imp_t.shape, 0)
    cur = (q0 + lax.broadcasted_iota(I32, imp_t.shape, 1)) // SEL_BLOCK
    valid_b = j <= cur
    forced = (j == 0) | (j == cur) | (j == cur - 1)
    score = jnp.where(valid_b, jnp.where(forced, FORCE_SCORE, imp_t), NEG_INF)
    rank = jnp.zeros(score.shape, I32)
    for i in range(SEL_LANES):
        si = score[i:i + 1, :]
        rank = rank + jnp.where(j > i, jnp.where(si >= score, 1, 0), jnp.where(si > score, 1, 0))
    selected = (rank < N_SELECT) & valid_b
    bias = jnp.where(selected, 0.0, SEL_MASK_BIAS)
    sel_ref[0] = jnp.concatenate([bias, bias], axis=0).T.astype(BF16)


def _cmp_attn(q, kcmp, gates, ovt):
    B, T, _ = q.shape
    tq = min(CMP_TQ, T)
    nc = kcmp.shape[2]
    G = N_KV_GROUPS
    tokg = lambda w: pl.BlockSpec((1, tq, w), lambda b, g, i: (b, i, g))
    return pl.pallas_call(
        _cmp_attn_kernel,
        grid=(B, G, T // tq),
        in_specs=[tokg(GROUP_W),
                  pl.BlockSpec((1, 1, nc, HEAD_DIM), lambda b, g, i: (0, b * G + g, 0, 0)),
                  pl.BlockSpec((1, 1, nc, HEAD_DIM), lambda b, g, i: (1, b * G + g, 0, 0)),
                  tokg(LANES),
                  pl.BlockSpec(ovt.shape, lambda b, g, i: (0, 0))],
        out_specs=[tokg(GROUP_W), tokg(LANES)],
        out_shape=[jax.ShapeDtypeStruct((B, T, NSA_WIDTH), BF16),
                   jax.ShapeDtypeStruct((B, T, G * LANES), BF16)],
        name="cmp_attn",
    )(q, kcmp, kcmp, gates, ovt)


def _sel_attn_kernel(q_ref, sel_ref, k_ref, v_ref, g_ref, o_ref, *, kc):
    qi = pl.program_id(2)
    tq = q_ref.shape[1]
    q0 = qi * tq
    qb = q_ref[0]
    sb = sel_ref[0][:, :SEL_LANES]
    qa = jnp.concatenate(
        [jnp.concatenate([qb[:, r * HEAD_DIM:(r + 1) * HEAD_DIM], sb], axis=1) for r in range(HEADS_PER_GROUP)],
        axis=0)
    rows = qa.shape[0]

    def step(kstart, carry, causal):
        m, l, acc = carry
        kblk = k_ref[0, 0, pl.ds(kstart, kc), :]
        vblk = v_ref[0, 0, pl.ds(kstart, kc), :]
        s = _dot_nt(qa, kblk)
        if causal:
            t = q0 + (lax.broadcasted_iota(I32, s.shape, 0) & (tq - 1))
            kp = kstart + lax.broadcasted_iota(I32, s.shape, 1)
            s = jnp.where(kp <= t, s, NEG_INF)
        mn = jnp.maximum(m, jnp.max(s, axis=-1, keepdims=True))
        alpha = jnp.exp(m - mn)
        p = jnp.exp(s - mn)
        l = alpha * l + jnp.sum(p, axis=-1, keepdims=True)
        acc = alpha * acc + _dot(p.astype(BF16), vblk)
        return mn, l, acc

    init = (jnp.full((rows, 1), NEG_INF, F32), jnp.zeros((rows, 1), F32), jnp.zeros((rows, HEAD_DIM), F32))
    n_full = q0 // kc
    carry = lax.fori_loop(0, n_full, lambda i, cr: step(pl.multiple_of(i * kc, kc), cr, False), init)
    _, l, acc = step(pl.multiple_of(n_full * kc, kc), carry, True)
    o_ref[0] = _gated_out(acc / l, g_ref[0], 1, tq)


def _sel_attn(q, sel, kaug, v, gates):
    B, T, _ = q.shape
    tq = min(SEL_TQ, T)
    kc = min(SEL_KC, T)
    G = N_KV_GROUPS
    tokg = lambda w: pl.BlockSpec((1, tq, w), lambda b, g, i: (b, i, g))
    kvspec = lambda a: pl.BlockSpec((1, 1) + a.shape[2:], lambda b, g, i: (b, g, 0, 0))
    return pl.pallas_call(
        functools.partial(_sel_attn_kernel, kc=kc),
        grid=(B, G, T // tq),
        in_specs=[tokg(GROUP_W), tokg(LANES), kvspec(kaug), kvspec(v), tokg(LANES)],
        out_specs=tokg(GROUP_W),
        out_shape=jax.ShapeDtypeStruct((B, T, NSA_WIDTH), BF16),
        name="sel_attn",
    )(q, sel, kaug, v, gates)


def _win_attn_kernel(q_ref, k_ref, v_ref, g_ref, o_ref, *, span):
    qi = pl.program_id(2)
    tq = q_ref.shape[1]
    T = k_ref.shape[2]
    q0 = qi * tq
    start = pl.multiple_of(jnp.clip(q0 + tq - span, 0, T - span), tq)
    qrows = _head_rows(q_ref[0])
    k = k_ref[0, 0, pl.ds(start, span), :]
    v = v_ref[0, 0, pl.ds(start, span), :]
    s = _dot_nt(qrows, k)
    t = q0 + (lax.broadcasted_iota(I32, s.shape, 0) & (tq - 1))
    diff = t - (start + lax.broadcasted_iota(I32, s.shape, 1))
    s = jnp.where((diff >= 0) & (diff < WINDOW), s, NEG_INF)
    m = jnp.max(s, axis=-1, keepdims=True)
    p = jnp.exp(s - m)
    l = jnp.sum(p, axis=-1, keepdims=True)
    o = _dot(p.astype(BF16), v) / l
    o_ref[0] = _gated_out(o, g_ref[0], 2, tq)


def _win_attn(q, k, v, gates):
    B, T, _ = q.shape
    tq = min(WIN_TQ, T)
    span = min(WINDOW + tq, T)
    G = N_KV_GROUPS
    tokg = lambda w: pl.BlockSpec((1, tq, w), lambda b, g, i: (b, i, g))
    kvspec = lambda a: pl.BlockSpec((1, 1) + a.shape[2:], lambda b, g, i: (b, g, 0, 0))
    return pl.pallas_call(
        functools.partial(_win_attn_kernel, span=span),
        grid=(B, G, T // tq),
        in_specs=[tokg(GROUP_W), kvspec(k), kvspec(v), tokg(LANES)],
        out_specs=tokg(GROUP_W),
        out_shape=jax.ShapeDtypeStruct((B, T, NSA_WIDTH), BF16),
        name="win_attn",
    )(q, k, v, gates)


def _layer_norm(h, g, b):
    mu = jnp.mean(h, axis=-1, keepdims=True)
    c = h - mu
    var = jnp.mean(c * c, axis=-1, keepdims=True)
    return c * lax.rsqrt(var + LN_EPS) * g + b


def _merge_kernel(x_ref, ya_ref, oc_ref, os_ref, ow_ref, wmg_ref, wuc_ref, wun_ref, wo_ref, g1_ref, b1_ref,
                  wr_ref, br_ref, tri_ref, x1_ref, ri_ref, rg_ref, cnt_ref, carry_ref, *, alpha):
    i = pl.program_id(0)
    x = x_ref[...]
    xb = x.astype(BF16)
    mg = _dot(xb, wmg_ref[...])
    y_a = _dot(ya_ref[...], wuc_ref[...])
    o_nsa = (oc_ref[...].astype(F32) + os_ref[...].astype(F32) + ow_ref[...].astype(F32)).astype(BF16)
    y_b = _dot(o_nsa, wun_ref[...])
    merged = jax.nn.sigmoid(mg[:, :D_MODEL]) * y_a + jax.nn.sigmoid(mg[:, D_MODEL:]) * y_b
    h = alpha * x + _dot(merged.astype(BF16), wo_ref[...])
    x1 = _layer_norm(h, g1_ref[...], b1_ref[...])
    x1_ref[...] = x1

    logits = jnp.dot(x1, wr_ref[...], precision=lax.Precision.HIGHEST, preferred_element_type=F32) + br_ref[...]
    lane = lax.broadcasted_iota(I32, logits.shape, 1).astype(F32)
    rem = logits
    vals, idxs, hots = [], [], []
    for _ in range(TOP_K):
        m = jnp.max(rem, axis=-1, keepdims=True)
        idx = jnp.min(jnp.where(rem == m, lane, float(LANES)), axis=-1, keepdims=True)
        hot = lane == idx
        vals.append(m)
        idxs.append(idx)
        hots.append(hot)
        rem = jnp.where(hot, -3.0e38, rem)
    es = [jnp.exp(v - vals[0]) for v in vals]
    den = es[0]
    for e in es[1:]:
        den = den + e
    chosen = hots[0]
    for hot in hots[1:]:
        chosen = chosen | hot
    chosen_f = jnp.where(chosen, 1.0, 0.0)

    @pl.when(i == 0)
    def _():
        carry_ref[...] = jnp.zeros_like(carry_ref)

    before = _dot(tri_ref[...], chosen_f.astype(BF16)) + carry_ref[...]
    total = carry_ref[...] + jnp.sum(chosen_f, axis=0, keepdims=True)
    carry_ref[...] = total
    cnt_ref[...] = total.astype(I32)

    ri = jnp.zeros(logits.shape, F32)
    rg = jnp.zeros(logits.shape, F32)
    for k in range(TOP_K):
        rank_k = jnp.sum(jnp.where(hots[k], before, 0.0), axis=-1, keepdims=True)
        ri = jnp.where(lane == float(k), idxs[k], ri)
        ri = jnp.where(lane == float(TOP_K + k), rank_k, ri)
        rg = jnp.where(lane == float(k), es[k] / den, rg)
    ri_ref[...] = ri.astype(I32)
    rg_ref[...] = rg


def _merge(x, ya, oc, os_, ow, wmg, wuc, wun, wo, g1, b1, wr, br, tri, alpha):
    N, D = x.shape
    tm = tri.shape[0]
    tok = lambda w: pl.BlockSpec((tm, w), lambda i: (i, 0))
    full = lambda a: pl.BlockSpec(a.shape, lambda i: (0,) * a.ndim)
    return pl.pallas_call(
        functools.partial(_merge_kernel, alpha=alpha),
        grid=(N // tm,),
        in_specs=[tok(D), tok(CONV_CH), tok(NSA_WIDTH), tok(NSA_WIDTH), tok(NSA_WIDTH),
                  full(wmg), full(wuc), full(wun), full(wo), full(g1), full(b1), full(wr), full(br), full(tri)],
        out_specs=[tok(D), tok(LANES), tok(LANES), pl.BlockSpec((1, LANES), lambda i: (0, 0))],
        out_shape=[jax.ShapeDtypeStruct((N, D), F32),
                   jax.ShapeDtypeStruct((N, LANES), I32),
                   jax.ShapeDtypeStruct((N, LANES), F32),
                   jax.ShapeDtypeStruct((1, LANES), I32)],
        scratch_shapes=[pltpu.VMEM((1, LANES), F32)],
        compiler_params=pltpu.CompilerParams(dimension_semantics=("arbitrary",)),
        name="merge",
    )(x, ya, oc, os_, ow, wmg, wuc, wun, wo, g1, b1, wr, br, tri)


def _row_copy(src, si, dst, di, sem):
    return pltpu.make_async_copy(src.at[pl.ds(si, 1), :], dst.at[pl.ds(di, 1), :], sem)


def _dispatch_kernel(dest_ref, cnt_ref, pstart_ref, pend_ref, x_ref, xs_hbm, zero_ref, sem, zsem):
    i = pl.program_id(0)
    n = dest_ref.shape[0]

    @pl.when(i == 0)
    def _():
        zero_ref[...] = jnp.zeros_like(zero_ref)
        for e in range(N_EXPERTS):
            lo = pstart_ref[e] + cnt_ref[e]
            hi = pend_ref[e]

            def zstart(r, c):
                _row_copy(zero_ref, 0, xs_hbm, r, zsem).start()
                return c

            def zwait(r, c):
                _row_copy(zero_ref, 0, xs_hbm, r, zsem).wait()
                return c

            lax.fori_loop(lo, hi, zstart, 0)
            lax.fori_loop(lo, hi, zwait, 0)

    def start(t, c):
        for k in range(TOP_K):
            _row_copy(x_ref, t, xs_hbm, dest_ref[t * TOP_K + k], sem).start()
        return c

    def wait(t, c):
        for k in range(TOP_K):
            _row_copy(x_ref, 0, xs_hbm, 0, sem).wait()
        return c

    lax.fori_loop(0, n // TOP_K, start, 0)
    lax.fori_loop(0, n // TOP_K, wait, 0)


def _dispatch(dest, cnt, pstart, pend, x1, n_rows):
    N, D = x1.shape
    tm = min(DISP_TM, N)
    smem = pl.BlockSpec(memory_space=pltpu.SMEM)
    return pl.pallas_call(
        _dispatch_kernel,
        grid=(N // tm,),
        in_specs=[pl.BlockSpec((tm * TOP_K,), lambda i: (i,), memory_space=pltpu.SMEM), smem, smem, smem,
                  pl.BlockSpec((tm, D), lambda i: (i, 0))],
        out_specs=pl.BlockSpec(memory_space=pl.ANY),
        out_shape=jax.ShapeDtypeStruct((n_rows, D), F32),
        scratch_shapes=[pltpu.VMEM((8, D), F32), pltpu.SemaphoreType.DMA(()), pltpu.SemaphoreType.DMA(())],
        compiler_params=pltpu.CompilerParams(dimension_semantics=("arbitrary",)),
        name="dispatch",
    )(dest, cnt, pstart, pend, x1)


def _experts_kernel(be_ref, nu_ref, xs_ref, wgu_ref, bg_ref, bl_ref, wd_ref, bd_ref, ys_ref, wg_s, wl_s, wd_s):
    i = pl.program_id(0)

    @pl.when((i == 0) | (be_ref[i] != be_ref[jnp.maximum(i, 1) - 1]))
    def _():
        u = wgu_ref[0]
        wg_s[...] = lax.bitcast_convert_type(u << 16, F32).astype(BF16)
        wl_s[...] = lax.bitcast_convert_type(u & jnp.uint32(0xFFFF0000), F32).astype(BF16)
        wd_s[...] = wd_ref[0].astype(BF16)

    @pl.when(i < nu_ref[0])
    def _():
        xb = xs_ref[...].astype(BF16)
        x_glu = jnp.minimum(_dot(xb, wg_s[...]) + bg_ref[0], SWIGLU_LIMIT)
        x_lin = jnp.clip(_dot(xb, wl_s[...]) + bl_ref[0], -SWIGLU_LIMIT, SWIGLU_LIMIT)
        act = x_glu * jax.nn.sigmoid(SWIGLU_ALPHA * x_glu) * (x_lin + 1.0)
        ys_ref[...] = _dot(act.astype(BF16), wd_s[...]) + bd_ref[0]


def _experts(block_expert, n_used, xs, wgu, bg, bl, wd, bd):
    P, D = xs.shape
    n_blocks = P // MOE_BLK
    rows = lambda i, be, nu: (jnp.minimum(i, nu[0] - 1), 0)
    wsel = lambda i, be, nu: (be[i], 0, 0)
    return pl.pallas_call(
        _experts_kernel,
        grid_spec=pltpu.PrefetchScalarGridSpec(
            num_scalar_prefetch=2,
            grid=(n_blocks,),
            in_specs=[pl.BlockSpec((MOE_BLK, D), rows),
                      pl.BlockSpec((1, D, D_FF), wsel),
                      pl.BlockSpec((1, 1, D_FF), wsel), pl.BlockSpec((1, 1, D_FF), wsel),
                      pl.BlockSpec((1, D_FF, D), wsel), pl.BlockSpec((1, 1, D), wsel)],
            out_specs=pl.BlockSpec((MOE_BLK, D), rows),
            scratch_shapes=[pltpu.VMEM((D, D_FF), BF16), pltpu.VMEM((D, D_FF), BF16), pltpu.VMEM((D_FF, D), BF16)]),
        out_shape=jax.ShapeDtypeStruct((P, D), F32),
        compiler_params=pltpu.CompilerParams(dimension_semantics=("arbitrary",)),
        name="experts",
    )(block_expert, n_used, xs, wgu, bg, bl, wd, bd)


def _combine_kernel(dest_ref, ys_hbm, x1_ref, rg_ref, g2_ref, b2_ref, o_ref, buf, sem, *, alpha):
    tm = o_ref.shape[0]

    def start(t, c):
        for k in range(TOP_K):
            pltpu.make_async_copy(ys_hbm.at[pl.ds(dest_ref[t * TOP_K + k], 1), :], buf.at[k, pl.ds(t, 1), :], sem).start()
        return c

    def wait(t, c):
        for k in range(TOP_K):
            pltpu.make_async_copy(ys_hbm.at[pl.ds(0, 1), :], buf.at[k, pl.ds(t, 1), :], sem).wait()
        return c

    lax.fori_loop(0, tm, start, 0)
    lax.fori_loop(0, tm, wait, 0)
    gate = rg_ref[...]
    y = gate[:, 0:1] * buf[0]
    for k in range(1, TOP_K):
        y = y + gate[:, k:k + 1] * buf[k]
    o_ref[...] = _layer_norm(alpha * x1_ref[...] + y, g2_ref[...], b2_ref[...])


def _combine(dest, ys, x1, rg, g2, b2, alpha):
    N, D = x1.shape
    tm = min(COMB_TM, N)
    tok = lambda w: pl.BlockSpec((tm, w), lambda i: (i, 0))
    full = lambda a: pl.BlockSpec(a.shape, lambda i: (0,) * a.ndim)
    return pl.pallas_call(
        functools.partial(_combine_kernel, alpha=alpha),
        grid=(N // tm,),
        in_specs=[pl.BlockSpec((tm * TOP_K,), lambda i: (i,), memory_space=pltpu.SMEM),
                  pl.BlockSpec(memory_space=pl.ANY), tok(D), tok(LANES), full(g2), full(b2)],
        out_specs=tok(D),
        out_shape=jax.ShapeDtypeStruct((N, D), F32),
        scratch_shapes=[pltpu.VMEM((TOP_K, tm, D), F32), pltpu.SemaphoreType.DMA(())],
        compiler_params=pltpu.CompilerParams(dimension_semantics=("arbitrary",)),
        name="combine",
    )(dest, ys, x1, rg, g2, b2)


def _rope_freq_lanes():
    half = ROPE_DIM // 2
    inv = (np.float32(ROPE_THETA) ** (-np.arange(half, dtype=np.float32) * np.float32(2.0 / ROPE_DIM))).astype(np.float32)
    return np.tile(inv, LANES // half)[None, :]


def _overlap_t(T):
    nc = T // CMP_STRIDE
    c0 = np.arange(nc) * CMP_STRIDE
    j0 = np.arange(SEL_LANES) * SEL_BLOCK
    ov = (c0[None, :] < j0[:, None] + SEL_BLOCK) & (c0[None, :] + CMP_BLOCK > j0[:, None])
    ov &= (np.arange(nc) < nc - 1)[None, :] & (np.arange(SEL_LANES) < T // SEL_BLOCK)[:, None]
    return ov.astype(np.float32)


def _group_heads(t, B, T):
    return t.reshape(B, T, N_KV_GROUPS, HEAD_DIM).transpose(0, 2, 1, 3)


def _layer(x, positions, w_in, conv_w, cmp_pos_k, cmp_w1_k, cmp_w2_k, cmp_pos_v, cmp_w1_v, cmp_w2_v,
           w_up_conv, w_up_nsa, w_o, ln1_g, ln1_b, w_router, b_router, w_gate_up, b_gate_up,
           w_down, b_down, ln2_g, ln2_b, alpha):
    B, T, D = x.shape
    G, R = N_KV_GROUPS, HEADS_PER_GROUP
    N = B * T
    assert D == D_MODEL and T % SEL_BLOCK == 0 and T // SEL_BLOCK <= SEL_LANES

    c0 = 3 * CONV_CH
    c1 = c0 + NSA_WIDTH
    c2 = c1 + 6 * KV_WIDTH
    c3 = c2 + 3 * N_HEADS
    wc = w_in[:, :c0].astype(BF16)
    wq = w_in[:, c0:c1].astype(BF16)
    wkv = w_in[:, c1:c2].astype(BF16)
    w_nsa_gate = w_in[:, c2:c3]
    gcols = np.zeros((G, LANES), np.int64)
    gmask = np.zeros((G, LANES), np.float32)
    for g in range(G):
        for br in range(3):
            for r in range(R):
                gcols[g, br * R + r] = br * N_HEADS + g * R + r
                gmask[g, br * R + r] = 1.0
    wg = (w_nsa_gate[:, gcols.reshape(-1)] * gmask.reshape(-1)).astype(BF16)
    wmg = w_in[:, c3:].astype(BF16)
    cw = conv_w.reshape(CONV_K, CONV_CH)
    freq = jnp.asarray(_rope_freq_lanes())
    pos_f = positions.astype(F32)[..., None]

    ya_pre, q, kv, gates = _proj(x, pos_f, wc, wq, wkv, wg, cw, freq)

    nc = T // CMP_STRIDE
    to_chunks = lambda t: _group_heads(t, B, T).reshape(B * G, nc, CMP_STRIDE * HEAD_DIM)
    xin = jnp.stack([to_chunks(kv[..., 0:KV_WIDTH]), to_chunks(kv[..., KV_WIDTH:2 * KV_WIDTH])])
    w1 = jnp.stack([cmp_w1_k, cmp_w1_v])
    w2 = jnp.stack([cmp_w2_k, cmp_w2_v])
    pos_flat = jnp.stack([cmp_pos_k.reshape(1, -1), cmp_pos_v.reshape(1, -1)])
    pos_flat = jnp.pad(pos_flat, ((0, 0), (0, 7), (0, 0)))
    kcmp = _compress(xin, w1, w2, pos_flat)

    o_c, sel = _cmp_attn(q, kcmp, gates, jnp.asarray(_overlap_t(T)))

    ks = _group_heads(kv[..., 2 * KV_WIDTH:3 * KV_WIDTH], B, T)
    vs = _group_heads(kv[..., 3 * KV_WIDTH:4 * KV_WIDTH], B, T)
    kw = _group_heads(kv[..., 4 * KV_WIDTH:5 * KV_WIDTH], B, T)
    vw = _group_heads(kv[..., 5 * KV_WIDTH:6 * KV_WIDTH], B, T)
    onehot = (np.arange(T)[:, None] // SEL_BLOCK == np.arange(SEL_LANES)[None, :]).astype(np.float32)
    kaug = jnp.concatenate([ks, jnp.broadcast_to(jnp.asarray(onehot, BF16), (B, G, T, SEL_LANES))], axis=-1)
    o_s = _sel_attn(q, sel, kaug, vs, gates)
    o_w = _win_attn(q, kw, vw, gates)

    tm = min(MERGE_TM, N)
    tri = jnp.asarray(np.tril(np.ones((tm, tm), np.float32), -1), BF16)
    wr = jnp.pad(w_router, ((0, 0), (0, LANES - N_EXPERTS)))
    br = jnp.pad(b_router, (0, LANES - N_EXPERTS), constant_values=NEG_INF)[None, :]
    x1, ri, rg, cnt = _merge(x.reshape(N, D), ya_pre.reshape(N, CONV_CH), o_c.reshape(N, NSA_WIDTH),
                             o_s.reshape(N, NSA_WIDTH), o_w.reshape(N, NSA_WIDTH), wmg,
                             w_up_conv.astype(BF16), w_up_nsa.astype(BF16), w_o.astype(BF16),
                             ln1_g[None, :], ln1_b[None, :], wr, br, tri, alpha)

    counts = cnt[0, :N_EXPERTS]
    padded = (counts + MOE_BLK - 1) // MOE_BLK * MOE_BLK
    pend = jnp.cumsum(padded).astype(I32)
    pstart = pend - padded
    idx4 = ri[:, :TOP_K]
    rank4 = ri[:, TOP_K:2 * TOP_K]
    dest = (pstart[idx4] + rank4).reshape(N * TOP_K)
    n_blocks = (N * TOP_K) // MOE_BLK + N_EXPERTS
    blk_start = jnp.arange(n_blocks, dtype=I32) * MOE_BLK
    block_expert = jnp.minimum(jnp.sum((blk_start[:, None] >= pend[None, :]).astype(I32), axis=1), N_EXPERTS - 1)
    n_used = (pend[-1:] // MOE_BLK).astype(I32)

    xs = _dispatch(dest, counts, pstart, pend, x1, n_blocks * MOE_BLK)
    wgu = lax.bitcast_convert_type(w_gate_up.astype(BF16).reshape(N_EXPERTS, D, D_FF, 2), jnp.uint32)
    ys = _experts(block_expert, n_used, xs, wgu,
                  b_gate_up[:, None, 0::2], b_gate_up[:, None, 1::2], w_down, b_down[:, None, :])
    out = _combine(dest, ys, x1, rg, ln2_g[None, :], ln2_b[None, :], alpha)
    return out.reshape(B, T, D)


def kernel(x, positions, w_in, conv_w, cmp_pos_k, cmp_w1_k, cmp_w2_k, cmp_pos_v, cmp_w1_v, cmp_w2_v, w_up_conv, w_up_nsa, w_o, ln1_g, ln1_b, w_router, b_router, w_gate_up, b_gate_up, w_down, b_down, ln2_g, ln2_b):
    depth = w_in.shape[0]
    alpha = float((2 * depth) ** 0.25)
    h = x
    for l in range(depth):
        h = _layer(h, positions, w_in[l], conv_w[l], cmp_pos_k[l], cmp_w1_k[l], cmp_w2_k[l],
                   cmp_pos_v[l], cmp_w1_v[l], cmp_w2_v[l], w_up_conv[l], w_up_nsa[l], w_o[l],
                   ln1_g[l], ln1_b[l], w_router[l], b_router[l], w_gate_up[l], b_gate_up[l],
                   w_down[l], b_down[l], ln2_g[l], ln2_b[l], alpha)
    return h
```

```python
import functools

import numpy as np
import jax
import jax.numpy as jnp
from jax import lax
from jax.experimental import pallas as pl
from jax.experimental.pallas import tpu as pltpu

F32 = jnp.float32
BF16 = jnp.bfloat16
I32 = jnp.int32

D_MODEL = 1024
CONV_CH = 512
CONV_K = 3
N_HEADS = 8
N_KV_GROUPS = 2
HEADS_PER_GROUP = N_HEADS // N_KV_GROUPS
HEAD_DIM = 64
NSA_WIDTH = N_HEADS * HEAD_DIM
KV_WIDTH = N_KV_GROUPS * HEAD_DIM
ROPE_DIM = HEAD_DIM // 4
ROPE_THETA = 500000.0
CMP_BLOCK = 32
CMP_STRIDE = 16
CMP_HIDDEN = 256
SEL_BLOCK = 64
N_SELECT = 16
WINDOW = 512
N_EXPERTS = 32
TOP_K = 4
D_FF = 1024
SWIGLU_LIMIT = 7.0
SWIGLU_ALPHA = 1.702
LN_EPS = 1e-5
NEG_INF = -1e30
FORCE_SCORE = 1e4

LANES = 128
SEL_LANES = 64
SEL_MASK_BIAS = -32768.0
GROUP_W = HEADS_PER_GROUP * HEAD_DIM

PROJ_TM = 512
CMP_TQ = 256
SEL_TQ = 128
SEL_KC = 512
WIN_TQ = 128
MERGE_TM = 256
MOE_BLK = 512
DISP_TM = 512
COMB_TM = 256


def _dot(a, b):
    return jnp.dot(a, b, preferred_element_type=F32)


def _dot_nt(a, b, precision=None):
    return lax.dot_general(a, b, (((1,), (1,)), ((), ())), precision=precision, preferred_element_type=F32)


def _proj_kernel(x_ref, pos_ref, wc_ref, wq_ref, wkv_ref, wg_ref, cw_ref, freq_ref,
                 ya_ref, q_ref, kv_ref, g_ref, carry_ref):
    ti = pl.program_id(1)
    tm = x_ref.shape[1]
    xb = x_ref[0].astype(BF16)

    pc = _dot(xb, wc_ref[...])
    xv = pc[:, :CONV_CH]
    bg = pc[:, CONV_CH:2 * CONV_CH]
    cg = pc[:, 2 * CONV_CH:]
    u = cg * xv

    @pl.when(ti == 0)
    def _():
        carry_ref[...] = jnp.zeros_like(carry_ref)

    prev = carry_ref[...]
    row = lax.broadcasted_iota(I32, u.shape, 0)
    u1 = jnp.where(row == 0, prev[7:8], pltpu.roll(u, 1, 0))
    u2 = jnp.where(row == 0, prev[6:7], jnp.where(row == 1, prev[7:8], pltpu.roll(u, 2, 0)))
    cw = cw_ref[...]
    conv = cw[2:3] * u + cw[1:2] * u1 + cw[0:1] * u2
    carry_ref[...] = u[tm - 8:]
    ya_ref[0] = (bg * conv).astype(BF16)

    ang = pos_ref[0] * freq_ref[...]
    cos = jnp.cos(ang)
    sin = jnp.sin(ang)
    d = lax.broadcasted_iota(I32, (1, LANES), 1) % HEAD_DIM
    half = ROPE_DIM // 2
    cm = jnp.where(d < ROPE_DIM, cos, 1.0)
    s_lo = jnp.where(d < half, -sin, 0.0)
    s_hi = jnp.where((d >= half) & (d < ROPE_DIM), sin, 0.0)

    def rope(t):
        return t * cm + pltpu.roll(t, LANES - half, 1) * s_lo + pltpu.roll(t, half, 1) * s_hi

    pq = _dot(xb, wq_ref[...])
    scale = HEAD_DIM ** -0.5
    q_ref[0] = jnp.concatenate(
        [rope(pq[:, j * LANES:(j + 1) * LANES]) * scale for j in range(NSA_WIDTH // LANES)], axis=1).astype(BF16)

    pkv = _dot(xb, wkv_ref[...])
    chunks = []
    for j in range(6):
        t = pkv[:, j * KV_WIDTH:(j + 1) * KV_WIDTH]
        chunks.append(rope(t) if j % 2 == 0 else t)
    kv_ref[0] = jnp.concatenate(chunks, axis=1).astype(BF16)

    g_ref[0] = jax.nn.sigmoid(_dot(xb, wg_ref[...]))


def _proj(x, pos_f, wc, wq, wkv, wg, cw, freq):
    B, T, D = x.shape
    tm = min(PROJ_TM, T)
    grid = (B, T // tm)
    full = lambda a: pl.BlockSpec(a.shape, lambda b, t: (0,) * a.ndim)
    tok = lambda w: pl.BlockSpec((1, tm, w), lambda b, t: (b, t, 0))
    return pl.pallas_call(
        _proj_kernel,
        grid=grid,
        in_specs=[tok(D), tok(1), full(wc), full(wq), full(wkv), full(wg), full(cw), full(freq)],
        out_specs=[tok(CONV_CH), tok(NSA_WIDTH), tok(6 * KV_WIDTH), tok(N_KV_GROUPS * LANES)],
        out_shape=[jax.ShapeDtypeStruct((B, T, CONV_CH), BF16),
                   jax.ShapeDtypeStruct((B, T, NSA_WIDTH), BF16),
                   jax.ShapeDtypeStruct((B, T, 6 * KV_WIDTH), BF16),
                   jax.ShapeDtypeStruct((B, T, N_KV_GROUPS * LANES), F32)],
        scratch_shapes=[pltpu.VMEM((8, CONV_CH), F32)],
        compiler_params=pltpu.CompilerParams(dimension_semantics=("arbitrary", "arbitrary")),
        name="proj",
    )(x, pos_f, wc, wq, wkv, wg, cw, freq)


def _compress_kernel(x_ref, w1_ref, w2_ref, pos_ref, o_ref):
    xb = x_ref[0, 0]
    nc = xb.shape[0]
    w1 = w1_ref[0]
    w1b = w1.astype(BF16)
    half = CMP_STRIDE * HEAD_DIM
    a = _dot(xb, w1b[:half])
    b = _dot(xb, w1b[half:])
    b_next = pltpu.roll(b, nc - 1, 0)
    pb = jnp.dot(pos_ref[0], w1, precision=lax.Precision.HIGHEST, preferred_element_type=F32)[0:1]
    h = a + b_next + pb
    g = 0.5 * h * (1.0 + jnp.tanh(np.sqrt(2.0 / np.pi) * (h + 0.044715 * (h * h * h))))
    o_ref[0, 0] = _dot(g.astype(BF16), w2_ref[0].astype(BF16)).astype(BF16)


def _compress(xin, w1, w2, pos):
    _, BG, nc, W = xin.shape
    return pl.pallas_call(
        _compress_kernel,
        grid=(2, BG),
        in_specs=[pl.BlockSpec((1, 1, nc, W), lambda s, i: (s, i, 0, 0)),
                  pl.BlockSpec((1,) + w1.shape[1:], lambda s, i: (s, 0, 0)),
                  pl.BlockSpec((1,) + w2.shape[1:], lambda s, i: (s, 0, 0)),
                  pl.BlockSpec((1,) + pos.shape[1:], lambda s, i: (s, 0, 0))],
        out_specs=pl.BlockSpec((1, 1, nc, HEAD_DIM), lambda s, i: (s, i, 0, 0)),
        out_shape=jax.ShapeDtypeStruct((2, BG, nc, HEAD_DIM), BF16),
        name="compress",
    )(xin, w1, w2, pos)


def _head_rows(qb):
    return jnp.concatenate([qb[:, r * HEAD_DIM:(r + 1) * HEAD_DIM] for r in range(HEADS_PER_GROUP)], axis=0)


def _gated_out(o, gates, branch, tq):
    cols = []
    for r in range(HEADS_PER_GROUP):
        c = branch * HEADS_PER_GROUP + r
        cols.append(o[r * tq:(r + 1) * tq] * gates[:, c:c + 1])
    return jnp.concatenate(cols, axis=1).astype(BF16)


def _cmp_attn_kernel(q_ref, kc_ref, vc_ref, g_ref, ovt_ref, o_ref, sel_ref):
    qi = pl.program_id(2)
    tq = q_ref.shape[1]
    q0 = qi * tq
    qrows = _head_rows(q_ref[0])
    kc = kc_ref[0, 0]
    vc = vc_ref[0, 0]
    nc = kc.shape[0]
    s = _dot_nt(qrows, kc)
    t = q0 + (lax.broadcasted_iota(I32, s.shape, 0) & (tq - 1))
    c = lax.broadcasted_iota(I32, s.shape, 1)
    valid = c * CMP_STRIDE + (CMP_BLOCK - 1) <= t
    sm = jnp.where(valid, s, NEG_INF)
    m = jnp.max(sm, axis=-1, keepdims=True)
    p = jnp.where(valid, jnp.exp(sm - m), 0.0)
    l = jnp.sum(p, axis=-1, keepdims=True)
    p = p / jnp.where(l > 0.0, l, 1.0)
    o = _dot(p.astype(BF16), vc)
    o_ref[0] = _gated_out(o, g_ref[0], 0, tq)

    psum = p[0:tq]
    for r in range(1, HEADS_PER_GROUP):
        psum = psum + p[r * tq:(r + 1) * tq]
    imp_t = _dot_nt(ovt_ref[...], psum, precision=lax.Precision.HIGHEST)
    j = lax.broadcasted_iota(I32, imp_t.shape, 0)
    cur = (q0 + lax.broadcasted_iota(I32, imp_t.shape, 1)) // SEL_BLOCK
    valid_b = j <= cur
    forced = (j == 0) | (j == cur) | (j == cur - 1)
    score = jnp.where(valid_b, jnp.where(forced, FORCE_SCORE, imp_t), NEG_INF)
    rank = jnp.zeros(score.shape, I32)
    for i in range(SEL_LANES):
        si = score[i:i + 1, :]
        rank = rank + jnp.where(j > i, jnp.where(si >= score, 1, 0), jnp.where(si > score, 1, 0))
    selected = (rank < N_SELECT) & valid_b
    bias = jnp.where(selected, 0.0, SEL_MASK_BIAS)
    sel_ref[0] = jnp.concatenate([bias, bias], axis=0).T.astype(BF16)


def _cmp_attn(q, kcmp, gates, ovt):
    B, T, _ = q.shape
    tq = min(CMP_TQ, T)
    nc = kcmp.shape[2]
    G = N_KV_GROUPS
    tokg = lambda w: pl.BlockSpec((1, tq, w), lambda b, g, i: (b, i, g))
    return pl.pallas_call(
        _cmp_attn_kernel,
        grid=(B, G, T // tq),
        in_specs=[tokg(GROUP_W),
                  pl.BlockSpec((1, 1, nc, HEAD_DIM), lambda b, g, i: (0, b * G + g, 0, 0)),
                  pl.BlockSpec((1, 1, nc, HEAD_DIM), lambda b, g, i: (1, b * G + g, 0, 0)),
                  tokg(LANES),
                  pl.BlockSpec(ovt.shape, lambda b, g, i: (0, 0))],
        out_specs=[tokg(GROUP_W), tokg(LANES)],
        out_shape=[jax.ShapeDtypeStruct((B, T, NSA_WIDTH), BF16),
                   jax.ShapeDtypeStruct((B, T, G * LANES), BF16)],
        name="cmp_attn",
    )(q, kcmp, kcmp, gates, ovt)


def _sel_attn_kernel(q_ref, sel_ref, k_ref, v_ref, g_ref, o_ref, *, kc):
    qi = pl.program_id(2)
    tq = q_ref.shape[1]
    q0 = qi * tq
    qb = q_ref[0]
    sb = sel_ref[0][:, :SEL_LANES]
    qa = jnp.concatenate(
        [jnp.concatenate([qb[:, r * HEAD_DIM:(r + 1) * HEAD_DIM], sb], axis=1) for r in range(HEADS_PER_GROUP)],
        axis=0)
    rows = qa.shape[0]

    def step(kstart, carry, causal):
        m, l, acc = carry
        kblk = k_ref[0, 0, pl.ds(kstart, kc), :]
        vblk = v_ref[0, 0, pl.ds(kstart, kc), :]
        s = _dot_nt(qa, kblk)
        if causal:
            t = q0 + (lax.broadcasted_iota(I32, s.shape, 0) & (tq - 1))
            kp = kstart + lax.broadcasted_iota(I32, s.shape, 1)
            s = jnp.where(kp <= t, s, NEG_INF)
        mn = jnp.maximum(m, jnp.max(s, axis=-1, keepdims=True))
        alpha = jnp.exp(m - mn)
        p = jnp.exp(s - mn)
        l = alpha * l + jnp.sum(p, axis=-1, keepdims=True)
        acc = alpha * acc + _dot(p.astype(BF16), vblk)
        return mn, l, acc

    init = (jnp.full((rows, 1), NEG_INF, F32), jnp.zeros((rows, 1), F32), jnp.zeros((rows, HEAD_DIM), F32))
    n_full = q0 // kc
    carry = lax.fori_loop(0, n_full, lambda i, cr: step(pl.multiple_of(i * kc, kc), cr, False), init)
    _, l, acc = step(pl.multiple_of(n_full * kc, kc), carry, True)
    o_ref[0] = _gated_out(acc / l, g_ref[0], 1, tq)


def _sel_attn(q, sel, kaug, v, gates):
    B, T, _ = q.shape
    tq = min(SEL_TQ, T)
    kc = min(SEL_KC, T)
    G = N_KV_GROUPS
    tokg = lambda w: pl.BlockSpec((1, tq, w), lambda b, g, i: (b, i, g))
    kvspec = lambda a: pl.BlockSpec((1, 1) + a.shape[2:], lambda b, g, i: (b, g, 0, 0))
    return pl.pallas_call(
        functools.partial(_sel_attn_kernel, kc=kc),
        grid=(B, G, T // tq),
        in_specs=[tokg(GROUP_W), tokg(LANES), kvspec(kaug), kvspec(v), tokg(LANES)],
        out_specs=tokg(GROUP_W),
        out_shape=jax.ShapeDtypeStruct((B, T, NSA_WIDTH), BF16),
        name="sel_attn",
    )(q, sel, kaug, v, gates)


def _win_attn_kernel(q_ref, k_ref, v_ref, g_ref, o_ref, *, span):
    qi = pl.program_id(2)
    tq = q_ref.shape[1]
    T = k_ref.shape[2]
    q0 = qi * tq
    start = pl.multiple_of(jnp.clip(q0 + tq - span, 0, T - span), tq)
    qrows = _head_rows(q_ref[0])
    k = k_ref[0, 0, pl.ds(start, span), :]
    v = v_ref[0, 0, pl.ds(start, span), :]
    s = _dot_nt(qrows, k)
    t = q0 + (lax.broadcasted_iota(I32, s.shape, 0) & (tq - 1))
    diff = t - (start + lax.broadcasted_iota(I32, s.shape, 1))
    s = jnp.where((diff >= 0) & (diff < WINDOW), s, NEG_INF)
    m = jnp.max(s, axis=-1, keepdims=True)
    p = jnp.exp(s - m)
    l = jnp.sum(p, axis=-1, keepdims=True)
    o = _dot(p.astype(BF16), v) / l
    o_ref[0] = _gated_out(o, g_ref[0], 2, tq)


def _win_attn(q, k, v, gates):
    B, T, _ = q.shape
    tq = min(WIN_TQ, T)
    span = min(WINDOW + tq, T)
    G = N_KV_GROUPS
    tokg = lambda w: pl.BlockSpec((1, tq, w), lambda b, g, i: (b, i, g))
    kvspec = lambda a: pl.BlockSpec((1, 1) + a.shape[2:], lambda b, g, i: (b, g, 0, 0))
    return pl.pallas_call(
        functools.partial(_win_attn_kernel, span=span),
        grid=(B, G, T // tq),
        in_specs=[tokg(GROUP_W), kvspec(k), kvspec(v), tokg(LANES)],
        out_specs=tokg(GROUP_W),
        out_shape=jax.ShapeDtypeStruct((B, T, NSA_WIDTH), BF16),
        name="win_attn",
    )(q, k, v, gates)


def _layer_norm(h, g, b):
    mu = jnp.mean(h, axis=-1, keepdims=True)
    c = h - mu
    var = jnp.mean(c * c, axis=-1, keepdims=True)
    return c * lax.rsqrt(var + LN_EPS) * g + b


def _merge_kernel(x_ref, ya_ref, oc_ref, os_ref, ow_ref, wmg_ref, wuc_ref, wun_ref, wo_ref, g1_ref, b1_ref,
                  wr_ref, br_ref, tri_ref, x1_ref, ri_ref, rg_ref, cnt_ref, carry_ref, *, alpha):
    i = pl.program_id(0)
    x = x_ref[...]
    xb = x.astype(BF16)
    mg = _dot(xb, wmg_ref[...])
    y_a = _dot(ya_ref[...], wuc_ref[...])
    o_nsa = (oc_ref[...].astype(F32) + os_ref[...].astype(F32) + ow_ref[...].astype(F32)).astype(BF16)
    y_b = _dot(o_nsa, wun_ref[...])
    merged = jax.nn.sigmoid(mg[:, :D_MODEL]) * y_a + jax.nn.sigmoid(mg[:, D_MODEL:]) * y_b
    h = alpha * x + _dot(merged.astype(BF16), wo_ref[...])
    x1 = _layer_norm(h, g1_ref[...], b1_ref[...])
    x1_ref[...] = x1

    logits = lax.dot_general(wr_ref[...], x1, (((1,), (1,)), ((), ())), precision=lax.Precision.HIGHEST,
                             preferred_element_type=F32) + br_ref[...]
    expert = lax.broadcasted_iota(I32, logits.shape, 0).astype(F32)
    rem = logits
    vals, idxs, hots = [], [], []
    for _ in range(TOP_K):
        m = jnp.max(rem, axis=0, keepdims=True)
        idx = jnp.min(jnp.where(rem == m, expert, float(N_EXPERTS)), axis=0, keepdims=True)
        hot = expert == idx
        vals.append(m)
        idxs.append(idx)
        hots.append(hot)
        rem = jnp.where(hot, -3.0e38, rem)
    es = [jnp.exp(v - vals[0]) for v in vals]
    den = es[0]
    for e in es[1:]:
        den = den + e
    chosen = hots[0]
    for hot in hots[1:]:
        chosen = chosen | hot
    chosen_f = jnp.where(chosen, 1.0, 0.0)

    @pl.when(i == 0)
    def _():
        carry_ref[...] = jnp.zeros_like(carry_ref)

    before = _dot(chosen_f.astype(BF16), tri_ref[...]) + carry_ref[...]
    total = carry_ref[...] + jnp.sum(chosen_f, axis=1, keepdims=True)
    carry_ref[...] = total
    cnt_ref[...] = jnp.broadcast_to(total, cnt_ref.shape).astype(I32)

    ranks = [jnp.sum(jnp.where(hot, before, 0.0), axis=0, keepdims=True) for hot in hots]
    ri_ref[...] = jnp.concatenate(idxs + ranks, axis=0).astype(I32)
    rg_ref[...] = jnp.concatenate([e / den for e in es] + [jnp.zeros_like(den)] * TOP_K, axis=0)


def _merge(x, ya, oc, os_, ow, wmg, wuc, wun, wo, g1, b1, wr, br, tri, alpha):
    N, D = x.shape
    tm = tri.shape[0]
    tok = lambda w: pl.BlockSpec((tm, w), lambda i: (i, 0))
    full = lambda a: pl.BlockSpec(a.shape, lambda i: (0,) * a.ndim)
    return pl.pallas_call(
        functools.partial(_merge_kernel, alpha=alpha),
        grid=(N // tm,),
        in_specs=[tok(D), tok(CONV_CH), tok(NSA_WIDTH), tok(NSA_WIDTH), tok(NSA_WIDTH),
                  full(wmg), full(wuc), full(wun), full(wo), full(g1), full(b1), full(wr), full(br), full(tri)],
        out_specs=[tok(D), pl.BlockSpec((2 * TOP_K, tm), lambda i: (0, i)), pl.BlockSpec((2 * TOP_K, tm), lambda i: (0, i)),
                   pl.BlockSpec((N_EXPERTS, LANES), lambda i: (0, 0))],
        out_shape=[jax.ShapeDtypeStruct((N, D), F32),
                   jax.ShapeDtypeStruct((2 * TOP_K, N), I32),
                   jax.ShapeDtypeStruct((2 * TOP_K, N), F32),
                   jax.ShapeDtypeStruct((N_EXPERTS, LANES), I32)],
        scratch_shapes=[pltpu.VMEM((N_EXPERTS, 1), F32)],
        compiler_params=pltpu.CompilerParams(dimension_semantics=("arbitrary",)),
        name="merge",
    )(x, ya, oc, os_, ow, wmg, wuc, wun, wo, g1, b1, wr, br, tri)


def _row_copy(src, si, dst, di, sem):
    return pltpu.make_async_copy(src.at[pl.ds(si, 1), :], dst.at[pl.ds(di, 1), :], sem)


def _dispatch_kernel(dest_ref, cnt_ref, pstart_ref, pend_ref, x_ref, xs_hbm, zero_ref, sem, zsem):
    i = pl.program_id(0)
    n = dest_ref.shape[0]

    @pl.when(i == 0)
    def _():
        zero_ref[...] = jnp.zeros_like(zero_ref)
        for e in range(N_EXPERTS):
            lo = pstart_ref[e] + cnt_ref[e]
            hi = pend_ref[e]

            def zstart(r, c):
                _row_copy(zero_ref, 0, xs_hbm, r, zsem).start()
                return c

            def zwait(r, c):
                _row_copy(zero_ref, 0, xs_hbm, r, zsem).wait()
                return c

            lax.fori_loop(lo, hi, zstart, 0)
            lax.fori_loop(lo, hi, zwait, 0)

    def start(t, c):
        for k in range(TOP_K):
            _row_copy(x_ref, t, xs_hbm, dest_ref[t * TOP_K + k], sem).start()
        return c

    def wait(t, c):
        for k in range(TOP_K):
            _row_copy(x_ref, 0, xs_hbm, 0, sem).wait()
        return c

    lax.fori_loop(0, n // TOP_K, start, 0)
    lax.fori_loop(0, n // TOP_K, wait, 0)


def _dispatch(dest, cnt, pstart, pend, x1, n_rows):
    N, D = x1.shape
    tm = min(DISP_TM, N)
    smem = pl.BlockSpec(memory_space=pltpu.SMEM)
    return pl.pallas_call(
        _dispatch_kernel,
        grid=(N // tm,),
        in_specs=[pl.BlockSpec((tm * TOP_K,), lambda i: (i,), memory_space=pltpu.SMEM), smem, smem, smem,
                  pl.BlockSpec((tm, D), lambda i: (i, 0))],
        out_specs=pl.BlockSpec(memory_space=pl.ANY),
        out_shape=jax.ShapeDtypeStruct((n_rows, D), F32),
        scratch_shapes=[pltpu.VMEM((8, D), F32), pltpu.SemaphoreType.DMA(()), pltpu.SemaphoreType.DMA(())],
        compiler_params=pltpu.CompilerParams(dimension_semantics=("arbitrary",)),
        name="dispatch",
    )(dest, cnt, pstart, pend, x1)


def _experts_kernel(be_ref, nu_ref, xs_ref, wgu_ref, perm_ref, bg_ref, bl_ref, wd_ref, bd_ref, ys_ref, wg_s, wl_s, wd_s):
    i = pl.program_id(0)

    @pl.when((i == 0) | (be_ref[i] != be_ref[jnp.maximum(i, 1) - 1]))
    def _():
        w = perm_ref.shape[0]
        for c in range(2 * D_FF // w):
            t = _dot(wgu_ref[0, :, c * w:(c + 1) * w].astype(BF16), perm_ref[...])
            wg_s[:, c * (w // 2):(c + 1) * (w // 2)] = t[:, :w // 2].astype(BF16)
            wl_s[:, c * (w // 2):(c + 1) * (w // 2)] = t[:, w // 2:].astype(BF16)
        wd_s[...] = wd_ref[0].astype(BF16)

    @pl.when(i < nu_ref[0])
    def _():
        xb = xs_ref[...].astype(BF16)
        x_glu = jnp.minimum(_dot(xb, wg_s[...]) + bg_ref[0], SWIGLU_LIMIT)
        x_lin = jnp.clip(_dot(xb, wl_s[...]) + bl_ref[0], -SWIGLU_LIMIT, SWIGLU_LIMIT)
        act = x_glu * jax.nn.sigmoid(SWIGLU_ALPHA * x_glu) * (x_lin + 1.0)
        ys_ref[...] = _dot(act.astype(BF16), wd_s[...]) + bd_ref[0]


def _experts(block_expert, n_used, xs, wgu, perm, bg, bl, wd, bd):
    P, D = xs.shape
    n_blocks = P // MOE_BLK
    rows = lambda i, be, nu: (jnp.minimum(i, nu[0] - 1), 0)
    wsel = lambda i, be, nu: (be[i], 0, 0)
    return pl.pallas_call(
        _experts_kernel,
        grid_spec=pltpu.PrefetchScalarGridSpec(
            num_scalar_prefetch=2,
            grid=(n_blocks,),
            in_specs=[pl.BlockSpec((MOE_BLK, D), rows),
                      pl.BlockSpec((1, D, 2 * D_FF), wsel),
                      pl.BlockSpec(perm.shape, lambda i, be, nu: (0, 0)),
                      pl.BlockSpec((1, 1, D_FF), wsel), pl.BlockSpec((1, 1, D_FF), wsel),
                      pl.BlockSpec((1, D_FF, D), wsel), pl.BlockSpec((1, 1, D), wsel)],
            out_specs=pl.BlockSpec((MOE_BLK, D), rows),
            scratch_shapes=[pltpu.VMEM((D, D_FF), BF16), pltpu.VMEM((D, D_FF), BF16), pltpu.VMEM((D_FF, D), BF16)]),
        out_shape=jax.ShapeDtypeStruct((P, D), F32),
        compiler_params=pltpu.CompilerParams(dimension_semantics=("arbitrary",)),
        name="experts",
    )(block_expert, n_used, xs, wgu, perm, bg, bl, wd, bd)


def _combine_kernel(dest_ref, ys_hbm, x1_ref, rg_ref, g2_ref, b2_ref, o_ref, buf, sem, *, alpha):
    tm = o_ref.shape[0]

    def start(t, c):
        for k in range(TOP_K):
            pltpu.make_async_copy(ys_hbm.at[pl.ds(dest_ref[t * TOP_K + k], 1), :], buf.at[k, pl.ds(t, 1), :], sem).start()
        return c

    def wait(t, c):
        for k in range(TOP_K):
            pltpu.make_async_copy(ys_hbm.at[pl.ds(0, 1), :], buf.at[k, pl.ds(t, 1), :], sem).wait()
        return c

    lax.fori_loop(0, tm, start, 0)
    lax.fori_loop(0, tm, wait, 0)
    gate = rg_ref[...]
    y = gate[:, 0:1] * buf[0]
    for k in range(1, TOP_K):
        y = y + gate[:, k:k + 1] * buf[k]
    o_ref[...] = _layer_norm(alpha * x1_ref[...] + y, g2_ref[...], b2_ref[...])


def _combine(dest, ys, x1, rg, g2, b2, alpha):
    N, D = x1.shape
    tm = min(COMB_TM, N)
    tok = lambda w: pl.BlockSpec((tm, w), lambda i: (i, 0))
    full = lambda a: pl.BlockSpec(a.shape, lambda i: (0,) * a.ndim)
    return pl.pallas_call(
        functools.partial(_combine_kernel, alpha=alpha),
        grid=(N // tm,),
        in_specs=[pl.BlockSpec((tm * TOP_K,), lambda i: (i,), memory_space=pltpu.SMEM),
                  pl.BlockSpec(memory_space=pl.ANY), tok(D), tok(rg.shape[1]), full(g2), full(b2)],
        out_specs=tok(D),
        out_shape=jax.ShapeDtypeStruct((N, D), F32),
        scratch_shapes=[pltpu.VMEM((TOP_K, tm, D), F32), pltpu.SemaphoreType.DMA(())],
        compiler_params=pltpu.CompilerParams(dimension_semantics=("arbitrary",)),
        name="combine",
    )(dest, ys, x1, rg, g2, b2)


def _rope_freq_lanes():
    half = ROPE_DIM // 2
    inv = (np.float32(ROPE_THETA) ** (-np.arange(half, dtype=np.float32) * np.float32(2.0 / ROPE_DIM))).astype(np.float32)
    return np.tile(inv, LANES // half)[None, :]


def _overlap_t(T):
    nc = T // CMP_STRIDE
    c0 = np.arange(nc) * CMP_STRIDE
    j0 = np.arange(SEL_LANES) * SEL_BLOCK
    ov = (c0[None, :] < j0[:, None] + SEL_BLOCK) & (c0[None, :] + CMP_BLOCK > j0[:, None])
    ov &= (np.arange(nc) < nc - 1)[None, :] & (np.arange(SEL_LANES) < T // SEL_BLOCK)[:, None]
    return ov.astype(np.float32)


def _deinterleave_perm():
    w = 2 * LANES
    p = np.zeros((w, w), np.float32)
    p[np.arange(0, w, 2), np.arange(w // 2)] = 1.0
    p[np.arange(1, w, 2), w // 2 + np.arange(w // 2)] = 1.0
    return p


def _group_heads(t, B, T):
    return t.reshape(B, T, N_KV_GROUPS, HEAD_DIM).transpose(0, 2, 1, 3)


def _layer(x, positions, w_in, conv_w, cmp_pos_k, cmp_w1_k, cmp_w2_k, cmp_pos_v, cmp_w1_v, cmp_w2_v,
           w_up_conv, w_up_nsa, w_o, ln1_g, ln1_b, w_router, b_router, w_gate_up, b_gate_up,
           w_down, b_down, ln2_g, ln2_b, alpha):
    B, T, D = x.shape
    G, R = N_KV_GROUPS, HEADS_PER_GROUP
    N = B * T
    assert D == D_MODEL and T % SEL_BLOCK == 0 and T // SEL_BLOCK <= SEL_LANES

    c0 = 3 * CONV_CH
    c1 = c0 + NSA_WIDTH
    c2 = c1 + 6 * KV_WIDTH
    c3 = c2 + 3 * N_HEADS
    wc = w_in[:, :c0].astype(BF16)
    wq = w_in[:, c0:c1].astype(BF16)
    wkv = w_in[:, c1:c2].astype(BF16)
    w_nsa_gate = w_in[:, c2:c3]
    gcols = np.zeros((G, LANES), np.int64)
    gmask = np.zeros((G, LANES), np.float32)
    for g in range(G):
        for br in range(3):
            for r in range(R):
                gcols[g, br * R + r] = br * N_HEADS + g * R + r
                gmask[g, br * R + r] = 1.0
    wg = (w_nsa_gate[:, gcols.reshape(-1)] * gmask.reshape(-1)).astype(BF16)
    wmg = w_in[:, c3:].astype(BF16)
    cw = conv_w.reshape(CONV_K, CONV_CH)
    freq = jnp.asarray(_rope_freq_lanes())
    pos_f = positions.astype(F32)[..., None]

    ya_pre, q, kv, gates = _proj(x, pos_f, wc, wq, wkv, wg, cw, freq)

    nc = T // CMP_STRIDE
    to_chunks = lambda t: _group_heads(t, B, T).reshape(B * G, nc, CMP_STRIDE * HEAD_DIM)
    xin = jnp.stack([to_chunks(kv[..., 0:KV_WIDTH]), to_chunks(kv[..., KV_WIDTH:2 * KV_WIDTH])])
    w1 = jnp.stack([cmp_w1_k, cmp_w1_v])
    w2 = jnp.stack([cmp_w2_k, cmp_w2_v])
    pos_flat = jnp.stack([cmp_pos_k.reshape(1, -1), cmp_pos_v.reshape(1, -1)])
    pos_flat = jnp.pad(pos_flat, ((0, 0), (0, 7), (0, 0)))
    kcmp = _compress(xin, w1, w2, pos_flat)

    o_c, sel = _cmp_attn(q, kcmp, gates, jnp.asarray(_overlap_t(T)))

    ks = _group_heads(kv[..., 2 * KV_WIDTH:3 * KV_WIDTH], B, T)
    vs = _group_heads(kv[..., 3 * KV_WIDTH:4 * KV_WIDTH], B, T)
    kw = _group_heads(kv[..., 4 * KV_WIDTH:5 * KV_WIDTH], B, T)
    vw = _group_heads(kv[..., 5 * KV_WIDTH:6 * KV_WIDTH], B, T)
    onehot = (np.arange(T)[:, None] // SEL_BLOCK == np.arange(SEL_LANES)[None, :]).astype(np.float32)
    kaug = jnp.concatenate([ks, jnp.broadcast_to(jnp.asarray(onehot, BF16), (B, G, T, SEL_LANES))], axis=-1)
    o_s = _sel_attn(q, sel, kaug, vs, gates)
    o_w = _win_attn(q, kw, vw, gates)

    tm = min(MERGE_TM, N)
    tri = jnp.asarray(np.triu(np.ones((tm, tm), np.float32), 1), BF16)
    wr = w_router.T
    br = b_router[:, None]
    x1, ri, rg, cnt = _merge(x.reshape(N, D), ya_pre.reshape(N, CONV_CH), o_c.reshape(N, NSA_WIDTH),
                             o_s.reshape(N, NSA_WIDTH), o_w.reshape(N, NSA_WIDTH), wmg,
                             w_up_conv.astype(BF16), w_up_nsa.astype(BF16), w_o.astype(BF16),
                             ln1_g[None, :], ln1_b[None, :], wr, br, tri, alpha)

    counts = cnt[:, 0]
    padded = (counts + MOE_BLK - 1) // MOE_BLK * MOE_BLK
    pend = jnp.cumsum(padded).astype(I32)
    pstart = pend - padded
    dest = (pstart[ri[:TOP_K]] + ri[TOP_K:]).T.reshape(N * TOP_K)
    rg = rg.T
    n_blocks = (N * TOP_K) // MOE_BLK + N_EXPERTS
    blk_start = jnp.arange(n_blocks, dtype=I32) * MOE_BLK
    block_expert = jnp.minimum(jnp.sum((blk_start[:, None] >= pend[None, :]).astype(I32), axis=1), N_EXPERTS - 1)
    n_used = (pend[-1:] // MOE_BLK).astype(I32)

    xs = _dispatch(dest, counts, pstart, pend, x1, n_blocks * MOE_BLK)
    ys = _experts(block_expert, n_used, xs, w_gate_up, jnp.asarray(_deinterleave_perm(), BF16),
                  b_gate_up[:, None, 0::2], b_gate_up[:, None, 1::2], w_down, b_down[:, None, :])
    out = _combine(dest, ys, x1, rg, ln2_g[None, :], ln2_b[None, :], alpha)
    return out.reshape(B, T, D)


def kernel(x, positions, w_in, conv_w, cmp_pos_k, cmp_w1_k, cmp_w2_k, cmp_pos_v, cmp_w1_v, cmp_w2_v, w_up_conv, w_up_nsa, w_o, ln1_g, ln1_b, w_router, b_router, w_gate_up, b_gate_up, w_down, b_down, ln2_g, ln2_b):
    depth = w_in.shape[0]
    alpha = float((2 * depth) ** 0.25)
    h = x
    for l in range(depth):
        h = _layer(h, positions, w_in[l], conv_w[l], cmp_pos_k[l], cmp_w1_k[l], cmp_w2_k[l],
                   cmp_pos_v[l], cmp_w1_v[l], cmp_w2_v[l], w_up_conv[l], w_up_nsa[l], w_o[l],
                   ln1_g[l], ln1_b[l], w_router[l], b_router[l], w_gate_up[l], b_gate_up[l],
                   w_down[l], b_down[l], ln2_g[l], ln2_b[l], alpha)
    return h
```

```python
import functools

import numpy as np
import jax
import jax.numpy as jnp
from jax import lax
from jax.experimental import pallas as pl
from jax.experimental.pallas import tpu as pltpu

F32 = jnp.float32
BF16 = jnp.bfloat16
I32 = jnp.int32

D_MODEL = 1024
CONV_CH = 512
CONV_K = 3
N_HEADS = 8
N_KV_GROUPS = 2
HEADS_PER_GROUP = N_HEADS // N_KV_GROUPS
HEAD_DIM = 64
NSA_WIDTH = N_HEADS * HEAD_DIM
KV_WIDTH = N_KV_GROUPS * HEAD_DIM
ROPE_DIM = HEAD_DIM // 4
ROPE_THETA = 500000.0
CMP_BLOCK = 32
CMP_STRIDE = 16
CMP_HIDDEN = 256
SEL_BLOCK = 64
N_SELECT = 16
WINDOW = 512
N_EXPERTS = 32
TOP_K = 4
D_FF = 1024
SWIGLU_LIMIT = 7.0
SWIGLU_ALPHA = 1.702
LN_EPS = 1e-5
NEG_INF = -1e30
FORCE_SCORE = 1e4
LOG2E = 1.4426950408889634

LANES = 128
SEL_LANES = 64
SEL_MASK_BIAS = -32768.0
GROUP_W = HEADS_PER_GROUP * HEAD_DIM
GATE_ROWS = 16

PROJ_TM = 512
CMP_TQ = 256
SEL_TQ = 256
SEL_KC = 512
SEL_SPLIT = 4
WIN_TQ = 128
MERGE_TM = 256
MOE_BLK = 512
DISP_TM = 512
COMB_TM = 256


def _dot(a, b, precision=None):
    return jnp.dot(a, b, precision=precision, preferred_element_type=F32)


def _dot_nt(a, b, precision=None):
    return lax.dot_general(a, b, (((1,), (1,)), ((), ())), precision=precision, preferred_element_type=F32)


def _proj_kernel(x_ref, pos_ref, wc_ref, wqt_ref, wkvt_ref, wgt_ref, cw_ref, freq_ref,
                 ya_ref, qt_ref, kvt_ref, gt_ref, carry_ref):
    ti = pl.program_id(1)
    tm = x_ref.shape[1]
    xb = x_ref[0].astype(BF16)

    pc = _dot(xb, wc_ref[...])
    xv = pc[:, :CONV_CH]
    bg = pc[:, CONV_CH:2 * CONV_CH]
    cg = pc[:, 2 * CONV_CH:]
    u = cg * xv

    @pl.when(ti == 0)
    def _():
        carry_ref[...] = jnp.zeros_like(carry_ref)

    prev = carry_ref[...]
    row = lax.broadcasted_iota(I32, u.shape, 0)
    u1 = jnp.where(row == 0, prev[7:8], pltpu.roll(u, 1, 0))
    u2 = jnp.where(row == 0, prev[6:7], jnp.where(row == 1, prev[7:8], pltpu.roll(u, 2, 0)))
    cw = cw_ref[...]
    conv = cw[2:3] * u + cw[1:2] * u1 + cw[0:1] * u2
    carry_ref[...] = u[tm - 8:]
    ya_ref[0] = (bg * conv).astype(BF16)

    ang = freq_ref[...] * pos_ref[0]
    cos = jnp.cos(ang)
    sin = jnp.sin(ang)
    half = ROPE_DIM // 2

    def rope_head(t):
        t1 = t[:half]
        t2 = t[half:ROPE_DIM]
        return [t1 * cos - t2 * sin, t2 * cos + t1 * sin, t[ROPE_DIM:]]

    def heads(t, rotate):
        out = []
        for h in range(t.shape[0] // HEAD_DIM):
            th = t[h * HEAD_DIM:(h + 1) * HEAD_DIM]
            out.extend(rope_head(th) if rotate(h) else [th])
        return jnp.concatenate(out, axis=0)

    qt = heads(_dot_nt(wqt_ref[...], xb), lambda h: True)
    qt_ref[0] = (qt * (HEAD_DIM ** -0.5 * LOG2E)).astype(BF16)
    kvt = heads(_dot_nt(wkvt_ref[...], xb), lambda h: (h // N_KV_GROUPS) % 2 == 0)
    kvt_ref[0] = kvt.astype(BF16)
    gt_ref[0] = jax.nn.sigmoid(_dot_nt(wgt_ref[...], xb))


def _proj(x, pos_row, wc, wqt, wkvt, wgt, cw, freq):
    B, T, D = x.shape
    tm = min(PROJ_TM, T)
    grid = (B, T // tm)
    full = lambda a: pl.BlockSpec(a.shape, lambda b, t: (0,) * a.ndim)
    tok = lambda w: pl.BlockSpec((1, tm, w), lambda b, t: (b, t, 0))
    feat = lambda r: pl.BlockSpec((1, r, tm), lambda b, t: (b, 0, t))
    n_gate = wgt.shape[0]
    return pl.pallas_call(
        _proj_kernel,
        grid=grid,
        in_specs=[tok(D), feat(1), full(wc), full(wqt), full(wkvt), full(wgt), full(cw), full(freq)],
        out_specs=[tok(CONV_CH), feat(NSA_WIDTH), feat(6 * KV_WIDTH), feat(n_gate)],
        out_shape=[jax.ShapeDtypeStruct((B, T, CONV_CH), BF16),
                   jax.ShapeDtypeStruct((B, NSA_WIDTH, T), BF16),
                   jax.ShapeDtypeStruct((B, 6 * KV_WIDTH, T), BF16),
                   jax.ShapeDtypeStruct((B, n_gate, T), F32)],
        scratch_shapes=[pltpu.VMEM((8, CONV_CH), F32)],
        compiler_params=pltpu.CompilerParams(dimension_semantics=("arbitrary", "arbitrary")),
        name="proj",
    )(x, pos_row, wc, wqt, wkvt, wgt, cw, freq)


def _compress_kernel(x_ref, w1_ref, w2_ref, w2t_ref, pos_ref, o_ref, ot_ref):
    xb = x_ref[0, 0]
    nc = xb.shape[0]
    w1 = w1_ref[0]
    w1b = w1.astype(BF16)
    half = CMP_STRIDE * HEAD_DIM
    a = _dot(xb, w1b[:half])
    b = _dot(xb, w1b[half:])
    b_next = pltpu.roll(b, nc - 1, 0)
    pb = _dot(pos_ref[0], w1, precision=lax.Precision.HIGHEST)[0:1]
    h = a + b_next + pb
    g = (0.5 * h * (1.0 + jnp.tanh(np.sqrt(2.0 / np.pi) * (h + 0.044715 * (h * h * h))))).astype(BF16)
    o_ref[0, 0] = _dot(g, w2_ref[0].astype(BF16)).astype(BF16)
    ot_ref[0, 0] = _dot_nt(w2t_ref[0].astype(BF16), g).astype(BF16)


def _compress(xin, w1, w2, w2t, pos):
    _, BG, nc, W = xin.shape
    per = lambda a: pl.BlockSpec((1,) + a.shape[1:], lambda s, i: (s, 0, 0))
    return pl.pallas_call(
        _compress_kernel,
        grid=(2, BG),
        in_specs=[pl.BlockSpec((1, 1, nc, W), lambda s, i: (s, i, 0, 0)), per(w1), per(w2), per(w2t), per(pos)],
        out_specs=[pl.BlockSpec((1, 1, nc, HEAD_DIM), lambda s, i: (s, i, 0, 0)),
                   pl.BlockSpec((1, 1, HEAD_DIM, nc), lambda s, i: (s, i, 0, 0))],
        out_shape=[jax.ShapeDtypeStruct((2, BG, nc, HEAD_DIM), BF16),
                   jax.ShapeDtypeStruct((2, BG, HEAD_DIM, nc), BF16)],
        name="compress",
    )(xin, w1, w2, w2t, pos)


def _head_lanes(qt):
    return jnp.concatenate([qt[r * HEAD_DIM:(r + 1) * HEAD_DIM] for r in range(HEADS_PER_GROUP)], axis=1)


def _gated_out(ot, scale, gates, branch, tq):
    rows = []
    for r in range(HEADS_PER_GROUP):
        c = branch * HEADS_PER_GROUP + r
        sl = slice(r * tq, (r + 1) * tq)
        rows.append(ot[:, sl] * (scale[:, sl] * gates[c:c + 1]))
    return jnp.concatenate(rows, axis=0).astype(BF16)


def _attn_specs(tq):
    G = N_KV_GROUPS
    qspec = pl.BlockSpec((1, GROUP_W, tq), lambda b, g, i: (b, g, i))
    gspec = pl.BlockSpec((1, GATE_ROWS, tq), lambda b, g, i: (b, g, i))
    bg4 = lambda a: pl.BlockSpec((1, 1) + a.shape[2:], lambda b, g, i: (b, g, 0, 0))
    return G, qspec, gspec, bg4


def _cmp_attn_kernel(q_ref, kc_ref, vct_ref, g_ref, ovt_ref, o_ref, sel_ref):
    qi = pl.program_id(2)
    tq = q_ref.shape[2]
    q0 = qi * tq
    qt = _head_lanes(q_ref[0])
    s = _dot(kc_ref[0, 0], qt)
    c = lax.broadcasted_iota(I32, s.shape, 0)
    t = q0 + (lax.broadcasted_iota(I32, s.shape, 1) & (tq - 1))
    valid = c * CMP_STRIDE + (CMP_BLOCK - 1) <= t
    sm = jnp.where(valid, s, NEG_INF)
    m = jnp.max(sm, axis=0, keepdims=True)
    p = jnp.where(valid, jnp.exp2(sm - m), 0.0)
    l = jnp.sum(p, axis=0, keepdims=True)
    inv = 1.0 / jnp.where(l > 0.0, l, 1.0)
    ot = _dot(vct_ref[0, 0], p.astype(BF16))
    o_ref[0] = _gated_out(ot, inv, g_ref[0], 0, tq)

    pn = p * inv
    psum = pn[:, 0:tq]
    for r in range(1, HEADS_PER_GROUP):
        psum = psum + pn[:, r * tq:(r + 1) * tq]
    imp = _dot(ovt_ref[...], psum, precision=lax.Precision.HIGHEST)
    j = lax.broadcasted_iota(I32, imp.shape, 0)
    cur = (q0 + lax.broadcasted_iota(I32, imp.shape, 1)) // SEL_BLOCK
    valid_b = j <= cur
    forced = (j == 0) | (j == cur) | (j == cur - 1)
    score = jnp.where(valid_b, jnp.where(forced, FORCE_SCORE, imp), NEG_INF)
    sub = 8
    groups = [score[a:a + sub] for a in range(0, SEL_LANES, sub)]
    jrow = lax.broadcasted_iota(I32, groups[0].shape, 0)
    ranks = [jnp.zeros(g_.shape, I32) for g_ in groups]
    for i in range(SEL_LANES):
        si = score[i:i + 1, :]
        for a, g_ in enumerate(groups):
            if a > i // sub:
                inc = jnp.where(si >= g_, 1, 0)
            elif a < i // sub:
                inc = jnp.where(si > g_, 1, 0)
            else:
                inc = jnp.where(jrow > i % sub, jnp.where(si >= g_, 1, 0), jnp.where(si > g_, 1, 0))
            ranks[a] = ranks[a] + inc
    rank = jnp.concatenate(ranks, axis=0)
    selected = (rank < N_SELECT) & valid_b
    sel_ref[0, 0] = jnp.where(selected, 0.0, SEL_MASK_BIAS).astype(BF16)


def _cmp_attn(qt, kcmp, vcmp_t, gates, ovt):
    B, _, T = qt.shape
    tq = min(CMP_TQ, T)
    G, qspec, gspec, _ = _attn_specs(tq)
    cspec = lambda a: pl.BlockSpec((1, 1) + a.shape[2:], lambda b, g, i: (0, b * G + g, 0, 0))
    vspec = lambda a: pl.BlockSpec((1, 1) + a.shape[2:], lambda b, g, i: (1, b * G + g, 0, 0))
    return pl.pallas_call(
        _cmp_attn_kernel,
        grid=(B, G, T // tq),
        in_specs=[qspec, cspec(kcmp), vspec(vcmp_t), gspec, pl.BlockSpec(ovt.shape, lambda b, g, i: (0, 0))],
        out_specs=[qspec, pl.BlockSpec((1, 1, SEL_LANES, tq), lambda b, g, i: (b, g, 0, i))],
        out_shape=[jax.ShapeDtypeStruct((B, NSA_WIDTH, T), BF16),
                   jax.ShapeDtypeStruct((B, G, SEL_LANES, T), BF16)],
        name="cmp_attn",
    )(qt, kcmp, vcmp_t, gates, ovt)


def _sel_attn_kernel(q_ref, sel_ref, k_ref, vt_ref, g_ref, o_ref, *, kc):
    qi = pl.program_id(2)
    tq = q_ref.shape[2]
    q0 = qi * tq
    qt = q_ref[0]
    sb = sel_ref[0, 0]
    qa = jnp.concatenate(
        [jnp.concatenate([qt[r * HEAD_DIM:(r + 1) * HEAD_DIM], sb], axis=0) for r in range(HEADS_PER_GROUP)],
        axis=1)
    gw = qa.shape[1] // SEL_SPLIT
    qas = [qa[:, i * gw:(i + 1) * gw] for i in range(SEL_SPLIT)]

    def step(kstart, kn, carry, diagonal):
        kblk = k_ref[0, 0, pl.ds(kstart, kn), :]
        vblk = vt_ref[0, :, pl.ds(kstart, kn)]
        ss = [_dot(kblk, qg) for qg in qas]
        out = []
        for gi, (s, (m, l, acc)) in enumerate(zip(ss, carry)):
            if diagonal:
                row = lax.broadcasted_iota(I32, s.shape, 0)
                off = (gi * gw + lax.broadcasted_iota(I32, s.shape, 1)) & (tq - 1)
                s = jnp.where(row <= off, s, NEG_INF)
            mn = jnp.maximum(m, jnp.max(s, axis=0, keepdims=True))
            alpha = jnp.exp2(m - mn)
            p = jnp.exp2(s - mn)
            l = alpha * l + jnp.sum(p, axis=0, keepdims=True)
            acc = alpha * acc + _dot(vblk, p.astype(BF16))
            out.append((mn, l, acc))
        return tuple(out)

    init = tuple((jnp.full((1, gw), NEG_INF, F32), jnp.zeros((1, gw), F32), jnp.zeros((HEAD_DIM, gw), F32))
                 for _ in range(SEL_SPLIT))
    n_full = q0 // kc
    carry = lax.fori_loop(0, n_full, lambda i, cr: step(pl.multiple_of(i * kc, kc), kc, cr, False), init)
    carry = lax.fori_loop(n_full * (kc // tq), qi, lambda i, cr: step(pl.multiple_of(i * tq, tq), tq, cr, False), carry)
    carry = step(pl.multiple_of(q0, tq), tq, carry, True)
    l = jnp.concatenate([c[1] for c in carry], axis=1)
    acc = jnp.concatenate([c[2] for c in carry], axis=1)
    o_ref[0] = _gated_out(acc, 1.0 / l, g_ref[0], 1, tq)


def _sel_attn(qt, sel, kaug, kvt, v_row_block, gates):
    B, _, T = qt.shape
    tq = min(SEL_TQ, T)
    kc = min(SEL_KC, T)
    G, qspec, gspec, bg4 = _attn_specs(tq)
    return pl.pallas_call(
        functools.partial(_sel_attn_kernel, kc=kc),
        grid=(B, G, T // tq),
        in_specs=[qspec, pl.BlockSpec((1, 1, SEL_LANES, tq), lambda b, g, i: (b, g, 0, i)), bg4(kaug),
                  pl.BlockSpec((1, HEAD_DIM, T), lambda b, g, i: (b, v_row_block + g, 0)), gspec],
        out_specs=qspec,
        out_shape=jax.ShapeDtypeStruct((B, NSA_WIDTH, T), BF16),
        name="sel_attn",
    )(qt, sel, kaug, kvt, gates)


def _win_attn_kernel(q_ref, k_ref, vt_ref, g_ref, o_ref, *, span):
    qi = pl.program_id(2)
    tq = q_ref.shape[2]
    T = k_ref.shape[2]
    q0 = qi * tq
    start = pl.multiple_of(jnp.clip(q0 + tq - span, 0, T - span), tq)
    qt = _head_lanes(q_ref[0])
    s = _dot(k_ref[0, 0, pl.ds(start, span), :], qt)
    kp = start + lax.broadcasted_iota(I32, s.shape, 0)
    t = q0 + (lax.broadcasted_iota(I32, s.shape, 1) & (tq - 1))
    diff = t - kp
    s = jnp.where((diff >= 0) & (diff < WINDOW), s, NEG_INF)
    m = jnp.max(s, axis=0, keepdims=True)
    p = jnp.exp2(s - m)
    l = jnp.sum(p, axis=0, keepdims=True)
    ot = _dot(vt_ref[0, :, pl.ds(start, span)], p.astype(BF16))
    o_ref[0] = _gated_out(ot, 1.0 / l, g_ref[0], 2, tq)


def _win_attn(qt, k, kvt, v_row_block, gates):
    B, _, T = qt.shape
    tq = min(WIN_TQ, T)
    span = min(WINDOW + tq, T)
    G, qspec, gspec, bg4 = _attn_specs(tq)
    return pl.pallas_call(
        functools.partial(_win_attn_kernel, span=span),
        grid=(B, G, T // tq),
        in_specs=[qspec, bg4(k), pl.BlockSpec((1, HEAD_DIM, T), lambda b, g, i: (b, v_row_block + g, 0)), gspec],
        out_specs=qspec,
        out_shape=jax.ShapeDtypeStruct((B, NSA_WIDTH, T), BF16),
        name="win_attn",
    )(qt, k, kvt, gates)


def _layer_norm(h, g, b):
    mu = jnp.mean(h, axis=-1, keepdims=True)
    c = h - mu
    var = jnp.mean(c * c, axis=-1, keepdims=True)
    return c * lax.rsqrt(var + LN_EPS) * g + b


def _merge_kernel(x_ref, ya_ref, oc_ref, os_ref, ow_ref, wmg_ref, wuc_ref, wun_ref, wo_ref, g1_ref, b1_ref,
                  wr_ref, br_ref, tri_ref, x1_ref, ri_ref, rg_ref, cnt_ref, carry_ref, *, alpha):
    i = pl.program_id(0)
    x = x_ref[...]
    xb = x.astype(BF16)
    mg = _dot(xb, wmg_ref[...])
    y_a = _dot(ya_ref[...], wuc_ref[...])
    o_nsa_t = oc_ref[0].astype(F32) + os_ref[0].astype(F32) + ow_ref[0].astype(F32)
    y_b = _dot(o_nsa_t.T.astype(BF16), wun_ref[...])
    merged = jax.nn.sigmoid(mg[:, :D_MODEL]) * y_a + jax.nn.sigmoid(mg[:, D_MODEL:]) * y_b
    h = alpha * x + _dot(merged.astype(BF16), wo_ref[...])
    x1 = _layer_norm(h, g1_ref[...], b1_ref[...])
    x1_ref[...] = x1

    logits = _dot_nt(wr_ref[...], x1, precision=lax.Precision.HIGHEST) + br_ref[...]
    expert = lax.broadcasted_iota(I32, logits.shape, 0).astype(F32)
    rem = logits
    vals, idxs, hots = [], [], []
    for _ in range(TOP_K):
        m = jnp.max(rem, axis=0, keepdims=True)
        idx = jnp.min(jnp.where(rem == m, expert, float(N_EXPERTS)), axis=0, keepdims=True)
        hot = expert == idx
        vals.append(m)
        idxs.append(idx)
        hots.append(hot)
        rem = jnp.where(hot, -3.0e38, rem)
    es = [jnp.exp(v - vals[0]) for v in vals]
    den = es[0]
    for e in es[1:]:
        den = den + e
    chosen = hots[0]
    for hot in hots[1:]:
        chosen = chosen | hot
    chosen_f = jnp.where(chosen, 1.0, 0.0)

    @pl.when(i == 0)
    def _():
        carry_ref[...] = jnp.zeros_like(carry_ref)

    before = _dot(chosen_f.astype(BF16), tri_ref[...]) + carry_ref[...]
    total = carry_ref[...] + jnp.sum(chosen_f, axis=1, keepdims=True)
    carry_ref[...] = total
    cnt_ref[...] = jnp.broadcast_to(total, cnt_ref.shape).astype(I32)

    ranks = [jnp.sum(jnp.where(hot, before, 0.0), axis=0, keepdims=True) for hot in hots]
    ri_ref[...] = jnp.concatenate(idxs + ranks, axis=0).astype(I32)
    rg_ref[...] = jnp.concatenate([e / den for e in es] + [jnp.zeros_like(den)] * TOP_K, axis=0)


def _merge(x, ya, oc, os_, ow, wmg, wuc, wun, wo, g1, b1, wr, br, tri, alpha):
    N, D = x.shape
    B, _, T = oc.shape
    tm = tri.shape[0]
    nt = T // tm
    tok = lambda w: pl.BlockSpec((tm, w), lambda i: (i, 0))
    feat = pl.BlockSpec((1, NSA_WIDTH, tm), lambda i: (i // nt, 0, i % nt))
    full = lambda a: pl.BlockSpec(a.shape, lambda i: (0,) * a.ndim)
    return pl.pallas_call(
        functools.partial(_merge_kernel, alpha=alpha),
        grid=(N // tm,),
        in_specs=[tok(D), tok(CONV_CH), feat, feat, feat,
                  full(wmg), full(wuc), full(wun), full(wo), full(g1), full(b1), full(wr), full(br), full(tri)],
        out_specs=[tok(D), pl.BlockSpec((2 * TOP_K, tm), lambda i: (0, i)), pl.BlockSpec((2 * TOP_K, tm), lambda i: (0, i)),
                   pl.BlockSpec((N_EXPERTS, LANES), lambda i: (0, 0))],
        out_shape=[jax.ShapeDtypeStruct((N, D), F32),
                   jax.ShapeDtypeStruct((2 * TOP_K, N), I32),
                   jax.ShapeDtypeStruct((2 * TOP_K, N), F32),
                   jax.ShapeDtypeStruct((N_EXPERTS, LANES), I32)],
        scratch_shapes=[pltpu.VMEM((N_EXPERTS, 1), F32)],
        compiler_params=pltpu.CompilerParams(dimension_semantics=("arbitrary",)),
        name="merge",
    )(x, ya, oc, os_, ow, wmg, wuc, wun, wo, g1, b1, wr, br, tri)


def _row_copy(src, si, dst, di, sem):
    return pltpu.make_async_copy(src.at[pl.ds(si, 1), :], dst.at[pl.ds(di, 1), :], sem)


def _dispatch_kernel(dest_ref, cnt_ref, pstart_ref, pend_ref, x_ref, xs_hbm, zero_ref, sem, zsem):
    i = pl.program_id(0)
    n = dest_ref.shape[0]

    @pl.when(i == 0)
    def _():
        zero_ref[...] = jnp.zeros_like(zero_ref)
        for e in range(N_EXPERTS):
            lo = pstart_ref[e] + cnt_ref[e]
            hi = pend_ref[e]

            def zstart(r, c):
                _row_copy(zero_ref, 0, xs_hbm, r, zsem).start()
                return c

            def zwait(r, c):
                _row_copy(zero_ref, 0, xs_hbm, r, zsem).wait()
                return c

            lax.fori_loop(lo, hi, zstart, 0)
            lax.fori_loop(lo, hi, zwait, 0)

    def start(t, c):
        for k in range(TOP_K):
            _row_copy(x_ref, t, xs_hbm, dest_ref[t * TOP_K + k], sem).start()
        return c

    def wait(t, c):
        for k in range(TOP_K):
            _row_copy(x_ref, 0, xs_hbm, 0, sem).wait()
        return c

    lax.fori_loop(0, n // TOP_K, start, 0)
    lax.fori_loop(0, n // TOP_K, wait, 0)


def _dispatch(dest, cnt, pstart, pend, x1, n_rows):
    N, D = x1.shape
    tm = min(DISP_TM, N)
    smem = pl.BlockSpec(memory_space=pltpu.SMEM)
    return pl.pallas_call(
        _dispatch_kernel,
        grid=(N // tm,),
        in_specs=[pl.BlockSpec((tm * TOP_K,), lambda i: (i,), memory_space=pltpu.SMEM), smem, smem, smem,
                  pl.BlockSpec((tm, D), lambda i: (i, 0))],
        out_specs=pl.BlockSpec(memory_space=pl.ANY),
        out_shape=jax.ShapeDtypeStruct((n_rows, D), F32),
        scratch_shapes=[pltpu.VMEM((8, D), F32), pltpu.SemaphoreType.DMA(()), pltpu.SemaphoreType.DMA(())],
        compiler_params=pltpu.CompilerParams(dimension_semantics=("arbitrary",)),
        name="dispatch",
    )(dest, cnt, pstart, pend, x1)


def _experts_kernel(be_ref, nu_ref, xs_ref, wgu_ref, perm_ref, bg_ref, bl_ref, wd_ref, bd_ref, ys_ref, wg_s, wl_s, wd_s):
    i = pl.program_id(0)

    @pl.when((i == 0) | (be_ref[i] != be_ref[jnp.maximum(i, 1) - 1]))
    def _():
        w = perm_ref.shape[0]
        for c in range(2 * D_FF // w):
            t = _dot(wgu_ref[0, :, c * w:(c + 1) * w].astype(BF16), perm_ref[...])
            wg_s[:, c * (w // 2):(c + 1) * (w // 2)] = t[:, :w // 2].astype(BF16)
            wl_s[:, c * (w // 2):(c + 1) * (w // 2)] = t[:, w // 2:].astype(BF16)
        wd_s[...] = wd_ref[0].astype(BF16)

    @pl.when(i < nu_ref[0])
    def _():
        xb = xs_ref[...].astype(BF16)
        x_glu = jnp.minimum(_dot(xb, wg_s[...]) + bg_ref[0], SWIGLU_LIMIT)
        x_lin = jnp.clip(_dot(xb, wl_s[...]) + bl_ref[0], -SWIGLU_LIMIT, SWIGLU_LIMIT)
        act = x_glu * jax.nn.sigmoid(SWIGLU_ALPHA * x_glu) * (x_lin + 1.0)
        ys_ref[...] = _dot(act.astype(BF16), wd_s[...]) + bd_ref[0]


def _experts(block_expert, n_used, xs, wgu, perm, bg, bl, wd, bd):
    P, D = xs.shape
    n_blocks = P // MOE_BLK
    rows = lambda i, be, nu: (jnp.minimum(i, nu[0] - 1), 0)
    wsel = lambda i, be, nu: (be[i], 0, 0)
    return pl.pallas_call(
        _experts_kernel,
        grid_spec=pltpu.PrefetchScalarGridSpec(
            num_scalar_prefetch=2,
            grid=(n_blocks,),
            in_specs=[pl.BlockSpec((MOE_BLK, D), rows),
                      pl.BlockSpec((1, D, 2 * D_FF), wsel),
                      pl.BlockSpec(perm.shape, lambda i, be, nu: (0, 0)),
                      pl.BlockSpec((1, 1, D_FF), wsel), pl.BlockSpec((1, 1, D_FF), wsel),
                      pl.BlockSpec((1, D_FF, D), wsel), pl.BlockSpec((1, 1, D), wsel)],
            out_specs=pl.BlockSpec((MOE_BLK, D), rows),
            scratch_shapes=[pltpu.VMEM((D, D_FF), BF16), pltpu.VMEM((D, D_FF), BF16), pltpu.VMEM((D_FF, D), BF16)]),
        out_shape=jax.ShapeDtypeStruct((P, D), F32),
        compiler_params=pltpu.CompilerParams(dimension_semantics=("arbitrary",)),
        name="experts",
    )(block_expert, n_used, xs, wgu, perm, bg, bl, wd, bd)


def _combine_kernel(dest_ref, ys_hbm, x1_ref, rg_ref, g2_ref, b2_ref, o_ref, buf, sem, *, alpha):
    tm = o_ref.shape[0]

    def start(t, c):
        for k in range(TOP_K):
            pltpu.make_async_copy(ys_hbm.at[pl.ds(dest_ref[t * TOP_K + k], 1), :], buf.at[k, pl.ds(t, 1), :], sem).start()
        return c

    def wait(t, c):
        for k in range(TOP_K):
            pltpu.make_async_copy(ys_hbm.at[pl.ds(0, 1), :], buf.at[k, pl.ds(t, 1), :], sem).wait()
        return c

    lax.fori_loop(0, tm, start, 0)
    lax.fori_loop(0, tm, wait, 0)
    gate = rg_ref[...]
    y = gate[:, 0:1] * buf[0]
    for k in range(1, TOP_K):
        y = y + gate[:, k:k + 1] * buf[k]
    o_ref[...] = _layer_norm(alpha * x1_ref[...] + y, g2_ref[...], b2_ref[...])


def _combine(dest, ys, x1, rg, g2, b2, alpha):
    N, D = x1.shape
    tm = min(COMB_TM, N)
    tok = lambda w: pl.BlockSpec((tm, w), lambda i: (i, 0))
    full = lambda a: pl.BlockSpec(a.shape, lambda i: (0,) * a.ndim)
    return pl.pallas_call(
        functools.partial(_combine_kernel, alpha=alpha),
        grid=(N // tm,),
        in_specs=[pl.BlockSpec((tm * TOP_K,), lambda i: (i,), memory_space=pltpu.SMEM),
                  pl.BlockSpec(memory_space=pl.ANY), tok(D), tok(rg.shape[1]), full(g2), full(b2)],
        out_specs=tok(D),
        out_shape=jax.ShapeDtypeStruct((N, D), F32),
        scratch_shapes=[pltpu.VMEM((TOP_K, tm, D), F32), pltpu.SemaphoreType.DMA(())],
        compiler_params=pltpu.CompilerParams(dimension_semantics=("arbitrary",)),
        name="combine",
    )(dest, ys, x1, rg, g2, b2)


def _rope_freq():
    half = ROPE_DIM // 2
    inv = (np.float32(ROPE_THETA) ** (-np.arange(half, dtype=np.float32) * np.float32(2.0 / ROPE_DIM))).astype(np.float32)
    return inv[:, None]


def _overlap_t(T):
    nc = T // CMP_STRIDE
    c0 = np.arange(nc) * CMP_STRIDE
    j0 = np.arange(SEL_LANES) * SEL_BLOCK
    ov = (c0[None, :] < j0[:, None] + SEL_BLOCK) & (c0[None, :] + CMP_BLOCK > j0[:, None])
    ov &= (np.arange(nc) < nc - 1)[None, :] & (np.arange(SEL_LANES) < T // SEL_BLOCK)[:, None]
    return ov.astype(np.float32)


def _deinterleave_perm():
    w = 2 * LANES
    p = np.zeros((w, w), np.float32)
    p[np.arange(0, w, 2), np.arange(w // 2)] = 1.0
    p[np.arange(1, w, 2), w // 2 + np.arange(w // 2)] = 1.0
    return p


def _token_major(t, B, T):
    return t.reshape(B, N_KV_GROUPS, HEAD_DIM, T).transpose(0, 1, 3, 2)


def _layer(x, positions, w_in, conv_w, cmp_pos_k, cmp_w1_k, cmp_w2_k, cmp_pos_v, cmp_w1_v, cmp_w2_v,
           w_up_conv, w_up_nsa, w_o, ln1_g, ln1_b, w_router, b_router, w_gate_up, b_gate_up,
           w_down, b_down, ln2_g, ln2_b, alpha):
    B, T, D = x.shape
    G, R = N_KV_GROUPS, HEADS_PER_GROUP
    N = B * T
    assert D == D_MODEL and T % SEL_BLOCK == 0 and T // SEL_BLOCK <= SEL_LANES

    c0 = 3 * CONV_CH
    c1 = c0 + NSA_WIDTH
    c2 = c1 + 6 * KV_WIDTH
    c3 = c2 + 3 * N_HEADS
    wc = w_in[:, :c0].astype(BF16)
    wqt = w_in[:, c0:c1].T.astype(BF16)
    wkvt = w_in[:, c1:c2].T.astype(BF16)
    gcols = np.zeros((G * GATE_ROWS,), np.int64)
    gmask = np.zeros((G * GATE_ROWS,), np.float32)
    for g in range(G):
        for br in range(3):
            for r in range(R):
                gcols[g * GATE_ROWS + br * R + r] = br * N_HEADS + g * R + r
                gmask[g * GATE_ROWS + br * R + r] = 1.0
    wgt = (w_in[:, c2:c3][:, gcols] * gmask).T.astype(BF16)
    wmg = w_in[:, c3:].astype(BF16)
    cw = conv_w.reshape(CONV_K, CONV_CH)
    pos_row = positions.astype(F32)[:, None, :]

    ya_pre, qt, kvt, gates = _proj(x, pos_row, wc, wqt, wkvt, wgt, cw, jnp.asarray(_rope_freq()))

    nc = T // CMP_STRIDE
    to_chunks = lambda t: _token_major(t, B, T).reshape(B * G, nc, CMP_STRIDE * HEAD_DIM)
    xin = jnp.stack([to_chunks(kvt[:, 0:KV_WIDTH]), to_chunks(kvt[:, KV_WIDTH:2 * KV_WIDTH])])
    w1 = jnp.stack([cmp_w1_k, cmp_w1_v])
    w2 = jnp.stack([cmp_w2_k, cmp_w2_v])
    pos_flat = jnp.stack([cmp_pos_k.reshape(1, -1), cmp_pos_v.reshape(1, -1)])
    pos_flat = jnp.pad(pos_flat, ((0, 0), (0, 7), (0, 0)))
    kcmp, kcmp_t = _compress(xin, w1, w2, w2.transpose(0, 2, 1), pos_flat)

    o_c, sel = _cmp_attn(qt, kcmp, kcmp_t, gates, jnp.asarray(_overlap_t(T)))

    ks = _token_major(kvt[:, 2 * KV_WIDTH:3 * KV_WIDTH], B, T)
    kw = _token_major(kvt[:, 4 * KV_WIDTH:5 * KV_WIDTH], B, T)
    onehot = (np.arange(T)[:, None] // SEL_BLOCK == np.arange(SEL_LANES)[None, :]).astype(np.float32)
    kaug = jnp.concatenate([ks, jnp.broadcast_to(jnp.asarray(onehot, BF16), (B, G, T, SEL_LANES))], axis=-1)
    o_s = _sel_attn(qt, sel, kaug, kvt, 3 * G, gates)
    o_w = _win_attn(qt, kw, kvt, 5 * G, gates)

    tm = min(MERGE_TM, T)
    tri = jnp.asarray(np.triu(np.ones((tm, tm), np.float32), 1), BF16)
    x1, ri, rg, cnt = _merge(x.reshape(N, D), ya_pre.reshape(N, CONV_CH), o_c, o_s, o_w, wmg,
                             w_up_conv.astype(BF16), w_up_nsa.astype(BF16), w_o.astype(BF16),
                             ln1_g[None, :], ln1_b[None, :], w_router.T, b_router[:, None], tri, alpha)

    counts = cnt[:, 0]
    padded = (counts + MOE_BLK - 1) // MOE_BLK * MOE_BLK
    pend = jnp.cumsum(padded).astype(I32)
    pstart = pend - padded
    idx = ri[:TOP_K]
    pstart_sel = jnp.zeros_like(idx)
    for e in range(N_EXPERTS):
        pstart_sel = jnp.where(idx == e, pstart[e], pstart_sel)
    dest = (pstart_sel + ri[TOP_K:]).T.reshape(N * TOP_K)
    n_blocks = (N * TOP_K) // MOE_BLK + N_EXPERTS
    blk_start = jnp.arange(n_blocks, dtype=I32) * MOE_BLK
    block_expert = jnp.minimum(jnp.sum((blk_start[:, None] >= pend[None, :]).astype(I32), axis=1), N_EXPERTS - 1)
    n_used = (pend[-1:] // MOE_BLK).astype(I32)

    xs = _dispatch(dest, counts, pstart, pend, x1, n_blocks * MOE_BLK)
    ys = _experts(block_expert, n_used, xs, w_gate_up, jnp.asarray(_deinterleave_perm(), BF16),
                  b_gate_up[:, None, 0::2], b_gate_up[:, None, 1::2], w_down, b_down[:, None, :])
    out = _combine(dest, ys, x1, rg.T, ln2_g[None, :], ln2_b[None, :], alpha)
    return out.reshape(B, T, D)


def kernel(x, positions, w_in, conv_w, cmp_pos_k, cmp_w1_k, cmp_w2_k, cmp_pos_v, cmp_w1_v, cmp_w2_v, w_up_conv, w_up_nsa, w_o, ln1_g, ln1_b, w_router, b_router, w_gate_up, b_gate_up, w_down, b_down, ln2_g, ln2_b):
    depth = w_in.shape[0]
    alpha = float((2 * depth) ** 0.25)
    h = x
    for l in range(depth):
        h = _layer(h, positions, w_in[l], conv_w[l], cmp_pos_k[l], cmp_w1_k[l], cmp_w2_k[l],
                   cmp_pos_v[l], cmp_w1_v[l], cmp_w2_v[l], w_up_conv[l], w_up_nsa[l], w_o[l],
                   ln1_g[l], ln1_b[l], w_router[l], b_router[l], w_gate_up[l], b_gate_up[l],
                   w_down[l], b_down[l], ln2_g[l], ln2_b[l], alpha)
    return h
```

```python
import functools

import numpy as np
import jax
import jax.numpy as jnp
from jax import lax
from jax.experimental import pallas as pl
from jax.experimental.pallas import tpu as pltpu

F32 = jnp.float32
BF16 = jnp.bfloat16
I32 = jnp.int32

D_MODEL = 1024
CONV_CH = 512
CONV_K = 3
N_HEADS = 8
N_KV_GROUPS = 2
HEADS_PER_GROUP = N_HEADS // N_KV_GROUPS
HEAD_DIM = 64
NSA_WIDTH = N_HEADS * HEAD_DIM
KV_WIDTH = N_KV_GROUPS * HEAD_DIM
ROPE_DIM = HEAD_DIM // 4
ROPE_THETA = 500000.0
CMP_BLOCK = 32
CMP_STRIDE = 16
CMP_HIDDEN = 256
SEL_BLOCK = 64
N_SELECT = 16
WINDOW = 512
N_EXPERTS = 32
TOP_K = 4
D_FF = 1024
SWIGLU_LIMIT = 7.0
SWIGLU_ALPHA = 1.702
LN_EPS = 1e-5
NEG_INF = -1e30
FORCE_SCORE = 1e4
LOG2E = 1.4426950408889634

LANES = 128
ROW_SUB = D_MODEL // LANES
SEL_LANES = 64
SEL_MASK_BIAS = -32768.0
GROUP_W = HEADS_PER_GROUP * HEAD_DIM
GATE_ROWS = 16

PROJ_TM = 512
CMP_TQ = 256
SEL_TQ = 256
SEL_KC = 512
SEL_SPLIT = 4
WIN_TQ = 128
MERGE_TM = 256
MOE_BLK = 512
DISP_TM = 512
COMB_TM = 256


def _dot(a, b, precision=None):
    return jnp.dot(a, b, precision=precision, preferred_element_type=F32)


def _dot_nt(a, b, precision=None):
    return lax.dot_general(a, b, (((1,), (1,)), ((), ())), precision=precision, preferred_element_type=F32)


def _proj_kernel(x_ref, pos_ref, wc_ref, wqt_ref, wkvt_ref, wgt_ref, cw_ref, freq_ref,
                 ya_ref, qt_ref, kvt_ref, gt_ref, carry_ref):
    ti = pl.program_id(1)
    tm = x_ref.shape[1]
    xb = x_ref[0].astype(BF16)

    pc = _dot(xb, wc_ref[...])
    xv = pc[:, :CONV_CH]
    bg = pc[:, CONV_CH:2 * CONV_CH]
    cg = pc[:, 2 * CONV_CH:]
    u = cg * xv

    @pl.when(ti == 0)
    def _():
        carry_ref[...] = jnp.zeros_like(carry_ref)

    prev = carry_ref[...]
    row = lax.broadcasted_iota(I32, u.shape, 0)
    u1 = jnp.where(row == 0, prev[7:8], pltpu.roll(u, 1, 0))
    u2 = jnp.where(row == 0, prev[6:7], jnp.where(row == 1, prev[7:8], pltpu.roll(u, 2, 0)))
    cw = cw_ref[...]
    conv = cw[2:3] * u + cw[1:2] * u1 + cw[0:1] * u2
    carry_ref[...] = u[tm - 8:]
    ya_ref[0] = (bg * conv).astype(BF16)

    ang = freq_ref[...] * pos_ref[0]
    cos = jnp.cos(ang)
    sin = jnp.sin(ang)
    half = ROPE_DIM // 2

    def rope_head(t):
        t1 = t[:half]
        t2 = t[half:ROPE_DIM]
        return [t1 * cos - t2 * sin, t2 * cos + t1 * sin, t[ROPE_DIM:]]

    def heads(t, rotate):
        out = []
        for h in range(t.shape[0] // HEAD_DIM):
            th = t[h * HEAD_DIM:(h + 1) * HEAD_DIM]
            out.extend(rope_head(th) if rotate(h) else [th])
        return jnp.concatenate(out, axis=0)

    qt = heads(_dot_nt(wqt_ref[...], xb), lambda h: True)
    qt_ref[0] = (qt * (HEAD_DIM ** -0.5 * LOG2E)).astype(BF16)
    kvt = heads(_dot_nt(wkvt_ref[...], xb), lambda h: (h // N_KV_GROUPS) % 2 == 0)
    kvt_ref[0] = kvt.astype(BF16)
    gt_ref[0] = jax.nn.sigmoid(_dot_nt(wgt_ref[...], xb))


def _proj(x, pos_row, wc, wqt, wkvt, wgt, cw, freq):
    B, T, D = x.shape
    tm = min(PROJ_TM, T)
    grid = (B, T // tm)
    full = lambda a: pl.BlockSpec(a.shape, lambda b, t: (0,) * a.ndim)
    tok = lambda w: pl.BlockSpec((1, tm, w), lambda b, t: (b, t, 0))
    feat = lambda r: pl.BlockSpec((1, r, tm), lambda b, t: (b, 0, t))
    n_gate = wgt.shape[0]
    return pl.pallas_call(
        _proj_kernel,
        grid=grid,
        in_specs=[tok(D), feat(1), full(wc), full(wqt), full(wkvt), full(wgt), full(cw), full(freq)],
        out_specs=[tok(CONV_CH), feat(NSA_WIDTH), feat(6 * KV_WIDTH), feat(n_gate)],
        out_shape=[jax.ShapeDtypeStruct((B, T, CONV_CH), BF16),
                   jax.ShapeDtypeStruct((B, NSA_WIDTH, T), BF16),
                   jax.ShapeDtypeStruct((B, 6 * KV_WIDTH, T), BF16),
                   jax.ShapeDtypeStruct((B, n_gate, T), F32)],
        scratch_shapes=[pltpu.VMEM((8, CONV_CH), F32)],
        compiler_params=pltpu.CompilerParams(dimension_semantics=("arbitrary", "arbitrary")),
        name="proj",
    )(x, pos_row, wc, wqt, wkvt, wgt, cw, freq)


def _compress_kernel(x_ref, w1_ref, w2_ref, w2t_ref, pos_ref, o_ref, ot_ref):
    xb = x_ref[0, 0]
    nc = xb.shape[0]
    w1 = w1_ref[0]
    w1b = w1.astype(BF16)
    half = CMP_STRIDE * HEAD_DIM
    a = _dot(xb, w1b[:half])
    b = _dot(xb, w1b[half:])
    b_next = pltpu.roll(b, nc - 1, 0)
    pb = _dot(pos_ref[0], w1, precision=lax.Precision.HIGHEST)[0:1]
    h = a + b_next + pb
    g = (0.5 * h * (1.0 + jnp.tanh(np.sqrt(2.0 / np.pi) * (h + 0.044715 * (h * h * h))))).astype(BF16)
    o_ref[0, 0] = _dot(g, w2_ref[0].astype(BF16)).astype(BF16)
    ot_ref[0, 0] = _dot_nt(w2t_ref[0].astype(BF16), g).astype(BF16)


def _compress(xin, w1, w2, w2t, pos):
    _, BG, nc, W = xin.shape
    per = lambda a: pl.BlockSpec((1,) + a.shape[1:], lambda s, i: (s, 0, 0))
    return pl.pallas_call(
        _compress_kernel,
        grid=(2, BG),
        in_specs=[pl.BlockSpec((1, 1, nc, W), lambda s, i: (s, i, 0, 0)), per(w1), per(w2), per(w2t), per(pos)],
        out_specs=[pl.BlockSpec((1, 1, nc, HEAD_DIM), lambda s, i: (s, i, 0, 0)),
                   pl.BlockSpec((1, 1, HEAD_DIM, nc), lambda s, i: (s, i, 0, 0))],
        out_shape=[jax.ShapeDtypeStruct((2, BG, nc, HEAD_DIM), BF16),
                   jax.ShapeDtypeStruct((2, BG, HEAD_DIM, nc), BF16)],
        name="compress",
    )(xin, w1, w2, w2t, pos)


def _head_lanes(qt):
    return jnp.concatenate([qt[r * HEAD_DIM:(r + 1) * HEAD_DIM] for r in range(HEADS_PER_GROUP)], axis=1)


def _gated_out(ot, scale, gates, branch, tq):
    rows = []
    for r in range(HEADS_PER_GROUP):
        c = branch * HEADS_PER_GROUP + r
        sl = slice(r * tq, (r + 1) * tq)
        rows.append(ot[:, sl] * (scale[:, sl] * gates[c:c + 1]))
    return jnp.concatenate(rows, axis=0).astype(BF16)


def _attn_specs(tq):
    G = N_KV_GROUPS
    qspec = pl.BlockSpec((1, GROUP_W, tq), lambda b, g, i: (b, g, i))
    gspec = pl.BlockSpec((1, GATE_ROWS, tq), lambda b, g, i: (b, g, i))
    bg4 = lambda a: pl.BlockSpec((1, 1) + a.shape[2:], lambda b, g, i: (b, g, 0, 0))
    return G, qspec, gspec, bg4


def _cmp_attn_kernel(q_ref, kc_ref, vct_ref, g_ref, ovt_ref, o_ref, sel_ref):
    qi = pl.program_id(2)
    tq = q_ref.shape[2]
    q0 = qi * tq
    qt = _head_lanes(q_ref[0])
    s = _dot(kc_ref[0, 0], qt)
    c = lax.broadcasted_iota(I32, s.shape, 0)
    t = q0 + (lax.broadcasted_iota(I32, s.shape, 1) & (tq - 1))
    valid = c * CMP_STRIDE + (CMP_BLOCK - 1) <= t
    sm = jnp.where(valid, s, NEG_INF)
    m = jnp.max(sm, axis=0, keepdims=True)
    p = jnp.where(valid, jnp.exp2(sm - m), 0.0)
    l = jnp.sum(p, axis=0, keepdims=True)
    inv = 1.0 / jnp.where(l > 0.0, l, 1.0)
    ot = _dot(vct_ref[0, 0], p.astype(BF16))
    o_ref[0] = _gated_out(ot, inv, g_ref[0], 0, tq)

    pn = p * inv
    psum = pn[:, 0:tq]
    for r in range(1, HEADS_PER_GROUP):
        psum = psum + pn[:, r * tq:(r + 1) * tq]
    imp = _dot(ovt_ref[...], psum, precision=lax.Precision.HIGHEST)
    j = lax.broadcasted_iota(I32, imp.shape, 0)
    cur = (q0 + lax.broadcasted_iota(I32, imp.shape, 1)) // SEL_BLOCK
    valid_b = j <= cur
    forced = (j == 0) | (j == cur) | (j == cur - 1)
    score = jnp.where(valid_b, jnp.where(forced, FORCE_SCORE, imp), NEG_INF)
    sub = 8
    groups = [score[a:a + sub] for a in range(0, SEL_LANES, sub)]
    jrow = lax.broadcasted_iota(I32, groups[0].shape, 0)
    ranks = [jnp.zeros(g_.shape, I32) for g_ in groups]
    for i in range(SEL_LANES):
        si = score[i:i + 1, :]
        for a, g_ in enumerate(groups):
            if a > i // sub:
                inc = jnp.where(si >= g_, 1, 0)
            elif a < i // sub:
                inc = jnp.where(si > g_, 1, 0)
            else:
                inc = jnp.where(jrow > i % sub, jnp.where(si >= g_, 1, 0), jnp.where(si > g_, 1, 0))
            ranks[a] = ranks[a] + inc
    rank = jnp.concatenate(ranks, axis=0)
    selected = (rank < N_SELECT) & valid_b
    sel_ref[0, 0] = jnp.where(selected, 0.0, SEL_MASK_BIAS).astype(BF16)


def _cmp_attn(qt, kcmp, vcmp_t, gates, ovt):
    B, _, T = qt.shape
    tq = min(CMP_TQ, T)
    G, qspec, gspec, _ = _attn_specs(tq)
    cspec = lambda a: pl.BlockSpec((1, 1) + a.shape[2:], lambda b, g, i: (0, b * G + g, 0, 0))
    vspec = lambda a: pl.BlockSpec((1, 1) + a.shape[2:], lambda b, g, i: (1, b * G + g, 0, 0))
    return pl.pallas_call(
        _cmp_attn_kernel,
        grid=(B, G, T // tq),
        in_specs=[qspec, cspec(kcmp), vspec(vcmp_t), gspec, pl.BlockSpec(ovt.shape, lambda b, g, i: (0, 0))],
        out_specs=[qspec, pl.BlockSpec((1, 1, SEL_LANES, tq), lambda b, g, i: (b, g, 0, i))],
        out_shape=[jax.ShapeDtypeStruct((B, NSA_WIDTH, T), BF16),
                   jax.ShapeDtypeStruct((B, G, SEL_LANES, T), BF16)],
        name="cmp_attn",
    )(qt, kcmp, vcmp_t, gates, ovt)


def _sel_attn_kernel(q_ref, sel_ref, k_ref, vt_ref, g_ref, o_ref, *, kc):
    qi = pl.program_id(2)
    tq = q_ref.shape[2]
    q0 = qi * tq
    qt = q_ref[0]
    sb = sel_ref[0, 0]
    qa = jnp.concatenate(
        [jnp.concatenate([qt[r * HEAD_DIM:(r + 1) * HEAD_DIM], sb], axis=0) for r in range(HEADS_PER_GROUP)],
        axis=1)
    gw = qa.shape[1] // SEL_SPLIT
    qas = [qa[:, i * gw:(i + 1) * gw] for i in range(SEL_SPLIT)]

    def step(kstart, kn, carry, diagonal):
        kblk = k_ref[0, 0, pl.ds(kstart, kn), :]
        vblk = vt_ref[0, :, pl.ds(kstart, kn)]
        ss = [_dot(kblk, qg) for qg in qas]
        out = []
        for gi, (s, (m, l, acc)) in enumerate(zip(ss, carry)):
            if diagonal:
                row = lax.broadcasted_iota(I32, s.shape, 0)
                off = (gi * gw + lax.broadcasted_iota(I32, s.shape, 1)) & (tq - 1)
                s = jnp.where(row <= off, s, NEG_INF)
            mn = jnp.maximum(m, jnp.max(s, axis=0, keepdims=True))
            alpha = jnp.exp2(m - mn)
            p = jnp.exp2(s - mn)
            l = alpha * l + jnp.sum(p, axis=0, keepdims=True)
            acc = alpha * acc + _dot(vblk, p.astype(BF16))
            out.append((mn, l, acc))
        return tuple(out)

    init = tuple((jnp.full((1, gw), NEG_INF, F32), jnp.zeros((1, gw), F32), jnp.zeros((HEAD_DIM, gw), F32))
                 for _ in range(SEL_SPLIT))
    n_full = q0 // kc
    carry = lax.fori_loop(0, n_full, lambda i, cr: step(pl.multiple_of(i * kc, kc), kc, cr, False), init)
    carry = lax.fori_loop(n_full * (kc // tq), qi, lambda i, cr: step(pl.multiple_of(i * tq, tq), tq, cr, False), carry)
    carry = step(pl.multiple_of(q0, tq), tq, carry, True)
    l = jnp.concatenate([c[1] for c in carry], axis=1)
    acc = jnp.concatenate([c[2] for c in carry], axis=1)
    o_ref[0] = _gated_out(acc, 1.0 / l, g_ref[0], 1, tq)


def _sel_attn(qt, sel, kaug, kvt, v_row_block, gates):
    B, _, T = qt.shape
    tq = min(SEL_TQ, T)
    kc = min(SEL_KC, T)
    G, qspec, gspec, bg4 = _attn_specs(tq)
    return pl.pallas_call(
        functools.partial(_sel_attn_kernel, kc=kc),
        grid=(B, G, T // tq),
        in_specs=[qspec, pl.BlockSpec((1, 1, SEL_LANES, tq), lambda b, g, i: (b, g, 0, i)), bg4(kaug),
                  pl.BlockSpec((1, HEAD_DIM, T), lambda b, g, i: (b, v_row_block + g, 0)), gspec],
        out_specs=qspec,
        out_shape=jax.ShapeDtypeStruct((B, NSA_WIDTH, T), BF16),
        name="sel_attn",
    )(qt, sel, kaug, kvt, gates)


def _win_attn_kernel(q_ref, k_ref, vt_ref, g_ref, o_ref, *, span):
    qi = pl.program_id(2)
    tq = q_ref.shape[2]
    T = k_ref.shape[2]
    q0 = qi * tq
    start = pl.multiple_of(jnp.clip(q0 + tq - span, 0, T - span), tq)
    qt = _head_lanes(q_ref[0])
    s = _dot(k_ref[0, 0, pl.ds(start, span), :], qt)
    kp = start + lax.broadcasted_iota(I32, s.shape, 0)
    t = q0 + (lax.broadcasted_iota(I32, s.shape, 1) & (tq - 1))
    diff = t - kp
    s = jnp.where((diff >= 0) & (diff < WINDOW), s, NEG_INF)
    m = jnp.max(s, axis=0, keepdims=True)
    p = jnp.exp2(s - m)
    l = jnp.sum(p, axis=0, keepdims=True)
    ot = _dot(vt_ref[0, :, pl.ds(start, span)], p.astype(BF16))
    o_ref[0] = _gated_out(ot, 1.0 / l, g_ref[0], 2, tq)


def _win_attn(qt, k, kvt, v_row_block, gates):
    B, _, T = qt.shape
    tq = min(WIN_TQ, T)
    span = min(WINDOW + tq, T)
    G, qspec, gspec, bg4 = _attn_specs(tq)
    return pl.pallas_call(
        functools.partial(_win_attn_kernel, span=span),
        grid=(B, G, T // tq),
        in_specs=[qspec, bg4(k), pl.BlockSpec((1, HEAD_DIM, T), lambda b, g, i: (b, v_row_block + g, 0)), gspec],
        out_specs=qspec,
        out_shape=jax.ShapeDtypeStruct((B, NSA_WIDTH, T), BF16),
        name="win_attn",
    )(qt, k, kvt, gates)


def _store_row_tiles(ref, x):
    rows = x.shape[0]
    for s in range(ROW_SUB):
        ref[pl.ds(s, rows, stride=ROW_SUB), :] = x[:, s * LANES:(s + 1) * LANES]


def _load_row_tiles(ref):
    rows = ref.shape[0] // ROW_SUB
    return jnp.concatenate([ref[pl.ds(s, rows, stride=ROW_SUB), :] for s in range(ROW_SUB)], axis=1)


def _tile_copy(src, si, dst, di, sem):
    return pltpu.make_async_copy(src.at[pl.ds(pl.multiple_of(si * ROW_SUB, ROW_SUB), ROW_SUB), :],
                                 dst.at[pl.ds(pl.multiple_of(di * ROW_SUB, ROW_SUB), ROW_SUB), :], sem)


def _layer_norm(h, g, b):
    mu = jnp.mean(h, axis=-1, keepdims=True)
    c = h - mu
    var = jnp.mean(c * c, axis=-1, keepdims=True)
    return c * lax.rsqrt(var + LN_EPS) * g + b


def _merge_kernel(x_ref, ya_ref, oc_ref, os_ref, ow_ref, wmg_ref, wuc_ref, wun_ref, wo_ref, g1_ref, b1_ref,
                  wr_ref, br_ref, tri_ref, x1_ref, x1t_ref, ri_ref, rg_ref, cnt_ref, carry_ref, *, alpha):
    i = pl.program_id(0)
    x = x_ref[...]
    xb = x.astype(BF16)
    mg = _dot(xb, wmg_ref[...])
    y_a = _dot(ya_ref[...], wuc_ref[...])
    o_nsa_t = oc_ref[0].astype(F32) + os_ref[0].astype(F32) + ow_ref[0].astype(F32)
    y_b = _dot(o_nsa_t.T.astype(BF16), wun_ref[...])
    merged = jax.nn.sigmoid(mg[:, :D_MODEL]) * y_a + jax.nn.sigmoid(mg[:, D_MODEL:]) * y_b
    h = alpha * x + _dot(merged.astype(BF16), wo_ref[...])
    x1 = _layer_norm(h, g1_ref[...], b1_ref[...])
    x1_ref[...] = x1
    _store_row_tiles(x1t_ref, x1)

    logits = _dot_nt(wr_ref[...], x1, precision=lax.Precision.HIGHEST) + br_ref[...]
    expert = lax.broadcasted_iota(I32, logits.shape, 0).astype(F32)
    rem = logits
    vals, idxs, hots = [], [], []
    for _ in range(TOP_K):
        m = jnp.max(rem, axis=0, keepdims=True)
        idx = jnp.min(jnp.where(rem == m, expert, float(N_EXPERTS)), axis=0, keepdims=True)
        hot = expert == idx
        vals.append(m)
        idxs.append(idx)
        hots.append(hot)
        rem = jnp.where(hot, -3.0e38, rem)
    es = [jnp.exp(v - vals[0]) for v in vals]
    den = es[0]
    for e in es[1:]:
        den = den + e
    chosen = hots[0]
    for hot in hots[1:]:
        chosen = chosen | hot
    chosen_f = jnp.where(chosen, 1.0, 0.0)

    @pl.when(i == 0)
    def _():
        carry_ref[...] = jnp.zeros_like(carry_ref)

    before = _dot(chosen_f.astype(BF16), tri_ref[...]) + carry_ref[...]
    total = carry_ref[...] + jnp.sum(chosen_f, axis=1, keepdims=True)
    carry_ref[...] = total
    cnt_ref[...] = jnp.broadcast_to(total, cnt_ref.shape).astype(I32)

    ranks = [jnp.sum(jnp.where(hot, before, 0.0), axis=0, keepdims=True) for hot in hots]
    ri_ref[...] = jnp.concatenate(idxs + ranks, axis=0).astype(I32)
    rg_ref[...] = jnp.concatenate([e / den for e in es] + [jnp.zeros_like(den)] * TOP_K, axis=0)


def _merge(x, ya, oc, os_, ow, wmg, wuc, wun, wo, g1, b1, wr, br, tri, alpha):
    N, D = x.shape
    B, _, T = oc.shape
    tm = tri.shape[0]
    nt = T // tm
    tok = lambda w: pl.BlockSpec((tm, w), lambda i: (i, 0))
    feat = pl.BlockSpec((1, NSA_WIDTH, tm), lambda i: (i // nt, 0, i % nt))
    full = lambda a: pl.BlockSpec(a.shape, lambda i: (0,) * a.ndim)
    return pl.pallas_call(
        functools.partial(_merge_kernel, alpha=alpha),
        grid=(N // tm,),
        in_specs=[tok(D), tok(CONV_CH), feat, feat, feat,
                  full(wmg), full(wuc), full(wun), full(wo), full(g1), full(b1), full(wr), full(br), full(tri)],
        out_specs=[tok(D), pl.BlockSpec((tm * ROW_SUB, LANES), lambda i: (i, 0)),
                   pl.BlockSpec((2 * TOP_K, tm), lambda i: (0, i)), pl.BlockSpec((2 * TOP_K, tm), lambda i: (0, i)),
                   pl.BlockSpec((N_EXPERTS, LANES), lambda i: (0, 0))],
        out_shape=[jax.ShapeDtypeStruct((N, D), F32),
                   jax.ShapeDtypeStruct((N * ROW_SUB, LANES), F32),
                   jax.ShapeDtypeStruct((2 * TOP_K, N), I32),
                   jax.ShapeDtypeStruct((2 * TOP_K, N), F32),
                   jax.ShapeDtypeStruct((N_EXPERTS, LANES), I32)],
        scratch_shapes=[pltpu.VMEM((N_EXPERTS, 1), F32)],
        compiler_params=pltpu.CompilerParams(dimension_semantics=("arbitrary",)),
        name="merge",
    )(x, ya, oc, os_, ow, wmg, wuc, wun, wo, g1, b1, wr, br, tri)


def _dispatch_kernel(dest_ref, cnt_ref, pstart_ref, pend_ref, x_ref, xs_hbm, zero_ref, sem, zsem):
    i = pl.program_id(0)
    n = dest_ref.shape[0]

    @pl.when(i == 0)
    def _():
        zero_ref[...] = jnp.zeros_like(zero_ref)
        for e in range(N_EXPERTS):
            lo = pstart_ref[e] + cnt_ref[e]
            hi = pend_ref[e]

            def zstart(r, c):
                _tile_copy(zero_ref, 0, xs_hbm, r, zsem).start()
                return c

            def zwait(r, c):
                _tile_copy(zero_ref, 0, xs_hbm, r, zsem).wait()
                return c

            lax.fori_loop(lo, hi, zstart, 0)
            lax.fori_loop(lo, hi, zwait, 0)

    def start(t, c):
        for k in range(TOP_K):
            _tile_copy(x_ref, t, xs_hbm, dest_ref[t * TOP_K + k], sem).start()
        return c

    def wait(t, c):
        for k in range(TOP_K):
            _tile_copy(x_ref, 0, xs_hbm, 0, sem).wait()
        return c

    lax.fori_loop(0, n // TOP_K, start, 0)
    lax.fori_loop(0, n // TOP_K, wait, 0)


def _dispatch(dest, cnt, pstart, pend, x1t, n_rows):
    N = x1t.shape[0] // ROW_SUB
    tm = min(DISP_TM, N)
    smem = pl.BlockSpec(memory_space=pltpu.SMEM)
    return pl.pallas_call(
        _dispatch_kernel,
        grid=(N // tm,),
        in_specs=[pl.BlockSpec((tm * TOP_K,), lambda i: (i,), memory_space=pltpu.SMEM), smem, smem, smem,
                  pl.BlockSpec((tm * ROW_SUB, LANES), lambda i: (i, 0))],
        out_specs=pl.BlockSpec(memory_space=pl.ANY),
        out_shape=jax.ShapeDtypeStruct((n_rows * ROW_SUB, LANES), F32),
        scratch_shapes=[pltpu.VMEM((ROW_SUB, LANES), F32), pltpu.SemaphoreType.DMA(()), pltpu.SemaphoreType.DMA(())],
        compiler_params=pltpu.CompilerParams(dimension_semantics=("arbitrary",)),
        name="dispatch",
    )(dest, cnt, pstart, pend, x1t)


def _experts_kernel(be_ref, nu_ref, xs_ref, wgu_ref, perm_ref, bg_ref, bl_ref, wd_ref, bd_ref, ys_ref, wg_s, wl_s, wd_s):
    i = pl.program_id(0)

    @pl.when((i == 0) | (be_ref[i] != be_ref[jnp.maximum(i, 1) - 1]))
    def _():
        w = perm_ref.shape[0]
        for c in range(2 * D_FF // w):
            t = _dot(wgu_ref[0, :, c * w:(c + 1) * w].astype(BF16), perm_ref[...])
            wg_s[:, c * (w // 2):(c + 1) * (w // 2)] = t[:, :w // 2].astype(BF16)
            wl_s[:, c * (w // 2):(c + 1) * (w // 2)] = t[:, w // 2:].astype(BF16)
        wd_s[...] = wd_ref[0].astype(BF16)

    @pl.when(i < nu_ref[0])
    def _():
        xb = _load_row_tiles(xs_ref).astype(BF16)
        x_glu = jnp.minimum(_dot(xb, wg_s[...]) + bg_ref[0], SWIGLU_LIMIT)
        x_lin = jnp.clip(_dot(xb, wl_s[...]) + bl_ref[0], -SWIGLU_LIMIT, SWIGLU_LIMIT)
        act = x_glu * jax.nn.sigmoid(SWIGLU_ALPHA * x_glu) * (x_lin + 1.0)
        _store_row_tiles(ys_ref, _dot(act.astype(BF16), wd_s[...]) + bd_ref[0])


def _experts(block_expert, n_used, xs, wgu, perm, bg, bl, wd, bd):
    P = xs.shape[0] // ROW_SUB
    D = D_MODEL
    n_blocks = P // MOE_BLK
    rows = lambda i, be, nu: (jnp.minimum(i, nu[0] - 1), 0)
    wsel = lambda i, be, nu: (be[i], 0, 0)
    return pl.pallas_call(
        _experts_kernel,
        grid_spec=pltpu.PrefetchScalarGridSpec(
            num_scalar_prefetch=2,
            grid=(n_blocks,),
            in_specs=[pl.BlockSpec((MOE_BLK * ROW_SUB, LANES), rows),
                      pl.BlockSpec((1, D, 2 * D_FF), wsel),
                      pl.BlockSpec(perm.shape, lambda i, be, nu: (0, 0)),
                      pl.BlockSpec((1, 1, D_FF), wsel), pl.BlockSpec((1, 1, D_FF), wsel),
                      pl.BlockSpec((1, D_FF, D), wsel), pl.BlockSpec((1, 1, D), wsel)],
            out_specs=pl.BlockSpec((MOE_BLK * ROW_SUB, LANES), rows),
            scratch_shapes=[pltpu.VMEM((D, D_FF), BF16), pltpu.VMEM((D, D_FF), BF16), pltpu.VMEM((D_FF, D), BF16)]),
        out_shape=jax.ShapeDtypeStruct((P * ROW_SUB, LANES), F32),
        compiler_params=pltpu.CompilerParams(dimension_semantics=("arbitrary",)),
        name="experts",
    )(block_expert, n_used, xs, wgu, perm, bg, bl, wd, bd)


def _combine_kernel(dest_ref, ys_hbm, x1_ref, rg_ref, g2_ref, b2_ref, o_ref, buf, sem, *, alpha):
    tm = o_ref.shape[0]

    def start(t, c):
        for k in range(TOP_K):
            _tile_copy(ys_hbm, dest_ref[t * TOP_K + k], buf.at[k], t, sem).start()
        return c

    def wait(t, c):
        for k in range(TOP_K):
            _tile_copy(ys_hbm, 0, buf.at[k], t, sem).wait()
        return c

    lax.fori_loop(0, tm, start, 0)
    lax.fori_loop(0, tm, wait, 0)
    gate = rg_ref[...]
    y = gate[:, 0:1] * _load_row_tiles(buf.at[0])
    for k in range(1, TOP_K):
        y = y + gate[:, k:k + 1] * _load_row_tiles(buf.at[k])
    o_ref[...] = _layer_norm(alpha * x1_ref[...] + y, g2_ref[...], b2_ref[...])


def _combine(dest, ys, x1, rg, g2, b2, alpha):
    N, D = x1.shape
    tm = min(COMB_TM, N)
    tok = lambda w: pl.BlockSpec((tm, w), lambda i: (i, 0))
    full = lambda a: pl.BlockSpec(a.shape, lambda i: (0,) * a.ndim)
    return pl.pallas_call(
        functools.partial(_combine_kernel, alpha=alpha),
        grid=(N // tm,),
        in_specs=[pl.BlockSpec((tm * TOP_K,), lambda i: (i,), memory_space=pltpu.SMEM),
                  pl.BlockSpec(memory_space=pl.ANY), tok(D), tok(rg.shape[1]), full(g2), full(b2)],
        out_specs=tok(D),
        out_shape=jax.ShapeDtypeStruct((N, D), F32),
        scratch_shapes=[pltpu.VMEM((TOP_K, tm * ROW_SUB, LANES), F32), pltpu.SemaphoreType.DMA(())],
        compiler_params=pltpu.CompilerParams(dimension_semantics=("arbitrary",)),
        name="combine",
    )(dest, ys, x1, rg, g2, b2)


def _rope_freq():
    half = ROPE_DIM // 2
    inv = (np.float32(ROPE_THETA) ** (-np.arange(half, dtype=np.float32) * np.float32(2.0 / ROPE_DIM))).astype(np.float32)
    return inv[:, None]


def _overlap_t(T):
    nc = T // CMP_STRIDE
    c0 = np.arange(nc) * CMP_STRIDE
    j0 = np.arange(SEL_LANES) * SEL_BLOCK
    ov = (c0[None, :] < j0[:, None] + SEL_BLOCK) & (c0[None, :] + CMP_BLOCK > j0[:, None])
    ov &= (np.arange(nc) < nc - 1)[None, :] & (np.arange(SEL_LANES) < T // SEL_BLOCK)[:, None]
    return ov.astype(np.float32)


def _deinterleave_perm():
    w = 2 * LANES
    p = np.zeros((w, w), np.float32)
    p[np.arange(0, w, 2), np.arange(w // 2)] = 1.0
    p[np.arange(1, w, 2), w // 2 + np.arange(w // 2)] = 1.0
    return p


def _token_major(t, B, T):
    return t.reshape(B, N_KV_GROUPS, HEAD_DIM, T).transpose(0, 1, 3, 2)


def _layer(x, positions, w_in, conv_w, cmp_pos_k, cmp_w1_k, cmp_w2_k, cmp_pos_v, cmp_w1_v, cmp_w2_v,
           w_up_conv, w_up_nsa, w_o, ln1_g, ln1_b, w_router, b_router, w_gate_up, b_gate_up,
           w_down, b_down, ln2_g, ln2_b, alpha):
    B, T, D = x.shape
    G, R = N_KV_GROUPS, HEADS_PER_GROUP
    N = B * T
    assert D == D_MODEL and T % SEL_BLOCK == 0 and T // SEL_BLOCK <= SEL_LANES

    c0 = 3 * CONV_CH
    c1 = c0 + NSA_WIDTH
    c2 = c1 + 6 * KV_WIDTH
    c3 = c2 + 3 * N_HEADS
    wc = w_in[:, :c0].astype(BF16)
    wqt = w_in[:, c0:c1].T.astype(BF16)
    wkvt = w_in[:, c1:c2].T.astype(BF16)
    gcols = np.zeros((G * GATE_ROWS,), np.int64)
    gmask = np.zeros((G * GATE_ROWS,), np.float32)
    for g in range(G):
        for br in range(3):
            for r in range(R):
                gcols[g * GATE_ROWS + br * R + r] = br * N_HEADS + g * R + r
                gmask[g * GATE_ROWS + br * R + r] = 1.0
    wgt = (w_in[:, c2:c3][:, gcols] * gmask).T.astype(BF16)
    wmg = w_in[:, c3:].astype(BF16)
    cw = conv_w.reshape(CONV_K, CONV_CH)
    pos_row = positions.astype(F32)[:, None, :]

    ya_pre, qt, kvt, gates = _proj(x, pos_row, wc, wqt, wkvt, wgt, cw, jnp.asarray(_rope_freq()))

    nc = T // CMP_STRIDE
    to_chunks = lambda t: _token_major(t, B, T).reshape(B * G, nc, CMP_STRIDE * HEAD_DIM)
    xin = jnp.stack([to_chunks(kvt[:, 0:KV_WIDTH]), to_chunks(kvt[:, KV_WIDTH:2 * KV_WIDTH])])
    w1 = jnp.stack([cmp_w1_k, cmp_w1_v])
    w2 = jnp.stack([cmp_w2_k, cmp_w2_v])
    pos_flat = jnp.stack([cmp_pos_k.reshape(1, -1), cmp_pos_v.reshape(1, -1)])
    pos_flat = jnp.pad(pos_flat, ((0, 0), (0, 7), (0, 0)))
    kcmp, kcmp_t = _compress(xin, w1, w2, w2.transpose(0, 2, 1), pos_flat)

    o_c, sel = _cmp_attn(qt, kcmp, kcmp_t, gates, jnp.asarray(_overlap_t(T)))

    ks = _token_major(kvt[:, 2 * KV_WIDTH:3 * KV_WIDTH], B, T)
    kw = _token_major(kvt[:, 4 * KV_WIDTH:5 * KV_WIDTH], B, T)
    onehot = (np.arange(T)[:, None] // SEL_BLOCK == np.arange(SEL_LANES)[None, :]).astype(np.float32)
    kaug = jnp.concatenate([ks, jnp.broadcast_to(jnp.asarray(onehot, BF16), (B, G, T, SEL_LANES))], axis=-1)
    o_s = _sel_attn(qt, sel, kaug, kvt, 3 * G, gates)
    o_w = _win_attn(qt, kw, kvt, 5 * G, gates)

    tm = min(MERGE_TM, T)
    tri = jnp.asarray(np.triu(np.ones((tm, tm), np.float32), 1), BF16)
    x1, x1t, ri, rg, cnt = _merge(x.reshape(N, D), ya_pre.reshape(N, CONV_CH), o_c, o_s, o_w, wmg,
                             w_up_conv.astype(BF16), w_up_nsa.astype(BF16), w_o.astype(BF16),
                             ln1_g[None, :], ln1_b[None, :], w_router.T, b_router[:, None], tri, alpha)

    counts = cnt[:, 0]
    padded = (counts + MOE_BLK - 1) // MOE_BLK * MOE_BLK
    pend = jnp.cumsum(padded).astype(I32)
    pstart = pend - padded
    idx = ri[:TOP_K]
    pstart_sel = jnp.zeros_like(idx)
    for e in range(N_EXPERTS):
        pstart_sel = jnp.where(idx == e, pstart[e], pstart_sel)
    dest = (pstart_sel + ri[TOP_K:]).T.reshape(N * TOP_K)
    n_blocks = (N * TOP_K) // MOE_BLK + N_EXPERTS
    blk_start = jnp.arange(n_blocks, dtype=I32) * MOE_BLK
    block_expert = jnp.minimum(jnp.sum((blk_start[:, None] >= pend[None, :]).astype(I32), axis=1), N_EXPERTS - 1)
    n_used = (pend[-1:] // MOE_BLK).astype(I32)

    xs = _dispatch(dest, counts, pstart, pend, x1t, n_blocks * MOE_BLK)
    ys = _experts(block_expert, n_used, xs, w_gate_up, jnp.asarray(_deinterleave_perm(), BF16),
                  b_gate_up[:, None, 0::2], b_gate_up[:, None, 1::2], w_down, b_down[:, None, :])
    out = _combine(dest, ys, x1, rg.T, ln2_g[None, :], ln2_b[None, :], alpha)
    return out.reshape(B, T, D)


def kernel(x, positions, w_in, conv_w, cmp_pos_k, cmp_w1_k, cmp_w2_k, cmp_pos_v, cmp_w1_v, cmp_w2_v, w_up_conv, w_up_nsa, w_o, ln1_g, ln1_b, w_router, b_router, w_gate_up, b_gate_up, w_down, b_down, ln2_g, ln2_b):
    depth = w_in.shape[0]
    alpha = float((2 * depth) ** 0.25)
    h = x
    for l in range(depth):
        h = _layer(h, positions, w_in[l], conv_w[l], cmp_pos_k[l], cmp_w1_k[l], cmp_w2_k[l],
                   cmp_pos_v[l], cmp_w1_v[l], cmp_w2_v[l], w_up_conv[l], w_up_nsa[l], w_o[l],
                   ln1_g[l], ln1_b[l], w_router[l], b_router[l], w_gate_up[l], b_gate_up[l],
                   w_down[l], b_down[l], ln2_g[l], ln2_b[l], alpha)
    return h
```

```python
import functools

import numpy as np
import jax
import jax.numpy as jnp
from jax import lax
from jax.experimental import pallas as pl
from jax.experimental.pallas import tpu as pltpu

F32 = jnp.float32
BF16 = jnp.bfloat16
I32 = jnp.int32

D_MODEL = 1024
CONV_CH = 512
CONV_K = 3
N_HEADS = 8
N_KV_GROUPS = 2
HEADS_PER_GROUP = N_HEADS // N_KV_GROUPS
HEAD_DIM = 64
NSA_WIDTH = N_HEADS * HEAD_DIM
KV_WIDTH = N_KV_GROUPS * HEAD_DIM
ROPE_DIM = HEAD_DIM // 4
ROPE_THETA = 500000.0
CMP_BLOCK = 32
CMP_STRIDE = 16
CMP_HIDDEN = 256
SEL_BLOCK = 64
N_SELECT = 16
WINDOW = 512
N_EXPERTS = 32
TOP_K = 4
D_FF = 1024
SWIGLU_LIMIT = 7.0
SWIGLU_ALPHA = 1.702
LN_EPS = 1e-5
NEG_INF = -1e30
FORCE_SCORE = 1e4
LOG2E = 1.4426950408889634

LANES = 128
ROW_SUB = D_MODEL // LANES
SEL_LANES = 64
SEL_MASK_BIAS = -32768.0
GROUP_W = HEADS_PER_GROUP * HEAD_DIM
GATE_ROWS = 16

PROJ_TM = 512
CMP_TQ = 256
SEL_TQ = 256
SEL_KC = 512
SEL_SPLIT = 2
WIN_TQ = 512
WIN_SUB = 128
MERGE_TM = 512
MERGE_SUB = 256
MOE_BLK = 512
DISP_TM = 512
COMB_TM = 256


def _dot(a, b, precision=None):
    return jnp.dot(a, b, precision=precision, preferred_element_type=F32)


def _dot_nt(a, b, precision=None):
    return lax.dot_general(a, b, (((1,), (1,)), ((), ())), precision=precision, preferred_element_type=F32)


def _proj_kernel(x_ref, pos_ref, wc_ref, wqt_ref, wkvt_ref, wgt_ref, cw_ref, freq_ref,
                 ya_ref, qt_ref, kvt_ref, gt_ref, carry_ref):
    ti = pl.program_id(1)
    tm = x_ref.shape[1]
    xb = x_ref[0].astype(BF16)

    pc = _dot(xb, wc_ref[...])
    xv = pc[:, :CONV_CH]
    bg = pc[:, CONV_CH:2 * CONV_CH]
    cg = pc[:, 2 * CONV_CH:]
    u = cg * xv

    @pl.when(ti == 0)
    def _():
        carry_ref[...] = jnp.zeros_like(carry_ref)

    prev = carry_ref[...]
    row = lax.broadcasted_iota(I32, u.shape, 0)
    u1 = jnp.where(row == 0, prev[7:8], pltpu.roll(u, 1, 0))
    u2 = jnp.where(row == 0, prev[6:7], jnp.where(row == 1, prev[7:8], pltpu.roll(u, 2, 0)))
    cw = cw_ref[...]
    conv = cw[2:3] * u + cw[1:2] * u1 + cw[0:1] * u2
    carry_ref[...] = u[tm - 8:]
    ya_ref[0] = (bg * conv).astype(BF16)

    ang = freq_ref[...] * pos_ref[0]
    cos = jnp.cos(ang)
    sin = jnp.sin(ang)
    half = ROPE_DIM // 2

    def rope_head(t):
        t1 = t[:half]
        t2 = t[half:ROPE_DIM]
        return [t1 * cos - t2 * sin, t2 * cos + t1 * sin, t[ROPE_DIM:]]

    def heads(t, rotate):
        out = []
        for h in range(t.shape[0] // HEAD_DIM):
            th = t[h * HEAD_DIM:(h + 1) * HEAD_DIM]
            out.extend(rope_head(th) if rotate(h) else [th])
        return jnp.concatenate(out, axis=0)

    qt = heads(_dot_nt(wqt_ref[...], xb), lambda h: True)
    qt_ref[0] = (qt * (HEAD_DIM ** -0.5 * LOG2E)).astype(BF16)
    kvt = heads(_dot_nt(wkvt_ref[...], xb), lambda h: (h // N_KV_GROUPS) % 2 == 0)
    kvt_ref[0] = kvt.astype(BF16)
    gt_ref[0] = jax.nn.sigmoid(_dot_nt(wgt_ref[...], xb))


def _proj(x, pos_row, wc, wqt, wkvt, wgt, cw, freq):
    B, T, D = x.shape
    tm = min(PROJ_TM, T)
    grid = (B, T // tm)
    full = lambda a: pl.BlockSpec(a.shape, lambda b, t: (0,) * a.ndim)
    tok = lambda w: pl.BlockSpec((1, tm, w), lambda b, t: (b, t, 0))
    feat = lambda r: pl.BlockSpec((1, r, tm), lambda b, t: (b, 0, t))
    n_gate = wgt.shape[0]
    return pl.pallas_call(
        _proj_kernel,
        grid=grid,
        in_specs=[tok(D), feat(1), full(wc), full(wqt), full(wkvt), full(wgt), full(cw), full(freq)],
        out_specs=[tok(CONV_CH), feat(NSA_WIDTH), feat(6 * KV_WIDTH), feat(n_gate)],
        out_shape=[jax.ShapeDtypeStruct((B, T, CONV_CH), BF16),
                   jax.ShapeDtypeStruct((B, NSA_WIDTH, T), BF16),
                   jax.ShapeDtypeStruct((B, 6 * KV_WIDTH, T), BF16),
                   jax.ShapeDtypeStruct((B, n_gate, T), F32)],
        scratch_shapes=[pltpu.VMEM((8, CONV_CH), F32)],
        compiler_params=pltpu.CompilerParams(dimension_semantics=("arbitrary", "arbitrary")),
        name="proj",
    )(x, pos_row, wc, wqt, wkvt, wgt, cw, freq)


def _compress_kernel(x_ref, w1_ref, w2_ref, w2t_ref, pos_ref, o_ref, ot_ref):
    xb = x_ref[0, 0]
    nc = xb.shape[0]
    w1 = w1_ref[0]
    w1b = w1.astype(BF16)
    half = CMP_STRIDE * HEAD_DIM
    a = _dot(xb, w1b[:half])
    b = _dot(xb, w1b[half:])
    b_next = pltpu.roll(b, nc - 1, 0)
    pb = _dot(pos_ref[0], w1, precision=lax.Precision.HIGHEST)[0:1]
    h = a + b_next + pb
    g = (0.5 * h * (1.0 + jnp.tanh(np.sqrt(2.0 / np.pi) * (h + 0.044715 * (h * h * h))))).astype(BF16)
    o_ref[0, 0] = _dot(g, w2_ref[0].astype(BF16)).astype(BF16)
    ot_ref[0, 0] = _dot_nt(w2t_ref[0].astype(BF16), g).astype(BF16)


def _compress(xin, w1, w2, w2t, pos):
    _, BG, nc, W = xin.shape
    per = lambda a: pl.BlockSpec((1,) + a.shape[1:], lambda s, i: (s, 0, 0))
    return pl.pallas_call(
        _compress_kernel,
        grid=(2, BG),
        in_specs=[pl.BlockSpec((1, 1, nc, W), lambda s, i: (s, i, 0, 0)), per(w1), per(w2), per(w2t), per(pos)],
        out_specs=[pl.BlockSpec((1, 1, nc, HEAD_DIM), lambda s, i: (s, i, 0, 0)),
                   pl.BlockSpec((1, 1, HEAD_DIM, nc), lambda s, i: (s, i, 0, 0))],
        out_shape=[jax.ShapeDtypeStruct((2, BG, nc, HEAD_DIM), BF16),
                   jax.ShapeDtypeStruct((2, BG, HEAD_DIM, nc), BF16)],
        name="compress",
    )(xin, w1, w2, w2t, pos)


def _head_lanes(qt):
    return jnp.concatenate([qt[r * HEAD_DIM:(r + 1) * HEAD_DIM] for r in range(HEADS_PER_GROUP)], axis=1)


def _gated_out(ot, scale, gates, branch, tq):
    rows = []
    for r in range(HEADS_PER_GROUP):
        c = branch * HEADS_PER_GROUP + r
        sl = slice(r * tq, (r + 1) * tq)
        rows.append(ot[:, sl] * (scale[:, sl] * gates[c:c + 1]))
    return jnp.concatenate(rows, axis=0).astype(BF16)


def _attn_specs(tq):
    G = N_KV_GROUPS
    qspec = pl.BlockSpec((1, GROUP_W, tq), lambda b, g, i: (b, g, i))
    gspec = pl.BlockSpec((1, GATE_ROWS, tq), lambda b, g, i: (b, g, i))
    bg4 = lambda a: pl.BlockSpec((1, 1) + a.shape[2:], lambda b, g, i: (b, g, 0, 0))
    return G, qspec, gspec, bg4


def _cmp_attn_kernel(q_ref, kc_ref, vct_ref, g_ref, ovt_ref, o_ref, sel_ref):
    qi = pl.program_id(2)
    tq = q_ref.shape[2]
    q0 = qi * tq
    qt = _head_lanes(q_ref[0])
    s = _dot(kc_ref[0, 0], qt)
    c = lax.broadcasted_iota(I32, s.shape, 0)
    t = q0 + (lax.broadcasted_iota(I32, s.shape, 1) & (tq - 1))
    valid = c * CMP_STRIDE + (CMP_BLOCK - 1) <= t
    sm = jnp.where(valid, s, NEG_INF)
    m = jnp.max(sm, axis=0, keepdims=True)
    p = jnp.where(valid, jnp.exp2(sm - m), 0.0)
    l = jnp.sum(p, axis=0, keepdims=True)
    inv = 1.0 / jnp.where(l > 0.0, l, 1.0)
    ot = _dot(vct_ref[0, 0], p.astype(BF16))
    o_ref[0] = _gated_out(ot, inv, g_ref[0], 0, tq)

    pn = p * inv
    psum = pn[:, 0:tq]
    for r in range(1, HEADS_PER_GROUP):
        psum = psum + pn[:, r * tq:(r + 1) * tq]
    imp = _dot(ovt_ref[...], psum, precision=lax.Precision.HIGHEST)
    j = lax.broadcasted_iota(I32, imp.shape, 0)
    cur = (q0 + lax.broadcasted_iota(I32, imp.shape, 1)) // SEL_BLOCK
    valid_b = j <= cur
    forced = (j == 0) | (j == cur) | (j == cur - 1)
    score = jnp.where(valid_b, jnp.where(forced, FORCE_SCORE, imp), NEG_INF)
    sub = 8
    groups = [score[a:a + sub] for a in range(0, SEL_LANES, sub)]
    jrow = lax.broadcasted_iota(I32, groups[0].shape, 0)
    ranks = [jnp.zeros(g_.shape, I32) for g_ in groups]
    for i in range(SEL_LANES):
        si = score[i:i + 1, :]
        for a, g_ in enumerate(groups):
            if a > i // sub:
                inc = jnp.where(si >= g_, 1, 0)
            elif a < i // sub:
                inc = jnp.where(si > g_, 1, 0)
            else:
                inc = jnp.where(jrow > i % sub, jnp.where(si >= g_, 1, 0), jnp.where(si > g_, 1, 0))
            ranks[a] = ranks[a] + inc
    rank = jnp.concatenate(ranks, axis=0)
    selected = (rank < N_SELECT) & valid_b
    sel_ref[0, 0] = jnp.where(selected, 0.0, SEL_MASK_BIAS).astype(BF16)


def _cmp_attn(qt, kcmp, vcmp_t, gates, ovt):
    B, _, T = qt.shape
    tq = min(CMP_TQ, T)
    G, qspec, gspec, _ = _attn_specs(tq)
    cspec = lambda a: pl.BlockSpec((1, 1) + a.shape[2:], lambda b, g, i: (0, b * G + g, 0, 0))
    vspec = lambda a: pl.BlockSpec((1, 1) + a.shape[2:], lambda b, g, i: (1, b * G + g, 0, 0))
    return pl.pallas_call(
        _cmp_attn_kernel,
        grid=(B, G, T // tq),
        in_specs=[qspec, cspec(kcmp), vspec(vcmp_t), gspec, pl.BlockSpec(ovt.shape, lambda b, g, i: (0, 0))],
        out_specs=[qspec, pl.BlockSpec((1, 1, SEL_LANES, tq), lambda b, g, i: (b, g, 0, i))],
        out_shape=[jax.ShapeDtypeStruct((B, NSA_WIDTH, T), BF16),
                   jax.ShapeDtypeStruct((B, G, SEL_LANES, T), BF16)],
        name="cmp_attn",
    )(qt, kcmp, vcmp_t, gates, ovt)


def _sel_attn_kernel(q_ref, sel_ref, k_ref, vt_ref, g_ref, o_ref, *, kc):
    qi = pl.program_id(2)
    tq = q_ref.shape[2]
    q0 = qi * tq
    qt = q_ref[0]
    sb = sel_ref[0, 0]
    qa = jnp.concatenate(
        [jnp.concatenate([qt[r * HEAD_DIM:(r + 1) * HEAD_DIM], sb], axis=0) for r in range(HEADS_PER_GROUP)],
        axis=1)
    gw = qa.shape[1] // SEL_SPLIT
    qas = [qa[:, i * gw:(i + 1) * gw] for i in range(SEL_SPLIT)]

    def step(kstart, kn, carry, diagonal):
        kblk = k_ref[0, 0, pl.ds(kstart, kn), :]
        vblk = vt_ref[0, :, pl.ds(kstart, kn)]
        ss = [_dot(kblk, qg) for qg in qas]
        out = []
        for gi, (s, (m, l, acc)) in enumerate(zip(ss, carry)):
            if diagonal:
                row = lax.broadcasted_iota(I32, s.shape, 0)
                off = (gi * gw + lax.broadcasted_iota(I32, s.shape, 1)) & (tq - 1)
                s = jnp.where(row <= off, s, NEG_INF)
            mn = jnp.maximum(m, jnp.max(s, axis=0, keepdims=True))
            alpha = jnp.exp2(m - mn)
            p = jnp.exp2(s - mn)
            l = alpha * l + jnp.sum(p, axis=0, keepdims=True)
            acc = alpha * acc + _dot(vblk, p.astype(BF16))
            out.append((mn, l, acc))
        return tuple(out)

    init = tuple((jnp.full((1, gw), NEG_INF, F32), jnp.zeros((1, gw), F32), jnp.zeros((HEAD_DIM, gw), F32))
                 for _ in range(SEL_SPLIT))
    n_full = q0 // kc
    carry = lax.fori_loop(0, n_full, lambda i, cr: step(pl.multiple_of(i * kc, kc), kc, cr, False), init)
    carry = lax.fori_loop(n_full * (kc // tq), qi, lambda i, cr: step(pl.multiple_of(i * tq, tq), tq, cr, False), carry)
    carry = step(pl.multiple_of(q0, tq), tq, carry, True)
    l = jnp.concatenate([c[1] for c in carry], axis=1)
    acc = jnp.concatenate([c[2] for c in carry], axis=1)
    o_ref[0] = _gated_out(acc, 1.0 / l, g_ref[0], 1, tq)


def _sel_attn(qt, sel, kaug, kvt, v_row_block, gates):
    B, _, T = qt.shape
    tq = min(SEL_TQ, T)
    kc = min(SEL_KC, T)
    G, qspec, gspec, bg4 = _attn_specs(tq)
    return pl.pallas_call(
        functools.partial(_sel_attn_kernel, kc=kc),
        grid=(B, G, T // tq),
        in_specs=[qspec, pl.BlockSpec((1, 1, SEL_LANES, tq), lambda b, g, i: (b, g, 0, i)), bg4(kaug),
                  pl.BlockSpec((1, HEAD_DIM, T), lambda b, g, i: (b, v_row_block + g, 0)), gspec],
        out_specs=qspec,
        out_shape=jax.ShapeDtypeStruct((B, NSA_WIDTH, T), BF16),
        name="sel_attn",
    )(qt, sel, kaug, kvt, gates)


def _win_attn_kernel(q_ref, k_ref, vt_ref, g_ref, o_ref, *, span, sub):
    qi = pl.program_id(2)
    tq = q_ref.shape[2]
    T = k_ref.shape[2]

    def tile(s0, interior):
        cs = slice(s0, s0 + sub)
        q0 = qi * tq + s0
        start = pl.multiple_of(jnp.clip(q0 + sub - span, 0, T - span), sub)
        qt = _head_lanes(q_ref[0, :, cs])
        s = _dot(k_ref[0, 0, pl.ds(start, span), :], qt)
        if interior:
            row = lax.broadcasted_iota(I32, (sub, s.shape[1]), 0)
            off = lax.broadcasted_iota(I32, (sub, s.shape[1]), 1) & (sub - 1)
            s = jnp.concatenate([jnp.where(row > off, s[:sub], NEG_INF), s[sub:span - sub],
                                 jnp.where(row <= off, s[span - sub:], NEG_INF)], axis=0)
        else:
            kp = start + lax.broadcasted_iota(I32, s.shape, 0)
            t = q0 + (lax.broadcasted_iota(I32, s.shape, 1) & (sub - 1))
            diff = t - kp
            s = jnp.where((diff >= 0) & (diff < WINDOW), s, NEG_INF)
        m = jnp.max(s, axis=0, keepdims=True)
        p = jnp.exp2(s - m)
        l = jnp.sum(p, axis=0, keepdims=True)
        ot = _dot(vt_ref[0, :, pl.ds(start, span)], p.astype(BF16))
        o_ref[0, :, cs] = _gated_out(ot, 1.0 / l, g_ref[0, :, cs], 2, sub)

    first_interior = -(-WINDOW // tq)
    if span == WINDOW + sub:
        @pl.when(qi >= first_interior)
        def _():
            for s0 in range(0, tq, sub):
                tile(s0, True)

    @pl.when((qi < first_interior) | (span != WINDOW + sub))
    def _():
        for s0 in range(0, tq, sub):
            tile(s0, False)


def _win_attn(qt, k, kvt, v_row_block, gates):
    B, _, T = qt.shape
    tq = min(WIN_TQ, T)
    sub = min(WIN_SUB, T)
    span = min(WINDOW + sub, T)
    G, qspec, gspec, bg4 = _attn_specs(tq)
    return pl.pallas_call(
        functools.partial(_win_attn_kernel, span=span, sub=sub),
        grid=(B, G, T // tq),
        in_specs=[qspec, bg4(k), pl.BlockSpec((1, HEAD_DIM, T), lambda b, g, i: (b, v_row_block + g, 0)), gspec],
        out_specs=qspec,
        out_shape=jax.ShapeDtypeStruct((B, NSA_WIDTH, T), BF16),
        name="win_attn",
    )(qt, k, kvt, gates)


def _store_row_tiles(ref, x, row0=0):
    rows = x.shape[0]
    for s in range(ROW_SUB):
        ref[pl.ds(row0 * ROW_SUB + s, rows, stride=ROW_SUB), :] = x[:, s * LANES:(s + 1) * LANES]


def _load_row_tiles(ref):
    rows = ref.shape[0] // ROW_SUB
    return jnp.concatenate([ref[pl.ds(s, rows, stride=ROW_SUB), :] for s in range(ROW_SUB)], axis=1)


def _tile_copy(src, si, dst, di, sem):
    return pltpu.make_async_copy(src.at[pl.ds(pl.multiple_of(si * ROW_SUB, ROW_SUB), ROW_SUB), :],
                                 dst.at[pl.ds(pl.multiple_of(di * ROW_SUB, ROW_SUB), ROW_SUB), :], sem)


def _layer_norm(h, g, b):
    mu = jnp.mean(h, axis=-1, keepdims=True)
    c = h - mu
    var = jnp.mean(c * c, axis=-1, keepdims=True)
    return c * lax.rsqrt(var + LN_EPS) * g + b


def _merge_kernel(x_ref, ya_ref, oc_ref, os_ref, ow_ref, wmg_ref, wuc_ref, wun_ref, wo_ref, g1_ref, b1_ref,
                  wr_ref, br_ref, tri_ref, x1_ref, x1t_ref, ri_ref, rg_ref, cnt_ref, carry_ref, *, alpha):
    i = pl.program_id(0)
    sub = tri_ref.shape[0]

    @pl.when(i == 0)
    def _():
        carry_ref[...] = jnp.zeros_like(carry_ref)

    wr = wr_ref[...]
    wr_hi = wr.astype(BF16)
    wr_hl = jnp.concatenate([wr_hi, (wr - wr_hi.astype(F32)).astype(BF16)], axis=0)
    total = carry_ref[...]
    for s0 in range(0, x_ref.shape[0], sub):
        rs = slice(s0, s0 + sub)
        x = x_ref[rs, :]
        mg = _dot(x.astype(BF16), wmg_ref[...])
        y_a = _dot(ya_ref[rs, :], wuc_ref[...])
        o_nsa_t = (oc_ref[0, :, rs].astype(F32) + os_ref[0, :, rs].astype(F32)
                   + ow_ref[0, :, rs].astype(F32))
        y_b = _dot(o_nsa_t.T.astype(BF16), wun_ref[...])
        merged = jax.nn.sigmoid(mg[:, :D_MODEL]) * y_a + jax.nn.sigmoid(mg[:, D_MODEL:]) * y_b
        h = alpha * x + _dot(merged.astype(BF16), wo_ref[...])
        x1 = _layer_norm(h, g1_ref[...], b1_ref[...])
        x1_ref[rs, :] = x1
        _store_row_tiles(x1t_ref, x1, s0)

        x_hi = x1.astype(BF16)
        x_lo = (x1 - x_hi.astype(F32)).astype(BF16)
        both = _dot_nt(wr_hl, x_hi)
        logits = both[:N_EXPERTS] + both[N_EXPERTS:] + _dot_nt(wr_hi, x_lo) + br_ref[...]
        expert = lax.broadcasted_iota(I32, logits.shape, 0).astype(F32)
        rem = logits
        vals, idxs, hots = [], [], []
        for _ in range(TOP_K):
            m = jnp.max(rem, axis=0, keepdims=True)
            idx = jnp.min(jnp.where(rem == m, expert, float(N_EXPERTS)), axis=0, keepdims=True)
            hot = expert == idx
            vals.append(m)
            idxs.append(idx)
            hots.append(hot)
            rem = jnp.where(hot, -3.0e38, rem)
        es = [jnp.exp(v - vals[0]) for v in vals]
        den = es[0]
        for e in es[1:]:
            den = den + e
        chosen = hots[0]
        for hot in hots[1:]:
            chosen = chosen | hot
        chosen_f = jnp.where(chosen, 1.0, 0.0)

        before = _dot(chosen_f.astype(BF16), tri_ref[...]) + total
        total = total + jnp.sum(chosen_f, axis=1, keepdims=True)
        ranks = [jnp.sum(jnp.where(hot, before, 0.0), axis=0, keepdims=True) for hot in hots]
        ri_ref[:, rs] = jnp.concatenate(idxs + ranks, axis=0).astype(I32)
        rg_ref[:, rs] = jnp.concatenate([e / den for e in es] + [jnp.zeros_like(den)] * TOP_K, axis=0)

    carry_ref[...] = total
    cnt_ref[...] = jnp.broadcast_to(total, cnt_ref.shape).astype(I32)


def _merge(x, ya, oc, os_, ow, wmg, wuc, wun, wo, g1, b1, wr, br, tri, alpha):
    N, D = x.shape
    B, _, T = oc.shape
    tm = min(MERGE_TM, T)
    nt = T // tm
    tok = lambda w: pl.BlockSpec((tm, w), lambda i: (i, 0))
    feat = pl.BlockSpec((1, NSA_WIDTH, tm), lambda i: (i // nt, 0, i % nt))
    full = lambda a: pl.BlockSpec(a.shape, lambda i: (0,) * a.ndim)
    return pl.pallas_call(
        functools.partial(_merge_kernel, alpha=alpha),
        grid=(N // tm,),
        in_specs=[tok(D), tok(CONV_CH), feat, feat, feat,
                  full(wmg), full(wuc), full(wun), full(wo), full(g1), full(b1), full(wr), full(br), full(tri)],
        out_specs=[tok(D), pl.BlockSpec((tm * ROW_SUB, LANES), lambda i: (i, 0)),
                   pl.BlockSpec((2 * TOP_K, tm), lambda i: (0, i)), pl.BlockSpec((2 * TOP_K, tm), lambda i: (0, i)),
                   pl.BlockSpec((N_EXPERTS, LANES), lambda i: (0, 0))],
        out_shape=[jax.ShapeDtypeStruct((N, D), F32),
                   jax.ShapeDtypeStruct((N * ROW_SUB, LANES), F32),
                   jax.ShapeDtypeStruct((2 * TOP_K, N), I32),
                   jax.ShapeDtypeStruct((2 * TOP_K, N), F32),
                   jax.ShapeDtypeStruct((N_EXPERTS, LANES), I32)],
        scratch_shapes=[pltpu.VMEM((N_EXPERTS, 1), F32)],
        compiler_params=pltpu.CompilerParams(dimension_semantics=("arbitrary",)),
        name="merge",
    )(x, ya, oc, os_, ow, wmg, wuc, wun, wo, g1, b1, wr, br, tri)


def _dispatch_kernel(dest_ref, cnt_ref, pstart_ref, pend_ref, x_ref, xs_hbm, zero_ref, sem, zsem):
    i = pl.program_id(0)
    n = dest_ref.shape[0]

    @pl.when(i == 0)
    def _():
        zero_ref[...] = jnp.zeros_like(zero_ref)
        for e in range(N_EXPERTS):
            lo = pstart_ref[e] + cnt_ref[e]
            hi = pend_ref[e]

            def zstart(r, c):
                _tile_copy(zero_ref, 0, xs_hbm, r, zsem).start()
                return c

            def zwait(r, c):
                _tile_copy(zero_ref, 0, xs_hbm, r, zsem).wait()
                return c

            lax.fori_loop(lo, hi, zstart, 0)
            lax.fori_loop(lo, hi, zwait, 0)

    def start(t, c):
        for k in range(TOP_K):
            _tile_copy(x_ref, t, xs_hbm, dest_ref[t * TOP_K + k], sem).start(priority=k % 2)
        return c

    def wait(t, c):
        for k in range(TOP_K):
            _tile_copy(x_ref, 0, xs_hbm, 0, sem).wait()
        return c

    lax.fori_loop(0, n // TOP_K, start, 0)
    lax.fori_loop(0, n // TOP_K, wait, 0)


def _dispatch(dest, cnt, pstart, pend, x1t, n_rows):
    N = x1t.shape[0] // ROW_SUB
    tm = min(DISP_TM, N)
    smem = pl.BlockSpec(memory_space=pltpu.SMEM)
    return pl.pallas_call(
        _dispatch_kernel,
        grid=(N // tm,),
        in_specs=[pl.BlockSpec((tm * TOP_K,), lambda i: (i,), memory_space=pltpu.SMEM), smem, smem, smem,
                  pl.BlockSpec((tm * ROW_SUB, LANES), lambda i: (i, 0))],
        out_specs=pl.BlockSpec(memory_space=pl.ANY),
        out_shape=jax.ShapeDtypeStruct((n_rows * ROW_SUB, LANES), F32),
        scratch_shapes=[pltpu.VMEM((ROW_SUB, LANES), F32), pltpu.SemaphoreType.DMA(()), pltpu.SemaphoreType.DMA(())],
        compiler_params=pltpu.CompilerParams(dimension_semantics=("arbitrary",)),
        name="dispatch",
    )(dest, cnt, pstart, pend, x1t)


def _experts_kernel(be_ref, nu_ref, xs_ref, wgu_ref, perm_ref, bg_ref, bl_ref, wd_ref, bd_ref, ys_ref, wg_s, wl_s, wd_s):
    i = pl.program_id(0)

    @pl.when((i == 0) | (be_ref[i] != be_ref[jnp.maximum(i, 1) - 1]))
    def _():
        w = perm_ref.shape[0]
        for c in range(2 * D_FF // w):
            t = _dot(wgu_ref[0, :, c * w:(c + 1) * w].astype(BF16), perm_ref[...])
            wg_s[:, c * (w // 2):(c + 1) * (w // 2)] = t[:, :w // 2].astype(BF16)
            wl_s[:, c * (w // 2):(c + 1) * (w // 2)] = t[:, w // 2:].astype(BF16)
        wd_s[...] = wd_ref[0].astype(BF16)

    @pl.when(i < nu_ref[0])
    def _():
        xb = _load_row_tiles(xs_ref).astype(BF16)
        x_glu = jnp.minimum(_dot(xb, wg_s[...]) + bg_ref[0], SWIGLU_LIMIT)
        x_lin = jnp.clip(_dot(xb, wl_s[...]) + bl_ref[0], -SWIGLU_LIMIT, SWIGLU_LIMIT)
        act = x_glu * jax.nn.sigmoid(SWIGLU_ALPHA * x_glu) * (x_lin + 1.0)
        _store_row_tiles(ys_ref, _dot(act.astype(BF16), wd_s[...]) + bd_ref[0])


def _experts(block_expert, n_used, xs, wgu, perm, bg, bl, wd, bd):
    P = xs.shape[0] // ROW_SUB
    D = D_MODEL
    n_blocks = P // MOE_BLK
    rows = lambda i, be, nu: (jnp.minimum(i, nu[0] - 1), 0)
    wsel = lambda i, be, nu: (be[i], 0, 0)
    return pl.pallas_call(
        _experts_kernel,
        grid_spec=pltpu.PrefetchScalarGridSpec(
            num_scalar_prefetch=2,
            grid=(n_blocks,),
            in_specs=[pl.BlockSpec((MOE_BLK * ROW_SUB, LANES), rows),
                      pl.BlockSpec((1, D, 2 * D_FF), wsel),
                      pl.BlockSpec(perm.shape, lambda i, be, nu: (0, 0)),
                      pl.BlockSpec((1, 1, D_FF), wsel), pl.BlockSpec((1, 1, D_FF), wsel),
                      pl.BlockSpec((1, D_FF, D), wsel), pl.BlockSpec((1, 1, D), wsel)],
            out_specs=pl.BlockSpec((MOE_BLK * ROW_SUB, LANES), rows),
            scratch_shapes=[pltpu.VMEM((D, D_FF), BF16), pltpu.VMEM((D, D_FF), BF16), pltpu.VMEM((D_FF, D), BF16)]),
        out_shape=jax.ShapeDtypeStruct((P * ROW_SUB, LANES), F32),
        compiler_params=pltpu.CompilerParams(dimension_semantics=("arbitrary",)),
        name="experts",
    )(block_expert, n_used, xs, wgu, perm, bg, bl, wd, bd)


def _combine_kernel(dest_ref, ys_hbm, x1_ref, rg_ref, g2_ref, b2_ref, o_ref, buf, sem, *, alpha):
    tm = o_ref.shape[0]

    def start(t, c):
        for k in range(TOP_K):
            _tile_copy(ys_hbm, dest_ref[t * TOP_K + k], buf.at[k], t, sem).start(priority=k % 2)
        return c

    def wait(t, c):
        for k in range(TOP_K):
            _tile_copy(ys_hbm, 0, buf.at[k], t, sem).wait()
        return c

    lax.fori_loop(0, tm, start, 0)
    lax.fori_loop(0, tm, wait, 0)
    gate = rg_ref[...]
    y = gate[:, 0:1] * _load_row_tiles(buf.at[0])
    for k in range(1, TOP_K):
        y = y + gate[:, k:k + 1] * _load_row_tiles(buf.at[k])
    o_ref[...] = _layer_norm(alpha * x1_ref[...] + y, g2_ref[...], b2_ref[...])


def _combine(dest, ys, x1, rg, g2, b2, alpha):
    N, D = x1.shape
    tm = min(COMB_TM, N)
    tok = lambda w: pl.BlockSpec((tm, w), lambda i: (i, 0))
    full = lambda a: pl.BlockSpec(a.shape, lambda i: (0,) * a.ndim)
    return pl.pallas_call(
        functools.partial(_combine_kernel, alpha=alpha),
        grid=(N // tm,),
        in_specs=[pl.BlockSpec((tm * TOP_K,), lambda i: (i,), memory_space=pltpu.SMEM),
                  pl.BlockSpec(memory_space=pl.ANY), tok(D), tok(rg.shape[1]), full(g2), full(b2)],
        out_specs=tok(D),
        out_shape=jax.ShapeDtypeStruct((N, D), F32),
        scratch_shapes=[pltpu.VMEM((TOP_K, tm * ROW_SUB, LANES), F32), pltpu.SemaphoreType.DMA(())],
        compiler_params=pltpu.CompilerParams(dimension_semantics=("arbitrary",)),
        name="combine",
    )(dest, ys, x1, rg, g2, b2)


def _rope_freq():
    half = ROPE_DIM // 2
    inv = (np.float32(ROPE_THETA) ** (-np.arange(half, dtype=np.float32) * np.float32(2.0 / ROPE_DIM))).astype(np.float32)
    return inv[:, None]


def _overlap_t(T):
    nc = T // CMP_STRIDE
    c0 = np.arange(nc) * CMP_STRIDE
    j0 = np.arange(SEL_LANES) * SEL_BLOCK
    ov = (c0[None, :] < j0[:, None] + SEL_BLOCK) & (c0[None, :] + CMP_BLOCK > j0[:, None])
    ov &= (np.arange(nc) < nc - 1)[None, :] & (np.arange(SEL_LANES) < T // SEL_BLOCK)[:, None]
    return ov.astype(np.float32)


def _deinterleave_perm():
    w = 2 * LANES
    p = np.zeros((w, w), np.float32)
    p[np.arange(0, w, 2), np.arange(w // 2)] = 1.0
    p[np.arange(1, w, 2), w // 2 + np.arange(w // 2)] = 1.0
    return p


def _token_major(t, B, T):
    return t.reshape(B, N_KV_GROUPS, HEAD_DIM, T).transpose(0, 1, 3, 2)


def _layer(x, positions, w_in, conv_w, cmp_pos_k, cmp_w1_k, cmp_w2_k, cmp_pos_v, cmp_w1_v, cmp_w2_v,
           w_up_conv, w_up_nsa, w_o, ln1_g, ln1_b, w_router, b_router, w_gate_up, b_gate_up,
           w_down, b_down, ln2_g, ln2_b, alpha):
    B, T, D = x.shape
    G, R = N_KV_GROUPS, HEADS_PER_GROUP
    N = B * T
    assert D == D_MODEL and T % SEL_BLOCK == 0 and T // SEL_BLOCK <= SEL_LANES

    c0 = 3 * CONV_CH
    c1 = c0 + NSA_WIDTH
    c2 = c1 + 6 * KV_WIDTH
    c3 = c2 + 3 * N_HEADS
    wc = w_in[:, :c0].astype(BF16)
    wqt = w_in[:, c0:c1].T.astype(BF16)
    wkvt = w_in[:, c1:c2].T.astype(BF16)
    gcols = np.zeros((G * GATE_ROWS,), np.int64)
    gmask = np.zeros((G * GATE_ROWS,), np.float32)
    for g in range(G):
        for br in range(3):
            for r in range(R):
                gcols[g * GATE_ROWS + br * R + r] = br * N_HEADS + g * R + r
                gmask[g * GATE_ROWS + br * R + r] = 1.0
    wgt = (w_in[:, c2:c3][:, gcols] * gmask).T.astype(BF16)
    wmg = w_in[:, c3:].astype(BF16)
    cw = conv_w.reshape(CONV_K, CONV_CH)
    pos_row = positions.astype(F32)[:, None, :]

    ya_pre, qt, kvt, gates = _proj(x, pos_row, wc, wqt, wkvt, wgt, cw, jnp.asarray(_rope_freq()))

    nc = T // CMP_STRIDE
    to_chunks = lambda t: _token_major(t, B, T).reshape(B * G, nc, CMP_STRIDE * HEAD_DIM)
    xin = jnp.stack([to_chunks(kvt[:, 0:KV_WIDTH]), to_chunks(kvt[:, KV_WIDTH:2 * KV_WIDTH])])
    w1 = jnp.stack([cmp_w1_k, cmp_w1_v])
    w2 = jnp.stack([cmp_w2_k, cmp_w2_v])
    pos_flat = jnp.stack([cmp_pos_k.reshape(1, -1), cmp_pos_v.reshape(1, -1)])
    pos_flat = jnp.pad(pos_flat, ((0, 0), (0, 7), (0, 0)))
    kcmp, kcmp_t = _compress(xin, w1, w2, w2.transpose(0, 2, 1), pos_flat)

    o_c, sel = _cmp_attn(qt, kcmp, kcmp_t, gates, jnp.asarray(_overlap_t(T)))

    ks = _token_major(kvt[:, 2 * KV_WIDTH:3 * KV_WIDTH], B, T)
    kw = _token_major(kvt[:, 4 * KV_WIDTH:5 * KV_WIDTH], B, T)
    onehot = (np.arange(T)[:, None] // SEL_BLOCK == np.arange(SEL_LANES)[None, :]).astype(np.float32)
    kaug = jnp.concatenate([ks, jnp.broadcast_to(jnp.asarray(onehot, BF16), (B, G, T, SEL_LANES))], axis=-1)
    o_s = _sel_attn(qt, sel, kaug, kvt, 3 * G, gates)
    o_w = _win_attn(qt, kw, kvt, 5 * G, gates)

    sub = min(MERGE_SUB, T)
    tri = jnp.asarray(np.triu(np.ones((sub, sub), np.float32), 1), BF16)
    x1, x1t, ri, rg, cnt = _merge(x.reshape(N, D), ya_pre.reshape(N, CONV_CH), o_c, o_s, o_w, wmg,
                             w_up_conv.astype(BF16), w_up_nsa.astype(BF16), w_o.astype(BF16),
                             ln1_g[None, :], ln1_b[None, :], w_router.T, b_router[:, None], tri, alpha)

    counts = cnt[:, 0]
    padded = (counts + MOE_BLK - 1) // MOE_BLK * MOE_BLK
    pend = jnp.cumsum(padded).astype(I32)
    pstart = pend - padded
    idx = ri[:TOP_K]
    pstart_sel = jnp.zeros_like(idx)
    for e in range(N_EXPERTS):
        pstart_sel = jnp.where(idx == e, pstart[e], pstart_sel)
    dest = (pstart_sel + ri[TOP_K:]).T.reshape(N * TOP_K)
    n_blocks = (N * TOP_K) // MOE_BLK + N_EXPERTS
    blk_start = jnp.arange(n_blocks, dtype=I32) * MOE_BLK
    block_expert = jnp.minimum(jnp.sum((blk_start[:, None] >= pend[None, :]).astype(I32), axis=1), N_EXPERTS - 1)
    n_used = (pend[-1:] // MOE_BLK).astype(I32)

    xs = _dispatch(dest, counts, pstart, pend, x1t, n_blocks * MOE_BLK)
    ys = _experts(block_expert, n_used, xs, w_gate_up, jnp.asarray(_deinterleave_perm(), BF16),
                  b_gate_up[:, None, 0::2], b_gate_up[:, None, 1::2], w_down, b_down[:, None, :])
    out = _combine(dest, ys, x1, rg.T, ln2_g[None, :], ln2_b[None, :], alpha)
    return out.reshape(B, T, D)


def kernel(x, positions, w_in, conv_w, cmp_pos_k, cmp_w1_k, cmp_w2_k, cmp_pos_v, cmp_w1_v, cmp_w2_v, w_up_conv, w_up_nsa, w_o, ln1_g, ln1_b, w_router, b_router, w_gate_up, b_gate_up, w_down, b_down, ln2_g, ln2_b):
    depth = w_in.shape[0]
    alpha = float((2 * depth) ** 0.25)
    h = x
    for l in range(depth):
        h = _layer(h, positions, w_in[l], conv_w[l], cmp_pos_k[l], cmp_w1_k[l], cmp_w2_k[l],
                   cmp_pos_v[l], cmp_w1_v[l], cmp_w2_v[l], w_up_conv[l], w_up_nsa[l], w_o[l],
                   ln1_g[l], ln1_b[l], w_router[l], b_router[l], w_gate_up[l], b_gate_up[l],
                   w_down[l], b_down[l], ln2_g[l], ln2_b[l], alpha)
    return h
```

```python
import functools

import numpy as np
import jax
import jax.numpy as jnp
from jax import lax
from jax.experimental import pallas as pl
from jax.experimental.pallas import tpu as pltpu

F32 = jnp.float32
BF16 = jnp.bfloat16
I32 = jnp.int32

D_MODEL = 1024
CONV_CH = 512
CONV_K = 3
N_HEADS = 8
N_KV_GROUPS = 2
HEADS_PER_GROUP = N_HEADS // N_KV_GROUPS
HEAD_DIM = 64
NSA_WIDTH = N_HEADS * HEAD_DIM
KV_WIDTH = N_KV_GROUPS * HEAD_DIM
ROPE_DIM = HEAD_DIM // 4
ROPE_THETA = 500000.0
CMP_BLOCK = 32
CMP_STRIDE = 16
CMP_HIDDEN = 256
SEL_BLOCK = 64
N_SELECT = 16
WINDOW = 512
N_EXPERTS = 32
TOP_K = 4
D_FF = 1024
SWIGLU_LIMIT = 7.0
SWIGLU_ALPHA = 1.702
LN_EPS = 1e-5
NEG_INF = -1e30
FORCE_SCORE = 1e4
LOG2E = 1.4426950408889634

LANES = 128
ROW_SUB = D_MODEL // LANES
SEL_LANES = 64
SEL_MASK_BIAS = -32768.0
GROUP_W = HEADS_PER_GROUP * HEAD_DIM
GATE_ROWS = 16

PROJ_TM = 512
CMP_TQ = 256
SEL_TQ = 256
SEL_KC = 512
SEL_SPLIT = 2
WIN_TQ = 512
WIN_SUB = 128
MERGE_TM = 512
MERGE_SUB = 256
MOE_BLK = 512
DISP_TM = 512
COMB_TM = 256
COMB_GROUP = 128


def _dot(a, b, precision=None):
    return jnp.dot(a, b, precision=precision, preferred_element_type=F32)


def _dot_nt(a, b, precision=None):
    return lax.dot_general(a, b, (((1,), (1,)), ((), ())), precision=precision, preferred_element_type=F32)


def _proj_kernel(x_ref, pos_ref, wc_ref, wqt_ref, wkvt_ref, wgt_ref, cw_ref, freq_ref,
                 ya_ref, qt_ref, kvt_ref, gt_ref, carry_ref):
    ti = pl.program_id(1)
    tm = x_ref.shape[1]
    xb = x_ref[0].astype(BF16)

    pc = _dot(xb, wc_ref[...])
    xv = pc[:, :CONV_CH]
    bg = pc[:, CONV_CH:2 * CONV_CH]
    cg = pc[:, 2 * CONV_CH:]
    u = cg * xv

    @pl.when(ti == 0)
    def _():
        carry_ref[...] = jnp.zeros_like(carry_ref)

    prev = carry_ref[...]
    row = lax.broadcasted_iota(I32, u.shape, 0)
    u1 = jnp.where(row == 0, prev[7:8], pltpu.roll(u, 1, 0))
    u2 = jnp.where(row == 0, prev[6:7], jnp.where(row == 1, prev[7:8], pltpu.roll(u, 2, 0)))
    cw = cw_ref[...]
    conv = cw[2:3] * u + cw[1:2] * u1 + cw[0:1] * u2
    carry_ref[...] = u[tm - 8:]
    ya_ref[0] = (bg * conv).astype(BF16)

    ang = freq_ref[...] * pos_ref[0]
    cos = jnp.cos(ang)
    sin = jnp.sin(ang)
    half = ROPE_DIM // 2

    def rope_head(t):
        t1 = t[:half]
        t2 = t[half:ROPE_DIM]
        return [t1 * cos - t2 * sin, t2 * cos + t1 * sin, t[ROPE_DIM:]]

    def heads(t, rotate):
        out = []
        for h in range(t.shape[0] // HEAD_DIM):
            th = t[h * HEAD_DIM:(h + 1) * HEAD_DIM]
            out.extend(rope_head(th) if rotate(h) else [th])
        return jnp.concatenate(out, axis=0)

    qt = heads(_dot_nt(wqt_ref[...], xb), lambda h: True)
    qt_ref[0] = (qt * (HEAD_DIM ** -0.5 * LOG2E)).astype(BF16)
    kvt = heads(_dot_nt(wkvt_ref[...], xb), lambda h: (h // N_KV_GROUPS) % 2 == 0)
    kvt_ref[0] = kvt.astype(BF16)
    gt_ref[0] = jax.nn.sigmoid(_dot_nt(wgt_ref[...], xb))


def _proj(x, pos_row, wc, wqt, wkvt, wgt, cw, freq):
    B, T, D = x.shape
    tm = min(PROJ_TM, T)
    grid = (B, T // tm)
    full = lambda a: pl.BlockSpec(a.shape, lambda b, t: (0,) * a.ndim)
    tok = lambda w: pl.BlockSpec((1, tm, w), lambda b, t: (b, t, 0))
    feat = lambda r: pl.BlockSpec((1, r, tm), lambda b, t: (b, 0, t))
    n_gate = wgt.shape[0]
    return pl.pallas_call(
        _proj_kernel,
        grid=grid,
        in_specs=[tok(D), feat(1), full(wc), full(wqt), full(wkvt), full(wgt), full(cw), full(freq)],
        out_specs=[tok(CONV_CH), feat(NSA_WIDTH), feat(6 * KV_WIDTH), feat(n_gate)],
        out_shape=[jax.ShapeDtypeStruct((B, T, CONV_CH), BF16),
                   jax.ShapeDtypeStruct((B, NSA_WIDTH, T), BF16),
                   jax.ShapeDtypeStruct((B, 6 * KV_WIDTH, T), BF16),
                   jax.ShapeDtypeStruct((B, n_gate, T), F32)],
        scratch_shapes=[pltpu.VMEM((8, CONV_CH), F32)],
        compiler_params=pltpu.CompilerParams(dimension_semantics=("arbitrary", "arbitrary")),
        name="proj",
    )(x, pos_row, wc, wqt, wkvt, wgt, cw, freq)


def _compress_kernel(x_ref, w1_ref, w2_ref, w2t_ref, pos_ref, o_ref, ot_ref):
    xb = x_ref[0, 0]
    nc = xb.shape[0]
    w1 = w1_ref[0]
    w1b = w1.astype(BF16)
    half = CMP_STRIDE * HEAD_DIM
    a = _dot(xb, w1b[:half])
    b = _dot(xb, w1b[half:])
    b_next = pltpu.roll(b, nc - 1, 0)
    pb = _dot(pos_ref[0], w1, precision=lax.Precision.HIGHEST)[0:1]
    h = a + b_next + pb
    g = (0.5 * h * (1.0 + jnp.tanh(np.sqrt(2.0 / np.pi) * (h + 0.044715 * (h * h * h))))).astype(BF16)
    o_ref[0, 0] = _dot(g, w2_ref[0].astype(BF16)).astype(BF16)
    ot_ref[0, 0] = _dot_nt(w2t_ref[0].astype(BF16), g).astype(BF16)


def _compress(xin, w1, w2, w2t, pos):
    _, BG, nc, W = xin.shape
    per = lambda a: pl.BlockSpec((1,) + a.shape[1:], lambda s, i: (s, 0, 0))
    return pl.pallas_call(
        _compress_kernel,
        grid=(2, BG),
        in_specs=[pl.BlockSpec((1, 1, nc, W), lambda s, i: (s, i, 0, 0)), per(w1), per(w2), per(w2t), per(pos)],
        out_specs=[pl.BlockSpec((1, 1, nc, HEAD_DIM), lambda s, i: (s, i, 0, 0)),
                   pl.BlockSpec((1, 1, HEAD_DIM, nc), lambda s, i: (s, i, 0, 0))],
        out_shape=[jax.ShapeDtypeStruct((2, BG, nc, HEAD_DIM), BF16),
                   jax.ShapeDtypeStruct((2, BG, HEAD_DIM, nc), BF16)],
        name="compress",
    )(xin, w1, w2, w2t, pos)


def _head_lanes(qt):
    return jnp.concatenate([qt[r * HEAD_DIM:(r + 1) * HEAD_DIM] for r in range(HEADS_PER_GROUP)], axis=1)


def _gated_out(ot, scale, gates, branch, tq):
    rows = []
    for r in range(HEADS_PER_GROUP):
        c = branch * HEADS_PER_GROUP + r
        sl = slice(r * tq, (r + 1) * tq)
        rows.append(ot[:, sl] * (scale[:, sl] * gates[c:c + 1]))
    return jnp.concatenate(rows, axis=0).astype(BF16)


def _attn_specs(tq):
    G = N_KV_GROUPS
    qspec = pl.BlockSpec((1, GROUP_W, tq), lambda b, g, i: (b, g, i))
    gspec = pl.BlockSpec((1, GATE_ROWS, tq), lambda b, g, i: (b, g, i))
    bg4 = lambda a: pl.BlockSpec((1, 1) + a.shape[2:], lambda b, g, i: (b, g, 0, 0))
    return G, qspec, gspec, bg4


def _cmp_attn_kernel(q_ref, kc_ref, vct_ref, g_ref, ovt_ref, o_ref, sel_ref):
    qi = pl.program_id(2)
    tq = q_ref.shape[2]
    q0 = qi * tq
    qt = _head_lanes(q_ref[0])
    s = _dot(kc_ref[0, 0], qt)
    c = lax.broadcasted_iota(I32, s.shape, 0)
    t = q0 + (lax.broadcasted_iota(I32, s.shape, 1) & (tq - 1))
    valid = c * CMP_STRIDE + (CMP_BLOCK - 1) <= t
    sm = jnp.where(valid, s, NEG_INF)
    m = jnp.max(sm, axis=0, keepdims=True)
    p = jnp.where(valid, jnp.exp2(sm - m), 0.0)
    l = jnp.sum(p, axis=0, keepdims=True)
    inv = 1.0 / jnp.where(l > 0.0, l, 1.0)
    ot = _dot(vct_ref[0, 0], p.astype(BF16))
    o_ref[0] = _gated_out(ot, inv, g_ref[0], 0, tq)

    pn = p * inv
    psum = pn[:, 0:tq]
    for r in range(1, HEADS_PER_GROUP):
        psum = psum + pn[:, r * tq:(r + 1) * tq]
    imp = _dot(ovt_ref[...], psum, precision=lax.Precision.HIGHEST)
    j = lax.broadcasted_iota(I32, imp.shape, 0)
    cur = (q0 + lax.broadcasted_iota(I32, imp.shape, 1)) // SEL_BLOCK
    valid_b = j <= cur
    forced = (j == 0) | (j == cur) | (j == cur - 1)
    score = jnp.where(valid_b, jnp.where(forced, FORCE_SCORE, imp), NEG_INF)
    sub = 8
    groups = [score[a:a + sub] for a in range(0, SEL_LANES, sub)]
    jrow = lax.broadcasted_iota(I32, groups[0].shape, 0)
    ranks = [jnp.zeros(g_.shape, I32) for g_ in groups]
    for i in range(SEL_LANES):
        si = score[i:i + 1, :]
        for a, g_ in enumerate(groups):
            if a > i // sub:
                inc = jnp.where(si >= g_, 1, 0)
            elif a < i // sub:
                inc = jnp.where(si > g_, 1, 0)
            else:
                inc = jnp.where(jrow > i % sub, jnp.where(si >= g_, 1, 0), jnp.where(si > g_, 1, 0))
            ranks[a] = ranks[a] + inc
    rank = jnp.concatenate(ranks, axis=0)
    selected = (rank < N_SELECT) & valid_b
    sel_ref[0, 0] = jnp.where(selected, 0.0, SEL_MASK_BIAS).astype(BF16)


def _cmp_attn(qt, kcmp, vcmp_t, gates, ovt):
    B, _, T = qt.shape
    tq = min(CMP_TQ, T)
    G, qspec, gspec, _ = _attn_specs(tq)
    cspec = lambda a: pl.BlockSpec((1, 1) + a.shape[2:], lambda b, g, i: (0, b * G + g, 0, 0))
    vspec = lambda a: pl.BlockSpec((1, 1) + a.shape[2:], lambda b, g, i: (1, b * G + g, 0, 0))
    return pl.pallas_call(
        _cmp_attn_kernel,
        grid=(B, G, T // tq),
        in_specs=[qspec, cspec(kcmp), vspec(vcmp_t), gspec, pl.BlockSpec(ovt.shape, lambda b, g, i: (0, 0))],
        out_specs=[qspec, pl.BlockSpec((1, 1, SEL_LANES, tq), lambda b, g, i: (b, g, 0, i))],
        out_shape=[jax.ShapeDtypeStruct((B, NSA_WIDTH, T), BF16),
                   jax.ShapeDtypeStruct((B, G, SEL_LANES, T), BF16)],
        name="cmp_attn",
    )(qt, kcmp, vcmp_t, gates, ovt)


def _sel_attn_kernel(q_ref, sel_ref, k_ref, vt_ref, g_ref, o_ref, *, kc):
    qi = pl.program_id(2)
    tq = q_ref.shape[2]
    q0 = qi * tq
    qt = q_ref[0]
    sb = sel_ref[0, 0]
    qa = jnp.concatenate(
        [jnp.concatenate([qt[r * HEAD_DIM:(r + 1) * HEAD_DIM], sb], axis=0) for r in range(HEADS_PER_GROUP)],
        axis=1)
    gw = qa.shape[1] // SEL_SPLIT
    qas = [qa[:, i * gw:(i + 1) * gw] for i in range(SEL_SPLIT)]

    def step(kstart, kn, carry, diagonal):
        kblk = k_ref[0, 0, pl.ds(kstart, kn), :]
        vblk = vt_ref[0, :, pl.ds(kstart, kn)]
        ss = [_dot(kblk, qg) for qg in qas]
        out = []
        for gi, (s, (m, l, acc)) in enumerate(zip(ss, carry)):
            if diagonal:
                row = lax.broadcasted_iota(I32, s.shape, 0)
                off = (gi * gw + lax.broadcasted_iota(I32, s.shape, 1)) & (tq - 1)
                s = jnp.where(row <= off, s, NEG_INF)
            mn = jnp.maximum(m, jnp.max(s, axis=0, keepdims=True))
            alpha = jnp.exp2(m - mn)
            p = jnp.exp2(s - mn)
            l = alpha * l + jnp.sum(p, axis=0, keepdims=True)
            acc = alpha * acc + _dot(vblk, p.astype(BF16))
            out.append((mn, l, acc))
        return tuple(out)

    init = tuple((jnp.full((1, gw), NEG_INF, F32), jnp.zeros((1, gw), F32), jnp.zeros((HEAD_DIM, gw), F32))
                 for _ in range(SEL_SPLIT))
    n_full = q0 // kc
    carry = lax.fori_loop(0, n_full, lambda i, cr: step(pl.multiple_of(i * kc, kc), kc, cr, False), init)
    carry = lax.fori_loop(n_full * (kc // tq), qi, lambda i, cr: step(pl.multiple_of(i * tq, tq), tq, cr, False), carry)
    carry = step(pl.multiple_of(q0, tq), tq, carry, True)
    l = jnp.concatenate([c[1] for c in carry], axis=1)
    acc = jnp.concatenate([c[2] for c in carry], axis=1)
    o_ref[0] = _gated_out(acc, 1.0 / l, g_ref[0], 1, tq)


def _sel_attn(qt, sel, kaug, kvt, v_row_block, gates):
    B, _, T = qt.shape
    tq = min(SEL_TQ, T)
    kc = min(SEL_KC, T)
    G, qspec, gspec, bg4 = _attn_specs(tq)
    return pl.pallas_call(
        functools.partial(_sel_attn_kernel, kc=kc),
        grid=(B, G, T // tq),
        in_specs=[qspec, pl.BlockSpec((1, 1, SEL_LANES, tq), lambda b, g, i: (b, g, 0, i)), bg4(kaug),
                  pl.BlockSpec((1, HEAD_DIM, T), lambda b, g, i: (b, v_row_block + g, 0)), gspec],
        out_specs=qspec,
        out_shape=jax.ShapeDtypeStruct((B, NSA_WIDTH, T), BF16),
        name="sel_attn",
    )(qt, sel, kaug, kvt, gates)


def _win_attn_kernel(q_ref, k_ref, vt_ref, g_ref, o_ref, *, span, sub):
    qi = pl.program_id(2)
    tq = q_ref.shape[2]
    T = k_ref.shape[2]

    def tile(s0, interior):
        cs = slice(s0, s0 + sub)
        q0 = qi * tq + s0
        start = pl.multiple_of(jnp.clip(q0 + sub - span, 0, T - span), sub)
        qt = _head_lanes(q_ref[0, :, cs])
        s = _dot(k_ref[0, 0, pl.ds(start, span), :], qt)
        if interior:
            row = lax.broadcasted_iota(I32, (sub, s.shape[1]), 0)
            off = lax.broadcasted_iota(I32, (sub, s.shape[1]), 1) & (sub - 1)
            s = jnp.concatenate([jnp.where(row > off, s[:sub], NEG_INF), s[sub:span - sub],
                                 jnp.where(row <= off, s[span - sub:], NEG_INF)], axis=0)
        else:
            kp = start + lax.broadcasted_iota(I32, s.shape, 0)
            t = q0 + (lax.broadcasted_iota(I32, s.shape, 1) & (sub - 1))
            diff = t - kp
            s = jnp.where((diff >= 0) & (diff < WINDOW), s, NEG_INF)
        m = jnp.max(s, axis=0, keepdims=True)
        p = jnp.exp2(s - m)
        l = jnp.sum(p, axis=0, keepdims=True)
        ot = _dot(vt_ref[0, :, pl.ds(start, span)], p.astype(BF16))
        o_ref[0, :, cs] = _gated_out(ot, 1.0 / l, g_ref[0, :, cs], 2, sub)

    first_interior = -(-WINDOW // tq)
    if span == WINDOW + sub:
        @pl.when(qi >= first_interior)
        def _():
            for s0 in range(0, tq, sub):
                tile(s0, True)

    @pl.when((qi < first_interior) | (span != WINDOW + sub))
    def _():
        for s0 in range(0, tq, sub):
            tile(s0, False)


def _win_attn(qt, k, kvt, v_row_block, gates):
    B, _, T = qt.shape
    tq = min(WIN_TQ, T)
    sub = min(WIN_SUB, T)
    span = min(WINDOW + sub, T)
    G, qspec, gspec, bg4 = _attn_specs(tq)
    return pl.pallas_call(
        functools.partial(_win_attn_kernel, span=span, sub=sub),
        grid=(B, G, T // tq),
        in_specs=[qspec, bg4(k), pl.BlockSpec((1, HEAD_DIM, T), lambda b, g, i: (b, v_row_block + g, 0)), gspec],
        out_specs=qspec,
        out_shape=jax.ShapeDtypeStruct((B, NSA_WIDTH, T), BF16),
        name="win_attn",
    )(qt, k, kvt, gates)


def _store_row_tiles(ref, x, row0=0):
    rows = x.shape[0]
    for s in range(ROW_SUB):
        ref[pl.ds(row0 * ROW_SUB + s, rows, stride=ROW_SUB), :] = x[:, s * LANES:(s + 1) * LANES]


def _load_row_tiles(ref):
    rows = ref.shape[0] // ROW_SUB
    return jnp.concatenate([ref[pl.ds(s, rows, stride=ROW_SUB), :] for s in range(ROW_SUB)], axis=1)


def _tile_copy(src, si, dst, di, sem):
    return pltpu.make_async_copy(src.at[pl.ds(pl.multiple_of(si * ROW_SUB, ROW_SUB), ROW_SUB), :],
                                 dst.at[pl.ds(pl.multiple_of(di * ROW_SUB, ROW_SUB), ROW_SUB), :], sem)


def _layer_norm(h, g, b):
    mu = jnp.mean(h, axis=-1, keepdims=True)
    c = h - mu
    var = jnp.mean(c * c, axis=-1, keepdims=True)
    return c * lax.rsqrt(var + LN_EPS) * g + b


def _merge_kernel(x_ref, ya_ref, oc_ref, os_ref, ow_ref, wmg_ref, wuc_ref, wun_ref, wo_ref, g1_ref, b1_ref,
                  wr_ref, br_ref, tri_ref, x1_ref, x1t_ref, ri_ref, rg_ref, cnt_ref, carry_ref, *, alpha):
    i = pl.program_id(0)
    sub = tri_ref.shape[0]

    @pl.when(i == 0)
    def _():
        carry_ref[...] = jnp.zeros_like(carry_ref)

    wr = wr_ref[...]
    wr_hi = wr.astype(BF16)
    wr_hl = jnp.concatenate([wr_hi, (wr - wr_hi.astype(F32)).astype(BF16)], axis=0)
    total = carry_ref[...]
    for s0 in range(0, x_ref.shape[0], sub):
        rs = slice(s0, s0 + sub)
        x = x_ref[rs, :]
        mg = _dot(x.astype(BF16), wmg_ref[...])
        y_a = _dot(ya_ref[rs, :], wuc_ref[...])
        o_nsa_t = (oc_ref[0, :, rs].astype(F32) + os_ref[0, :, rs].astype(F32)
                   + ow_ref[0, :, rs].astype(F32))
        y_b = _dot(o_nsa_t.T.astype(BF16), wun_ref[...])
        merged = jax.nn.sigmoid(mg[:, :D_MODEL]) * y_a + jax.nn.sigmoid(mg[:, D_MODEL:]) * y_b
        h = alpha * x + _dot(merged.astype(BF16), wo_ref[...])
        x1 = _layer_norm(h, g1_ref[...], b1_ref[...])
        x1_ref[rs, :] = x1
        _store_row_tiles(x1t_ref, x1, s0)

        x_hi = x1.astype(BF16)
        x_lo = (x1 - x_hi.astype(F32)).astype(BF16)
        both = _dot_nt(wr_hl, x_hi)
        logits = both[:N_EXPERTS] + both[N_EXPERTS:] + _dot_nt(wr_hi, x_lo) + br_ref[...]
        expert = lax.broadcasted_iota(I32, logits.shape, 0).astype(F32)
        rem = logits
        vals, idxs, hots = [], [], []
        for _ in range(TOP_K):
            m = jnp.max(rem, axis=0, keepdims=True)
            idx = jnp.min(jnp.where(rem == m, expert, float(N_EXPERTS)), axis=0, keepdims=True)
            hot = expert == idx
            vals.append(m)
            idxs.append(idx)
            hots.append(hot)
            rem = jnp.where(hot, -3.0e38, rem)
        es = [jnp.exp(v - vals[0]) for v in vals]
        den = es[0]
        for e in es[1:]:
            den = den + e
        chosen = hots[0]
        for hot in hots[1:]:
            chosen = chosen | hot
        chosen_f = jnp.where(chosen, 1.0, 0.0)

        before = _dot(chosen_f.astype(BF16), tri_ref[...]) + total
        total = total + jnp.sum(chosen_f, axis=1, keepdims=True)
        ranks = [jnp.sum(jnp.where(hot, before, 0.0), axis=0, keepdims=True) for hot in hots]
        ri_ref[:, rs] = jnp.concatenate(idxs + ranks, axis=0).astype(I32)
        rg_ref[:, rs] = jnp.concatenate([e / den for e in es] + [jnp.zeros_like(den)] * TOP_K, axis=0)

    carry_ref[...] = total
    cnt_ref[...] = jnp.broadcast_to(total, cnt_ref.shape).astype(I32)


def _merge(x, ya, oc, os_, ow, wmg, wuc, wun, wo, g1, b1, wr, br, tri, alpha):
    N, D = x.shape
    B, _, T = oc.shape
    tm = min(MERGE_TM, T)
    nt = T // tm
    tok = lambda w: pl.BlockSpec((tm, w), lambda i: (i, 0))
    feat = pl.BlockSpec((1, NSA_WIDTH, tm), lambda i: (i // nt, 0, i % nt))
    full = lambda a: pl.BlockSpec(a.shape, lambda i: (0,) * a.ndim)
    return pl.pallas_call(
        functools.partial(_merge_kernel, alpha=alpha),
        grid=(N // tm,),
        in_specs=[tok(D), tok(CONV_CH), feat, feat, feat,
                  full(wmg), full(wuc), full(wun), full(wo), full(g1), full(b1), full(wr), full(br), full(tri)],
        out_specs=[tok(D), pl.BlockSpec((tm * ROW_SUB, LANES), lambda i: (i, 0)),
                   pl.BlockSpec((2 * TOP_K, tm), lambda i: (0, i)), pl.BlockSpec((2 * TOP_K, tm), lambda i: (0, i)),
                   pl.BlockSpec((N_EXPERTS, LANES), lambda i: (0, 0))],
        out_shape=[jax.ShapeDtypeStruct((N, D), F32),
                   jax.ShapeDtypeStruct((N * ROW_SUB, LANES), F32),
                   jax.ShapeDtypeStruct((2 * TOP_K, N), I32),
                   jax.ShapeDtypeStruct((2 * TOP_K, N), F32),
                   jax.ShapeDtypeStruct((N_EXPERTS, LANES), I32)],
        scratch_shapes=[pltpu.VMEM((N_EXPERTS, 1), F32)],
        compiler_params=pltpu.CompilerParams(dimension_semantics=("arbitrary",)),
        name="merge",
    )(x, ya, oc, os_, ow, wmg, wuc, wun, wo, g1, b1, wr, br, tri)


def _dispatch_kernel(dest_ref, cnt_ref, pstart_ref, pend_ref, x_ref, xs_hbm, zero_ref, sem, zsem):
    i = pl.program_id(0)
    n = dest_ref.shape[0]

    @pl.when(i == 0)
    def _():
        zero_ref[...] = jnp.zeros_like(zero_ref)
        for e in range(N_EXPERTS):
            lo = pstart_ref[e] + cnt_ref[e]
            hi = pend_ref[e]

            def zstart(r, c):
                _tile_copy(zero_ref, 0, xs_hbm, r, zsem).start()
                return c

            def zwait(r, c):
                _tile_copy(zero_ref, 0, xs_hbm, r, zsem).wait()
                return c

            lax.fori_loop(lo, hi, zstart, 0)
            lax.fori_loop(lo, hi, zwait, 0)

    def start(t, c):
        for k in range(TOP_K):
            _tile_copy(x_ref, t, xs_hbm, dest_ref[t * TOP_K + k], sem).start(priority=k % 2)
        return c

    lax.fori_loop(0, n // TOP_K, start, 0)
    for k in range(TOP_K):
        pltpu.make_async_copy(x_ref, xs_hbm.at[pl.ds(0, x_ref.shape[0]), :], sem).wait()


def _dispatch(dest, cnt, pstart, pend, x1t, n_rows):
    N = x1t.shape[0] // ROW_SUB
    tm = min(DISP_TM, N)
    smem = pl.BlockSpec(memory_space=pltpu.SMEM)
    return pl.pallas_call(
        _dispatch_kernel,
        grid=(N // tm,),
        in_specs=[pl.BlockSpec((tm * TOP_K,), lambda i: (i,), memory_space=pltpu.SMEM), smem, smem, smem,
                  pl.BlockSpec((tm * ROW_SUB, LANES), lambda i: (i, 0))],
        out_specs=pl.BlockSpec(memory_space=pl.ANY),
        out_shape=jax.ShapeDtypeStruct((n_rows * ROW_SUB, LANES), F32),
        scratch_shapes=[pltpu.VMEM((ROW_SUB, LANES), F32), pltpu.SemaphoreType.DMA(()), pltpu.SemaphoreType.DMA(())],
        compiler_params=pltpu.CompilerParams(dimension_semantics=("arbitrary",)),
        name="dispatch",
    )(dest, cnt, pstart, pend, x1t)


def _experts_kernel(be_ref, nu_ref, xs_ref, wgu_ref, perm_ref, bg_ref, bl_ref, wd_ref, bd_ref, ys_ref, wg_s, wl_s, wd_s):
    i = pl.program_id(0)

    @pl.when((i == 0) | (be_ref[i] != be_ref[jnp.maximum(i, 1) - 1]))
    def _():
        w = perm_ref.shape[0]
        for c in range(2 * D_FF // w):
            t = _dot(wgu_ref[0, :, c * w:(c + 1) * w].astype(BF16), perm_ref[...])
            wg_s[:, c * (w // 2):(c + 1) * (w // 2)] = t[:, :w // 2].astype(BF16)
            wl_s[:, c * (w // 2):(c + 1) * (w // 2)] = t[:, w // 2:].astype(BF16)
        wd_s[...] = wd_ref[0].astype(BF16)

    @pl.when(i < nu_ref[0])
    def _():
        xb = _load_row_tiles(xs_ref).astype(BF16)
        x_glu = jnp.minimum(_dot(xb, wg_s[...]) + bg_ref[0], SWIGLU_LIMIT)
        x_lin = jnp.clip(_dot(xb, wl_s[...]) + bl_ref[0], -SWIGLU_LIMIT, SWIGLU_LIMIT)
        act = x_glu * jax.nn.sigmoid(SWIGLU_ALPHA * x_glu) * (x_lin + 1.0)
        _store_row_tiles(ys_ref, _dot(act.astype(BF16), wd_s[...]) + bd_ref[0])


def _experts(block_expert, n_used, xs, wgu, perm, bg, bl, wd, bd):
    P = xs.shape[0] // ROW_SUB
    D = D_MODEL
    n_blocks = P // MOE_BLK
    rows = lambda i, be, nu: (jnp.minimum(i, nu[0] - 1), 0)
    wsel = lambda i, be, nu: (be[i], 0, 0)
    return pl.pallas_call(
        _experts_kernel,
        grid_spec=pltpu.PrefetchScalarGridSpec(
            num_scalar_prefetch=2,
            grid=(n_blocks,),
            in_specs=[pl.BlockSpec((MOE_BLK * ROW_SUB, LANES), rows),
                      pl.BlockSpec((1, D, 2 * D_FF), wsel),
                      pl.BlockSpec(perm.shape, lambda i, be, nu: (0, 0)),
                      pl.BlockSpec((1, 1, D_FF), wsel), pl.BlockSpec((1, 1, D_FF), wsel),
                      pl.BlockSpec((1, D_FF, D), wsel), pl.BlockSpec((1, 1, D), wsel)],
            out_specs=pl.BlockSpec((MOE_BLK * ROW_SUB, LANES), rows),
            scratch_shapes=[pltpu.VMEM((D, D_FF), BF16), pltpu.VMEM((D, D_FF), BF16), pltpu.VMEM((D_FF, D), BF16)]),
        out_shape=jax.ShapeDtypeStruct((P * ROW_SUB, LANES), F32),
        compiler_params=pltpu.CompilerParams(dimension_semantics=("arbitrary",)),
        name="experts",
    )(block_expert, n_used, xs, wgu, perm, bg, bl, wd, bd)


def _combine_kernel(dest_ref, dest_next_ref, ys_hbm, x1_ref, rg_ref, g2_ref, b2_ref, o_ref, buf, sem, *, alpha):
    i = pl.program_id(0)
    last = pl.num_programs(0) - 1
    tm = o_ref.shape[0]
    slot = i % 2
    other = 1 - slot

    def gather(ids_ref, t, to_slot):
        for k in range(TOP_K):
            _tile_copy(ys_hbm, ids_ref[t * TOP_K + k], buf.at[to_slot, k], t, sem.at[to_slot]).start(priority=k % 2)

    def wait_slot(s):
        for k in range(TOP_K):
            pltpu.make_async_copy(ys_hbm.at[pl.ds(0, tm * ROW_SUB), :], buf.at[s, k], sem.at[s]).wait()

    @pl.when(i == 0)
    def _():
        def first(t, c):
            gather(dest_ref, t, slot)
            return c
        lax.fori_loop(0, tm, first, 0)

    wait_slot(slot)

    gs = min(COMB_GROUP, tm)

    def group(g, c):
        r0 = pl.multiple_of(g * gs, gs)
        for j in range(gs):
            gather(dest_next_ref, r0 + j, other)
        rows = pl.ds(r0, gs)
        gate = rg_ref[rows, :]
        y = alpha * x1_ref[rows, :]
        for k in range(TOP_K):
            tiles = buf.at[slot, k, pl.ds(pl.multiple_of(r0 * ROW_SUB, gs * ROW_SUB), gs * ROW_SUB), :]
            y = y + gate[:, k:k + 1] * _load_row_tiles(tiles)
        o_ref[rows, :] = _layer_norm(y, g2_ref[...], b2_ref[...])
        return c

    lax.fori_loop(0, tm // gs, group, 0)

    @pl.when(i == last)
    def _():
        wait_slot(other)


def _combine(dest, ys, x1, rg, g2, b2, alpha):
    N, D = x1.shape
    tm = min(COMB_TM, N)
    n = N // tm
    tok = lambda w: pl.BlockSpec((tm, w), lambda i: (i, 0))
    full = lambda a: pl.BlockSpec(a.shape, lambda i: (0,) * a.ndim)
    ids = lambda f: pl.BlockSpec((tm * TOP_K,), f, memory_space=pltpu.SMEM)
    return pl.pallas_call(
        functools.partial(_combine_kernel, alpha=alpha),
        grid=(n,),
        in_specs=[ids(lambda i: (i,)), ids(lambda i: (jnp.minimum(i + 1, n - 1),)),
                  pl.BlockSpec(memory_space=pl.ANY), tok(D), tok(rg.shape[1]), full(g2), full(b2)],
        out_specs=tok(D),
        out_shape=jax.ShapeDtypeStruct((N, D), F32),
        scratch_shapes=[pltpu.VMEM((2, TOP_K, tm * ROW_SUB, LANES), F32), pltpu.SemaphoreType.DMA((2,))],
        compiler_params=pltpu.CompilerParams(dimension_semantics=("arbitrary",)),
        name="combine",
    )(dest, dest, ys, x1, rg, g2, b2)


def _rope_freq():
    half = ROPE_DIM // 2
    inv = (np.float32(ROPE_THETA) ** (-np.arange(half, dtype=np.float32) * np.float32(2.0 / ROPE_DIM))).astype(np.float32)
    return inv[:, None]


def _overlap_t(T):
    nc = T // CMP_STRIDE
    c0 = np.arange(nc) * CMP_STRIDE
    j0 = np.arange(SEL_LANES) * SEL_BLOCK
    ov = (c0[None, :] < j0[:, None] + SEL_BLOCK) & (c0[None, :] + CMP_BLOCK > j0[:, None])
    ov &= (np.arange(nc) < nc - 1)[None, :] & (np.arange(SEL_LANES) < T // SEL_BLOCK)[:, None]
    return ov.astype(np.float32)


def _deinterleave_perm():
    w = 2 * LANES
    p = np.zeros((w, w), np.float32)
    p[np.arange(0, w, 2), np.arange(w // 2)] = 1.0
    p[np.arange(1, w, 2), w // 2 + np.arange(w // 2)] = 1.0
    return p


def _token_major(t, B, T):
    return t.reshape(B, N_KV_GROUPS, HEAD_DIM, T).transpose(0, 1, 3, 2)


def _layer(x, positions, w_in, conv_w, cmp_pos_k, cmp_w1_k, cmp_w2_k, cmp_pos_v, cmp_w1_v, cmp_w2_v,
           w_up_conv, w_up_nsa, w_o, ln1_g, ln1_b, w_router, b_router, w_gate_up, b_gate_up,
           w_down, b_down, ln2_g, ln2_b, alpha):
    B, T, D = x.shape
    G, R = N_KV_GROUPS, HEADS_PER_GROUP
    N = B * T
    assert D == D_MODEL and T % SEL_BLOCK == 0 and T // SEL_BLOCK <= SEL_LANES

    c0 = 3 * CONV_CH
    c1 = c0 + NSA_WIDTH
    c2 = c1 + 6 * KV_WIDTH
    c3 = c2 + 3 * N_HEADS
    wc = w_in[:, :c0].astype(BF16)
    wqt = w_in[:, c0:c1].T.astype(BF16)
    wkvt = w_in[:, c1:c2].T.astype(BF16)
    gcols = np.zeros((G * GATE_ROWS,), np.int64)
    gmask = np.zeros((G * GATE_ROWS,), np.float32)
    for g in range(G):
        for br in range(3):
            for r in range(R):
                gcols[g * GATE_ROWS + br * R + r] = br * N_HEADS + g * R + r
                gmask[g * GATE_ROWS + br * R + r] = 1.0
    wgt = (w_in[:, c2:c3][:, gcols] * gmask).T.astype(BF16)
    wmg = w_in[:, c3:].astype(BF16)
    cw = conv_w.reshape(CONV_K, CONV_CH)
    pos_row = positions.astype(F32)[:, None, :]

    ya_pre, qt, kvt, gates = _proj(x, pos_row, wc, wqt, wkvt, wgt, cw, jnp.asarray(_rope_freq()))

    nc = T // CMP_STRIDE
    to_chunks = lambda t: _token_major(t, B, T).reshape(B * G, nc, CMP_STRIDE * HEAD_DIM)
    xin = jnp.stack([to_chunks(kvt[:, 0:KV_WIDTH]), to_chunks(kvt[:, KV_WIDTH:2 * KV_WIDTH])])
    w1 = jnp.stack([cmp_w1_k, cmp_w1_v])
    w2 = jnp.stack([cmp_w2_k, cmp_w2_v])
    pos_flat = jnp.stack([cmp_pos_k.reshape(1, -1), cmp_pos_v.reshape(1, -1)])
    pos_flat = jnp.pad(pos_flat, ((0, 0), (0, 7), (0, 0)))
    kcmp, kcmp_t = _compress(xin, w1, w2, w2.transpose(0, 2, 1), pos_flat)

    o_c, sel = _cmp_attn(qt, kcmp, kcmp_t, gates, jnp.asarray(_overlap_t(T)))

    ks = _token_major(kvt[:, 2 * KV_WIDTH:3 * KV_WIDTH], B, T)
    kw = _token_major(kvt[:, 4 * KV_WIDTH:5 * KV_WIDTH], B, T)
    onehot = (np.arange(T)[:, None] // SEL_BLOCK == np.arange(SEL_LANES)[None, :]).astype(np.float32)
    kaug = jnp.concatenate([ks, jnp.broadcast_to(jnp.asarray(onehot, BF16), (B, G, T, SEL_LANES))], axis=-1)
    o_s = _sel_attn(qt, sel, kaug, kvt, 3 * G, gates)
    o_w = _win_attn(qt, kw, kvt, 5 * G, gates)

    sub = min(MERGE_SUB, T)
    tri = jnp.asarray(np.triu(np.ones((sub, sub), np.float32), 1), BF16)
    x1, x1t, ri, rg, cnt = _merge(x.reshape(N, D), ya_pre.reshape(N, CONV_CH), o_c, o_s, o_w, wmg,
                             w_up_conv.astype(BF16), w_up_nsa.astype(BF16), w_o.astype(BF16),
                             ln1_g[None, :], ln1_b[None, :], w_router.T, b_router[:, None], tri, alpha)

    counts = cnt[:, 0]
    padded = (counts + MOE_BLK - 1) // MOE_BLK * MOE_BLK
    pend = jnp.cumsum(padded).astype(I32)
    pstart = pend - padded
    idx = ri[:TOP_K]
    pstart_sel = jnp.zeros_like(idx)
    for e in range(N_EXPERTS):
        pstart_sel = jnp.where(idx == e, pstart[e], pstart_sel)
    dest = (pstart_sel + ri[TOP_K:]).T.reshape(N * TOP_K)
    n_blocks = (N * TOP_K) // MOE_BLK + N_EXPERTS
    blk_start = jnp.arange(n_blocks, dtype=I32) * MOE_BLK
    block_expert = jnp.minimum(jnp.sum((blk_start[:, None] >= pend[None, :]).astype(I32), axis=1), N_EXPERTS - 1)
    n_used = (pend[-1:] // MOE_BLK).astype(I32)

    xs = _dispatch(dest, counts, pstart, pend, x1t, n_blocks * MOE_BLK)
    ys = _experts(block_expert, n_used, xs, w_gate_up, jnp.asarray(_deinterleave_perm(), BF16),
                  b_gate_up[:, None, 0::2], b_gate_up[:, None, 1::2], w_down, b_down[:, None, :])
    out = _combine(dest, ys, x1, rg.T, ln2_g[None, :], ln2_b[None, :], alpha)
    return out.reshape(B, T, D)


def kernel(x, positions, w_in, conv_w, cmp_pos_k, cmp_w1_k, cmp_w2_k, cmp_pos_v, cmp_w1_v, cmp_w2_v, w_up_conv, w_up_nsa, w_o, ln1_g, ln1_b, w_router, b_router, w_gate_up, b_gate_up, w_down, b_down, ln2_g, ln2_b):
    depth = w_in.shape[0]
    alpha = float((2 * depth) ** 0.25)
    h = x
    for l in range(depth):
        h = _layer(h, positions, w_in[l], conv_w[l], cmp_pos_k[l], cmp_w1_k[l], cmp_w2_k[l],
                   cmp_pos_v[l], cmp_w1_v[l], cmp_w2_v[l], w_up_conv[l], w_up_nsa[l], w_o[l],
                   ln1_g[l], ln1_b[l], w_router[l], b_router[l], w_gate_up[l], b_gate_up[l],
                   w_down[l], b_down[l], ln2_g[l], ln2_b[l], alpha)
    return h
```

```python
import functools

import numpy as np
import jax
import jax.numpy as jnp
from jax import lax
from jax.experimental import pallas as pl
from jax.experimental.pallas import tpu as pltpu

F32 = jnp.float32
BF16 = jnp.bfloat16
I32 = jnp.int32

D_MODEL = 1024
CONV_CH = 512
CONV_K = 3
N_HEADS = 8
N_KV_GROUPS = 2
HEADS_PER_GROUP = N_HEADS // N_KV_GROUPS
HEAD_DIM = 64
NSA_WIDTH = N_HEADS * HEAD_DIM
KV_WIDTH = N_KV_GROUPS * HEAD_DIM
ROPE_DIM = HEAD_DIM // 4
ROPE_THETA = 500000.0
CMP_BLOCK = 32
CMP_STRIDE = 16
CMP_HIDDEN = 256
SEL_BLOCK = 64
N_SELECT = 16
WINDOW = 512
N_EXPERTS = 32
TOP_K = 4
D_FF = 1024
SWIGLU_LIMIT = 7.0
SWIGLU_ALPHA = 1.702
LN_EPS = 1e-5
NEG_INF = -1e30
FORCE_SCORE = 1e4
LOG2E = 1.4426950408889634

LANES = 128
ROW_SUB = D_MODEL // LANES
SEL_LANES = 64
SEL_MASK_BIAS = -32768.0
GROUP_W = HEADS_PER_GROUP * HEAD_DIM
GATE_ROWS = 16

PROJ_TM = 1024
CMP_TQ = 256
SEL_TQ = 512
SEL_KC = 512
SEL_SPLIT = 2
WIN_TQ = 512
WIN_SUB = 128
MERGE_TM = 512
MERGE_SUB = 256
MOE_BLK = 512
DISP_TM = 512
COMB_TM = 256
COMB_GROUP = 128


def _dot(a, b, precision=None):
    return jnp.dot(a, b, precision=precision, preferred_element_type=F32)


def _dot_nt(a, b, precision=None):
    return lax.dot_general(a, b, (((1,), (1,)), ((), ())), precision=precision, preferred_element_type=F32)


def _proj_kernel(x_ref, pos_ref, wc_ref, wqt_ref, wkvt_ref, wgt_ref, cw_ref, freq_ref,
                 ya_ref, qt_ref, kvt_ref, gt_ref, carry_ref):
    ti = pl.program_id(1)
    tm = x_ref.shape[1]
    xb = x_ref[0].astype(BF16)

    pc = _dot(xb, wc_ref[...])
    xv = pc[:, :CONV_CH]
    bg = pc[:, CONV_CH:2 * CONV_CH]
    cg = pc[:, 2 * CONV_CH:]
    u = cg * xv

    @pl.when(ti == 0)
    def _():
        carry_ref[...] = jnp.zeros_like(carry_ref)

    prev = carry_ref[...]
    row = lax.broadcasted_iota(I32, u.shape, 0)
    u1 = jnp.where(row == 0, prev[7:8], pltpu.roll(u, 1, 0))
    u2 = jnp.where(row == 0, prev[6:7], jnp.where(row == 1, prev[7:8], pltpu.roll(u, 2, 0)))
    cw = cw_ref[...]
    conv = cw[2:3] * u + cw[1:2] * u1 + cw[0:1] * u2
    carry_ref[...] = u[tm - 8:]
    ya_ref[0] = (bg * conv).astype(BF16)

    ang = freq_ref[...] * pos_ref[0]
    cos = jnp.cos(ang)
    sin = jnp.sin(ang)
    half = ROPE_DIM // 2

    def rope_head(t):
        t1 = t[:half]
        t2 = t[half:ROPE_DIM]
        return [t1 * cos - t2 * sin, t2 * cos + t1 * sin, t[ROPE_DIM:]]

    def heads(t, rotate):
        out = []
        for h in range(t.shape[0] // HEAD_DIM):
            th = t[h * HEAD_DIM:(h + 1) * HEAD_DIM]
            out.extend(rope_head(th) if rotate(h) else [th])
        return jnp.concatenate(out, axis=0)

    qt = heads(_dot_nt(wqt_ref[...], xb), lambda h: True)
    qt_ref[0] = (qt * (HEAD_DIM ** -0.5 * LOG2E)).astype(BF16)
    kvt = heads(_dot_nt(wkvt_ref[...], xb), lambda h: (h // N_KV_GROUPS) % 2 == 0)
    kvt_ref[0] = kvt.astype(BF16)
    gt_ref[0] = jax.nn.sigmoid(_dot_nt(wgt_ref[...], xb))


def _proj(x, pos_row, wc, wqt, wkvt, wgt, cw, freq):
    B, T, D = x.shape
    tm = min(PROJ_TM, T)
    grid = (B, T // tm)
    full = lambda a: pl.BlockSpec(a.shape, lambda b, t: (0,) * a.ndim)
    tok = lambda w: pl.BlockSpec((1, tm, w), lambda b, t: (b, t, 0))
    feat = lambda r: pl.BlockSpec((1, r, tm), lambda b, t: (b, 0, t))
    n_gate = wgt.shape[0]
    return pl.pallas_call(
        _proj_kernel,
        grid=grid,
        in_specs=[tok(D), feat(1), full(wc), full(wqt), full(wkvt), full(wgt), full(cw), full(freq)],
        out_specs=[tok(CONV_CH), feat(NSA_WIDTH), feat(6 * KV_WIDTH), feat(n_gate)],
        out_shape=[jax.ShapeDtypeStruct((B, T, CONV_CH), BF16),
                   jax.ShapeDtypeStruct((B, NSA_WIDTH, T), BF16),
                   jax.ShapeDtypeStruct((B, 6 * KV_WIDTH, T), BF16),
                   jax.ShapeDtypeStruct((B, n_gate, T), F32)],
        scratch_shapes=[pltpu.VMEM((8, CONV_CH), F32)],
        compiler_params=pltpu.CompilerParams(dimension_semantics=("arbitrary", "arbitrary")),
        name="proj",
    )(x, pos_row, wc, wqt, wkvt, wgt, cw, freq)


def _compress_kernel(x_ref, w1_ref, w2_ref, w2t_ref, pos_ref, o_ref, ot_ref):
    xb = x_ref[0, 0]
    nc = xb.shape[0]
    w1 = w1_ref[0]
    w1b = w1.astype(BF16)
    half = CMP_STRIDE * HEAD_DIM
    a = _dot(xb, w1b[:half])
    b = _dot(xb, w1b[half:])
    b_next = pltpu.roll(b, nc - 1, 0)
    pb = _dot(pos_ref[0], w1, precision=lax.Precision.HIGHEST)[0:1]
    h = a + b_next + pb
    g = (0.5 * h * (1.0 + jnp.tanh(np.sqrt(2.0 / np.pi) * (h + 0.044715 * (h * h * h))))).astype(BF16)
    o_ref[0, 0] = _dot(g, w2_ref[0].astype(BF16)).astype(BF16)
    ot_ref[0, 0] = _dot_nt(w2t_ref[0].astype(BF16), g).astype(BF16)


def _compress(xin, w1, w2, w2t, pos):
    _, BG, nc, W = xin.shape
    per = lambda a: pl.BlockSpec((1,) + a.shape[1:], lambda s, i: (s, 0, 0))
    return pl.pallas_call(
        _compress_kernel,
        grid=(2, BG),
        in_specs=[pl.BlockSpec((1, 1, nc, W), lambda s, i: (s, i, 0, 0)), per(w1), per(w2), per(w2t), per(pos)],
        out_specs=[pl.BlockSpec((1, 1, nc, HEAD_DIM), lambda s, i: (s, i, 0, 0)),
                   pl.BlockSpec((1, 1, HEAD_DIM, nc), lambda s, i: (s, i, 0, 0))],
        out_shape=[jax.ShapeDtypeStruct((2, BG, nc, HEAD_DIM), BF16),
                   jax.ShapeDtypeStruct((2, BG, HEAD_DIM, nc), BF16)],
        name="compress",
    )(xin, w1, w2, w2t, pos)


def _head_lanes(qt):
    return jnp.concatenate([qt[r * HEAD_DIM:(r + 1) * HEAD_DIM] for r in range(HEADS_PER_GROUP)], axis=1)


def _gated_out(ot, scale, gates, branch, tq):
    rows = []
    for r in range(HEADS_PER_GROUP):
        c = branch * HEADS_PER_GROUP + r
        sl = slice(r * tq, (r + 1) * tq)
        rows.append(ot[:, sl] * (scale[:, sl] * gates[c:c + 1]))
    return jnp.concatenate(rows, axis=0).astype(BF16)


def _attn_specs(tq):
    G = N_KV_GROUPS
    qspec = pl.BlockSpec((1, GROUP_W, tq), lambda b, g, i: (b, g, i))
    gspec = pl.BlockSpec((1, GATE_ROWS, tq), lambda b, g, i: (b, g, i))
    bg4 = lambda a: pl.BlockSpec((1, 1) + a.shape[2:], lambda b, g, i: (b, g, 0, 0))
    return G, qspec, gspec, bg4


def _cmp_attn_kernel(q_ref, kc_ref, vct_ref, g_ref, ovt_ref, o_ref, sel_ref):
    qi = pl.program_id(2)
    tq = q_ref.shape[2]
    q0 = qi * tq
    qt = _head_lanes(q_ref[0])
    s = _dot(kc_ref[0, 0], qt)
    c = lax.broadcasted_iota(I32, s.shape, 0)
    t = q0 + (lax.broadcasted_iota(I32, (1, s.shape[1]), 1) & (tq - 1))
    valid = c <= (t - (CMP_BLOCK - 1)) // CMP_STRIDE
    sm = jnp.where(valid, s, NEG_INF)
    m = jnp.max(sm, axis=0, keepdims=True)
    p = jnp.where(valid, jnp.exp2(sm - m), 0.0)
    l = jnp.sum(p, axis=0, keepdims=True)
    inv = 1.0 / jnp.where(l > 0.0, l, 1.0)
    ot = _dot(vct_ref[0, 0], p.astype(BF16))
    o_ref[0] = _gated_out(ot, inv, g_ref[0], 0, tq)

    pn = p * inv
    psum = pn[:, 0:tq]
    for r in range(1, HEADS_PER_GROUP):
        psum = psum + pn[:, r * tq:(r + 1) * tq]
    imp = _dot(ovt_ref[...], psum, precision=lax.Precision.HIGHEST)
    j = lax.broadcasted_iota(I32, imp.shape, 0)
    cur = (q0 + lax.broadcasted_iota(I32, imp.shape, 1)) // SEL_BLOCK
    valid_b = j <= cur
    forced = (j == 0) | (j == cur) | (j == cur - 1)
    score = jnp.where(valid_b, jnp.where(forced, FORCE_SCORE, imp), NEG_INF)
    sub = 8
    groups = [score[a:a + sub] for a in range(0, SEL_LANES, sub)]
    jrow = lax.broadcasted_iota(I32, groups[0].shape, 0)
    ranks = [jnp.zeros(g_.shape, I32) for g_ in groups]
    for i in range(SEL_LANES):
        si = score[i:i + 1, :]
        for a, g_ in enumerate(groups):
            if a > i // sub:
                inc = jnp.where(si >= g_, 1, 0)
            elif a < i // sub:
                inc = jnp.where(si > g_, 1, 0)
            else:
                inc = jnp.where(jrow > i % sub, jnp.where(si >= g_, 1, 0), jnp.where(si > g_, 1, 0))
            ranks[a] = ranks[a] + inc
    rank = jnp.concatenate(ranks, axis=0)
    selected = (rank < N_SELECT) & valid_b
    sel_ref[0, 0] = jnp.where(selected, 0.0, SEL_MASK_BIAS).astype(BF16)


def _cmp_attn(qt, kcmp, vcmp_t, gates, ovt):
    B, _, T = qt.shape
    tq = min(CMP_TQ, T)
    G, qspec, gspec, _ = _attn_specs(tq)
    cspec = lambda a: pl.BlockSpec((1, 1) + a.shape[2:], lambda b, g, i: (0, b * G + g, 0, 0))
    vspec = lambda a: pl.BlockSpec((1, 1) + a.shape[2:], lambda b, g, i: (1, b * G + g, 0, 0))
    return pl.pallas_call(
        _cmp_attn_kernel,
        grid=(B, G, T // tq),
        in_specs=[qspec, cspec(kcmp), vspec(vcmp_t), gspec, pl.BlockSpec(ovt.shape, lambda b, g, i: (0, 0))],
        out_specs=[qspec, pl.BlockSpec((1, 1, SEL_LANES, tq), lambda b, g, i: (b, g, 0, i))],
        out_shape=[jax.ShapeDtypeStruct((B, NSA_WIDTH, T), BF16),
                   jax.ShapeDtypeStruct((B, G, SEL_LANES, T), BF16)],
        name="cmp_attn",
    )(qt, kcmp, vcmp_t, gates, ovt)


def _sel_attn_kernel(q_ref, sel_ref, k_ref, vt_ref, g_ref, o_ref, *, kc):
    qi = pl.program_id(2)
    tq = q_ref.shape[2]
    q0 = qi * tq
    qt = q_ref[0]
    sb = sel_ref[0, 0]
    qa = jnp.concatenate(
        [jnp.concatenate([qt[r * HEAD_DIM:(r + 1) * HEAD_DIM], sb], axis=0) for r in range(HEADS_PER_GROUP)],
        axis=1)
    gw = qa.shape[1] // SEL_SPLIT
    qas = [qa[:, i * gw:(i + 1) * gw] for i in range(SEL_SPLIT)]

    def step(kstart, kn, carry, diagonal):
        kblk = k_ref[0, 0, pl.ds(kstart, kn), :]
        vblk = vt_ref[0, :, pl.ds(kstart, kn)]
        ss = [_dot(kblk, qg) for qg in qas]
        out = []
        for gi, (s, (m, l, acc)) in enumerate(zip(ss, carry)):
            if diagonal:
                row = lax.broadcasted_iota(I32, s.shape, 0)
                off = (gi * gw + lax.broadcasted_iota(I32, s.shape, 1)) & (tq - 1)
                s = jnp.where(row <= off, s, NEG_INF)
            mn = jnp.maximum(m, jnp.max(s, axis=0, keepdims=True))
            alpha = jnp.exp2(m - mn)
            p = jnp.exp2(s - mn)
            l = alpha * l + jnp.sum(p, axis=0, keepdims=True)
            acc = alpha * acc + _dot(vblk, p.astype(BF16))
            out.append((mn, l, acc))
        return tuple(out)

    init = tuple((jnp.full((1, gw), NEG_INF, F32), jnp.zeros((1, gw), F32), jnp.zeros((HEAD_DIM, gw), F32))
                 for _ in range(SEL_SPLIT))
    n_full = q0 // kc
    carry = lax.fori_loop(0, n_full, lambda i, cr: step(pl.multiple_of(i * kc, kc), kc, cr, False), init)
    carry = lax.fori_loop(n_full * (kc // tq), qi, lambda i, cr: step(pl.multiple_of(i * tq, tq), tq, cr, False), carry)
    carry = step(pl.multiple_of(q0, tq), tq, carry, True)
    l = jnp.concatenate([c[1] for c in carry], axis=1)
    acc = jnp.concatenate([c[2] for c in carry], axis=1)
    o_ref[0] = _gated_out(acc, 1.0 / l, g_ref[0], 1, tq)


def _sel_attn(qt, sel, kaug, kvt, v_row_block, gates):
    B, _, T = qt.shape
    tq = min(SEL_TQ, T)
    kc = min(SEL_KC, T)
    G, qspec, gspec, bg4 = _attn_specs(tq)
    return pl.pallas_call(
        functools.partial(_sel_attn_kernel, kc=kc),
        grid=(B, G, T // tq),
        in_specs=[qspec, pl.BlockSpec((1, 1, SEL_LANES, tq), lambda b, g, i: (b, g, 0, i)), bg4(kaug),
                  pl.BlockSpec((1, HEAD_DIM, T), lambda b, g, i: (b, v_row_block + g, 0)), gspec],
        out_specs=qspec,
        out_shape=jax.ShapeDtypeStruct((B, NSA_WIDTH, T), BF16),
        name="sel_attn",
    )(qt, sel, kaug, kvt, gates)


def _win_attn_kernel(q_ref, k_ref, vt_ref, g_ref, o_ref, *, span, sub):
    qi = pl.program_id(2)
    tq = q_ref.shape[2]
    T = k_ref.shape[2]

    def tile(s0, interior):
        cs = slice(s0, s0 + sub)
        q0 = qi * tq + s0
        start = pl.multiple_of(jnp.clip(q0 + sub - span, 0, T - span), sub)
        qt = _head_lanes(q_ref[0, :, cs])
        s = _dot(k_ref[0, 0, pl.ds(start, span), :], qt)
        if interior:
            row = lax.broadcasted_iota(I32, (sub, s.shape[1]), 0)
            off = lax.broadcasted_iota(I32, (sub, s.shape[1]), 1) & (sub - 1)
            s = jnp.concatenate([jnp.where(row > off, s[:sub], NEG_INF), s[sub:span - sub],
                                 jnp.where(row <= off, s[span - sub:], NEG_INF)], axis=0)
        else:
            kp = start + lax.broadcasted_iota(I32, s.shape, 0)
            t = q0 + (lax.broadcasted_iota(I32, s.shape, 1) & (sub - 1))
            diff = t - kp
            s = jnp.where((diff >= 0) & (diff < WINDOW), s, NEG_INF)
        m = jnp.max(s, axis=0, keepdims=True)
        p = jnp.exp2(s - m)
        l = jnp.sum(p, axis=0, keepdims=True)
        ot = _dot(vt_ref[0, :, pl.ds(start, span)], p.astype(BF16))
        o_ref[0, :, cs] = _gated_out(ot, 1.0 / l, g_ref[0, :, cs], 2, sub)

    first_interior = -(-WINDOW // tq)
    if span == WINDOW + sub:
        @pl.when(qi >= first_interior)
        def _():
            for s0 in range(0, tq, sub):
                tile(s0, True)

    @pl.when((qi < first_interior) | (span != WINDOW + sub))
    def _():
        for s0 in range(0, tq, sub):
            tile(s0, False)


def _win_attn(qt, k, kvt, v_row_block, gates):
    B, _, T = qt.shape
    tq = min(WIN_TQ, T)
    sub = min(WIN_SUB, T)
    span = min(WINDOW + sub, T)
    G, qspec, gspec, bg4 = _attn_specs(tq)
    return pl.pallas_call(
        functools.partial(_win_attn_kernel, span=span, sub=sub),
        grid=(B, G, T // tq),
        in_specs=[qspec, bg4(k), pl.BlockSpec((1, HEAD_DIM, T), lambda b, g, i: (b, v_row_block + g, 0)), gspec],
        out_specs=qspec,
        out_shape=jax.ShapeDtypeStruct((B, NSA_WIDTH, T), BF16),
        name="win_attn",
    )(qt, k, kvt, gates)


def _store_row_tiles(ref, x, row0=0):
    rows = x.shape[0]
    for s in range(ROW_SUB):
        ref[pl.ds(row0 * ROW_SUB + s, rows, stride=ROW_SUB), :] = x[:, s * LANES:(s + 1) * LANES]


def _load_row_tiles(ref):
    rows = ref.shape[0] // ROW_SUB
    return jnp.concatenate([ref[pl.ds(s, rows, stride=ROW_SUB), :] for s in range(ROW_SUB)], axis=1)


def _tile_copy(src, si, dst, di, sem):
    return pltpu.make_async_copy(src.at[pl.ds(pl.multiple_of(si * ROW_SUB, ROW_SUB), ROW_SUB), :],
                                 dst.at[pl.ds(pl.multiple_of(di * ROW_SUB, ROW_SUB), ROW_SUB), :], sem)


def _layer_norm(h, g, b):
    mu = jnp.mean(h, axis=-1, keepdims=True)
    c = h - mu
    var = jnp.mean(c * c, axis=-1, keepdims=True)
    return c * lax.rsqrt(var + LN_EPS) * g + b


def _merge_kernel(x_ref, ya_ref, oc_ref, os_ref, ow_ref, wmg_ref, wuc_ref, wun_ref, wo_ref, g1_ref, b1_ref,
                  wr_ref, br_ref, tri_ref, x1_ref, x1t_ref, ri_ref, rg_ref, cnt_ref, carry_ref, *, alpha):
    i = pl.program_id(0)
    sub = tri_ref.shape[0]

    @pl.when(i == 0)
    def _():
        carry_ref[...] = jnp.zeros_like(carry_ref)

    wr = wr_ref[...]
    wr_hi = wr.astype(BF16)
    wr_hl = jnp.concatenate([wr_hi, (wr - wr_hi.astype(F32)).astype(BF16)], axis=0)
    total = carry_ref[...]
    for s0 in range(0, x_ref.shape[0], sub):
        rs = slice(s0, s0 + sub)
        x = x_ref[rs, :]
        mg = _dot(x.astype(BF16), wmg_ref[...])
        y_a = _dot(ya_ref[rs, :], wuc_ref[...])
        o_nsa_t = (oc_ref[0, :, rs].astype(F32) + os_ref[0, :, rs].astype(F32)
                   + ow_ref[0, :, rs].astype(F32))
        y_b = _dot(o_nsa_t.T.astype(BF16), wun_ref[...])
        merged = jax.nn.sigmoid(mg[:, :D_MODEL]) * y_a + jax.nn.sigmoid(mg[:, D_MODEL:]) * y_b
        h = alpha * x + _dot(merged.astype(BF16), wo_ref[...])
        x1 = _layer_norm(h, g1_ref[...], b1_ref[...])
        x1_ref[rs, :] = x1
        _store_row_tiles(x1t_ref, x1, s0)

        x_hi = x1.astype(BF16)
        x_lo = (x1 - x_hi.astype(F32)).astype(BF16)
        both = _dot_nt(wr_hl, x_hi)
        logits = both[:N_EXPERTS] + both[N_EXPERTS:] + _dot_nt(wr_hi, x_lo) + br_ref[...]
        expert = lax.broadcasted_iota(I32, logits.shape, 0).astype(F32)
        rem = logits
        vals, idxs, hots = [], [], []
        for _ in range(TOP_K):
            m = jnp.max(rem, axis=0, keepdims=True)
            idx = jnp.min(jnp.where(rem == m, expert, float(N_EXPERTS)), axis=0, keepdims=True)
            hot = expert == idx
            vals.append(m)
            idxs.append(idx)
            hots.append(hot)
            rem = jnp.where(hot, -3.0e38, rem)
        es = [jnp.exp(v - vals[0]) for v in vals]
        den = es[0]
        for e in es[1:]:
            den = den + e
        chosen = hots[0]
        for hot in hots[1:]:
            chosen = chosen | hot
        chosen_f = jnp.where(chosen, 1.0, 0.0)

        before = _dot(chosen_f.astype(BF16), tri_ref[...]) + total
        total = total + jnp.sum(chosen_f, axis=1, keepdims=True)
        ranks = [jnp.sum(jnp.where(hot, before, 0.0), axis=0, keepdims=True) for hot in hots]
        ri_ref[:, rs] = jnp.concatenate(idxs + ranks, axis=0).astype(I32)
        rg_ref[:, rs] = jnp.concatenate([e / den for e in es] + [jnp.zeros_like(den)] * TOP_K, axis=0)

    carry_ref[...] = total
    cnt_ref[...] = jnp.broadcast_to(total, cnt_ref.shape).astype(I32)


def _merge(x, ya, oc, os_, ow, wmg, wuc, wun, wo, g1, b1, wr, br, tri, alpha):
    N, D = x.shape
    B, _, T = oc.shape
    tm = min(MERGE_TM, T)
    nt = T // tm
    tok = lambda w: pl.BlockSpec((tm, w), lambda i: (i, 0))
    feat = pl.BlockSpec((1, NSA_WIDTH, tm), lambda i: (i // nt, 0, i % nt))
    full = lambda a: pl.BlockSpec(a.shape, lambda i: (0,) * a.ndim)
    return pl.pallas_call(
        functools.partial(_merge_kernel, alpha=alpha),
        grid=(N // tm,),
        in_specs=[tok(D), tok(CONV_CH), feat, feat, feat,
                  full(wmg), full(wuc), full(wun), full(wo), full(g1), full(b1), full(wr), full(br), full(tri)],
        out_specs=[tok(D), pl.BlockSpec((tm * ROW_SUB, LANES), lambda i: (i, 0)),
                   pl.BlockSpec((2 * TOP_K, tm), lambda i: (0, i)), pl.BlockSpec((2 * TOP_K, tm), lambda i: (0, i)),
                   pl.BlockSpec((N_EXPERTS, LANES), lambda i: (0, 0))],
        out_shape=[jax.ShapeDtypeStruct((N, D), F32),
                   jax.ShapeDtypeStruct((N * ROW_SUB, LANES), F32),
                   jax.ShapeDtypeStruct((2 * TOP_K, N), I32),
                   jax.ShapeDtypeStruct((2 * TOP_K, N), F32),
                   jax.ShapeDtypeStruct((N_EXPERTS, LANES), I32)],
        scratch_shapes=[pltpu.VMEM((N_EXPERTS, 1), F32)],
        compiler_params=pltpu.CompilerParams(dimension_semantics=("arbitrary",)),
        name="merge",
    )(x, ya, oc, os_, ow, wmg, wuc, wun, wo, g1, b1, wr, br, tri)


def _dispatch_kernel(dest_ref, cnt_ref, pstart_ref, pend_ref, x_ref, xs_hbm, zero_ref, sem, zsem):
    i = pl.program_id(0)
    n = dest_ref.shape[0]

    @pl.when(i == 0)
    def _():
        zero_ref[...] = jnp.zeros_like(zero_ref)
        for e in range(N_EXPERTS):
            lo = pstart_ref[e] + cnt_ref[e]
            hi = pend_ref[e]

            def zstart(r, c):
                _tile_copy(zero_ref, 0, xs_hbm, r, zsem).start()
                return c

            def zwait(r, c):
                _tile_copy(zero_ref, 0, xs_hbm, r, zsem).wait()
                return c

            lax.fori_loop(lo, hi, zstart, 0)
            lax.fori_loop(lo, hi, zwait, 0)

    def start(t, c):
        for k in range(TOP_K):
            _tile_copy(x_ref, t, xs_hbm, dest_ref[t * TOP_K + k], sem).start(priority=k % 2)
        return c

    lax.fori_loop(0, n // TOP_K, start, 0)
    for k in range(TOP_K):
        pltpu.make_async_copy(x_ref, xs_hbm.at[pl.ds(0, x_ref.shape[0]), :], sem).wait()


def _dispatch(dest, cnt, pstart, pend, x1t, n_rows):
    N = x1t.shape[0] // ROW_SUB
    tm = min(DISP_TM, N)
    smem = pl.BlockSpec(memory_space=pltpu.SMEM)
    return pl.pallas_call(
        _dispatch_kernel,
        grid=(N // tm,),
        in_specs=[pl.BlockSpec((tm * TOP_K,), lambda i: (i,), memory_space=pltpu.SMEM), smem, smem, smem,
                  pl.BlockSpec((tm * ROW_SUB, LANES), lambda i: (i, 0))],
        out_specs=pl.BlockSpec(memory_space=pl.ANY),
        out_shape=jax.ShapeDtypeStruct((n_rows * ROW_SUB, LANES), F32),
        scratch_shapes=[pltpu.VMEM((ROW_SUB, LANES), F32), pltpu.SemaphoreType.DMA(()), pltpu.SemaphoreType.DMA(())],
        compiler_params=pltpu.CompilerParams(dimension_semantics=("arbitrary",)),
        name="dispatch",
    )(dest, cnt, pstart, pend, x1t)


def _experts_kernel(be_ref, nu_ref, xs_ref, wgu_ref, perm_ref, bg_ref, bl_ref, wd_ref, bd_ref, ys_ref, wg_s, wl_s, wd_s):
    i = pl.program_id(0)

    @pl.when((i == 0) | (be_ref[i] != be_ref[jnp.maximum(i, 1) - 1]))
    def _():
        w = perm_ref.shape[0]
        for c in range(2 * D_FF // w):
            t = _dot(wgu_ref[0, :, c * w:(c + 1) * w].astype(BF16), perm_ref[...])
            wg_s[:, c * (w // 2):(c + 1) * (w // 2)] = t[:, :w // 2].astype(BF16)
            wl_s[:, c * (w // 2):(c + 1) * (w // 2)] = t[:, w // 2:].astype(BF16)
        wd_s[...] = wd_ref[0].astype(BF16)

    @pl.when(i < nu_ref[0])
    def _():
        xb = _load_row_tiles(xs_ref).astype(BF16)
        x_glu = jnp.minimum(_dot(xb, wg_s[...]) + bg_ref[0], SWIGLU_LIMIT)
        x_lin = jnp.clip(_dot(xb, wl_s[...]) + bl_ref[0], -SWIGLU_LIMIT, SWIGLU_LIMIT)
        act = x_glu * jax.nn.sigmoid(SWIGLU_ALPHA * x_glu) * (x_lin + 1.0)
        _store_row_tiles(ys_ref, _dot(act.astype(BF16), wd_s[...]) + bd_ref[0])


def _experts(block_expert, n_used, xs, wgu, perm, bg, bl, wd, bd):
    P = xs.shape[0] // ROW_SUB
    D = D_MODEL
    n_blocks = P // MOE_BLK
    rows = lambda i, be, nu: (jnp.minimum(i, nu[0] - 1), 0)
    wsel = lambda i, be, nu: (be[i], 0, 0)
    return pl.pallas_call(
        _experts_kernel,
        grid_spec=pltpu.PrefetchScalarGridSpec(
            num_scalar_prefetch=2,
            grid=(n_blocks,),
            in_specs=[pl.BlockSpec((MOE_BLK * ROW_SUB, LANES), rows),
                      pl.BlockSpec((1, D, 2 * D_FF), wsel),
                      pl.BlockSpec(perm.shape, lambda i, be, nu: (0, 0)),
                      pl.BlockSpec((1, 1, D_FF), wsel), pl.BlockSpec((1, 1, D_FF), wsel),
                      pl.BlockSpec((1, D_FF, D), wsel), pl.BlockSpec((1, 1, D), wsel)],
            out_specs=pl.BlockSpec((MOE_BLK * ROW_SUB, LANES), rows),
            scratch_shapes=[pltpu.VMEM((D, D_FF), BF16), pltpu.VMEM((D, D_FF), BF16), pltpu.VMEM((D_FF, D), BF16)]),
        out_shape=jax.ShapeDtypeStruct((P * ROW_SUB, LANES), F32),
        compiler_params=pltpu.CompilerParams(dimension_semantics=("arbitrary",)),
        name="experts",
    )(block_expert, n_used, xs, wgu, perm, bg, bl, wd, bd)


def _combine_kernel(dest_ref, dest_next_ref, ys_hbm, x1_ref, rg_ref, g2_ref, b2_ref, o_ref, buf, sem, *, alpha):
    i = pl.program_id(0)
    last = pl.num_programs(0) - 1
    tm = o_ref.shape[0]
    slot = i % 2
    other = 1 - slot

    def gather(ids_ref, t, to_slot):
        for k in range(TOP_K):
            _tile_copy(ys_hbm, ids_ref[t * TOP_K + k], buf.at[to_slot, k], t, sem.at[to_slot]).start(priority=k % 2)

    def wait_slot(s):
        for k in range(TOP_K):
            pltpu.make_async_copy(ys_hbm.at[pl.ds(0, tm * ROW_SUB), :], buf.at[s, k], sem.at[s]).wait()

    @pl.when(i == 0)
    def _():
        def first(t, c):
            gather(dest_ref, t, slot)
            return c
        lax.fori_loop(0, tm, first, 0)

    wait_slot(slot)

    gs = min(COMB_GROUP, tm)

    def group(g, c):
        r0 = pl.multiple_of(g * gs, gs)
        for j in range(gs):
            gather(dest_next_ref, r0 + j, other)
        rows = pl.ds(r0, gs)
        gate = rg_ref[rows, :]
        y = alpha * x1_ref[rows, :]
        for k in range(TOP_K):
            tiles = buf.at[slot, k, pl.ds(pl.multiple_of(r0 * ROW_SUB, gs * ROW_SUB), gs * ROW_SUB), :]
            y = y + gate[:, k:k + 1] * _load_row_tiles(tiles)
        o_ref[rows, :] = _layer_norm(y, g2_ref[...], b2_ref[...])
        return c

    lax.fori_loop(0, tm // gs, group, 0)

    @pl.when(i == last)
    def _():
        wait_slot(other)


def _combine(dest, ys, x1, rg, g2, b2, alpha):
    N, D = x1.shape
    tm = min(COMB_TM, N)
    n = N // tm
    tok = lambda w: pl.BlockSpec((tm, w), lambda i: (i, 0))
    full = lambda a: pl.BlockSpec(a.shape, lambda i: (0,) * a.ndim)
    ids = lambda f: pl.BlockSpec((tm * TOP_K,), f, memory_space=pltpu.SMEM)
    return pl.pallas_call(
        functools.partial(_combine_kernel, alpha=alpha),
        grid=(n,),
        in_specs=[ids(lambda i: (i,)), ids(lambda i: (jnp.minimum(i + 1, n - 1),)),
                  pl.BlockSpec(memory_space=pl.ANY), tok(D), tok(rg.shape[1]), full(g2), full(b2)],
        out_specs=tok(D),
        out_shape=jax.ShapeDtypeStruct((N, D), F32),
        scratch_shapes=[pltpu.VMEM((2, TOP_K, tm * ROW_SUB, LANES), F32), pltpu.SemaphoreType.DMA((2,))],
        compiler_params=pltpu.CompilerParams(dimension_semantics=("arbitrary",)),
        name="combine",
    )(dest, dest, ys, x1, rg, g2, b2)


def _rope_freq():
    half = ROPE_DIM // 2
    inv = (np.float32(ROPE_THETA) ** (-np.arange(half, dtype=np.float32) * np.float32(2.0 / ROPE_DIM))).astype(np.float32)
    return inv[:, None]


def _overlap_t(T):
    nc = T // CMP_STRIDE
    c0 = np.arange(nc) * CMP_STRIDE
    j0 = np.arange(SEL_LANES) * SEL_BLOCK
    ov = (c0[None, :] < j0[:, None] + SEL_BLOCK) & (c0[None, :] + CMP_BLOCK > j0[:, None])
    ov &= (np.arange(nc) < nc - 1)[None, :] & (np.arange(SEL_LANES) < T // SEL_BLOCK)[:, None]
    return ov.astype(np.float32)


def _deinterleave_perm():
    w = 2 * LANES
    p = np.zeros((w, w), np.float32)
    p[np.arange(0, w, 2), np.arange(w // 2)] = 1.0
    p[np.arange(1, w, 2), w // 2 + np.arange(w // 2)] = 1.0
    return p


def _token_major(t, B, T):
    return t.reshape(B, N_KV_GROUPS, HEAD_DIM, T).transpose(0, 1, 3, 2)


def _layer(x, positions, w_in, conv_w, cmp_pos_k, cmp_w1_k, cmp_w2_k, cmp_pos_v, cmp_w1_v, cmp_w2_v,
           w_up_conv, w_up_nsa, w_o, ln1_g, ln1_b, w_router, b_router, w_gate_up, b_gate_up,
           w_down, b_down, ln2_g, ln2_b, alpha):
    B, T, D = x.shape
    G, R = N_KV_GROUPS, HEADS_PER_GROUP
    N = B * T
    assert D == D_MODEL and T % SEL_BLOCK == 0 and T // SEL_BLOCK <= SEL_LANES

    c0 = 3 * CONV_CH
    c1 = c0 + NSA_WIDTH
    c2 = c1 + 6 * KV_WIDTH
    c3 = c2 + 3 * N_HEADS
    wc = w_in[:, :c0].astype(BF16)
    wqt = w_in[:, c0:c1].T.astype(BF16)
    wkvt = w_in[:, c1:c2].T.astype(BF16)
    gcols = np.zeros((G * GATE_ROWS,), np.int64)
    gmask = np.zeros((G * GATE_ROWS,), np.float32)
    for g in range(G):
        for br in range(3):
            for r in range(R):
                gcols[g * GATE_ROWS + br * R + r] = br * N_HEADS + g * R + r
                gmask[g * GATE_ROWS + br * R + r] = 1.0
    wgt = (w_in[:, c2:c3][:, gcols] * gmask).T.astype(BF16)
    wmg = w_in[:, c3:].astype(BF16)
    cw = conv_w.reshape(CONV_K, CONV_CH)
    pos_row = positions.astype(F32)[:, None, :]

    ya_pre, qt, kvt, gates = _proj(x, pos_row, wc, wqt, wkvt, wgt, cw, jnp.asarray(_rope_freq()))

    nc = T // CMP_STRIDE
    to_chunks = lambda t: _token_major(t, B, T).reshape(B * G, nc, CMP_STRIDE * HEAD_DIM)
    xin = jnp.stack([to_chunks(kvt[:, 0:KV_WIDTH]), to_chunks(kvt[:, KV_WIDTH:2 * KV_WIDTH])])
    w1 = jnp.stack([cmp_w1_k, cmp_w1_v])
    w2 = jnp.stack([cmp_w2_k, cmp_w2_v])
    pos_flat = jnp.stack([cmp_pos_k.reshape(1, -1), cmp_pos_v.reshape(1, -1)])
    pos_flat = jnp.pad(pos_flat, ((0, 0), (0, 7), (0, 0)))
    kcmp, kcmp_t = _compress(xin, w1, w2, w2.transpose(0, 2, 1), pos_flat)

    o_c, sel = _cmp_attn(qt, kcmp, kcmp_t, gates, jnp.asarray(_overlap_t(T)))

    ks = _token_major(kvt[:, 2 * KV_WIDTH:3 * KV_WIDTH], B, T)
    kw = _token_major(kvt[:, 4 * KV_WIDTH:5 * KV_WIDTH], B, T)
    onehot = (np.arange(T)[:, None] // SEL_BLOCK == np.arange(SEL_LANES)[None, :]).astype(np.float32)
    kaug = jnp.concatenate([ks, jnp.broadcast_to(jnp.asarray(onehot, BF16), (B, G, T, SEL_LANES))], axis=-1)
    o_s = _sel_attn(qt, sel, kaug, kvt, 3 * G, gates)
    o_w = _win_attn(qt, kw, kvt, 5 * G, gates)

    sub = min(MERGE_SUB, T)
    tri = jnp.asarray(np.triu(np.ones((sub, sub), np.float32), 1), BF16)
    x1, x1t, ri, rg, cnt = _merge(x.reshape(N, D), ya_pre.reshape(N, CONV_CH), o_c, o_s, o_w, wmg,
                             w_up_conv.astype(BF16), w_up_nsa.astype(BF16), w_o.astype(BF16),
                             ln1_g[None, :], ln1_b[None, :], w_router.T, b_router[:, None], tri, alpha)

    counts = cnt[:, 0]
    padded = (counts + MOE_BLK - 1) // MOE_BLK * MOE_BLK
    pend = jnp.cumsum(padded).astype(I32)
    pstart = pend - padded
    idx = ri[:TOP_K]
    pstart_sel = jnp.zeros_like(idx)
    for e in range(N_EXPERTS):
        pstart_sel = jnp.where(idx == e, pstart[e], pstart_sel)
    dest = (pstart_sel + ri[TOP_K:]).T.reshape(N * TOP_K)
    n_blocks = (N * TOP_K) // MOE_BLK + N_EXPERTS
    blk_start = jnp.arange(n_blocks, dtype=I32) * MOE_BLK
    block_expert = jnp.minimum(jnp.sum((blk_start[:, None] >= pend[None, :]).astype(I32), axis=1), N_EXPERTS - 1)
    n_used = (pend[-1:] // MOE_BLK).astype(I32)

    xs = _dispatch(dest, counts, pstart, pend, x1t, n_blocks * MOE_BLK)
    ys = _experts(block_expert, n_used, xs, w_gate_up, jnp.asarray(_deinterleave_perm(), BF16),
                  b_gate_up[:, None, 0::2], b_gate_up[:, None, 1::2], w_down, b_down[:, None, :])
    out = _combine(dest, ys, x1, rg.T, ln2_g[None, :], ln2_b[None, :], alpha)
    return out.reshape(B, T, D)


def kernel(x, positions, w_in, conv_w, cmp_pos_k, cmp_w1_k, cmp_w2_k, cmp_pos_v, cmp_w1_v, cmp_w2_v, w_up_conv, w_up_nsa, w_o, ln1_g, ln1_b, w_router, b_router, w_gate_up, b_gate_up, w_down, b_down, ln2_g, ln2_b):
    depth = w_in.shape[0]
    alpha = float((2 * depth) ** 0.25)
    h = x
    for l in range(depth):
        h = _layer(h, positions, w_in[l], conv_w[l], cmp_pos_k[l], cmp_w1_k[l], cmp_w2_k[l],
                   cmp_pos_v[l], cmp_w1_v[l], cmp_w2_v[l], w_up_conv[l], w_up_nsa[l], w_o[l],
                   ln1_g[l], ln1_b[l], w_router[l], b_router[l], w_gate_up[l], b_gate_up[l],
                   w_down[l], b_down[l], ln2_g[l], ln2_b[l], alpha)
    return h
```

```python
import functools

import numpy as np
import jax
import jax.numpy as jnp
from jax import lax
from jax.experimental import pallas as pl
from jax.experimental.pallas import tpu as pltpu

F32 = jnp.float32
BF16 = jnp.bfloat16
I32 = jnp.int32

D_MODEL = 1024
CONV_CH = 512
CONV_K = 3
N_HEADS = 8
N_KV_GROUPS = 2
HEADS_PER_GROUP = N_HEADS // N_KV_GROUPS
HEAD_DIM = 64
NSA_WIDTH = N_HEADS * HEAD_DIM
KV_WIDTH = N_KV_GROUPS * HEAD_DIM
ROPE_DIM = HEAD_DIM // 4
ROPE_THETA = 500000.0
CMP_BLOCK = 32
CMP_STRIDE = 16
CMP_HIDDEN = 256
SEL_BLOCK = 64
N_SELECT = 16
WINDOW = 512
N_EXPERTS = 32
TOP_K = 4
D_FF = 1024
SWIGLU_LIMIT = 7.0
SWIGLU_ALPHA = 1.702
LN_EPS = 1e-5
NEG_INF = -1e30
FORCE_SCORE = 1e4
LOG2E = 1.4426950408889634

LANES = 128
ROW_SUB = D_MODEL // LANES
SEL_LANES = 64
SEL_MASK_BIAS = -32768.0
GROUP_W = HEADS_PER_GROUP * HEAD_DIM
GATE_ROWS = 16

PROJ_TM = 1024
CMP_TQ = 256
SEL_TQ = 512
SEL_KC = 512
SEL_SPLIT = 2
WIN_TQ = 512
WIN_SUB = 128
MERGE_TM = 512
MERGE_SUB = 256
MOE_BLK = 512
COMB_TM = 256
COMB_GROUP = 128


def _dot(a, b, precision=None):
    return jnp.dot(a, b, precision=precision, preferred_element_type=F32)


def _dot_nt(a, b, precision=None):
    return lax.dot_general(a, b, (((1,), (1,)), ((), ())), precision=precision, preferred_element_type=F32)


def _proj_kernel(x_ref, pos_ref, wc_ref, wqt_ref, wkvt_ref, wgt_ref, cw_ref, freq_ref,
                 ya_ref, qt_ref, kvt_ref, gt_ref, carry_ref):
    ti = pl.program_id(1)
    tm = x_ref.shape[1]
    xb = x_ref[0].astype(BF16)

    pc = _dot(xb, wc_ref[...])
    xv = pc[:, :CONV_CH]
    bg = pc[:, CONV_CH:2 * CONV_CH]
    cg = pc[:, 2 * CONV_CH:]
    u = cg * xv

    @pl.when(ti == 0)
    def _():
        carry_ref[...] = jnp.zeros_like(carry_ref)

    prev = carry_ref[...]
    row = lax.broadcasted_iota(I32, u.shape, 0)
    u1 = jnp.where(row == 0, prev[7:8], pltpu.roll(u, 1, 0))
    u2 = jnp.where(row == 0, prev[6:7], jnp.where(row == 1, prev[7:8], pltpu.roll(u, 2, 0)))
    cw = cw_ref[...]
    conv = cw[2:3] * u + cw[1:2] * u1 + cw[0:1] * u2
    carry_ref[...] = u[tm - 8:]
    ya_ref[0] = (bg * conv).astype(BF16)

    ang = freq_ref[...] * pos_ref[0]
    cos = jnp.cos(ang)
    sin = jnp.sin(ang)
    half = ROPE_DIM // 2

    def rope_head(t):
        t1 = t[:half]
        t2 = t[half:ROPE_DIM]
        return [t1 * cos - t2 * sin, t2 * cos + t1 * sin, t[ROPE_DIM:]]

    def heads(t, rotate):
        out = []
        for h in range(t.shape[0] // HEAD_DIM):
            th = t[h * HEAD_DIM:(h + 1) * HEAD_DIM]
            out.extend(rope_head(th) if rotate(h) else [th])
        return jnp.concatenate(out, axis=0)

    qt = heads(_dot_nt(wqt_ref[...], xb), lambda h: True)
    qt_ref[0] = (qt * (HEAD_DIM ** -0.5 * LOG2E)).astype(BF16)
    kvt = heads(_dot_nt(wkvt_ref[...], xb), lambda h: (h // N_KV_GROUPS) % 2 == 0)
    kvt_ref[0] = kvt.astype(BF16)
    gt_ref[0] = jax.nn.sigmoid(_dot_nt(wgt_ref[...], xb))


def _proj(x, pos_row, wc, wqt, wkvt, wgt, cw, freq):
    B, T, D = x.shape
    tm = min(PROJ_TM, T)
    grid = (B, T // tm)
    full = lambda a: pl.BlockSpec(a.shape, lambda b, t: (0,) * a.ndim)
    tok = lambda w: pl.BlockSpec((1, tm, w), lambda b, t: (b, t, 0))
    feat = lambda r: pl.BlockSpec((1, r, tm), lambda b, t: (b, 0, t))
    n_gate = wgt.shape[0]
    return pl.pallas_call(
        _proj_kernel,
        grid=grid,
        in_specs=[tok(D), feat(1), full(wc), full(wqt), full(wkvt), full(wgt), full(cw), full(freq)],
        out_specs=[tok(CONV_CH), feat(NSA_WIDTH), feat(6 * KV_WIDTH), feat(n_gate)],
        out_shape=[jax.ShapeDtypeStruct((B, T, CONV_CH), BF16),
                   jax.ShapeDtypeStruct((B, NSA_WIDTH, T), BF16),
                   jax.ShapeDtypeStruct((B, 6 * KV_WIDTH, T), BF16),
                   jax.ShapeDtypeStruct((B, n_gate, T), F32)],
        scratch_shapes=[pltpu.VMEM((8, CONV_CH), F32)],
        compiler_params=pltpu.CompilerParams(dimension_semantics=("arbitrary", "arbitrary")),
        name="proj",
    )(x, pos_row, wc, wqt, wkvt, wgt, cw, freq)


def _compress_kernel(x_ref, w1_ref, w2_ref, w2t_ref, pos_ref, o_ref, ot_ref):
    xb = x_ref[0, 0]
    nc = xb.shape[0]
    w1 = w1_ref[0]
    w1b = w1.astype(BF16)
    half = CMP_STRIDE * HEAD_DIM
    a = _dot(xb, w1b[:half])
    b = _dot(xb, w1b[half:])
    b_next = pltpu.roll(b, nc - 1, 0)
    pb = _dot(pos_ref[0], w1, precision=lax.Precision.HIGHEST)[0:1]
    h = a + b_next + pb
    g = (0.5 * h * (1.0 + jnp.tanh(np.sqrt(2.0 / np.pi) * (h + 0.044715 * (h * h * h))))).astype(BF16)
    o_ref[0, 0] = _dot(g, w2_ref[0].astype(BF16)).astype(BF16)
    ot_ref[0, 0] = _dot_nt(w2t_ref[0].astype(BF16), g).astype(BF16)


def _compress(xin, w1, w2, w2t, pos):
    _, BG, nc, W = xin.shape
    per = lambda a: pl.BlockSpec((1,) + a.shape[1:], lambda s, i: (s, 0, 0))
    return pl.pallas_call(
        _compress_kernel,
        grid=(2, BG),
        in_specs=[pl.BlockSpec((1, 1, nc, W), lambda s, i: (s, i, 0, 0)), per(w1), per(w2), per(w2t), per(pos)],
        out_specs=[pl.BlockSpec((1, 1, nc, HEAD_DIM), lambda s, i: (s, i, 0, 0)),
                   pl.BlockSpec((1, 1, HEAD_DIM, nc), lambda s, i: (s, i, 0, 0))],
        out_shape=[jax.ShapeDtypeStruct((2, BG, nc, HEAD_DIM), BF16),
                   jax.ShapeDtypeStruct((2, BG, HEAD_DIM, nc), BF16)],
        name="compress",
    )(xin, w1, w2, w2t, pos)


def _head_lanes(qt):
    return jnp.concatenate([qt[r * HEAD_DIM:(r + 1) * HEAD_DIM] for r in range(HEADS_PER_GROUP)], axis=1)


def _gated_out(ot, scale, gates, branch, tq):
    rows = []
    for r in range(HEADS_PER_GROUP):
        c = branch * HEADS_PER_GROUP + r
        sl = slice(r * tq, (r + 1) * tq)
        rows.append(ot[:, sl] * (scale[:, sl] * gates[c:c + 1]))
    return jnp.concatenate(rows, axis=0).astype(BF16)


def _attn_specs(tq):
    G = N_KV_GROUPS
    qspec = pl.BlockSpec((1, GROUP_W, tq), lambda b, g, i: (b, g, i))
    gspec = pl.BlockSpec((1, GATE_ROWS, tq), lambda b, g, i: (b, g, i))
    bg4 = lambda a: pl.BlockSpec((1, 1) + a.shape[2:], lambda b, g, i: (b, g, 0, 0))
    return G, qspec, gspec, bg4


def _cmp_attn_kernel(q_ref, kc_ref, vct_ref, g_ref, ovt_ref, o_ref, sel_ref):
    qi = pl.program_id(2)
    tq = q_ref.shape[2]
    q0 = qi * tq
    qt = _head_lanes(q_ref[0])
    s = _dot(kc_ref[0, 0], qt)
    c = lax.broadcasted_iota(I32, s.shape, 0)
    t = q0 + (lax.broadcasted_iota(I32, (1, s.shape[1]), 1) & (tq - 1))
    valid = c <= (t - (CMP_BLOCK - 1)) // CMP_STRIDE
    sm = jnp.where(valid, s, NEG_INF)
    m = jnp.max(sm, axis=0, keepdims=True)
    p = jnp.where(valid, jnp.exp2(sm - m), 0.0)
    l = jnp.sum(p, axis=0, keepdims=True)
    inv = 1.0 / jnp.where(l > 0.0, l, 1.0)
    ot = _dot(vct_ref[0, 0], p.astype(BF16))
    o_ref[0] = _gated_out(ot, inv, g_ref[0], 0, tq)

    pn = p * inv
    psum = pn[:, 0:tq]
    for r in range(1, HEADS_PER_GROUP):
        psum = psum + pn[:, r * tq:(r + 1) * tq]
    imp = _dot(ovt_ref[...], psum, precision=lax.Precision.HIGHEST)
    j = lax.broadcasted_iota(I32, imp.shape, 0)
    cur = (q0 + lax.broadcasted_iota(I32, imp.shape, 1)) // SEL_BLOCK
    valid_b = j <= cur
    forced = (j == 0) | (j == cur) | (j == cur - 1)
    score = jnp.where(valid_b, jnp.where(forced, FORCE_SCORE, imp), NEG_INF)
    sub = 8
    groups = [score[a:a + sub] for a in range(0, SEL_LANES, sub)]
    jrow = lax.broadcasted_iota(I32, groups[0].shape, 0)
    ranks = [jnp.zeros(g_.shape, I32) for g_ in groups]
    for i in range(SEL_LANES):
        si = score[i:i + 1, :]
        for a, g_ in enumerate(groups):
            if a > i // sub:
                inc = jnp.where(si >= g_, 1, 0)
            elif a < i // sub:
                inc = jnp.where(si > g_, 1, 0)
            else:
                inc = jnp.where(jrow > i % sub, jnp.where(si >= g_, 1, 0), jnp.where(si > g_, 1, 0))
            ranks[a] = ranks[a] + inc
    rank = jnp.concatenate(ranks, axis=0)
    selected = (rank < N_SELECT) & valid_b
    sel_ref[0, 0] = jnp.where(selected, 0.0, SEL_MASK_BIAS).astype(BF16)


def _cmp_attn(qt, kcmp, vcmp_t, gates, ovt):
    B, _, T = qt.shape
    tq = min(CMP_TQ, T)
    G, qspec, gspec, _ = _attn_specs(tq)
    cspec = lambda a: pl.BlockSpec((1, 1) + a.shape[2:], lambda b, g, i: (0, b * G + g, 0, 0))
    vspec = lambda a: pl.BlockSpec((1, 1) + a.shape[2:], lambda b, g, i: (1, b * G + g, 0, 0))
    return pl.pallas_call(
        _cmp_attn_kernel,
        grid=(B, G, T // tq),
        in_specs=[qspec, cspec(kcmp), vspec(vcmp_t), gspec, pl.BlockSpec(ovt.shape, lambda b, g, i: (0, 0))],
        out_specs=[qspec, pl.BlockSpec((1, 1, SEL_LANES, tq), lambda b, g, i: (b, g, 0, i))],
        out_shape=[jax.ShapeDtypeStruct((B, NSA_WIDTH, T), BF16),
                   jax.ShapeDtypeStruct((B, G, SEL_LANES, T), BF16)],
        name="cmp_attn",
    )(qt, kcmp, vcmp_t, gates, ovt)


def _sel_attn_kernel(q_ref, sel_ref, k_ref, vt_ref, g_ref, o_ref, *, kc):
    qi = pl.program_id(2)
    tq = q_ref.shape[2]
    q0 = qi * tq
    qt = q_ref[0]
    sb = sel_ref[0, 0]
    qa = jnp.concatenate(
        [jnp.concatenate([qt[r * HEAD_DIM:(r + 1) * HEAD_DIM], sb], axis=0) for r in range(HEADS_PER_GROUP)],
        axis=1)
    gw = qa.shape[1] // SEL_SPLIT
    qas = [qa[:, i * gw:(i + 1) * gw] for i in range(SEL_SPLIT)]

    def step(kstart, kn, carry, diagonal):
        kblk = k_ref[0, 0, pl.ds(kstart, kn), :]
        vblk = vt_ref[0, :, pl.ds(kstart, kn)]
        ss = [_dot(kblk, qg) for qg in qas]
        out = []
        for gi, (s, (m, l, acc)) in enumerate(zip(ss, carry)):
            if diagonal:
                row = lax.broadcasted_iota(I32, s.shape, 0)
                off = (gi * gw + lax.broadcasted_iota(I32, s.shape, 1)) & (tq - 1)
                s = jnp.where(row <= off, s, NEG_INF)
            mn = jnp.maximum(m, jnp.max(s, axis=0, keepdims=True))
            alpha = jnp.exp2(m - mn)
            p = jnp.exp2(s - mn)
            l = alpha * l + jnp.sum(p, axis=0, keepdims=True)
            acc = alpha * acc + _dot(vblk, p.astype(BF16))
            out.append((mn, l, acc))
        return tuple(out)

    init = tuple((jnp.full((1, gw), NEG_INF, F32), jnp.zeros((1, gw), F32), jnp.zeros((HEAD_DIM, gw), F32))
                 for _ in range(SEL_SPLIT))
    n_full = q0 // kc
    carry = lax.fori_loop(0, n_full, lambda i, cr: step(pl.multiple_of(i * kc, kc), kc, cr, False), init)
    carry = lax.fori_loop(n_full * (kc // tq), qi, lambda i, cr: step(pl.multiple_of(i * tq, tq), tq, cr, False), carry)
    carry = step(pl.multiple_of(q0, tq), tq, carry, True)
    l = jnp.concatenate([c[1] for c in carry], axis=1)
    acc = jnp.concatenate([c[2] for c in carry], axis=1)
    o_ref[0] = _gated_out(acc, 1.0 / l, g_ref[0], 1, tq)


def _sel_attn(qt, sel, kaug, kvt, v_row_block, gates):
    B, _, T = qt.shape
    tq = min(SEL_TQ, T)
    kc = min(SEL_KC, T)
    G, qspec, gspec, bg4 = _attn_specs(tq)
    return pl.pallas_call(
        functools.partial(_sel_attn_kernel, kc=kc),
        grid=(B, G, T // tq),
        in_specs=[qspec, pl.BlockSpec((1, 1, SEL_LANES, tq), lambda b, g, i: (b, g, 0, i)), bg4(kaug),
                  pl.BlockSpec((1, HEAD_DIM, T), lambda b, g, i: (b, v_row_block + g, 0)), gspec],
        out_specs=qspec,
        out_shape=jax.ShapeDtypeStruct((B, NSA_WIDTH, T), BF16),
        name="sel_attn",
    )(qt, sel, kaug, kvt, gates)


def _win_attn_kernel(q_ref, k_ref, vt_ref, g_ref, o_ref, *, span, sub):
    qi = pl.program_id(2)
    tq = q_ref.shape[2]
    T = k_ref.shape[2]

    def tile(s0, interior):
        cs = slice(s0, s0 + sub)
        q0 = qi * tq + s0
        start = pl.multiple_of(jnp.clip(q0 + sub - span, 0, T - span), sub)
        qt = _head_lanes(q_ref[0, :, cs])
        s = _dot(k_ref[0, 0, pl.ds(start, span), :], qt)
        if interior:
            row = lax.broadcasted_iota(I32, (sub, s.shape[1]), 0)
            off = lax.broadcasted_iota(I32, (sub, s.shape[1]), 1) & (sub - 1)
            s = jnp.concatenate([jnp.where(row > off, s[:sub], NEG_INF), s[sub:span - sub],
                                 jnp.where(row <= off, s[span - sub:], NEG_INF)], axis=0)
        else:
            kp = start + lax.broadcasted_iota(I32, s.shape, 0)
            t = q0 + (lax.broadcasted_iota(I32, s.shape, 1) & (sub - 1))
            diff = t - kp
            s = jnp.where((diff >= 0) & (diff < WINDOW), s, NEG_INF)
        m = jnp.max(s, axis=0, keepdims=True)
        p = jnp.exp2(s - m)
        l = jnp.sum(p, axis=0, keepdims=True)
        ot = _dot(vt_ref[0, :, pl.ds(start, span)], p.astype(BF16))
        o_ref[0, :, cs] = _gated_out(ot, 1.0 / l, g_ref[0, :, cs], 2, sub)

    first_interior = -(-WINDOW // tq)
    if span == WINDOW + sub:
        @pl.when(qi >= first_interior)
        def _():
            for s0 in range(0, tq, sub):
                tile(s0, True)

    @pl.when((qi < first_interior) | (span != WINDOW + sub))
    def _():
        for s0 in range(0, tq, sub):
            tile(s0, False)


def _win_attn(qt, k, kvt, v_row_block, gates):
    B, _, T = qt.shape
    tq = min(WIN_TQ, T)
    sub = min(WIN_SUB, T)
    span = min(WINDOW + sub, T)
    G, qspec, gspec, bg4 = _attn_specs(tq)
    return pl.pallas_call(
        functools.partial(_win_attn_kernel, span=span, sub=sub),
        grid=(B, G, T // tq),
        in_specs=[qspec, bg4(k), pl.BlockSpec((1, HEAD_DIM, T), lambda b, g, i: (b, v_row_block + g, 0)), gspec],
        out_specs=qspec,
        out_shape=jax.ShapeDtypeStruct((B, NSA_WIDTH, T), BF16),
        name="win_attn",
    )(qt, k, kvt, gates)


def _store_row_tiles(ref, x, row0=0):
    rows = x.shape[0]
    for s in range(ROW_SUB):
        ref[pl.ds(row0 * ROW_SUB + s, rows, stride=ROW_SUB), :] = x[:, s * LANES:(s + 1) * LANES]


def _load_row_tiles(ref):
    rows = ref.shape[0] // ROW_SUB
    return jnp.concatenate([ref[pl.ds(s, rows, stride=ROW_SUB), :] for s in range(ROW_SUB)], axis=1)


def _tile_copy(src, si, dst, di, sem):
    return pltpu.make_async_copy(src.at[pl.ds(pl.multiple_of(si * ROW_SUB, ROW_SUB), ROW_SUB), :],
                                 dst.at[pl.ds(pl.multiple_of(di * ROW_SUB, ROW_SUB), ROW_SUB), :], sem)


def _layer_norm(h, g, b):
    mu = jnp.mean(h, axis=-1, keepdims=True)
    c = h - mu
    var = jnp.mean(c * c, axis=-1, keepdims=True)
    return c * lax.rsqrt(var + LN_EPS) * g + b


def _merge_kernel(x_ref, ya_ref, oc_ref, os_ref, ow_ref, wmg_ref, wuc_ref, wun_ref, wo_ref, g1_ref, b1_ref,
                  wr_ref, br_ref, tri_ref, x1_ref, ri_ref, rg_ref, cnt_ref, xs_hbm,
                  carry_ref, stage, dest_v, dest_s, copy_sem, row_sem, *, alpha, cap):
    i = pl.program_id(0)
    last = pl.num_programs(0) - 1
    sub = tri_ref.shape[0]
    tm = x_ref.shape[0]
    slot = i % 2

    def wait_rows(s):
        for k in range(TOP_K):
            pltpu.make_async_copy(stage.at[s], xs_hbm.at[pl.ds(0, tm * ROW_SUB), :], row_sem.at[s]).wait()

    @pl.when(i == 0)
    def _():
        carry_ref[...] = jnp.zeros_like(carry_ref)

    wr = wr_ref[...]
    wr_hi = wr.astype(BF16)
    wr_hl = jnp.concatenate([wr_hi, (wr - wr_hi.astype(F32)).astype(BF16)], axis=0)
    total = carry_ref[...]
    for s0 in range(0, x_ref.shape[0], sub):
        rs = slice(s0, s0 + sub)
        x = x_ref[rs, :]
        mg = _dot(x.astype(BF16), wmg_ref[...])
        y_a = _dot(ya_ref[rs, :], wuc_ref[...])
        o_nsa_t = (oc_ref[0, :, rs].astype(F32) + os_ref[0, :, rs].astype(F32)
                   + ow_ref[0, :, rs].astype(F32))
        y_b = _dot(o_nsa_t.T.astype(BF16), wun_ref[...])
        merged = jax.nn.sigmoid(mg[:, :D_MODEL]) * y_a + jax.nn.sigmoid(mg[:, D_MODEL:]) * y_b
        h = alpha * x + _dot(merged.astype(BF16), wo_ref[...])
        x1 = _layer_norm(h, g1_ref[...], b1_ref[...])
        x1_ref[rs, :] = x1
        _store_row_tiles(stage.at[slot], x1, s0)

        x_hi = x1.astype(BF16)
        x_lo = (x1 - x_hi.astype(F32)).astype(BF16)
        both = _dot_nt(wr_hl, x_hi)
        logits = both[:N_EXPERTS] + both[N_EXPERTS:] + _dot_nt(wr_hi, x_lo) + br_ref[...]
        expert = lax.broadcasted_iota(I32, logits.shape, 0).astype(F32)
        rem = logits
        vals, idxs, hots = [], [], []
        for _ in range(TOP_K):
            m = jnp.max(rem, axis=0, keepdims=True)
            idx = jnp.min(jnp.where(rem == m, expert, float(N_EXPERTS)), axis=0, keepdims=True)
            hot = expert == idx
            vals.append(m)
            idxs.append(idx)
            hots.append(hot)
            rem = jnp.where(hot, -3.0e38, rem)
        es = [jnp.exp(v - vals[0]) for v in vals]
        den = es[0]
        for e in es[1:]:
            den = den + e
        chosen = hots[0]
        for hot in hots[1:]:
            chosen = chosen | hot
        chosen_f = jnp.where(chosen, 1.0, 0.0)

        before = _dot(chosen_f.astype(BF16), tri_ref[...]) + total
        total = total + jnp.sum(chosen_f, axis=1, keepdims=True)
        ranks = [jnp.sum(jnp.where(hot, before, 0.0), axis=0, keepdims=True) for hot in hots]
        ri_ref[:, rs] = jnp.concatenate(idxs + ranks, axis=0).astype(I32)
        rg_ref[:, rs] = jnp.concatenate([e / den for e in es] + [jnp.zeros_like(den)] * TOP_K, axis=0)

        dest_v[...] = (jnp.concatenate(idxs, axis=0).astype(I32) * cap + jnp.concatenate(ranks, axis=0).astype(I32))
        ids = pltpu.make_async_copy(dest_v, dest_s, copy_sem)
        ids.start()
        ids.wait()
        for t in range(sub):
            for k in range(TOP_K):
                _tile_copy(stage.at[slot], s0 + t, xs_hbm, dest_s[k, t], row_sem.at[slot]).start(priority=k % 2)

    carry_ref[...] = total
    cnt_ref[...] = jnp.broadcast_to(total, cnt_ref.shape).astype(I32)

    @pl.when(i > 0)
    def _():
        wait_rows(1 - slot)

    @pl.when(i == last)
    def _():
        wait_rows(slot)


def _merge(x, ya, oc, os_, ow, wmg, wuc, wun, wo, g1, b1, wr, br, tri, alpha):
    N, D = x.shape
    B, _, T = oc.shape
    tm = min(MERGE_TM, T)
    nt = T // tm
    tok = lambda w: pl.BlockSpec((tm, w), lambda i: (i, 0))
    feat = pl.BlockSpec((1, NSA_WIDTH, tm), lambda i: (i // nt, 0, i % nt))
    full = lambda a: pl.BlockSpec(a.shape, lambda i: (0,) * a.ndim)
    sub = tri.shape[0]
    return pl.pallas_call(
        functools.partial(_merge_kernel, alpha=alpha, cap=N),
        grid=(N // tm,),
        in_specs=[tok(D), tok(CONV_CH), feat, feat, feat,
                  full(wmg), full(wuc), full(wun), full(wo), full(g1), full(b1), full(wr), full(br), full(tri)],
        out_specs=[tok(D),
                   pl.BlockSpec((2 * TOP_K, tm), lambda i: (0, i)), pl.BlockSpec((2 * TOP_K, tm), lambda i: (0, i)),
                   pl.BlockSpec((N_EXPERTS, LANES), lambda i: (0, 0)), pl.BlockSpec(memory_space=pl.ANY)],
        out_shape=[jax.ShapeDtypeStruct((N, D), F32),
                   jax.ShapeDtypeStruct((2 * TOP_K, N), I32),
                   jax.ShapeDtypeStruct((2 * TOP_K, N), F32),
                   jax.ShapeDtypeStruct((N_EXPERTS, LANES), I32),
                   jax.ShapeDtypeStruct((N_EXPERTS * N * ROW_SUB, LANES), F32)],
        scratch_shapes=[pltpu.VMEM((N_EXPERTS, 1), F32), pltpu.VMEM((2, tm * ROW_SUB, LANES), F32),
                        pltpu.VMEM((TOP_K, sub), I32), pltpu.SMEM((TOP_K, sub), I32),
                        pltpu.SemaphoreType.DMA(()), pltpu.SemaphoreType.DMA((2,))],
        compiler_params=pltpu.CompilerParams(dimension_semantics=("arbitrary",)),
        name="merge",
    )(x, ya, oc, os_, ow, wmg, wuc, wun, wo, g1, b1, wr, br, tri)


def _pad_rows_kernel(lo_ref, hi_ref, xs_in, xs_hbm, zero_ref, sem):
    del xs_in
    zero_ref[...] = jnp.zeros_like(zero_ref)
    for e in range(N_EXPERTS):
        lo = lo_ref[e]
        hi = hi_ref[e]

        def start(r, c):
            _tile_copy(zero_ref, 0, xs_hbm, r, sem).start()
            return c

        def wait(r, c):
            _tile_copy(zero_ref, 0, xs_hbm, r, sem).wait()
            return c

        lax.fori_loop(lo, hi, start, 0)
        lax.fori_loop(lo, hi, wait, 0)


def _pad_rows(lo, hi, xs):
    smem = pl.BlockSpec(memory_space=pltpu.SMEM)
    anyspace = pl.BlockSpec(memory_space=pl.ANY)
    return pl.pallas_call(
        _pad_rows_kernel,
        grid=(1,),
        in_specs=[smem, smem, anyspace],
        out_specs=anyspace,
        out_shape=jax.ShapeDtypeStruct(xs.shape, xs.dtype),
        input_output_aliases={2: 0},
        scratch_shapes=[pltpu.VMEM((ROW_SUB, LANES), F32), pltpu.SemaphoreType.DMA(())],
        compiler_params=pltpu.CompilerParams(dimension_semantics=("arbitrary",)),
        name="pad_rows",
    )(lo, hi, xs)


def _experts_kernel(be_ref, nu_ref, rb_ref, xs_ref, wgu_ref, perm_ref, bg_ref, bl_ref, wd_ref, bd_ref, ys_ref, wg_s, wl_s, wd_s):
    i = pl.program_id(0)

    @pl.when((i == 0) | (be_ref[i] != be_ref[jnp.maximum(i, 1) - 1]))
    def _():
        w = perm_ref.shape[0]
        for c in range(2 * D_FF // w):
            t = _dot(wgu_ref[0, :, c * w:(c + 1) * w].astype(BF16), perm_ref[...])
            wg_s[:, c * (w // 2):(c + 1) * (w // 2)] = t[:, :w // 2].astype(BF16)
            wl_s[:, c * (w // 2):(c + 1) * (w // 2)] = t[:, w // 2:].astype(BF16)
        wd_s[...] = wd_ref[0].astype(BF16)

    @pl.when(i < nu_ref[0])
    def _():
        xb = _load_row_tiles(xs_ref).astype(BF16)
        x_glu = jnp.minimum(_dot(xb, wg_s[...]) + bg_ref[0], SWIGLU_LIMIT)
        x_lin = jnp.clip(_dot(xb, wl_s[...]) + bl_ref[0], -SWIGLU_LIMIT, SWIGLU_LIMIT)
        act = x_glu * jax.nn.sigmoid(SWIGLU_ALPHA * x_glu) * (x_lin + 1.0)
        _store_row_tiles(ys_ref, _dot(act.astype(BF16), wd_s[...]) + bd_ref[0])


def _experts(block_expert, n_used, row_block, xs, wgu, perm, bg, bl, wd, bd):
    D = D_MODEL
    n_blocks = block_expert.shape[0]
    rows = lambda i, be, nu, rb: (rb[i], 0)
    wsel = lambda i, be, nu, rb: (be[i], 0, 0)
    return pl.pallas_call(
        _experts_kernel,
        grid_spec=pltpu.PrefetchScalarGridSpec(
            num_scalar_prefetch=3,
            grid=(n_blocks,),
            in_specs=[pl.BlockSpec((MOE_BLK * ROW_SUB, LANES), rows),
                      pl.BlockSpec((1, D, 2 * D_FF), wsel),
                      pl.BlockSpec(perm.shape, lambda i, be, nu, rb: (0, 0)),
                      pl.BlockSpec((1, 1, D_FF), wsel), pl.BlockSpec((1, 1, D_FF), wsel),
                      pl.BlockSpec((1, D_FF, D), wsel), pl.BlockSpec((1, 1, D), wsel)],
            out_specs=pl.BlockSpec((MOE_BLK * ROW_SUB, LANES), rows),
            scratch_shapes=[pltpu.VMEM((D, D_FF), BF16), pltpu.VMEM((D, D_FF), BF16), pltpu.VMEM((D_FF, D), BF16)]),
        out_shape=jax.ShapeDtypeStruct(xs.shape, F32),
        compiler_params=pltpu.CompilerParams(dimension_semantics=("arbitrary",)),
        name="experts",
    )(block_expert, n_used, row_block, xs, wgu, perm, bg, bl, wd, bd)


def _combine_kernel(dest_ref, dest_next_ref, ys_hbm, x1_ref, rg_ref, g2_ref, b2_ref, o_ref, buf, sem, *, alpha):
    i = pl.program_id(0)
    last = pl.num_programs(0) - 1
    tm = o_ref.shape[0]
    slot = i % 2
    other = 1 - slot

    def gather(ids_ref, t, to_slot):
        for k in range(TOP_K):
            _tile_copy(ys_hbm, ids_ref[t * TOP_K + k], buf.at[to_slot, k], t, sem.at[to_slot]).start(priority=k % 2)

    def wait_slot(s):
        for k in range(TOP_K):
            pltpu.make_async_copy(ys_hbm.at[pl.ds(0, tm * ROW_SUB), :], buf.at[s, k], sem.at[s]).wait()

    @pl.when(i == 0)
    def _():
        def first(t, c):
            gather(dest_ref, t, slot)
            return c
        lax.fori_loop(0, tm, first, 0)

    wait_slot(slot)

    gs = min(COMB_GROUP, tm)

    def group(g, c):
        r0 = pl.multiple_of(g * gs, gs)
        for j in range(gs):
            gather(dest_next_ref, r0 + j, other)
        rows = pl.ds(r0, gs)
        gate = rg_ref[rows, :]
        y = alpha * x1_ref[rows, :]
        for k in range(TOP_K):
            tiles = buf.at[slot, k, pl.ds(pl.multiple_of(r0 * ROW_SUB, gs * ROW_SUB), gs * ROW_SUB), :]
            y = y + gate[:, k:k + 1] * _load_row_tiles(tiles)
        o_ref[rows, :] = _layer_norm(y, g2_ref[...], b2_ref[...])
        return c

    lax.fori_loop(0, tm // gs, group, 0)

    @pl.when(i == last)
    def _():
        wait_slot(other)


def _combine(dest, ys, x1, rg, g2, b2, alpha):
    N, D = x1.shape
    tm = min(COMB_TM, N)
    n = N // tm
    tok = lambda w: pl.BlockSpec((tm, w), lambda i: (i, 0))
    full = lambda a: pl.BlockSpec(a.shape, lambda i: (0,) * a.ndim)
    ids = lambda f: pl.BlockSpec((tm * TOP_K,), f, memory_space=pltpu.SMEM)
    return pl.pallas_call(
        functools.partial(_combine_kernel, alpha=alpha),
        grid=(n,),
        in_specs=[ids(lambda i: (i,)), ids(lambda i: (jnp.minimum(i + 1, n - 1),)),
                  pl.BlockSpec(memory_space=pl.ANY), tok(D), tok(rg.shape[1]), full(g2), full(b2)],
        out_specs=tok(D),
        out_shape=jax.ShapeDtypeStruct((N, D), F32),
        scratch_shapes=[pltpu.VMEM((2, TOP_K, tm * ROW_SUB, LANES), F32), pltpu.SemaphoreType.DMA((2,))],
        compiler_params=pltpu.CompilerParams(dimension_semantics=("arbitrary",)),
        name="combine",
    )(dest, dest, ys, x1, rg, g2, b2)


def _rope_freq():
    half = ROPE_DIM // 2
    inv = (np.float32(ROPE_THETA) ** (-np.arange(half, dtype=np.float32) * np.float32(2.0 / ROPE_DIM))).astype(np.float32)
    return inv[:, None]


def _overlap_t(T):
    nc = T // CMP_STRIDE
    c0 = np.arange(nc) * CMP_STRIDE
    j0 = np.arange(SEL_LANES) * SEL_BLOCK
    ov = (c0[None, :] < j0[:, None] + SEL_BLOCK) & (c0[None, :] + CMP_BLOCK > j0[:, None])
    ov &= (np.arange(nc) < nc - 1)[None, :] & (np.arange(SEL_LANES) < T // SEL_BLOCK)[:, None]
    return ov.astype(np.float32)


def _deinterleave_perm():
    w = 2 * LANES
    p = np.zeros((w, w), np.float32)
    p[np.arange(0, w, 2), np.arange(w // 2)] = 1.0
    p[np.arange(1, w, 2), w // 2 + np.arange(w // 2)] = 1.0
    return p


def _token_major(t, B, T):
    return t.reshape(B, N_KV_GROUPS, HEAD_DIM, T).transpose(0, 1, 3, 2)


def _layer(x, positions, w_in, conv_w, cmp_pos_k, cmp_w1_k, cmp_w2_k, cmp_pos_v, cmp_w1_v, cmp_w2_v,
           w_up_conv, w_up_nsa, w_o, ln1_g, ln1_b, w_router, b_router, w_gate_up, b_gate_up,
           w_down, b_down, ln2_g, ln2_b, alpha):
    B, T, D = x.shape
    G, R = N_KV_GROUPS, HEADS_PER_GROUP
    N = B * T
    assert D == D_MODEL and T % SEL_BLOCK == 0 and T // SEL_BLOCK <= SEL_LANES

    c0 = 3 * CONV_CH
    c1 = c0 + NSA_WIDTH
    c2 = c1 + 6 * KV_WIDTH
    c3 = c2 + 3 * N_HEADS
    wc = w_in[:, :c0].astype(BF16)
    wqt = w_in[:, c0:c1].T.astype(BF16)
    wkvt = w_in[:, c1:c2].T.astype(BF16)
    gcols = np.zeros((G * GATE_ROWS,), np.int64)
    gmask = np.zeros((G * GATE_ROWS,), np.float32)
    for g in range(G):
        for br in range(3):
            for r in range(R):
                gcols[g * GATE_ROWS + br * R + r] = br * N_HEADS + g * R + r
                gmask[g * GATE_ROWS + br * R + r] = 1.0
    wgt = (w_in[:, c2:c3][:, gcols] * gmask).T.astype(BF16)
    wmg = w_in[:, c3:].astype(BF16)
    cw = conv_w.reshape(CONV_K, CONV_CH)
    pos_row = positions.astype(F32)[:, None, :]

    ya_pre, qt, kvt, gates = _proj(x, pos_row, wc, wqt, wkvt, wgt, cw, jnp.asarray(_rope_freq()))

    nc = T // CMP_STRIDE
    to_chunks = lambda t: _token_major(t, B, T).reshape(B * G, nc, CMP_STRIDE * HEAD_DIM)
    xin = jnp.stack([to_chunks(kvt[:, 0:KV_WIDTH]), to_chunks(kvt[:, KV_WIDTH:2 * KV_WIDTH])])
    w1 = jnp.stack([cmp_w1_k, cmp_w1_v])
    w2 = jnp.stack([cmp_w2_k, cmp_w2_v])
    pos_flat = jnp.stack([cmp_pos_k.reshape(1, -1), cmp_pos_v.reshape(1, -1)])
    pos_flat = jnp.pad(pos_flat, ((0, 0), (0, 7), (0, 0)))
    kcmp, kcmp_t = _compress(xin, w1, w2, w2.transpose(0, 2, 1), pos_flat)

    o_c, sel = _cmp_attn(qt, kcmp, kcmp_t, gates, jnp.asarray(_overlap_t(T)))

    ks = _token_major(kvt[:, 2 * KV_WIDTH:3 * KV_WIDTH], B, T)
    kw = _token_major(kvt[:, 4 * KV_WIDTH:5 * KV_WIDTH], B, T)
    onehot = (np.arange(T)[:, None] // SEL_BLOCK == np.arange(SEL_LANES)[None, :]).astype(np.float32)
    kaug = jnp.concatenate([ks, jnp.broadcast_to(jnp.asarray(onehot, BF16), (B, G, T, SEL_LANES))], axis=-1)
    o_s = _sel_attn(qt, sel, kaug, kvt, 3 * G, gates)
    o_w = _win_attn(qt, kw, kvt, 5 * G, gates)

    sub = min(MERGE_SUB, T)
    tri = jnp.asarray(np.triu(np.ones((sub, sub), np.float32), 1), BF16)
    assert N % MOE_BLK == 0
    x1, ri, rg, cnt, xs = _merge(x.reshape(N, D), ya_pre.reshape(N, CONV_CH), o_c, o_s, o_w, wmg,
                             w_up_conv.astype(BF16), w_up_nsa.astype(BF16), w_o.astype(BF16),
                             ln1_g[None, :], ln1_b[None, :], w_router.T, b_router[:, None], tri, alpha)

    counts = cnt[:, 0]
    blocks = (counts + MOE_BLK - 1) // MOE_BLK
    bend = jnp.cumsum(blocks).astype(I32)
    bstart = bend - blocks
    n_blocks = (N * TOP_K) // MOE_BLK + N_EXPERTS
    n_used = bend[-1:]
    b = jnp.minimum(jnp.arange(n_blocks, dtype=I32), n_used - 1)
    block_expert = jnp.minimum(jnp.sum((b[:, None] >= bend[None, :]).astype(I32), axis=1), N_EXPERTS - 1)
    first = jnp.sum(jnp.where(block_expert[:, None] == jnp.arange(N_EXPERTS)[None, :], bstart[None, :], 0), axis=1)
    row_block = block_expert * (N // MOE_BLK) + (b - first)
    base = jnp.arange(N_EXPERTS, dtype=I32) * N
    dest = (ri[:TOP_K] * N + ri[TOP_K:]).T.reshape(N * TOP_K)

    xs = _pad_rows(base + counts, base + blocks * MOE_BLK, xs)
    ys = _experts(block_expert, n_used, row_block, xs, w_gate_up, jnp.asarray(_deinterleave_perm(), BF16),
                  b_gate_up[:, None, 0::2], b_gate_up[:, None, 1::2], w_down, b_down[:, None, :])
    out = _combine(dest, ys, x1, rg.T, ln2_g[None, :], ln2_b[None, :], alpha)
    return out.reshape(B, T, D)


def kernel(x, positions, w_in, conv_w, cmp_pos_k, cmp_w1_k, cmp_w2_k, cmp_pos_v, cmp_w1_v, cmp_w2_v, w_up_conv, w_up_nsa, w_o, ln1_g, ln1_b, w_router, b_router, w_gate_up, b_gate_up, w_down, b_down, ln2_g, ln2_b):
    depth = w_in.shape[0]
    alpha = float((2 * depth) ** 0.25)
    h = x
    for l in range(depth):
        h = _layer(h, positions, w_in[l], conv_w[l], cmp_pos_k[l], cmp_w1_k[l], cmp_w2_k[l],
                   cmp_pos_v[l], cmp_w1_v[l], cmp_w2_v[l], w_up_conv[l], w_up_nsa[l], w_o[l],
                   ln1_g[l], ln1_b[l], w_router[l], b_router[l], w_gate_up[l], b_gate_up[l],
                   w_down[l], b_down[l], ln2_g[l], ln2_b[l], alpha)
    return h
```

```python
import functools

import numpy as np
import jax
import jax.numpy as jnp
from jax import lax
from jax.experimental import pallas as pl
from jax.experimental.pallas import tpu as pltpu

F32 = jnp.float32
BF16 = jnp.bfloat16
I32 = jnp.int32

D_MODEL = 1024
CONV_CH = 512
CONV_K = 3
N_HEADS = 8
N_KV_GROUPS = 2
HEADS_PER_GROUP = N_HEADS // N_KV_GROUPS
HEAD_DIM = 64
NSA_WIDTH = N_HEADS * HEAD_DIM
KV_WIDTH = N_KV_GROUPS * HEAD_DIM
ROPE_DIM = HEAD_DIM // 4
ROPE_THETA = 500000.0
CMP_BLOCK = 32
CMP_STRIDE = 16
CMP_HIDDEN = 256
SEL_BLOCK = 64
N_SELECT = 16
WINDOW = 512
N_EXPERTS = 32
TOP_K = 4
D_FF = 1024
SWIGLU_LIMIT = 7.0
SWIGLU_ALPHA = 1.702
LN_EPS = 1e-5
NEG_INF = -1e30
FORCE_SCORE = 1e4
LOG2E = 1.4426950408889634

LANES = 128
ROW_SUB = D_MODEL // LANES
SEL_LANES = 64
SEL_MASK_BIAS = -32768.0
GROUP_W = HEADS_PER_GROUP * HEAD_DIM
GATE_ROWS = 16

PROJ_TM = 1024
CMP_TQ = 256
SEL_TQ = 512
SEL_KC = 512
SEL_SPLIT = 2
WIN_TQ = 512
WIN_SUB = 128
MERGE_TM = 512
MERGE_SUB = 256
MOE_BLK = 512
COMB_TM = 256
COMB_GROUP = 128


def _dot(a, b, precision=None):
    return jnp.dot(a, b, precision=precision, preferred_element_type=F32)


def _dot_nt(a, b, precision=None):
    return lax.dot_general(a, b, (((1,), (1,)), ((), ())), precision=precision, preferred_element_type=F32)


def _proj_kernel(x_ref, pos_ref, wc_ref, wqt_ref, wkvt_ref, wgt_ref, cw_ref, freq_ref,
                 ya_ref, qt_ref, kvt_ref, gt_ref, carry_ref):
    ti = pl.program_id(1)
    tm = x_ref.shape[1]
    xb = x_ref[0].astype(BF16)

    pc = _dot(xb, wc_ref[...])
    xv = pc[:, :CONV_CH]
    bg = pc[:, CONV_CH:2 * CONV_CH]
    cg = pc[:, 2 * CONV_CH:]
    u = cg * xv

    @pl.when(ti == 0)
    def _():
        carry_ref[...] = jnp.zeros_like(carry_ref)

    prev = carry_ref[...]
    row = lax.broadcasted_iota(I32, u.shape, 0)
    u1 = jnp.where(row == 0, prev[7:8], pltpu.roll(u, 1, 0))
    u2 = jnp.where(row == 0, prev[6:7], jnp.where(row == 1, prev[7:8], pltpu.roll(u, 2, 0)))
    cw = cw_ref[...]
    conv = cw[2:3] * u + cw[1:2] * u1 + cw[0:1] * u2
    carry_ref[...] = u[tm - 8:]
    ya_ref[0] = (bg * conv).astype(BF16)

    ang = freq_ref[...] * pos_ref[0]
    cos = jnp.cos(ang)
    sin = jnp.sin(ang)
    half = ROPE_DIM // 2

    def rope_head(t):
        t1 = t[:half]
        t2 = t[half:ROPE_DIM]
        return [t1 * cos - t2 * sin, t2 * cos + t1 * sin, t[ROPE_DIM:]]

    def heads(t, rotate):
        out = []
        for h in range(t.shape[0] // HEAD_DIM):
            th = t[h * HEAD_DIM:(h + 1) * HEAD_DIM]
            out.extend(rope_head(th) if rotate(h) else [th])
        return jnp.concatenate(out, axis=0)

    qt = heads(_dot_nt(wqt_ref[...], xb), lambda h: True)
    qt_ref[0] = (qt * (HEAD_DIM ** -0.5 * LOG2E)).astype(BF16)
    kvt = heads(_dot_nt(wkvt_ref[...], xb), lambda h: (h // N_KV_GROUPS) % 2 == 0)
    kvt_ref[0] = kvt.astype(BF16)
    gt_ref[0] = jax.nn.sigmoid(_dot_nt(wgt_ref[...], xb))


def _proj(x, pos_row, wc, wqt, wkvt, wgt, cw, freq):
    B, T, D = x.shape
    tm = min(PROJ_TM, T)
    grid = (B, T // tm)
    full = lambda a: pl.BlockSpec(a.shape, lambda b, t: (0,) * a.ndim)
    tok = lambda w: pl.BlockSpec((1, tm, w), lambda b, t: (b, t, 0))
    feat = lambda r: pl.BlockSpec((1, r, tm), lambda b, t: (b, 0, t))
    n_gate = wgt.shape[0]
    return pl.pallas_call(
        _proj_kernel,
        grid=grid,
        in_specs=[tok(D), feat(1), full(wc), full(wqt), full(wkvt), full(wgt), full(cw), full(freq)],
        out_specs=[tok(CONV_CH), feat(NSA_WIDTH), feat(6 * KV_WIDTH), feat(n_gate)],
        out_shape=[jax.ShapeDtypeStruct((B, T, CONV_CH), BF16),
                   jax.ShapeDtypeStruct((B, NSA_WIDTH, T), BF16),
                   jax.ShapeDtypeStruct((B, 6 * KV_WIDTH, T), BF16),
                   jax.ShapeDtypeStruct((B, n_gate, T), F32)],
        scratch_shapes=[pltpu.VMEM((8, CONV_CH), F32)],
        compiler_params=pltpu.CompilerParams(dimension_semantics=("arbitrary", "arbitrary")),
        name="proj",
    )(x, pos_row, wc, wqt, wkvt, wgt, cw, freq)


def _compress_kernel(x_ref, w1_ref, w2_ref, w2t_ref, pos_ref, o_ref, ot_ref):
    xb = x_ref[0, 0]
    nc = xb.shape[0]
    w1 = w1_ref[0]
    w1b = w1.astype(BF16)
    half = CMP_STRIDE * HEAD_DIM
    a = _dot(xb, w1b[:half])
    b = _dot(xb, w1b[half:])
    b_next = pltpu.roll(b, nc - 1, 0)
    pb = _dot(pos_ref[0], w1, precision=lax.Precision.HIGHEST)[0:1]
    h = a + b_next + pb
    g = (0.5 * h * (1.0 + jnp.tanh(np.sqrt(2.0 / np.pi) * (h + 0.044715 * (h * h * h))))).astype(BF16)
    o_ref[0, 0] = _dot(g, w2_ref[0].astype(BF16)).astype(BF16)
    ot_ref[0, 0] = _dot_nt(w2t_ref[0].astype(BF16), g).astype(BF16)


def _compress(xin, w1, w2, w2t, pos):
    _, BG, nc, W = xin.shape
    per = lambda a: pl.BlockSpec((1,) + a.shape[1:], lambda s, i: (s, 0, 0))
    return pl.pallas_call(
        _compress_kernel,
        grid=(2, BG),
        in_specs=[pl.BlockSpec((1, 1, nc, W), lambda s, i: (s, i, 0, 0)), per(w1), per(w2), per(w2t), per(pos)],
        out_specs=[pl.BlockSpec((1, 1, nc, HEAD_DIM), lambda s, i: (s, i, 0, 0)),
                   pl.BlockSpec((1, 1, HEAD_DIM, nc), lambda s, i: (s, i, 0, 0))],
        out_shape=[jax.ShapeDtypeStruct((2, BG, nc, HEAD_DIM), BF16),
                   jax.ShapeDtypeStruct((2, BG, HEAD_DIM, nc), BF16)],
        name="compress",
    )(xin, w1, w2, w2t, pos)


def _head_lanes(qt):
    return jnp.concatenate([qt[r * HEAD_DIM:(r + 1) * HEAD_DIM] for r in range(HEADS_PER_GROUP)], axis=1)


def _gated_out(ot, scale, gates, branch, tq):
    rows = []
    for r in range(HEADS_PER_GROUP):
        c = branch * HEADS_PER_GROUP + r
        sl = slice(r * tq, (r + 1) * tq)
        rows.append(ot[:, sl] * (scale[:, sl] * gates[c:c + 1]))
    return jnp.concatenate(rows, axis=0).astype(BF16)


def _attn_specs(tq):
    G = N_KV_GROUPS
    qspec = pl.BlockSpec((1, GROUP_W, tq), lambda b, g, i: (b, g, i))
    gspec = pl.BlockSpec((1, GATE_ROWS, tq), lambda b, g, i: (b, g, i))
    bg4 = lambda a: pl.BlockSpec((1, 1) + a.shape[2:], lambda b, g, i: (b, g, 0, 0))
    return G, qspec, gspec, bg4


def _cmp_attn_kernel(q_ref, kc_ref, vct_ref, g_ref, ovt_ref, o_ref, sel_ref):
    qi = pl.program_id(2)
    tq = q_ref.shape[2]
    q0 = qi * tq
    qt = _head_lanes(q_ref[0])
    s = _dot(kc_ref[0, 0], qt)
    c = lax.broadcasted_iota(I32, s.shape, 0)
    t = q0 + (lax.broadcasted_iota(I32, (1, s.shape[1]), 1) & (tq - 1))
    valid = c <= (t - (CMP_BLOCK - 1)) // CMP_STRIDE
    sm = jnp.where(valid, s, NEG_INF)
    m = jnp.max(sm, axis=0, keepdims=True)
    p = jnp.where(valid, jnp.exp2(sm - m), 0.0)
    l = jnp.sum(p, axis=0, keepdims=True)
    inv = 1.0 / jnp.where(l > 0.0, l, 1.0)
    ot = _dot(vct_ref[0, 0], p.astype(BF16))
    o_ref[0] = _gated_out(ot, inv, g_ref[0], 0, tq)

    pn = p * inv
    psum = pn[:, 0:tq]
    for r in range(1, HEADS_PER_GROUP):
        psum = psum + pn[:, r * tq:(r + 1) * tq]
    imp = _dot(ovt_ref[...], psum, precision=lax.Precision.HIGHEST)
    j = lax.broadcasted_iota(I32, imp.shape, 0)
    cur = (q0 + lax.broadcasted_iota(I32, imp.shape, 1)) // SEL_BLOCK
    valid_b = j <= cur
    forced = (j == 0) | (j == cur) | (j == cur - 1)
    score = jnp.where(valid_b, jnp.where(forced, FORCE_SCORE, imp), NEG_INF)
    sub = 8
    groups = [score[a:a + sub] for a in range(0, SEL_LANES, sub)]
    jrow = lax.broadcasted_iota(I32, groups[0].shape, 0)
    ranks = [jnp.zeros(g_.shape, I32) for g_ in groups]
    for i in range(SEL_LANES):
        si = score[i:i + 1, :]
        for a, g_ in enumerate(groups):
            if a > i // sub:
                inc = jnp.where(si >= g_, 1, 0)
            elif a < i // sub:
                inc = jnp.where(si > g_, 1, 0)
            else:
                inc = jnp.where(jrow > i % sub, jnp.where(si >= g_, 1, 0), jnp.where(si > g_, 1, 0))
            ranks[a] = ranks[a] + inc
    rank = jnp.concatenate(ranks, axis=0)
    selected = (rank < N_SELECT) & valid_b
    sel_ref[0, 0] = jnp.where(selected, 0.0, SEL_MASK_BIAS).astype(BF16)


def _cmp_attn(qt, kcmp, vcmp_t, gates, ovt):
    B, _, T = qt.shape
    tq = min(CMP_TQ, T)
    G, qspec, gspec, _ = _attn_specs(tq)
    cspec = lambda a: pl.BlockSpec((1, 1) + a.shape[2:], lambda b, g, i: (0, b * G + g, 0, 0))
    vspec = lambda a: pl.BlockSpec((1, 1) + a.shape[2:], lambda b, g, i: (1, b * G + g, 0, 0))
    return pl.pallas_call(
        _cmp_attn_kernel,
        grid=(B, G, T // tq),
        in_specs=[qspec, cspec(kcmp), vspec(vcmp_t), gspec, pl.BlockSpec(ovt.shape, lambda b, g, i: (0, 0))],
        out_specs=[qspec, pl.BlockSpec((1, 1, SEL_LANES, tq), lambda b, g, i: (b, g, 0, i))],
        out_shape=[jax.ShapeDtypeStruct((B, NSA_WIDTH, T), BF16),
                   jax.ShapeDtypeStruct((B, G, SEL_LANES, T), BF16)],
        name="cmp_attn",
    )(qt, kcmp, vcmp_t, gates, ovt)


def _sel_attn_kernel(q_ref, sel_ref, k_ref, vt_ref, g_ref, o_ref, *, kc):
    qi = pl.program_id(2)
    tq = q_ref.shape[2]
    q0 = qi * tq
    qt = q_ref[0]
    sb = sel_ref[0, 0]
    qa = jnp.concatenate(
        [jnp.concatenate([qt[r * HEAD_DIM:(r + 1) * HEAD_DIM], sb], axis=0) for r in range(HEADS_PER_GROUP)],
        axis=1)
    gw = qa.shape[1] // SEL_SPLIT
    qas = [qa[:, i * gw:(i + 1) * gw] for i in range(SEL_SPLIT)]

    def step(kstart, kn, carry, diagonal):
        kblk = k_ref[0, 0, pl.ds(kstart, kn), :]
        vblk = vt_ref[0, :, pl.ds(kstart, kn)]
        ss = [_dot(kblk, qg) for qg in qas]
        out = []
        for gi, (s, (m, l, acc)) in enumerate(zip(ss, carry)):
            if diagonal:
                row = lax.broadcasted_iota(I32, s.shape, 0)
                off = (gi * gw + lax.broadcasted_iota(I32, s.shape, 1)) & (tq - 1)
                s = jnp.where(row <= off, s, NEG_INF)
            mn = jnp.maximum(m, jnp.max(s, axis=0, keepdims=True))
            alpha = jnp.exp2(m - mn)
            p = jnp.exp2(s - mn)
            l = alpha * l + jnp.sum(p, axis=0, keepdims=True)
            acc = alpha * acc + _dot(vblk, p.astype(BF16))
            out.append((mn, l, acc))
        return tuple(out)

    init = tuple((jnp.full((1, gw), NEG_INF, F32), jnp.zeros((1, gw), F32), jnp.zeros((HEAD_DIM, gw), F32))
                 for _ in range(SEL_SPLIT))
    n_full = q0 // kc
    carry = lax.fori_loop(0, n_full, lambda i, cr: step(pl.multiple_of(i * kc, kc), kc, cr, False), init)
    carry = lax.fori_loop(n_full * (kc // tq), qi, lambda i, cr: step(pl.multiple_of(i * tq, tq), tq, cr, False), carry)
    carry = step(pl.multiple_of(q0, tq), tq, carry, True)
    l = jnp.concatenate([c[1] for c in carry], axis=1)
    acc = jnp.concatenate([c[2] for c in carry], axis=1)
    o_ref[0] = _gated_out(acc, 1.0 / l, g_ref[0], 1, tq)


def _sel_attn(qt, sel, kaug, kvt, v_row_block, gates):
    B, _, T = qt.shape
    tq = min(SEL_TQ, T)
    kc = min(SEL_KC, T)
    G, qspec, gspec, bg4 = _attn_specs(tq)
    return pl.pallas_call(
        functools.partial(_sel_attn_kernel, kc=kc),
        grid=(B, G, T // tq),
        in_specs=[qspec, pl.BlockSpec((1, 1, SEL_LANES, tq), lambda b, g, i: (b, g, 0, i)), bg4(kaug),
                  pl.BlockSpec((1, HEAD_DIM, T), lambda b, g, i: (b, v_row_block + g, 0)), gspec],
        out_specs=qspec,
        out_shape=jax.ShapeDtypeStruct((B, NSA_WIDTH, T), BF16),
        name="sel_attn",
    )(qt, sel, kaug, kvt, gates)


def _win_attn_kernel(q_ref, k_ref, vt_ref, g_ref, o_ref, *, span, sub):
    qi = pl.program_id(2)
    tq = q_ref.shape[2]
    T = k_ref.shape[2]

    def tile(s0, interior):
        cs = slice(s0, s0 + sub)
        q0 = qi * tq + s0
        start = pl.multiple_of(jnp.clip(q0 + sub - span, 0, T - span), sub)
        qt = _head_lanes(q_ref[0, :, cs])
        s = _dot(k_ref[0, 0, pl.ds(start, span), :], qt)
        if interior:
            row = lax.broadcasted_iota(I32, (sub, s.shape[1]), 0)
            off = lax.broadcasted_iota(I32, (sub, s.shape[1]), 1) & (sub - 1)
            s = jnp.concatenate([jnp.where(row > off, s[:sub], NEG_INF), s[sub:span - sub],
                                 jnp.where(row <= off, s[span - sub:], NEG_INF)], axis=0)
        else:
            kp = start + lax.broadcasted_iota(I32, s.shape, 0)
            t = q0 + (lax.broadcasted_iota(I32, s.shape, 1) & (sub - 1))
            diff = t - kp
            s = jnp.where((diff >= 0) & (diff < WINDOW), s, NEG_INF)
        m = jnp.max(s, axis=0, keepdims=True)
        p = jnp.exp2(s - m)
        l = jnp.sum(p, axis=0, keepdims=True)
        ot = _dot(vt_ref[0, :, pl.ds(start, span)], p.astype(BF16))
        o_ref[0, :, cs] = _gated_out(ot, 1.0 / l, g_ref[0, :, cs], 2, sub)

    first_interior = -(-WINDOW // tq)
    if span == WINDOW + sub:
        @pl.when(qi >= first_interior)
        def _():
            for s0 in range(0, tq, sub):
                tile(s0, True)

    @pl.when((qi < first_interior) | (span != WINDOW + sub))
    def _():
        for s0 in range(0, tq, sub):
            tile(s0, False)


def _win_attn(qt, k, kvt, v_row_block, gates):
    B, _, T = qt.shape
    tq = min(WIN_TQ, T)
    sub = min(WIN_SUB, T)
    span = min(WINDOW + sub, T)
    G, qspec, gspec, bg4 = _attn_specs(tq)
    return pl.pallas_call(
        functools.partial(_win_attn_kernel, span=span, sub=sub),
        grid=(B, G, T // tq),
        in_specs=[qspec, bg4(k), pl.BlockSpec((1, HEAD_DIM, T), lambda b, g, i: (b, v_row_block + g, 0)), gspec],
        out_specs=qspec,
        out_shape=jax.ShapeDtypeStruct((B, NSA_WIDTH, T), BF16),
        name="win_attn",
    )(qt, k, kvt, gates)


def _store_row_tiles(ref, x, row0=0):
    rows = x.shape[0]
    for s in range(ROW_SUB):
        ref[pl.ds(row0 * ROW_SUB + s, rows, stride=ROW_SUB), :] = x[:, s * LANES:(s + 1) * LANES]


def _load_row_tiles(ref):
    rows = ref.shape[0] // ROW_SUB
    return jnp.concatenate([ref[pl.ds(s, rows, stride=ROW_SUB), :] for s in range(ROW_SUB)], axis=1)


def _tile_copy(src, si, dst, di, sem):
    return pltpu.make_async_copy(src.at[pl.ds(pl.multiple_of(si * ROW_SUB, ROW_SUB), ROW_SUB), :],
                                 dst.at[pl.ds(pl.multiple_of(di * ROW_SUB, ROW_SUB), ROW_SUB), :], sem)


def _layer_norm(h, g, b):
    mu = jnp.mean(h, axis=-1, keepdims=True)
    c = h - mu
    var = jnp.mean(c * c, axis=-1, keepdims=True)
    return c * lax.rsqrt(var + LN_EPS) * g + b


def _merge_kernel(x_ref, ya_ref, oc_ref, os_ref, ow_ref, wmg_ref, wuc_ref, wun_ref, wo_ref, g1_ref, b1_ref,
                  wr_ref, br_ref, tri_ref, x1_ref, ri_ref, rg_ref, cnt_ref, xs_hbm,
                  carry_ref, stage, dest_v, dest_s, copy_sem, row_sem, *, alpha, cap):
    i = pl.program_id(0)
    last = pl.num_programs(0) - 1
    sub = tri_ref.shape[0]
    tm = x_ref.shape[0]
    slot = i % 2
    nsub = tm // sub

    def id_copy(si):
        return pltpu.make_async_copy(dest_v.at[si], dest_s.at[si], copy_sem.at[si])

    def issue_rows(si, s):
        id_copy(si).wait()
        for t in range(sub):
            for k in range(TOP_K):
                _tile_copy(stage.at[s], si * sub + t, xs_hbm, dest_s[si, k, t], row_sem.at[s]).start(priority=k % 2)

    def wait_rows(s, rows):
        for k in range(TOP_K):
            n = rows * ROW_SUB
            pltpu.make_async_copy(stage.at[s, pl.ds(0, n), :], xs_hbm.at[pl.ds(0, n), :], row_sem.at[s]).wait()

    @pl.when(i == 0)
    def _():
        carry_ref[...] = jnp.zeros_like(carry_ref)
        stage[1, pl.ds((nsub - 1) * sub * ROW_SUB, sub * ROW_SUB), :] = jnp.zeros((sub * ROW_SUB, LANES), F32)
        spare = (N_EXPERTS * cap + lax.broadcasted_iota(I32, (TOP_K, sub), 0) * sub
                 + lax.broadcasted_iota(I32, (TOP_K, sub), 1))
        dest_v[nsub - 1] = spare
        id_copy(nsub - 1).start()

    wr = wr_ref[...]
    wr_hi = wr.astype(BF16)
    wr_hl = jnp.concatenate([wr_hi, (wr - wr_hi.astype(F32)).astype(BF16)], axis=0)
    total = carry_ref[...]
    for si in range(nsub):
        s0 = si * sub
        rs = slice(s0, s0 + sub)
        x = x_ref[rs, :]
        mg = _dot(x.astype(BF16), wmg_ref[...])
        if si == 0:
            issue_rows(nsub - 1, 1 - slot)
        else:
            issue_rows(si - 1, slot)
        y_a = _dot(ya_ref[rs, :], wuc_ref[...])
        o_nsa_t = (oc_ref[0, :, rs].astype(F32) + os_ref[0, :, rs].astype(F32)
                   + ow_ref[0, :, rs].astype(F32))
        y_b = _dot(o_nsa_t.T.astype(BF16), wun_ref[...])
        merged = jax.nn.sigmoid(mg[:, :D_MODEL]) * y_a + jax.nn.sigmoid(mg[:, D_MODEL:]) * y_b
        h = alpha * x + _dot(merged.astype(BF16), wo_ref[...])
        x1 = _layer_norm(h, g1_ref[...], b1_ref[...])
        x1_ref[rs, :] = x1
        _store_row_tiles(stage.at[slot], x1, s0)

        x_hi = x1.astype(BF16)
        x_lo = (x1 - x_hi.astype(F32)).astype(BF16)
        both = _dot_nt(wr_hl, x_hi)
        logits = both[:N_EXPERTS] + both[N_EXPERTS:] + _dot_nt(wr_hi, x_lo) + br_ref[...]
        expert = lax.broadcasted_iota(I32, logits.shape, 0).astype(F32)
        rem = logits
        vals, idxs, hots = [], [], []
        for _ in range(TOP_K):
            m = jnp.max(rem, axis=0, keepdims=True)
            idx = jnp.min(jnp.where(rem == m, expert, float(N_EXPERTS)), axis=0, keepdims=True)
            hot = expert == idx
            vals.append(m)
            idxs.append(idx)
            hots.append(hot)
            rem = jnp.where(hot, -3.0e38, rem)
        es = [jnp.exp(v - vals[0]) for v in vals]
        den = es[0]
        for e in es[1:]:
            den = den + e
        chosen = hots[0]
        for hot in hots[1:]:
            chosen = chosen | hot
        chosen_f = jnp.where(chosen, 1.0, 0.0)

        before = _dot(chosen_f.astype(BF16), tri_ref[...]) + total
        total = total + jnp.sum(chosen_f, axis=1, keepdims=True)
        ranks = [jnp.sum(jnp.where(hot, before, 0.0), axis=0, keepdims=True) for hot in hots]
        ri_ref[:, rs] = jnp.concatenate(idxs + ranks, axis=0).astype(I32)
        rg_ref[:, rs] = jnp.concatenate([e / den for e in es] + [jnp.zeros_like(den)] * TOP_K, axis=0)

        dest_v[si] = jnp.concatenate(idxs, axis=0).astype(I32) * cap + jnp.concatenate(ranks, axis=0).astype(I32)
        id_copy(si).start()

    carry_ref[...] = total
    cnt_ref[...] = jnp.broadcast_to(total, cnt_ref.shape).astype(I32)

    @pl.when(i == 0)
    def _():
        wait_rows(1, sub)

    @pl.when(i > 0)
    def _():
        wait_rows(1 - slot, tm)

    @pl.when(i == last)
    def _():
        issue_rows(nsub - 1, slot)
        wait_rows(slot, tm)


def _merge(x, ya, oc, os_, ow, wmg, wuc, wun, wo, g1, b1, wr, br, tri, alpha):
    N, D = x.shape
    B, _, T = oc.shape
    tm = min(MERGE_TM, T)
    nt = T // tm
    tok = lambda w: pl.BlockSpec((tm, w), lambda i: (i, 0))
    feat = pl.BlockSpec((1, NSA_WIDTH, tm), lambda i: (i // nt, 0, i % nt))
    full = lambda a: pl.BlockSpec(a.shape, lambda i: (0,) * a.ndim)
    sub = tri.shape[0]
    return pl.pallas_call(
        functools.partial(_merge_kernel, alpha=alpha, cap=N),
        grid=(N // tm,),
        in_specs=[tok(D), tok(CONV_CH), feat, feat, feat,
                  full(wmg), full(wuc), full(wun), full(wo), full(g1), full(b1), full(wr), full(br), full(tri)],
        out_specs=[tok(D),
                   pl.BlockSpec((2 * TOP_K, tm), lambda i: (0, i)), pl.BlockSpec((2 * TOP_K, tm), lambda i: (0, i)),
                   pl.BlockSpec((N_EXPERTS, LANES), lambda i: (0, 0)), pl.BlockSpec(memory_space=pl.ANY)],
        out_shape=[jax.ShapeDtypeStruct((N, D), F32),
                   jax.ShapeDtypeStruct((2 * TOP_K, N), I32),
                   jax.ShapeDtypeStruct((2 * TOP_K, N), F32),
                   jax.ShapeDtypeStruct((N_EXPERTS, LANES), I32),
                   jax.ShapeDtypeStruct(((N_EXPERTS * N + TOP_K * sub) * ROW_SUB, LANES), F32)],
        scratch_shapes=[pltpu.VMEM((N_EXPERTS, 1), F32), pltpu.VMEM((2, tm * ROW_SUB, LANES), F32),
                        pltpu.VMEM((tm // sub, TOP_K, sub), I32), pltpu.SMEM((tm // sub, TOP_K, sub), I32),
                        pltpu.SemaphoreType.DMA((tm // sub,)), pltpu.SemaphoreType.DMA((2,))],
        compiler_params=pltpu.CompilerParams(dimension_semantics=("arbitrary",)),
        name="merge",
    )(x, ya, oc, os_, ow, wmg, wuc, wun, wo, g1, b1, wr, br, tri)


def _pad_rows_kernel(lo_ref, hi_ref, xs_in, xs_hbm, zero_ref, sem):
    del xs_in
    zero_ref[...] = jnp.zeros_like(zero_ref)

    def start(r, c):
        _tile_copy(zero_ref, 0, xs_hbm, r, sem).start()
        return c

    def wait(r, c):
        _tile_copy(zero_ref, 0, xs_hbm, r, sem).wait()
        return c

    group = 4
    for e0 in range(0, N_EXPERTS, group):
        for e in range(e0, e0 + group):
            lax.fori_loop(lo_ref[e], hi_ref[e], start, 0)
        for e in range(e0, e0 + group):
            lax.fori_loop(lo_ref[e], hi_ref[e], wait, 0)


def _pad_rows(lo, hi, xs):
    smem = pl.BlockSpec(memory_space=pltpu.SMEM)
    anyspace = pl.BlockSpec(memory_space=pl.ANY)
    return pl.pallas_call(
        _pad_rows_kernel,
        grid=(1,),
        in_specs=[smem, smem, anyspace],
        out_specs=anyspace,
        out_shape=jax.ShapeDtypeStruct(xs.shape, xs.dtype),
        input_output_aliases={2: 0},
        scratch_shapes=[pltpu.VMEM((ROW_SUB, LANES), F32), pltpu.SemaphoreType.DMA(())],
        compiler_params=pltpu.CompilerParams(dimension_semantics=("arbitrary",)),
        name="pad_rows",
    )(lo, hi, xs)


def _experts_kernel(be_ref, nu_ref, rb_ref, xs_ref, wgu_ref, perm_ref, bg_ref, bl_ref, wd_ref, bd_ref, ys_ref, wg_s, wl_s, wd_s):
    i = pl.program_id(0)

    @pl.when((i == 0) | (be_ref[i] != be_ref[jnp.maximum(i, 1) - 1]))
    def _():
        w = perm_ref.shape[0]
        for c in range(2 * D_FF // w):
            t = _dot(wgu_ref[0, :, c * w:(c + 1) * w].astype(BF16), perm_ref[...])
            wg_s[:, c * (w // 2):(c + 1) * (w // 2)] = t[:, :w // 2].astype(BF16)
            wl_s[:, c * (w // 2):(c + 1) * (w // 2)] = t[:, w // 2:].astype(BF16)
        wd_s[...] = wd_ref[0].astype(BF16)

    @pl.when(i < nu_ref[0])
    def _():
        xb = _load_row_tiles(xs_ref).astype(BF16)
        x_glu = jnp.minimum(_dot(xb, wg_s[...]) + bg_ref[0], SWIGLU_LIMIT)
        x_lin = jnp.clip(_dot(xb, wl_s[...]) + bl_ref[0], -SWIGLU_LIMIT, SWIGLU_LIMIT)
        act = x_glu * jax.nn.sigmoid(SWIGLU_ALPHA * x_glu) * (x_lin + 1.0)
        _store_row_tiles(ys_ref, _dot(act.astype(BF16), wd_s[...]) + bd_ref[0])


def _experts(block_expert, n_used, row_block, xs, wgu, perm, bg, bl, wd, bd):
    D = D_MODEL
    n_blocks = block_expert.shape[0]
    rows = lambda i, be, nu, rb: (rb[i], 0)
    wsel = lambda i, be, nu, rb: (be[i], 0, 0)
    return pl.pallas_call(
        _experts_kernel,
        grid_spec=pltpu.PrefetchScalarGridSpec(
            num_scalar_prefetch=3,
            grid=(n_blocks,),
            in_specs=[pl.BlockSpec((MOE_BLK * ROW_SUB, LANES), rows),
                      pl.BlockSpec((1, D, 2 * D_FF), wsel),
                      pl.BlockSpec(perm.shape, lambda i, be, nu, rb: (0, 0)),
                      pl.BlockSpec((1, 1, D_FF), wsel), pl.BlockSpec((1, 1, D_FF), wsel),
                      pl.BlockSpec((1, D_FF, D), wsel), pl.BlockSpec((1, 1, D), wsel)],
            out_specs=pl.BlockSpec((MOE_BLK * ROW_SUB, LANES), rows),
            scratch_shapes=[pltpu.VMEM((D, D_FF), BF16), pltpu.VMEM((D, D_FF), BF16), pltpu.VMEM((D_FF, D), BF16)]),
        out_shape=jax.ShapeDtypeStruct(xs.shape, F32),
        compiler_params=pltpu.CompilerParams(dimension_semantics=("arbitrary",)),
        name="experts",
    )(block_expert, n_used, row_block, xs, wgu, perm, bg, bl, wd, bd)


def _combine_kernel(dest_ref, dest_next_ref, ys_hbm, x1_ref, rg_ref, g2_ref, b2_ref, o_ref, buf, sem, *, alpha):
    i = pl.program_id(0)
    last = pl.num_programs(0) - 1
    tm = o_ref.shape[0]
    slot = i % 2
    other = 1 - slot

    def gather(ids_ref, t, to_slot):
        for k in range(TOP_K):
            _tile_copy(ys_hbm, ids_ref[t * TOP_K + k], buf.at[to_slot, k], t, sem.at[to_slot]).start(priority=k % 2)

    def wait_slot(s):
        for k in range(TOP_K):
            pltpu.make_async_copy(ys_hbm.at[pl.ds(0, tm * ROW_SUB), :], buf.at[s, k], sem.at[s]).wait()

    @pl.when(i == 0)
    def _():
        def first(t, c):
            gather(dest_ref, t, slot)
            return c
        lax.fori_loop(0, tm, first, 0)

    wait_slot(slot)

    gs = min(COMB_GROUP, tm)

    def group(g, c):
        r0 = pl.multiple_of(g * gs, gs)
        for j in range(gs):
            gather(dest_next_ref, r0 + j, other)
        rows = pl.ds(r0, gs)
        gate = rg_ref[rows, :]
        y = alpha * x1_ref[rows, :]
        for k in range(TOP_K):
            tiles = buf.at[slot, k, pl.ds(pl.multiple_of(r0 * ROW_SUB, gs * ROW_SUB), gs * ROW_SUB), :]
            y = y + gate[:, k:k + 1] * _load_row_tiles(tiles)
        o_ref[rows, :] = _layer_norm(y, g2_ref[...], b2_ref[...])
        return c

    lax.fori_loop(0, tm // gs, group, 0)

    @pl.when(i == last)
    def _():
        wait_slot(other)


def _combine(dest, ys, x1, rg, g2, b2, alpha):
    N, D = x1.shape
    tm = min(COMB_TM, N)
    n = N // tm
    tok = lambda w: pl.BlockSpec((tm, w), lambda i: (i, 0))
    full = lambda a: pl.BlockSpec(a.shape, lambda i: (0,) * a.ndim)
    ids = lambda f: pl.BlockSpec((tm * TOP_K,), f, memory_space=pltpu.SMEM)
    return pl.pallas_call(
        functools.partial(_combine_kernel, alpha=alpha),
        grid=(n,),
        in_specs=[ids(lambda i: (i,)), ids(lambda i: (jnp.minimum(i + 1, n - 1),)),
                  pl.BlockSpec(memory_space=pl.ANY), tok(D), tok(rg.shape[1]), full(g2), full(b2)],
        out_specs=tok(D),
        out_shape=jax.ShapeDtypeStruct((N, D), F32),
        scratch_shapes=[pltpu.VMEM((2, TOP_K, tm * ROW_SUB, LANES), F32), pltpu.SemaphoreType.DMA((2,))],
        compiler_params=pltpu.CompilerParams(dimension_semantics=("arbitrary",)),
        name="combine",
    )(dest, dest, ys, x1, rg, g2, b2)


def _rope_freq():
    half = ROPE_DIM // 2
    inv = (np.float32(ROPE_THETA) ** (-np.arange(half, dtype=np.float32) * np.float32(2.0 / ROPE_DIM))).astype(np.float32)
    return inv[:, None]


def _overlap_t(T):
    nc = T // CMP_STRIDE
    c0 = np.arange(nc) * CMP_STRIDE
    j0 = np.arange(SEL_LANES) * SEL_BLOCK
    ov = (c0[None, :] < j0[:, None] + SEL_BLOCK) & (c0[None, :] + CMP_BLOCK > j0[:, None])
    ov &= (np.arange(nc) < nc - 1)[None, :] & (np.arange(SEL_LANES) < T // SEL_BLOCK)[:, None]
    return ov.astype(np.float32)


def _deinterleave_perm():
    w = 2 * LANES
    p = np.zeros((w, w), np.float32)
    p[np.arange(0, w, 2), np.arange(w // 2)] = 1.0
    p[np.arange(1, w, 2), w // 2 + np.arange(w // 2)] = 1.0
    return p


def _token_major(t, B, T):
    return t.reshape(B, N_KV_GROUPS, HEAD_DIM, T).transpose(0, 1, 3, 2)


def _layer(x, positions, w_in, conv_w, cmp_pos_k, cmp_w1_k, cmp_w2_k, cmp_pos_v, cmp_w1_v, cmp_w2_v,
           w_up_conv, w_up_nsa, w_o, ln1_g, ln1_b, w_router, b_router, w_gate_up, b_gate_up,
           w_down, b_down, ln2_g, ln2_b, alpha):
    B, T, D = x.shape
    G, R = N_KV_GROUPS, HEADS_PER_GROUP
    N = B * T
    assert D == D_MODEL and T % SEL_BLOCK == 0 and T // SEL_BLOCK <= SEL_LANES

    c0 = 3 * CONV_CH
    c1 = c0 + NSA_WIDTH
    c2 = c1 + 6 * KV_WIDTH
    c3 = c2 + 3 * N_HEADS
    wc = w_in[:, :c0].astype(BF16)
    wqt = w_in[:, c0:c1].T.astype(BF16)
    wkvt = w_in[:, c1:c2].T.astype(BF16)
    gcols = np.zeros((G * GATE_ROWS,), np.int64)
    gmask = np.zeros((G * GATE_ROWS,), np.float32)
    for g in range(G):
        for br in range(3):
            for r in range(R):
                gcols[g * GATE_ROWS + br * R + r] = br * N_HEADS + g * R + r
                gmask[g * GATE_ROWS + br * R + r] = 1.0
    wgt = (w_in[:, c2:c3][:, gcols] * gmask).T.astype(BF16)
    wmg = w_in[:, c3:].astype(BF16)
    cw = conv_w.reshape(CONV_K, CONV_CH)
    pos_row = positions.astype(F32)[:, None, :]

    ya_pre, qt, kvt, gates = _proj(x, pos_row, wc, wqt, wkvt, wgt, cw, jnp.asarray(_rope_freq()))

    nc = T // CMP_STRIDE
    to_chunks = lambda t: _token_major(t, B, T).reshape(B * G, nc, CMP_STRIDE * HEAD_DIM)
    xin = jnp.stack([to_chunks(kvt[:, 0:KV_WIDTH]), to_chunks(kvt[:, KV_WIDTH:2 * KV_WIDTH])])
    w1 = jnp.stack([cmp_w1_k, cmp_w1_v])
    w2 = jnp.stack([cmp_w2_k, cmp_w2_v])
    pos_flat = jnp.stack([cmp_pos_k.reshape(1, -1), cmp_pos_v.reshape(1, -1)])
    pos_flat = jnp.pad(pos_flat, ((0, 0), (0, 7), (0, 0)))
    kcmp, kcmp_t = _compress(xin, w1, w2, w2.transpose(0, 2, 1), pos_flat)

    o_c, sel = _cmp_attn(qt, kcmp, kcmp_t, gates, jnp.asarray(_overlap_t(T)))

    ks = _token_major(kvt[:, 2 * KV_WIDTH:3 * KV_WIDTH], B, T)
    kw = _token_major(kvt[:, 4 * KV_WIDTH:5 * KV_WIDTH], B, T)
    onehot = (np.arange(T)[:, None] // SEL_BLOCK == np.arange(SEL_LANES)[None, :]).astype(np.float32)
    kaug = jnp.concatenate([ks, jnp.broadcast_to(jnp.asarray(onehot, BF16), (B, G, T, SEL_LANES))], axis=-1)
    o_s = _sel_attn(qt, sel, kaug, kvt, 3 * G, gates)
    o_w = _win_attn(qt, kw, kvt, 5 * G, gates)

    sub = min(MERGE_SUB, T)
    tri = jnp.asarray(np.triu(np.ones((sub, sub), np.float32), 1), BF16)
    assert N % MOE_BLK == 0
    x1, ri, rg, cnt, xs = _merge(x.reshape(N, D), ya_pre.reshape(N, CONV_CH), o_c, o_s, o_w, wmg,
                             w_up_conv.astype(BF16), w_up_nsa.astype(BF16), w_o.astype(BF16),
                             ln1_g[None, :], ln1_b[None, :], w_router.T, b_router[:, None], tri, alpha)

    counts = cnt[:, 0]
    blocks = (counts + MOE_BLK - 1) // MOE_BLK
    bend = jnp.cumsum(blocks).astype(I32)
    bstart = bend - blocks
    n_blocks = (N * TOP_K) // MOE_BLK + N_EXPERTS
    n_used = bend[-1:]
    b = jnp.minimum(jnp.arange(n_blocks, dtype=I32), n_used - 1)
    block_expert = jnp.minimum(jnp.sum((b[:, None] >= bend[None, :]).astype(I32), axis=1), N_EXPERTS - 1)
    first = jnp.sum(jnp.where(block_expert[:, None] == jnp.arange(N_EXPERTS)[None, :], bstart[None, :], 0), axis=1)
    row_block = block_expert * (N // MOE_BLK) + (b - first)
    base = jnp.arange(N_EXPERTS, dtype=I32) * N
    dest = (ri[:TOP_K] * N + ri[TOP_K:]).T.reshape(N * TOP_K)

    xs = _pad_rows(base + counts, base + blocks * MOE_BLK, xs)
    ys = _experts(block_expert, n_used, row_block, xs, w_gate_up, jnp.asarray(_deinterleave_perm(), BF16),
                  b_gate_up[:, None, 0::2], b_gate_up[:, None, 1::2], w_down, b_down[:, None, :])
    out = _combine(dest, ys, x1, rg.T, ln2_g[None, :], ln2_b[None, :], alpha)
    return out.reshape(B, T, D)


def kernel(x, positions, w_in, conv_w, cmp_pos_k, cmp_w1_k, cmp_w2_k, cmp_pos_v, cmp_w1_v, cmp_w2_v, w_up_conv, w_up_nsa, w_o, ln1_g, ln1_b, w_router, b_router, w_gate_up, b_gate_up, w_down, b_down, ln2_g, ln2_b):
    depth = w_in.shape[0]
    alpha = float((2 * depth) ** 0.25)
    h = x
    for l in range(depth):
        h = _layer(h, positions, w_in[l], conv_w[l], cmp_pos_k[l], cmp_w1_k[l], cmp_w2_k[l],
                   cmp_pos_v[l], cmp_w1_v[l], cmp_w2_v[l], w_up_conv[l], w_up_nsa[l], w_o[l],
                   ln1_g[l], ln1_b[l], w_router[l], b_router[l], w_gate_up[l], b_gate_up[l],
                   w_down[l], b_down[l], ln2_g[l], ln2_b[l], alpha)
    return h
```

```python
import functools

import numpy as np
import jax
import jax.numpy as jnp
from jax import lax
from jax.experimental import pallas as pl
from jax.experimental.pallas import tpu as pltpu

F32 = jnp.float32
BF16 = jnp.bfloat16
I32 = jnp.int32

D_MODEL = 1024
CONV_CH = 512
CONV_K = 3
N_HEADS = 8
N_KV_GROUPS = 2
HEADS_PER_GROUP = N_HEADS // N_KV_GROUPS
HEAD_DIM = 64
NSA_WIDTH = N_HEADS * HEAD_DIM
KV_WIDTH = N_KV_GROUPS * HEAD_DIM
ROPE_DIM = HEAD_DIM // 4
ROPE_THETA = 500000.0
CMP_BLOCK = 32
CMP_STRIDE = 16
CMP_HIDDEN = 256
SEL_BLOCK = 64
N_SELECT = 16
WINDOW = 512
N_EXPERTS = 32
TOP_K = 4
D_FF = 1024
SWIGLU_LIMIT = 7.0
SWIGLU_ALPHA = 1.702
LN_EPS = 1e-5
NEG_INF = -1e30
FORCE_SCORE = 1e4
LOG2E = 1.4426950408889634

LANES = 128
ROW_SUB = D_MODEL // LANES
SEL_LANES = 64
SEL_MASK_BIAS = -32768.0
GROUP_W = HEADS_PER_GROUP * HEAD_DIM
GATE_ROWS = 16

PROJ_TM = 1024
CMP_TQ = 256
SEL_TQ = 512
SEL_KC = 512
SEL_SPLIT = 2
WIN_TQ = 512
WIN_SUB = 128
MERGE_TM = 512
MERGE_SUB = 256
MOE_BLK = 512
COMB_TM = 256
COMB_GROUP = 128


def _dot(a, b, precision=None):
    return jnp.dot(a, b, precision=precision, preferred_element_type=F32)


def _dot_nt(a, b, precision=None):
    return lax.dot_general(a, b, (((1,), (1,)), ((), ())), precision=precision, preferred_element_type=F32)


def _proj_kernel(x_ref, pos_ref, wc_ref, wqt_ref, wkvt_ref, wgt_ref, cw_ref, freq_ref,
                 ya_ref, qt_ref, kvt_ref, gt_ref, carry_ref):
    ti = pl.program_id(1)
    tm = x_ref.shape[1]
    xb = x_ref[0].astype(BF16)

    pc = _dot(xb, wc_ref[...])
    xv = pc[:, :CONV_CH]
    bg = pc[:, CONV_CH:2 * CONV_CH]
    cg = pc[:, 2 * CONV_CH:]
    u = cg * xv

    @pl.when(ti == 0)
    def _():
        carry_ref[...] = jnp.zeros_like(carry_ref)

    prev = carry_ref[...]
    row = lax.broadcasted_iota(I32, u.shape, 0)
    u1 = jnp.where(row == 0, prev[7:8], pltpu.roll(u, 1, 0))
    u2 = jnp.where(row == 0, prev[6:7], jnp.where(row == 1, prev[7:8], pltpu.roll(u, 2, 0)))
    cw = cw_ref[...]
    conv = cw[2:3] * u + cw[1:2] * u1 + cw[0:1] * u2
    carry_ref[...] = u[tm - 8:]
    ya_ref[0] = (bg * conv).astype(BF16)

    ang = freq_ref[...] * pos_ref[0]
    cos = jnp.cos(ang)
    sin = jnp.sin(ang)
    half = ROPE_DIM // 2

    def rope_head(t):
        t1 = t[:half]
        t2 = t[half:ROPE_DIM]
        return [t1 * cos - t2 * sin, t2 * cos + t1 * sin, t[ROPE_DIM:]]

    def heads(t, rotate):
        out = []
        for h in range(t.shape[0] // HEAD_DIM):
            th = t[h * HEAD_DIM:(h + 1) * HEAD_DIM]
            out.extend(rope_head(th) if rotate(h) else [th])
        return jnp.concatenate(out, axis=0)

    qt = heads(_dot_nt(wqt_ref[...], xb), lambda h: True)
    qt_ref[0] = (qt * (HEAD_DIM ** -0.5 * LOG2E)).astype(BF16)
    kvt = heads(_dot_nt(wkvt_ref[...], xb), lambda h: (h // N_KV_GROUPS) % 2 == 0)
    kvt_ref[0] = kvt.astype(BF16)
    gt_ref[0] = jax.nn.sigmoid(_dot_nt(wgt_ref[...], xb))


def _proj(x, pos_row, wc, wqt, wkvt, wgt, cw, freq):
    B, T, D = x.shape
    tm = min(PROJ_TM, T)
    grid = (B, T // tm)
    full = lambda a: pl.BlockSpec(a.shape, lambda b, t: (0,) * a.ndim)
    tok = lambda w: pl.BlockSpec((1, tm, w), lambda b, t: (b, t, 0))
    feat = lambda r: pl.BlockSpec((1, r, tm), lambda b, t: (b, 0, t))
    n_gate = wgt.shape[0]
    return pl.pallas_call(
        _proj_kernel,
        grid=grid,
        in_specs=[tok(D), feat(1), full(wc), full(wqt), full(wkvt), full(wgt), full(cw), full(freq)],
        out_specs=[tok(CONV_CH), feat(NSA_WIDTH), feat(6 * KV_WIDTH), feat(n_gate)],
        out_shape=[jax.ShapeDtypeStruct((B, T, CONV_CH), BF16),
                   jax.ShapeDtypeStruct((B, NSA_WIDTH, T), BF16),
                   jax.ShapeDtypeStruct((B, 6 * KV_WIDTH, T), BF16),
                   jax.ShapeDtypeStruct((B, n_gate, T), F32)],
        scratch_shapes=[pltpu.VMEM((8, CONV_CH), F32)],
        compiler_params=pltpu.CompilerParams(dimension_semantics=("arbitrary", "arbitrary")),
        name="proj",
    )(x, pos_row, wc, wqt, wkvt, wgt, cw, freq)


def _compress_kernel(x_ref, w1_ref, w2_ref, w2t_ref, pos_ref, o_ref, ot_ref):
    xb = x_ref[0, 0]
    nc = xb.shape[0]
    w1 = w1_ref[0]
    w1b = w1.astype(BF16)
    half = CMP_STRIDE * HEAD_DIM
    a = _dot(xb, w1b[:half])
    b = _dot(xb, w1b[half:])
    b_next = pltpu.roll(b, nc - 1, 0)
    pb = _dot(pos_ref[0], w1, precision=lax.Precision.HIGHEST)[0:1]
    h = a + b_next + pb
    g = (0.5 * h * (1.0 + jnp.tanh(np.sqrt(2.0 / np.pi) * (h + 0.044715 * (h * h * h))))).astype(BF16)
    o_ref[0, 0] = _dot(g, w2_ref[0].astype(BF16)).astype(BF16)
    ot_ref[0, 0] = _dot_nt(w2t_ref[0].astype(BF16), g).astype(BF16)


def _compress(xin, w1, w2, w2t, pos):
    _, BG, nc, W = xin.shape
    per = lambda a: pl.BlockSpec((1,) + a.shape[1:], lambda s, i: (s, 0, 0))
    return pl.pallas_call(
        _compress_kernel,
        grid=(2, BG),
        in_specs=[pl.BlockSpec((1, 1, nc, W), lambda s, i: (s, i, 0, 0)), per(w1), per(w2), per(w2t), per(pos)],
        out_specs=[pl.BlockSpec((1, 1, nc, HEAD_DIM), lambda s, i: (s, i, 0, 0)),
                   pl.BlockSpec((1, 1, HEAD_DIM, nc), lambda s, i: (s, i, 0, 0))],
        out_shape=[jax.ShapeDtypeStruct((2, BG, nc, HEAD_DIM), BF16),
                   jax.ShapeDtypeStruct((2, BG, HEAD_DIM, nc), BF16)],
        name="compress",
    )(xin, w1, w2, w2t, pos)


def _head_lanes(qt):
    return jnp.concatenate([qt[r * HEAD_DIM:(r + 1) * HEAD_DIM] for r in range(HEADS_PER_GROUP)], axis=1)


def _gated_out(ot, scale, gates, branch, tq):
    rows = []
    for r in range(HEADS_PER_GROUP):
        c = branch * HEADS_PER_GROUP + r
        sl = slice(r * tq, (r + 1) * tq)
        rows.append(ot[:, sl] * (scale[:, sl] * gates[c:c + 1]))
    return jnp.concatenate(rows, axis=0).astype(BF16)


def _attn_specs(tq):
    G = N_KV_GROUPS
    qspec = pl.BlockSpec((1, GROUP_W, tq), lambda b, g, i: (b, g, i))
    gspec = pl.BlockSpec((1, GATE_ROWS, tq), lambda b, g, i: (b, g, i))
    bg4 = lambda a: pl.BlockSpec((1, 1) + a.shape[2:], lambda b, g, i: (b, g, 0, 0))
    return G, qspec, gspec, bg4


def _cmp_attn_kernel(q_ref, kc_ref, vct_ref, g_ref, ovt_ref, o_ref, sel_ref):
    qi = pl.program_id(2)
    tq = q_ref.shape[2]
    q0 = qi * tq
    qt = _head_lanes(q_ref[0])
    s = _dot(kc_ref[0, 0], qt)
    c = lax.broadcasted_iota(I32, s.shape, 0)
    t = q0 + (lax.broadcasted_iota(I32, (1, s.shape[1]), 1) & (tq - 1))
    valid = c <= (t - (CMP_BLOCK - 1)) // CMP_STRIDE
    sm = jnp.where(valid, s, NEG_INF)
    m = jnp.max(sm, axis=0, keepdims=True)
    p = jnp.where(valid, jnp.exp2(sm - m), 0.0)
    l = jnp.sum(p, axis=0, keepdims=True)
    inv = 1.0 / jnp.where(l > 0.0, l, 1.0)
    ot = _dot(vct_ref[0, 0], p.astype(BF16))
    o_ref[0] = _gated_out(ot, inv, g_ref[0], 0, tq)

    pn = p * inv
    psum = pn[:, 0:tq]
    for r in range(1, HEADS_PER_GROUP):
        psum = psum + pn[:, r * tq:(r + 1) * tq]
    imp = _dot(ovt_ref[...], psum, precision=lax.Precision.HIGHEST)
    j = lax.broadcasted_iota(I32, imp.shape, 0)
    cur = (q0 + lax.broadcasted_iota(I32, imp.shape, 1)) // SEL_BLOCK
    valid_b = j <= cur
    forced = (j == 0) | (j == cur) | (j == cur - 1)
    score = jnp.where(valid_b, jnp.where(forced, FORCE_SCORE, imp), NEG_INF)
    sub = 8
    groups = [score[a:a + sub] for a in range(0, SEL_LANES, sub)]
    jrow = lax.broadcasted_iota(I32, groups[0].shape, 0)
    ranks = [jnp.zeros(g_.shape, I32) for g_ in groups]
    for i in range(SEL_LANES):
        si = score[i:i + 1, :]
        for a, g_ in enumerate(groups):
            if a > i // sub:
                inc = jnp.where(si >= g_, 1, 0)
            elif a < i // sub:
                inc = jnp.where(si > g_, 1, 0)
            else:
                inc = jnp.where(jrow > i % sub, jnp.where(si >= g_, 1, 0), jnp.where(si > g_, 1, 0))
            ranks[a] = ranks[a] + inc
    rank = jnp.concatenate(ranks, axis=0)
    selected = (rank < N_SELECT) & valid_b
    sel_ref[0, 0] = jnp.where(selected, 0.0, SEL_MASK_BIAS).astype(BF16)


def _cmp_attn(qt, kcmp, vcmp_t, gates, ovt):
    B, _, T = qt.shape
    tq = min(CMP_TQ, T)
    G, qspec, gspec, _ = _attn_specs(tq)
    cspec = lambda a: pl.BlockSpec((1, 1) + a.shape[2:], lambda b, g, i: (0, b * G + g, 0, 0))
    vspec = lambda a: pl.BlockSpec((1, 1) + a.shape[2:], lambda b, g, i: (1, b * G + g, 0, 0))
    return pl.pallas_call(
        _cmp_attn_kernel,
        grid=(B, G, T // tq),
        in_specs=[qspec, cspec(kcmp), vspec(vcmp_t), gspec, pl.BlockSpec(ovt.shape, lambda b, g, i: (0, 0))],
        out_specs=[qspec, pl.BlockSpec((1, 1, SEL_LANES, tq), lambda b, g, i: (b, g, 0, i))],
        out_shape=[jax.ShapeDtypeStruct((B, NSA_WIDTH, T), BF16),
                   jax.ShapeDtypeStruct((B, G, SEL_LANES, T), BF16)],
        name="cmp_attn",
    )(qt, kcmp, vcmp_t, gates, ovt)


def _sel_attn_kernel(q_ref, sel_ref, k_ref, vt_ref, g_ref, o_ref, *, kc):
    qi = pl.program_id(2)
    tq = q_ref.shape[2]
    q0 = qi * tq
    qt = q_ref[0]
    sb = sel_ref[0, 0]
    qa = jnp.concatenate(
        [jnp.concatenate([qt[r * HEAD_DIM:(r + 1) * HEAD_DIM], sb], axis=0) for r in range(HEADS_PER_GROUP)],
        axis=1)
    gw = qa.shape[1] // SEL_SPLIT
    qas = [qa[:, i * gw:(i + 1) * gw] for i in range(SEL_SPLIT)]

    def step(kstart, kn, carry, diagonal):
        kblk = k_ref[0, 0, pl.ds(kstart, kn), :]
        vblk = vt_ref[0, :, pl.ds(kstart, kn)]
        ss = [_dot(kblk, qg) for qg in qas]
        out = []
        for gi, (s, (m, l, acc)) in enumerate(zip(ss, carry)):
            if diagonal:
                row = lax.broadcasted_iota(I32, s.shape, 0)
                off = (gi * gw + lax.broadcasted_iota(I32, s.shape, 1)) & (tq - 1)
                s = jnp.where(row <= off, s, NEG_INF)
            mn = jnp.maximum(m, jnp.max(s, axis=0, keepdims=True))
            alpha = jnp.exp2(m - mn)
            p = jnp.exp2(s - mn)
            l = alpha * l + jnp.sum(p, axis=0, keepdims=True)
            acc = alpha * acc + _dot(vblk, p.astype(BF16))
            out.append((mn, l, acc))
        return tuple(out)

    init = tuple((jnp.full((1, gw), NEG_INF, F32), jnp.zeros((1, gw), F32), jnp.zeros((HEAD_DIM, gw), F32))
                 for _ in range(SEL_SPLIT))
    n_full = q0 // kc
    carry = lax.fori_loop(0, n_full, lambda i, cr: step(pl.multiple_of(i * kc, kc), kc, cr, False), init)
    carry = lax.fori_loop(n_full * (kc // tq), qi, lambda i, cr: step(pl.multiple_of(i * tq, tq), tq, cr, False), carry)
    carry = step(pl.multiple_of(q0, tq), tq, carry, True)
    l = jnp.concatenate([c[1] for c in carry], axis=1)
    acc = jnp.concatenate([c[2] for c in carry], axis=1)
    o_ref[0] = _gated_out(acc, 1.0 / l, g_ref[0], 1, tq)


def _sel_attn(qt, sel, kaug, kvt, v_row_block, gates):
    B, _, T = qt.shape
    tq = min(SEL_TQ, T)
    kc = min(SEL_KC, T)
    G, qspec, gspec, bg4 = _attn_specs(tq)
    return pl.pallas_call(
        functools.partial(_sel_attn_kernel, kc=kc),
        grid=(B, G, T // tq),
        in_specs=[qspec, pl.BlockSpec((1, 1, SEL_LANES, tq), lambda b, g, i: (b, g, 0, i)), bg4(kaug),
                  pl.BlockSpec((1, HEAD_DIM, T), lambda b, g, i: (b, v_row_block + g, 0)), gspec],
        out_specs=qspec,
        out_shape=jax.ShapeDtypeStruct((B, NSA_WIDTH, T), BF16),
        name="sel_attn",
    )(qt, sel, kaug, kvt, gates)


def _win_attn_kernel(q_ref, k_ref, vt_ref, g_ref, o_ref, *, span, sub):
    qi = pl.program_id(2)
    tq = q_ref.shape[2]
    T = k_ref.shape[2]

    def tile(s0, interior):
        cs = slice(s0, s0 + sub)
        q0 = qi * tq + s0
        start = pl.multiple_of(jnp.clip(q0 + sub - span, 0, T - span), sub)
        qt = _head_lanes(q_ref[0, :, cs])
        s = _dot(k_ref[0, 0, pl.ds(start, span), :], qt)
        if interior:
            row = lax.broadcasted_iota(I32, (sub, s.shape[1]), 0)
            off = lax.broadcasted_iota(I32, (sub, s.shape[1]), 1) & (sub - 1)
            s = jnp.concatenate([jnp.where(row > off, s[:sub], NEG_INF), s[sub:span - sub],
                                 jnp.where(row <= off, s[span - sub:], NEG_INF)], axis=0)
        else:
            kp = start + lax.broadcasted_iota(I32, s.shape, 0)
            t = q0 + (lax.broadcasted_iota(I32, s.shape, 1) & (sub - 1))
            diff = t - kp
            s = jnp.where((diff >= 0) & (diff < WINDOW), s, NEG_INF)
        m = jnp.max(s, axis=0, keepdims=True)
        p = jnp.exp2(s - m)
        l = jnp.sum(p, axis=0, keepdims=True)
        ot = _dot(vt_ref[0, :, pl.ds(start, span)], p.astype(BF16))
        o_ref[0, :, cs] = _gated_out(ot, 1.0 / l, g_ref[0, :, cs], 2, sub)

    first_interior = -(-WINDOW // tq)
    if span == WINDOW + sub:
        @pl.when(qi >= first_interior)
        def _():
            for s0 in range(0, tq, sub):
                tile(s0, True)

    @pl.when((qi < first_interior) | (span != WINDOW + sub))
    def _():
        for s0 in range(0, tq, sub):
            tile(s0, False)


def _win_attn(qt, k, kvt, v_row_block, gates):
    B, _, T = qt.shape
    tq = min(WIN_TQ, T)
    sub = min(WIN_SUB, T)
    span = min(WINDOW + sub, T)
    G, qspec, gspec, bg4 = _attn_specs(tq)
    return pl.pallas_call(
        functools.partial(_win_attn_kernel, span=span, sub=sub),
        grid=(B, G, T // tq),
        in_specs=[qspec, bg4(k), pl.BlockSpec((1, HEAD_DIM, T), lambda b, g, i: (b, v_row_block + g, 0)), gspec],
        out_specs=qspec,
        out_shape=jax.ShapeDtypeStruct((B, NSA_WIDTH, T), BF16),
        name="win_attn",
    )(qt, k, kvt, gates)


def _store_row_tiles(ref, x, row0=0):
    rows = x.shape[0]
    for s in range(ROW_SUB):
        ref[pl.ds(row0 * ROW_SUB + s, rows, stride=ROW_SUB), :] = x[:, s * LANES:(s + 1) * LANES]


def _load_row_tiles(ref):
    rows = ref.shape[0] // ROW_SUB
    return jnp.concatenate([ref[pl.ds(s, rows, stride=ROW_SUB), :] for s in range(ROW_SUB)], axis=1)


def _tile_copy(src, si, dst, di, sem):
    return pltpu.make_async_copy(src.at[pl.ds(pl.multiple_of(si * ROW_SUB, ROW_SUB), ROW_SUB), :],
                                 dst.at[pl.ds(pl.multiple_of(di * ROW_SUB, ROW_SUB), ROW_SUB), :], sem)


def _layer_norm(h, g, b):
    mu = jnp.mean(h, axis=-1, keepdims=True)
    c = h - mu
    var = jnp.mean(c * c, axis=-1, keepdims=True)
    return c * lax.rsqrt(var + LN_EPS) * g + b


def _merge_kernel(x_ref, ya_ref, oc_ref, os_ref, ow_ref, wmg_ref, wuc_ref, wun_ref, wo_ref, g1_ref, b1_ref,
                  wr_ref, br_ref, tri_ref, x1_ref, ri_ref, rg_ref, cnt_ref, xs_hbm,
                  carry_ref, stage, dest_v, dest_s, copy_sem, row_sem, *, alpha, cap):
    i = pl.program_id(0)
    last = pl.num_programs(0) - 1
    sub = tri_ref.shape[0]
    tm = x_ref.shape[0]
    slot = i % 2
    nsub = tm // sub

    def id_copy(si):
        return pltpu.make_async_copy(dest_v.at[si], dest_s.at[si], copy_sem.at[si])

    def issue_rows(si, s):
        id_copy(si).wait()
        for t in range(sub):
            for k in range(TOP_K):
                _tile_copy(stage.at[s], si * sub + t, xs_hbm, dest_s[si, k, t], row_sem.at[s]).start(priority=k % 2)

    def wait_rows(s, rows):
        for k in range(TOP_K):
            n = rows * ROW_SUB
            pltpu.make_async_copy(stage.at[s, pl.ds(0, n), :], xs_hbm.at[pl.ds(0, n), :], row_sem.at[s]).wait()

    @pl.when(i == 0)
    def _():
        carry_ref[...] = jnp.zeros_like(carry_ref)
        stage[1, pl.ds((nsub - 1) * sub * ROW_SUB, sub * ROW_SUB), :] = jnp.zeros((sub * ROW_SUB, LANES), F32)
        spare = (N_EXPERTS * cap + lax.broadcasted_iota(I32, (TOP_K, sub), 0) * sub
                 + lax.broadcasted_iota(I32, (TOP_K, sub), 1))
        dest_v[nsub - 1] = spare
        id_copy(nsub - 1).start()

    wr = wr_ref[...]
    wr_hi = wr.astype(BF16)
    wr_hl = jnp.concatenate([wr_hi, (wr - wr_hi.astype(F32)).astype(BF16)], axis=0)
    total = carry_ref[...]
    for si in range(nsub):
        s0 = si * sub
        rs = slice(s0, s0 + sub)
        x = x_ref[rs, :]
        mg = _dot(x.astype(BF16), wmg_ref[...])
        if si == 0:
            issue_rows(nsub - 1, 1 - slot)
        else:
            issue_rows(si - 1, slot)
        y_a = _dot(ya_ref[rs, :], wuc_ref[...])
        o_nsa_t = (oc_ref[0, :, rs].astype(F32) + os_ref[0, :, rs].astype(F32)
                   + ow_ref[0, :, rs].astype(F32))
        y_b = _dot(o_nsa_t.T.astype(BF16), wun_ref[...])
        merged = jax.nn.sigmoid(mg[:, :D_MODEL]) * y_a + jax.nn.sigmoid(mg[:, D_MODEL:]) * y_b
        h = alpha * x + _dot(merged.astype(BF16), wo_ref[...])
        x1 = _layer_norm(h, g1_ref[...], b1_ref[...])
        x1_ref[rs, :] = x1
        _store_row_tiles(stage.at[slot], x1, s0)

        x_hi = x1.astype(BF16)
        x_lo = (x1 - x_hi.astype(F32)).astype(BF16)
        both = _dot_nt(wr_hl, x_hi)
        logits = both[:N_EXPERTS] + both[N_EXPERTS:] + _dot_nt(wr_hi, x_lo) + br_ref[...]
        expert = lax.broadcasted_iota(I32, logits.shape, 0).astype(F32)
        rem = logits
        vals, idxs, hots = [], [], []
        for _ in range(TOP_K):
            m = jnp.max(rem, axis=0, keepdims=True)
            idx = jnp.min(jnp.where(rem == m, expert, float(N_EXPERTS)), axis=0, keepdims=True)
            hot = expert == idx
            vals.append(m)
            idxs.append(idx)
            hots.append(hot)
            rem = jnp.where(hot, -3.0e38, rem)
        es = [jnp.exp(v - vals[0]) for v in vals]
        den = es[0]
        for e in es[1:]:
            den = den + e
        chosen = hots[0]
        for hot in hots[1:]:
            chosen = chosen | hot
        chosen_f = jnp.where(chosen, 1.0, 0.0)

        before = _dot(chosen_f.astype(BF16), tri_ref[...]) + total
        total = total + jnp.sum(chosen_f, axis=1, keepdims=True)
        ranks = [jnp.sum(jnp.where(hot, before, 0.0), axis=0, keepdims=True) for hot in hots]
        ri_ref[:, rs] = jnp.concatenate(idxs + ranks, axis=0).astype(I32)
        rg_ref[:, rs] = jnp.concatenate([e / den for e in es] + [jnp.zeros_like(den)] * TOP_K, axis=0)

        dest_v[si] = jnp.concatenate(idxs, axis=0).astype(I32) * cap + jnp.concatenate(ranks, axis=0).astype(I32)
        id_copy(si).start()

    carry_ref[...] = total
    cnt_ref[...] = jnp.broadcast_to(total, cnt_ref.shape).astype(I32)

    @pl.when(i == 0)
    def _():
        wait_rows(1, sub)

    @pl.when(i > 0)
    def _():
        wait_rows(1 - slot, tm)

    @pl.when(i == last)
    def _():
        issue_rows(nsub - 1, slot)
        wait_rows(slot, tm)


def _merge(x, ya, oc, os_, ow, wmg, wuc, wun, wo, g1, b1, wr, br, tri, alpha):
    N, D = x.shape
    B, _, T = oc.shape
    tm = min(MERGE_TM, T)
    nt = T // tm
    tok = lambda w: pl.BlockSpec((tm, w), lambda i: (i, 0))
    feat = pl.BlockSpec((1, NSA_WIDTH, tm), lambda i: (i // nt, 0, i % nt))
    full = lambda a: pl.BlockSpec(a.shape, lambda i: (0,) * a.ndim)
    sub = tri.shape[0]
    return pl.pallas_call(
        functools.partial(_merge_kernel, alpha=alpha, cap=N),
        grid=(N // tm,),
        in_specs=[tok(D), tok(CONV_CH), feat, feat, feat,
                  full(wmg), full(wuc), full(wun), full(wo), full(g1), full(b1), full(wr), full(br), full(tri)],
        out_specs=[tok(D),
                   pl.BlockSpec((2 * TOP_K, tm), lambda i: (0, i)), pl.BlockSpec((2 * TOP_K, tm), lambda i: (0, i)),
                   pl.BlockSpec((N_EXPERTS, LANES), lambda i: (0, 0)), pl.BlockSpec(memory_space=pl.ANY)],
        out_shape=[jax.ShapeDtypeStruct((N, D), F32),
                   jax.ShapeDtypeStruct((2 * TOP_K, N), I32),
                   jax.ShapeDtypeStruct((2 * TOP_K, N), F32),
                   jax.ShapeDtypeStruct((N_EXPERTS, LANES), I32),
                   jax.ShapeDtypeStruct(((N_EXPERTS * N + TOP_K * sub) * ROW_SUB, LANES), F32)],
        scratch_shapes=[pltpu.VMEM((N_EXPERTS, 1), F32), pltpu.VMEM((2, tm * ROW_SUB, LANES), F32),
                        pltpu.VMEM((tm // sub, TOP_K, sub), I32), pltpu.SMEM((tm // sub, TOP_K, sub), I32),
                        pltpu.SemaphoreType.DMA((tm // sub,)), pltpu.SemaphoreType.DMA((2,))],
        compiler_params=pltpu.CompilerParams(dimension_semantics=("arbitrary",)),
        name="merge",
    )(x, ya, oc, os_, ow, wmg, wuc, wun, wo, g1, b1, wr, br, tri)


def _experts_kernel(be_ref, nu_ref, rb_ref, vr_ref, xs_ref, wgu_ref, perm_ref, bg_ref, bl_ref, wd_ref, bd_ref, ys_ref, wg_s, wl_s, wd_s):
    i = pl.program_id(0)

    @pl.when((i == 0) | (be_ref[i] != be_ref[jnp.maximum(i, 1) - 1]))
    def _():
        w = perm_ref.shape[0]
        for c in range(2 * D_FF // w):
            t = _dot(wgu_ref[0, :, c * w:(c + 1) * w].astype(BF16), perm_ref[...])
            wg_s[:, c * (w // 2):(c + 1) * (w // 2)] = t[:, :w // 2].astype(BF16)
            wl_s[:, c * (w // 2):(c + 1) * (w // 2)] = t[:, w // 2:].astype(BF16)
        wd_s[...] = wd_ref[0].astype(BF16)

    @pl.when(i < nu_ref[0])
    def _():
        row = lax.broadcasted_iota(I32, (MOE_BLK, 1), 0)
        xb = jnp.where(row < vr_ref[i], _load_row_tiles(xs_ref), 0.0).astype(BF16)
        x_glu = jnp.minimum(_dot(xb, wg_s[...]) + bg_ref[0], SWIGLU_LIMIT)
        x_lin = jnp.clip(_dot(xb, wl_s[...]) + bl_ref[0], -SWIGLU_LIMIT, SWIGLU_LIMIT)
        act = x_glu * jax.nn.sigmoid(SWIGLU_ALPHA * x_glu) * (x_lin + 1.0)
        _store_row_tiles(ys_ref, _dot(act.astype(BF16), wd_s[...]) + bd_ref[0])


def _experts(block_expert, n_used, row_block, valid_rows, xs, wgu, perm, bg, bl, wd, bd):
    D = D_MODEL
    n_blocks = block_expert.shape[0]
    rows = lambda i, be, nu, rb, vr: (rb[i], 0)
    wsel = lambda i, be, nu, rb, vr: (be[i], 0, 0)
    return pl.pallas_call(
        _experts_kernel,
        grid_spec=pltpu.PrefetchScalarGridSpec(
            num_scalar_prefetch=4,
            grid=(n_blocks,),
            in_specs=[pl.BlockSpec((MOE_BLK * ROW_SUB, LANES), rows),
                      pl.BlockSpec((1, D, 2 * D_FF), wsel),
                      pl.BlockSpec(perm.shape, lambda i, be, nu, rb, vr: (0, 0)),
                      pl.BlockSpec((1, 1, D_FF), wsel), pl.BlockSpec((1, 1, D_FF), wsel),
                      pl.BlockSpec((1, D_FF, D), wsel), pl.BlockSpec((1, 1, D), wsel)],
            out_specs=pl.BlockSpec((MOE_BLK * ROW_SUB, LANES), rows),
            scratch_shapes=[pltpu.VMEM((D, D_FF), BF16), pltpu.VMEM((D, D_FF), BF16), pltpu.VMEM((D_FF, D), BF16)]),
        out_shape=jax.ShapeDtypeStruct(xs.shape, F32),
        compiler_params=pltpu.CompilerParams(dimension_semantics=("arbitrary",)),
        name="experts",
    )(block_expert, n_used, row_block, valid_rows, xs, wgu, perm, bg, bl, wd, bd)


def _combine_kernel(dest_ref, dest_next_ref, ys_hbm, x1_ref, rg_ref, g2_ref, b2_ref, o_ref, buf, sem, *, alpha):
    i = pl.program_id(0)
    last = pl.num_programs(0) - 1
    tm = o_ref.shape[0]
    slot = i % 2
    other = 1 - slot

    def gather(ids_ref, t, to_slot):
        for k in range(TOP_K):
            _tile_copy(ys_hbm, ids_ref[t * TOP_K + k], buf.at[to_slot, k], t, sem.at[to_slot]).start(priority=k % 2)

    def wait_slot(s):
        for k in range(TOP_K):
            pltpu.make_async_copy(ys_hbm.at[pl.ds(0, tm * ROW_SUB), :], buf.at[s, k], sem.at[s]).wait()

    @pl.when(i == 0)
    def _():
        def first(t, c):
            gather(dest_ref, t, slot)
            return c
        lax.fori_loop(0, tm, first, 0)

    wait_slot(slot)

    gs = min(COMB_GROUP, tm)

    def group(g, c):
        r0 = pl.multiple_of(g * gs, gs)
        for j in range(gs):
            gather(dest_next_ref, r0 + j, other)
        rows = pl.ds(r0, gs)
        gate = rg_ref[rows, :]
        y = alpha * x1_ref[rows, :]
        for k in range(TOP_K):
            tiles = buf.at[slot, k, pl.ds(pl.multiple_of(r0 * ROW_SUB, gs * ROW_SUB), gs * ROW_SUB), :]
            y = y + gate[:, k:k + 1] * _load_row_tiles(tiles)
        o_ref[rows, :] = _layer_norm(y, g2_ref[...], b2_ref[...])
        return c

    lax.fori_loop(0, tm // gs, group, 0)

    @pl.when(i == last)
    def _():
        wait_slot(other)


def _combine(dest, ys, x1, rg, g2, b2, alpha):
    N, D = x1.shape
    tm = min(COMB_TM, N)
    n = N // tm
    tok = lambda w: pl.BlockSpec((tm, w), lambda i: (i, 0))
    full = lambda a: pl.BlockSpec(a.shape, lambda i: (0,) * a.ndim)
    ids = lambda f: pl.BlockSpec((tm * TOP_K,), f, memory_space=pltpu.SMEM)
    return pl.pallas_call(
        functools.partial(_combine_kernel, alpha=alpha),
        grid=(n,),
        in_specs=[ids(lambda i: (i,)), ids(lambda i: (jnp.minimum(i + 1, n - 1),)),
                  pl.BlockSpec(memory_space=pl.ANY), tok(D), tok(rg.shape[1]), full(g2), full(b2)],
        out_specs=tok(D),
        out_shape=jax.ShapeDtypeStruct((N, D), F32),
        scratch_shapes=[pltpu.VMEM((2, TOP_K, tm * ROW_SUB, LANES), F32), pltpu.SemaphoreType.DMA((2,))],
        compiler_params=pltpu.CompilerParams(dimension_semantics=("arbitrary",)),
        name="combine",
    )(dest, dest, ys, x1, rg, g2, b2)


def _rope_freq():
    half = ROPE_DIM // 2
    inv = (np.float32(ROPE_THETA) ** (-np.arange(half, dtype=np.float32) * np.float32(2.0 / ROPE_DIM))).astype(np.float32)
    return inv[:, None]


def _overlap_t(T):
    nc = T // CMP_STRIDE
    c0 = np.arange(nc) * CMP_STRIDE
    j0 = np.arange(SEL_LANES) * SEL_BLOCK
    ov = (c0[None, :] < j0[:, None] + SEL_BLOCK) & (c0[None, :] + CMP_BLOCK > j0[:, None])
    ov &= (np.arange(nc) < nc - 1)[None, :] & (np.arange(SEL_LANES) < T // SEL_BLOCK)[:, None]
    return ov.astype(np.float32)


def _deinterleave_perm():
    w = 2 * LANES
    p = np.zeros((w, w), np.float32)
    p[np.arange(0, w, 2), np.arange(w // 2)] = 1.0
    p[np.arange(1, w, 2), w // 2 + np.arange(w // 2)] = 1.0
    return p


def _token_major(t, B, T):
    return t.reshape(B, N_KV_GROUPS, HEAD_DIM, T).transpose(0, 1, 3, 2)


def _layer(x, positions, w_in, conv_w, cmp_pos_k, cmp_w1_k, cmp_w2_k, cmp_pos_v, cmp_w1_v, cmp_w2_v,
           w_up_conv, w_up_nsa, w_o, ln1_g, ln1_b, w_router, b_router, w_gate_up, b_gate_up,
           w_down, b_down, ln2_g, ln2_b, alpha):
    B, T, D = x.shape
    G, R = N_KV_GROUPS, HEADS_PER_GROUP
    N = B * T
    assert D == D_MODEL and T % SEL_BLOCK == 0 and T // SEL_BLOCK <= SEL_LANES

    c0 = 3 * CONV_CH
    c1 = c0 + NSA_WIDTH
    c2 = c1 + 6 * KV_WIDTH
    c3 = c2 + 3 * N_HEADS
    wc = w_in[:, :c0].astype(BF16)
    wqt = w_in[:, c0:c1].T.astype(BF16)
    wkvt = w_in[:, c1:c2].T.astype(BF16)
    gcols = np.zeros((G * GATE_ROWS,), np.int64)
    gmask = np.zeros((G * GATE_ROWS,), np.float32)
    for g in range(G):
        for br in range(3):
            for r in range(R):
                gcols[g * GATE_ROWS + br * R + r] = br * N_HEADS + g * R + r
                gmask[g * GATE_ROWS + br * R + r] = 1.0
    wgt = (w_in[:, c2:c3][:, gcols] * gmask).T.astype(BF16)
    wmg = w_in[:, c3:].astype(BF16)
    cw = conv_w.reshape(CONV_K, CONV_CH)
    pos_row = positions.astype(F32)[:, None, :]

    ya_pre, qt, kvt, gates = _proj(x, pos_row, wc, wqt, wkvt, wgt, cw, jnp.asarray(_rope_freq()))

    nc = T // CMP_STRIDE
    to_chunks = lambda t: _token_major(t, B, T).reshape(B * G, nc, CMP_STRIDE * HEAD_DIM)
    xin = jnp.stack([to_chunks(kvt[:, 0:KV_WIDTH]), to_chunks(kvt[:, KV_WIDTH:2 * KV_WIDTH])])
    w1 = jnp.stack([cmp_w1_k, cmp_w1_v])
    w2 = jnp.stack([cmp_w2_k, cmp_w2_v])
    pos_flat = jnp.stack([cmp_pos_k.reshape(1, -1), cmp_pos_v.reshape(1, -1)])
    pos_flat = jnp.pad(pos_flat, ((0, 0), (0, 7), (0, 0)))
    kcmp, kcmp_t = _compress(xin, w1, w2, w2.transpose(0, 2, 1), pos_flat)

    o_c, sel = _cmp_attn(qt, kcmp, kcmp_t, gates, jnp.asarray(_overlap_t(T)))

    ks = _token_major(kvt[:, 2 * KV_WIDTH:3 * KV_WIDTH], B, T)
    kw = _token_major(kvt[:, 4 * KV_WIDTH:5 * KV_WIDTH], B, T)
    onehot = (np.arange(T)[:, None] // SEL_BLOCK == np.arange(SEL_LANES)[None, :]).astype(np.float32)
    kaug = jnp.concatenate([ks, jnp.broadcast_to(jnp.asarray(onehot, BF16), (B, G, T, SEL_LANES))], axis=-1)
    o_s = _sel_attn(qt, sel, kaug, kvt, 3 * G, gates)
    o_w = _win_attn(qt, kw, kvt, 5 * G, gates)

    sub = min(MERGE_SUB, T)
    tri = jnp.asarray(np.triu(np.ones((sub, sub), np.float32), 1), BF16)
    assert N % MOE_BLK == 0
    x1, ri, rg, cnt, xs = _merge(x.reshape(N, D), ya_pre.reshape(N, CONV_CH), o_c, o_s, o_w, wmg,
                             w_up_conv.astype(BF16), w_up_nsa.astype(BF16), w_o.astype(BF16),
                             ln1_g[None, :], ln1_b[None, :], w_router.T, b_router[:, None], tri, alpha)

    counts = cnt[:, 0]
    blocks = (counts + MOE_BLK - 1) // MOE_BLK
    bend = jnp.cumsum(blocks).astype(I32)
    bstart = bend - blocks
    n_blocks = (N * TOP_K) // MOE_BLK + N_EXPERTS
    n_used = bend[-1:]
    b = jnp.minimum(jnp.arange(n_blocks, dtype=I32), n_used - 1)
    block_expert = jnp.minimum(jnp.sum((b[:, None] >= bend[None, :]).astype(I32), axis=1), N_EXPERTS - 1)
    first = jnp.sum(jnp.where(block_expert[:, None] == jnp.arange(N_EXPERTS)[None, :], bstart[None, :], 0), axis=1)
    row_block = block_expert * (N // MOE_BLK) + (b - first)
    count = jnp.sum(jnp.where(block_expert[:, None] == jnp.arange(N_EXPERTS)[None, :], counts[None, :], 0), axis=1)
    valid_rows = jnp.clip(count - (b - first) * MOE_BLK, 0, MOE_BLK).astype(I32)
    dest = (ri[:TOP_K] * N + ri[TOP_K:]).T.reshape(N * TOP_K)

    ys = _experts(block_expert, n_used, row_block, valid_rows, xs, w_gate_up, jnp.asarray(_deinterleave_perm(), BF16),
                  b_gate_up[:, None, 0::2], b_gate_up[:, None, 1::2], w_down, b_down[:, None, :])
    out = _combine(dest, ys, x1, rg.T, ln2_g[None, :], ln2_b[None, :], alpha)
    return out.reshape(B, T, D)


def kernel(x, positions, w_in, conv_w, cmp_pos_k, cmp_w1_k, cmp_w2_k, cmp_pos_v, cmp_w1_v, cmp_w2_v, w_up_conv, w_up_nsa, w_o, ln1_g, ln1_b, w_router, b_router, w_gate_up, b_gate_up, w_down, b_down, ln2_g, ln2_b):
    depth = w_in.shape[0]
    alpha = float((2 * depth) ** 0.25)
    h = x
    for l in range(depth):
        h = _layer(h, positions, w_in[l], conv_w[l], cmp_pos_k[l], cmp_w1_k[l], cmp_w2_k[l],
                   cmp_pos_v[l], cmp_w1_v[l], cmp_w2_v[l], w_up_conv[l], w_up_nsa[l], w_o[l],
                   ln1_g[l], ln1_b[l], w_router[l], b_router[l], w_gate_up[l], b_gate_up[l],
                   w_down[l], b_down[l], ln2_g[l], ln2_b[l], alpha)
    return h
```

```python
import functools

import numpy as np
import jax
import jax.numpy as jnp
from jax import lax
from jax.experimental import pallas as pl
from jax.experimental.pallas import tpu as pltpu

F32 = jnp.float32
BF16 = jnp.bfloat16
I32 = jnp.int32

D_MODEL = 1024
CONV_CH = 512
CONV_K = 3
N_HEADS = 8
N_KV_GROUPS = 2
HEADS_PER_GROUP = N_HEADS // N_KV_GROUPS
HEAD_DIM = 64
NSA_WIDTH = N_HEADS * HEAD_DIM
KV_WIDTH = N_KV_GROUPS * HEAD_DIM
ROPE_DIM = HEAD_DIM // 4
ROPE_THETA = 500000.0
CMP_BLOCK = 32
CMP_STRIDE = 16
CMP_HIDDEN = 256
SEL_BLOCK = 64
N_SELECT = 16
WINDOW = 512
N_EXPERTS = 32
TOP_K = 4
D_FF = 1024
SWIGLU_LIMIT = 7.0
SWIGLU_ALPHA = 1.702
LN_EPS = 1e-5
NEG_INF = -1e30
FORCE_SCORE = 1e4
LOG2E = 1.4426950408889634

LANES = 128
SUBLANES = 8
ROW_SUB = D_MODEL // LANES
assert ROW_SUB == SUBLANES
TAKEN = float("-inf")
SEL_LANES = 64
SEL_MASK_BIAS = -32768.0
GROUP_W = HEADS_PER_GROUP * HEAD_DIM
GATE_ROWS = 16

PROJ_TM = 1024
CMP_TQ = 512
SEL_TQ = 512
SEL_KC = 512
SEL_SPLIT = 2
WIN_TQ = 512
WIN_SUB = 128
MERGE_TM = 512
MERGE_SUB = 256
MOE_BLK = 512
COMB_TM = 256
COMB_GROUP = 128


def _dot(a, b, precision=None):
    return jnp.dot(a, b, precision=precision, preferred_element_type=F32)


def _dot_nt(a, b, precision=None):
    return lax.dot_general(a, b, (((1,), (1,)), ((), ())), precision=precision, preferred_element_type=F32)


def _proj_kernel(x_ref, pos_ref, wc_ref, wqt_ref, wkvt_ref, wgt_ref, cw_ref, freq_ref,
                 ya_ref, qt_ref, kvt_ref, gt_ref, carry_ref):
    ti = pl.program_id(1)
    tm = x_ref.shape[1]
    xb = x_ref[0].astype(BF16)

    pc = _dot(xb, wc_ref[...])
    xv = pc[:, :CONV_CH]
    bg = pc[:, CONV_CH:2 * CONV_CH]
    cg = pc[:, 2 * CONV_CH:]
    u = cg * xv

    @pl.when(ti == 0)
    def _():
        carry_ref[...] = jnp.zeros_like(carry_ref)

    prev = carry_ref[...]
    row = lax.broadcasted_iota(I32, u.shape, 0)
    u1 = jnp.where(row == 0, prev[7:8], pltpu.roll(u, 1, 0))
    u2 = jnp.where(row == 0, prev[6:7], jnp.where(row == 1, prev[7:8], pltpu.roll(u, 2, 0)))
    cw = cw_ref[...]
    conv = cw[2:3] * u + cw[1:2] * u1 + cw[0:1] * u2
    carry_ref[...] = u[tm - 8:]
    ya_ref[0] = (bg * conv).astype(BF16)

    ang = freq_ref[...] * pos_ref[0]
    cos = jnp.cos(ang)
    sin = jnp.sin(ang)
    half = ROPE_DIM // 2

    def rope_head(t):
        t1 = t[:half]
        t2 = t[half:ROPE_DIM]
        return [t1 * cos - t2 * sin, t2 * cos + t1 * sin, t[ROPE_DIM:]]

    def heads(t, rotate):
        out = []
        for h in range(t.shape[0] // HEAD_DIM):
            th = t[h * HEAD_DIM:(h + 1) * HEAD_DIM]
            out.extend(rope_head(th) if rotate(h) else [th])
        return jnp.concatenate(out, axis=0)

    qt = heads(_dot_nt(wqt_ref[...], xb), lambda h: True)
    qt_ref[0] = (qt * (HEAD_DIM ** -0.5 * LOG2E)).astype(BF16)
    kvt = heads(_dot_nt(wkvt_ref[...], xb), lambda h: (h // N_KV_GROUPS) % 2 == 0)
    kvt_ref[0] = kvt.astype(BF16)
    gt_ref[0] = jax.nn.sigmoid(_dot_nt(wgt_ref[...], xb))


def _proj(x, pos_row, wc, wqt, wkvt, wgt, cw, freq):
    B, T, D = x.shape
    tm = min(PROJ_TM, T)
    grid = (B, T // tm)
    full = lambda a: pl.BlockSpec(a.shape, lambda b, t: (0,) * a.ndim)
    tok = lambda w: pl.BlockSpec((1, tm, w), lambda b, t: (b, t, 0))
    feat = lambda r: pl.BlockSpec((1, r, tm), lambda b, t: (b, 0, t))
    n_gate = wgt.shape[0]
    return pl.pallas_call(
        _proj_kernel,
        grid=grid,
        in_specs=[tok(D), feat(1), full(wc), full(wqt), full(wkvt), full(wgt), full(cw), full(freq)],
        out_specs=[tok(CONV_CH), feat(NSA_WIDTH), feat(6 * KV_WIDTH), feat(n_gate)],
        out_shape=[jax.ShapeDtypeStruct((B, T, CONV_CH), BF16),
                   jax.ShapeDtypeStruct((B, NSA_WIDTH, T), BF16),
                   jax.ShapeDtypeStruct((B, 6 * KV_WIDTH, T), BF16),
                   jax.ShapeDtypeStruct((B, n_gate, T), F32)],
        scratch_shapes=[pltpu.VMEM((8, CONV_CH), F32)],
        compiler_params=pltpu.CompilerParams(dimension_semantics=("arbitrary", "arbitrary")),
        name="proj",
    )(x, pos_row, wc, wqt, wkvt, wgt, cw, freq)


def _compress_kernel(x_ref, w1_ref, w2_ref, w2t_ref, pos_ref, o_ref, ot_ref):
    xb = x_ref[0, 0]
    nc = xb.shape[0]
    w1 = w1_ref[0]
    w1b = w1.astype(BF16)
    half = CMP_STRIDE * HEAD_DIM
    a = _dot(xb, w1b[:half])
    b = _dot(xb, w1b[half:])
    b_next = pltpu.roll(b, nc - 1, 0)
    pb = _dot(pos_ref[0], w1, precision=lax.Precision.HIGHEST)[0:1]
    h = a + b_next + pb
    g = (0.5 * h * (1.0 + jnp.tanh(np.sqrt(2.0 / np.pi) * (h + 0.044715 * (h * h * h))))).astype(BF16)
    o_ref[0, 0] = _dot(g, w2_ref[0].astype(BF16)).astype(BF16)
    ot_ref[0, 0] = _dot_nt(w2t_ref[0].astype(BF16), g).astype(BF16)


def _compress(xin, w1, w2, w2t, pos):
    _, BG, nc, W = xin.shape
    per = lambda a: pl.BlockSpec((1,) + a.shape[1:], lambda s, i: (s, 0, 0))
    return pl.pallas_call(
        _compress_kernel,
        grid=(2, BG),
        in_specs=[pl.BlockSpec((1, 1, nc, W), lambda s, i: (s, i, 0, 0)), per(w1), per(w2), per(w2t), per(pos)],
        out_specs=[pl.BlockSpec((1, 1, nc, HEAD_DIM), lambda s, i: (s, i, 0, 0)),
                   pl.BlockSpec((1, 1, HEAD_DIM, nc), lambda s, i: (s, i, 0, 0))],
        out_shape=[jax.ShapeDtypeStruct((2, BG, nc, HEAD_DIM), BF16),
                   jax.ShapeDtypeStruct((2, BG, HEAD_DIM, nc), BF16)],
        name="compress",
    )(xin, w1, w2, w2t, pos)


def _head_lanes(qt):
    return jnp.concatenate([qt[r * HEAD_DIM:(r + 1) * HEAD_DIM] for r in range(HEADS_PER_GROUP)], axis=1)


def _gated_out(ot, scale, gates, branch, tq):
    rows = []
    for r in range(HEADS_PER_GROUP):
        c = branch * HEADS_PER_GROUP + r
        sl = slice(r * tq, (r + 1) * tq)
        rows.append(ot[:, sl] * (scale[:, sl] * gates[c:c + 1]))
    return jnp.concatenate(rows, axis=0).astype(BF16)


def _attn_specs(tq):
    G = N_KV_GROUPS
    qspec = pl.BlockSpec((1, GROUP_W, tq), lambda b, g, i: (b, g, i))
    gspec = pl.BlockSpec((1, GATE_ROWS, tq), lambda b, g, i: (b, g, i))
    bg4 = lambda a: pl.BlockSpec((1, 1) + a.shape[2:], lambda b, g, i: (b, g, 0, 0))
    return G, qspec, gspec, bg4


def _cmp_attn_kernel(q_ref, kc_ref, vct_ref, g_ref, ovt_ref, o_ref, sel_ref):
    qi = pl.program_id(2)
    tq = q_ref.shape[2]
    q0 = qi * tq
    qt = _head_lanes(q_ref[0])
    s = _dot(kc_ref[0, 0], qt)
    c = lax.broadcasted_iota(I32, s.shape, 0)
    t = q0 + (lax.broadcasted_iota(I32, (1, s.shape[1]), 1) & (tq - 1))
    valid = c <= (t - (CMP_BLOCK - 1)) // CMP_STRIDE
    sm = jnp.where(valid, s, NEG_INF)
    m = jnp.max(sm, axis=0, keepdims=True)
    p = jnp.where(valid, jnp.exp2(sm - m), 0.0)
    l = jnp.sum(p, axis=0, keepdims=True)
    inv = 1.0 / jnp.where(l > 0.0, l, 1.0)
    ot = _dot(vct_ref[0, 0], p.astype(BF16))
    o_ref[0] = _gated_out(ot, inv, g_ref[0], 0, tq)

    pn = p * inv
    psum = pn[:, 0:tq]
    for r in range(1, HEADS_PER_GROUP):
        psum = psum + pn[:, r * tq:(r + 1) * tq]
    imp = _dot(ovt_ref[...], psum, precision=lax.Precision.HIGHEST)
    j = lax.broadcasted_iota(I32, imp.shape, 0)
    cur = (q0 + lax.broadcasted_iota(I32, imp.shape, 1)) // SEL_BLOCK
    valid_b = j <= cur
    forced = (j == 0) | (j == cur) | (j == cur - 1)
    score = jnp.where(valid_b, jnp.where(forced, FORCE_SCORE, imp), NEG_INF)
    sub = SUBLANES
    groups = [score[a:a + sub] for a in range(0, SEL_LANES, sub)]
    jrow = lax.broadcasted_iota(I32, groups[0].shape, 0)
    ranks = [jnp.zeros(g_.shape, I32) for g_ in groups]
    for i in range(SEL_LANES):
        si = score[i:i + 1, :]
        for a, g_ in enumerate(groups):
            if a > i // sub:
                inc = jnp.where(si >= g_, 1, 0)
            elif a < i // sub:
                inc = jnp.where(si > g_, 1, 0)
            else:
                inc = jnp.where(jrow > i % sub, jnp.where(si >= g_, 1, 0), jnp.where(si > g_, 1, 0))
            ranks[a] = ranks[a] + inc
    rank = jnp.concatenate(ranks, axis=0)
    selected = (rank < N_SELECT) & valid_b
    sel_ref[0, 0] = jnp.where(selected, 0.0, SEL_MASK_BIAS).astype(BF16)


def _cmp_attn(qt, kcmp, vcmp_t, gates, ovt):
    B, _, T = qt.shape
    tq = min(CMP_TQ, T)
    G, qspec, gspec, _ = _attn_specs(tq)
    cspec = lambda a: pl.BlockSpec((1, 1) + a.shape[2:], lambda b, g, i: (0, b * G + g, 0, 0))
    vspec = lambda a: pl.BlockSpec((1, 1) + a.shape[2:], lambda b, g, i: (1, b * G + g, 0, 0))
    return pl.pallas_call(
        _cmp_attn_kernel,
        grid=(B, G, T // tq),
        in_specs=[qspec, cspec(kcmp), vspec(vcmp_t), gspec, pl.BlockSpec(ovt.shape, lambda b, g, i: (0, 0))],
        out_specs=[qspec, pl.BlockSpec((1, 1, SEL_LANES, tq), lambda b, g, i: (b, g, 0, i))],
        out_shape=[jax.ShapeDtypeStruct((B, NSA_WIDTH, T), BF16),
                   jax.ShapeDtypeStruct((B, G, SEL_LANES, T), BF16)],
        name="cmp_attn",
    )(qt, kcmp, vcmp_t, gates, ovt)


def _sel_attn_kernel(q_ref, sel_ref, k_ref, vt_ref, g_ref, o_ref, *, kc):
    qi = pl.program_id(2)
    tq = q_ref.shape[2]
    q0 = qi * tq
    qt = q_ref[0]
    sb = sel_ref[0, 0]
    qa = jnp.concatenate(
        [jnp.concatenate([qt[r * HEAD_DIM:(r + 1) * HEAD_DIM], sb], axis=0) for r in range(HEADS_PER_GROUP)],
        axis=1)
    gw = qa.shape[1] // SEL_SPLIT
    qas = [qa[:, i * gw:(i + 1) * gw] for i in range(SEL_SPLIT)]

    def step(kstart, kn, carry, diagonal):
        kblk = k_ref[0, 0, pl.ds(kstart, kn), :]
        vblk = vt_ref[0, :, pl.ds(kstart, kn)]
        ss = [_dot(kblk, qg) for qg in qas]
        out = []
        for gi, (s, (m, l, acc)) in enumerate(zip(ss, carry)):
            if diagonal:
                row = lax.broadcasted_iota(I32, s.shape, 0)
                off = (gi * gw + lax.broadcasted_iota(I32, s.shape, 1)) & (tq - 1)
                s = jnp.where(row <= off, s, NEG_INF)
            mn = jnp.maximum(m, jnp.max(s, axis=0, keepdims=True))
            alpha = jnp.exp2(m - mn)
            p = jnp.exp2(s - mn)
            l = alpha * l + jnp.sum(p, axis=0, keepdims=True)
            acc = alpha * acc + _dot(vblk, p.astype(BF16))
            out.append((mn, l, acc))
        return tuple(out)

    init = tuple((jnp.full((1, gw), NEG_INF, F32), jnp.zeros((1, gw), F32), jnp.zeros((HEAD_DIM, gw), F32))
                 for _ in range(SEL_SPLIT))
    n_full = q0 // kc
    carry = lax.fori_loop(0, n_full, lambda i, cr: step(pl.multiple_of(i * kc, kc), kc, cr, False), init)
    carry = lax.fori_loop(n_full * (kc // tq), qi, lambda i, cr: step(pl.multiple_of(i * tq, tq), tq, cr, False), carry)
    carry = step(pl.multiple_of(q0, tq), tq, carry, True)
    l = jnp.concatenate([c[1] for c in carry], axis=1)
    acc = jnp.concatenate([c[2] for c in carry], axis=1)
    o_ref[0] = _gated_out(acc, 1.0 / l, g_ref[0], 1, tq)


def _sel_attn(qt, sel, kaug, kvt, v_row_block, gates):
    B, _, T = qt.shape
    tq = min(SEL_TQ, T)
    kc = min(SEL_KC, T)
    G, qspec, gspec, bg4 = _attn_specs(tq)
    return pl.pallas_call(
        functools.partial(_sel_attn_kernel, kc=kc),
        grid=(B, G, T // tq),
        in_specs=[qspec, pl.BlockSpec((1, 1, SEL_LANES, tq), lambda b, g, i: (b, g, 0, i)), bg4(kaug),
                  pl.BlockSpec((1, HEAD_DIM, T), lambda b, g, i: (b, v_row_block + g, 0)), gspec],
        out_specs=qspec,
        out_shape=jax.ShapeDtypeStruct((B, NSA_WIDTH, T), BF16),
        name="sel_attn",
    )(qt, sel, kaug, kvt, gates)


def _win_attn_kernel(q_ref, k_ref, vt_ref, g_ref, o_ref, *, span, sub):
    qi = pl.program_id(2)
    tq = q_ref.shape[2]
    T = k_ref.shape[2]

    def tile(s0, interior):
        cs = slice(s0, s0 + sub)
        q0 = qi * tq + s0
        start = pl.multiple_of(jnp.clip(q0 + sub - span, 0, T - span), sub)
        qt = _head_lanes(q_ref[0, :, cs])
        s = _dot(k_ref[0, 0, pl.ds(start, span), :], qt)
        if interior:
            row = lax.broadcasted_iota(I32, (sub, s.shape[1]), 0)
            off = lax.broadcasted_iota(I32, (sub, s.shape[1]), 1) & (sub - 1)
            s = jnp.concatenate([jnp.where(row > off, s[:sub], NEG_INF), s[sub:span - sub],
                                 jnp.where(row <= off, s[span - sub:], NEG_INF)], axis=0)
        else:
            kp = start + lax.broadcasted_iota(I32, s.shape, 0)
            t = q0 + (lax.broadcasted_iota(I32, s.shape, 1) & (sub - 1))
            diff = t - kp
            s = jnp.where((diff >= 0) & (diff < WINDOW), s, NEG_INF)
        m = jnp.max(s, axis=0, keepdims=True)
        p = jnp.exp2(s - m)
        l = jnp.sum(p, axis=0, keepdims=True)
        ot = _dot(vt_ref[0, :, pl.ds(start, span)], p.astype(BF16))
        o_ref[0, :, cs] = _gated_out(ot, 1.0 / l, g_ref[0, :, cs], 2, sub)

    first_interior = -(-WINDOW // tq)
    if span == WINDOW + sub:
        @pl.when(qi >= first_interior)
        def _():
            for s0 in range(0, tq, sub):
                tile(s0, True)

    @pl.when((qi < first_interior) | (span != WINDOW + sub))
    def _():
        for s0 in range(0, tq, sub):
            tile(s0, False)


def _win_attn(qt, k, kvt, v_row_block, gates):
    B, _, T = qt.shape
    tq = min(WIN_TQ, T)
    sub = min(WIN_SUB, T)
    span = min(WINDOW + sub, T)
    G, qspec, gspec, bg4 = _attn_specs(tq)
    return pl.pallas_call(
        functools.partial(_win_attn_kernel, span=span, sub=sub),
        grid=(B, G, T // tq),
        in_specs=[qspec, bg4(k), pl.BlockSpec((1, HEAD_DIM, T), lambda b, g, i: (b, v_row_block + g, 0)), gspec],
        out_specs=qspec,
        out_shape=jax.ShapeDtypeStruct((B, NSA_WIDTH, T), BF16),
        name="win_attn",
    )(qt, k, kvt, gates)


def _store_row_tiles(ref, x, row0=0):
    rows = x.shape[0]
    for s in range(ROW_SUB):
        ref[pl.ds(row0 * ROW_SUB + s, rows, stride=ROW_SUB), :] = x[:, s * LANES:(s + 1) * LANES]


def _load_row_tiles(ref):
    rows = ref.shape[0] // ROW_SUB
    return jnp.concatenate([ref[pl.ds(s, rows, stride=ROW_SUB), :] for s in range(ROW_SUB)], axis=1)


def _tile_copy(src, si, dst, di, sem):
    return pltpu.make_async_copy(src.at[pl.ds(pl.multiple_of(si * ROW_SUB, ROW_SUB), ROW_SUB), :],
                                 dst.at[pl.ds(pl.multiple_of(di * ROW_SUB, ROW_SUB), ROW_SUB), :], sem)


def _layer_norm(h, g, b):
    mu = jnp.mean(h, axis=-1, keepdims=True)
    c = h - mu
    var = jnp.mean(c * c, axis=-1, keepdims=True)
    return c * lax.rsqrt(var + LN_EPS) * g + b


def _merge_kernel(x_ref, ya_ref, oc_ref, os_ref, ow_ref, wmg_ref, wuc_ref, wun_ref, wo_ref, g1_ref, b1_ref,
                  wr_ref, br_ref, tri_ref, x1_ref, ri_ref, rg_ref, cnt_ref, xs_hbm,
                  carry_ref, stage, dest_v, dest_s, copy_sem, row_sem, *, alpha, cap):
    i = pl.program_id(0)
    last = pl.num_programs(0) - 1
    sub = tri_ref.shape[0]
    tm = x_ref.shape[0]
    slot = i % 2
    nsub = tm // sub

    def id_copy(si):
        return pltpu.make_async_copy(dest_v.at[si], dest_s.at[si], copy_sem.at[si])

    def issue_rows(si, s):
        id_copy(si).wait()
        for t in range(sub):
            for k in range(TOP_K):
                _tile_copy(stage.at[s], si * sub + t, xs_hbm, dest_s[si, k, t], row_sem.at[s]).start(priority=k % 2)

    def wait_rows(s, rows):
        for k in range(TOP_K):
            n = rows * ROW_SUB
            pltpu.make_async_copy(stage.at[s, pl.ds(0, n), :], xs_hbm.at[pl.ds(0, n), :], row_sem.at[s]).wait()

    @pl.when(i == 0)
    def _():
        carry_ref[...] = jnp.zeros_like(carry_ref)
        stage[1, pl.ds((nsub - 1) * sub * ROW_SUB, sub * ROW_SUB), :] = jnp.zeros((sub * ROW_SUB, LANES), F32)
        spare = (N_EXPERTS * cap + lax.broadcasted_iota(I32, (TOP_K, sub), 0) * sub
                 + lax.broadcasted_iota(I32, (TOP_K, sub), 1))
        dest_v[nsub - 1] = spare
        id_copy(nsub - 1).start()

    wr = wr_ref[...]
    wr_hi = wr.astype(BF16)
    wr_hl = jnp.concatenate([wr_hi, (wr - wr_hi.astype(F32)).astype(BF16)], axis=0)
    total = carry_ref[...]
    for si in range(nsub):
        s0 = si * sub
        rs = slice(s0, s0 + sub)
        x = x_ref[rs, :]
        mg = _dot(x.astype(BF16), wmg_ref[...])
        if si == 0:
            issue_rows(nsub - 1, 1 - slot)
        else:
            issue_rows(si - 1, slot)
        y_a = _dot(ya_ref[rs, :], wuc_ref[...])
        o_nsa_t = (oc_ref[0, :, rs].astype(F32) + os_ref[0, :, rs].astype(F32)
                   + ow_ref[0, :, rs].astype(F32))
        y_b = _dot(o_nsa_t.T.astype(BF16), wun_ref[...])
        merged = jax.nn.sigmoid(mg[:, :D_MODEL]) * y_a + jax.nn.sigmoid(mg[:, D_MODEL:]) * y_b
        h = alpha * x + _dot(merged.astype(BF16), wo_ref[...])
        x1 = _layer_norm(h, g1_ref[...], b1_ref[...])
        x1_ref[rs, :] = x1
        _store_row_tiles(stage.at[slot], x1, s0)

        x_hi = x1.astype(BF16)
        x_lo = (x1 - x_hi.astype(F32)).astype(BF16)
        both = _dot_nt(wr_hl, x_hi)
        logits = both[:N_EXPERTS] + both[N_EXPERTS:] + _dot_nt(wr_hi, x_lo) + br_ref[...]
        expert = lax.broadcasted_iota(I32, logits.shape, 0).astype(F32)
        rem = logits
        vals, idxs, hots = [], [], []
        for _ in range(TOP_K):
            m = jnp.max(rem, axis=0, keepdims=True)
            idx = jnp.min(jnp.where(rem == m, expert, float(N_EXPERTS)), axis=0, keepdims=True)
            hot = expert == idx
            vals.append(m)
            idxs.append(idx)
            hots.append(hot)
            rem = jnp.where(hot, TAKEN, rem)
        es = [jnp.exp(v - vals[0]) for v in vals]
        den = es[0]
        for e in es[1:]:
            den = den + e
        chosen = hots[0]
        for hot in hots[1:]:
            chosen = chosen | hot
        chosen_f = jnp.where(chosen, 1.0, 0.0)

        before = _dot(chosen_f.astype(BF16), tri_ref[...]) + total
        total = total + jnp.sum(chosen_f, axis=1, keepdims=True)
        ranks = [jnp.sum(jnp.where(hot, before, 0.0), axis=0, keepdims=True) for hot in hots]
        ri_ref[:, rs] = jnp.concatenate(idxs + ranks, axis=0).astype(I32)
        rg_ref[:, rs] = jnp.concatenate([e / den for e in es] + [jnp.zeros_like(den)] * TOP_K, axis=0)

        dest_v[si] = jnp.concatenate(idxs, axis=0).astype(I32) * cap + jnp.concatenate(ranks, axis=0).astype(I32)
        id_copy(si).start()

    carry_ref[...] = total
    cnt_ref[...] = jnp.broadcast_to(total, cnt_ref.shape).astype(I32)

    @pl.when(i == 0)
    def _():
        wait_rows(1, sub)

    @pl.when(i > 0)
    def _():
        wait_rows(1 - slot, tm)

    @pl.when(i == last)
    def _():
        issue_rows(nsub - 1, slot)
        wait_rows(slot, tm)


def _merge(x, ya, oc, os_, ow, wmg, wuc, wun, wo, g1, b1, wr, br, tri, alpha):
    N, D = x.shape
    B, _, T = oc.shape
    tm = min(MERGE_TM, T)
    nt = T // tm
    tok = lambda w: pl.BlockSpec((tm, w), lambda i: (i, 0))
    feat = pl.BlockSpec((1, NSA_WIDTH, tm), lambda i: (i // nt, 0, i % nt))
    full = lambda a: pl.BlockSpec(a.shape, lambda i: (0,) * a.ndim)
    sub = tri.shape[0]
    return pl.pallas_call(
        functools.partial(_merge_kernel, alpha=alpha, cap=N),
        grid=(N // tm,),
        in_specs=[tok(D), tok(CONV_CH), feat, feat, feat,
                  full(wmg), full(wuc), full(wun), full(wo), full(g1), full(b1), full(wr), full(br), full(tri)],
        out_specs=[tok(D),
                   pl.BlockSpec((2 * TOP_K, tm), lambda i: (0, i)), pl.BlockSpec((2 * TOP_K, tm), lambda i: (0, i)),
                   pl.BlockSpec((N_EXPERTS, LANES), lambda i: (0, 0)), pl.BlockSpec(memory_space=pl.ANY)],
        out_shape=[jax.ShapeDtypeStruct((N, D), F32),
                   jax.ShapeDtypeStruct((2 * TOP_K, N), I32),
                   jax.ShapeDtypeStruct((2 * TOP_K, N), F32),
                   jax.ShapeDtypeStruct((N_EXPERTS, LANES), I32),
                   jax.ShapeDtypeStruct(((N_EXPERTS * N + TOP_K * sub) * ROW_SUB, LANES), F32)],
        scratch_shapes=[pltpu.VMEM((N_EXPERTS, 1), F32), pltpu.VMEM((2, tm * ROW_SUB, LANES), F32),
                        pltpu.VMEM((tm // sub, TOP_K, sub), I32), pltpu.SMEM((tm // sub, TOP_K, sub), I32),
                        pltpu.SemaphoreType.DMA((tm // sub,)), pltpu.SemaphoreType.DMA((2,))],
        compiler_params=pltpu.CompilerParams(dimension_semantics=("arbitrary",)),
        name="merge",
    )(x, ya, oc, os_, ow, wmg, wuc, wun, wo, g1, b1, wr, br, tri)


def _experts_kernel(be_ref, nu_ref, rb_ref, vr_ref, xs_ref, wgu_ref, perm_ref, bg_ref, bl_ref, wd_ref, bd_ref, ys_ref, wg_s, wl_s, wd_s):
    i = pl.program_id(0)

    @pl.when((i == 0) | (be_ref[i] != be_ref[jnp.maximum(i, 1) - 1]))
    def _():
        w = perm_ref.shape[0]
        for c in range(2 * D_FF // w):
            t = _dot(wgu_ref[0, :, c * w:(c + 1) * w].astype(BF16), perm_ref[...])
            wg_s[:, c * (w // 2):(c + 1) * (w // 2)] = t[:, :w // 2].astype(BF16)
            wl_s[:, c * (w // 2):(c + 1) * (w // 2)] = t[:, w // 2:].astype(BF16)
        wd_s[...] = wd_ref[0].astype(BF16)

    @pl.when(i < nu_ref[0])
    def _():
        row = lax.broadcasted_iota(I32, (MOE_BLK, 1), 0)
        xb = jnp.where(row < vr_ref[i], _load_row_tiles(xs_ref), 0.0).astype(BF16)
        x_glu = jnp.minimum(_dot(xb, wg_s[...]) + bg_ref[0], SWIGLU_LIMIT)
        x_lin = jnp.clip(_dot(xb, wl_s[...]) + bl_ref[0], -SWIGLU_LIMIT, SWIGLU_LIMIT)
        act = x_glu * jax.nn.sigmoid(SWIGLU_ALPHA * x_glu) * (x_lin + 1.0)
        _store_row_tiles(ys_ref, _dot(act.astype(BF16), wd_s[...]) + bd_ref[0])


def _experts(block_expert, n_used, row_block, valid_rows, xs, wgu, perm, bg, bl, wd, bd):
    D = D_MODEL
    n_blocks = block_expert.shape[0]
    rows = lambda i, be, nu, rb, vr: (rb[i], 0)
    wsel = lambda i, be, nu, rb, vr: (be[i], 0, 0)
    return pl.pallas_call(
        _experts_kernel,
        grid_spec=pltpu.PrefetchScalarGridSpec(
            num_scalar_prefetch=4,
            grid=(n_blocks,),
            in_specs=[pl.BlockSpec((MOE_BLK * ROW_SUB, LANES), rows),
                      pl.BlockSpec((1, D, 2 * D_FF), wsel),
                      pl.BlockSpec(perm.shape, lambda i, be, nu, rb, vr: (0, 0)),
                      pl.BlockSpec((1, 1, D_FF), wsel), pl.BlockSpec((1, 1, D_FF), wsel),
                      pl.BlockSpec((1, D_FF, D), wsel), pl.BlockSpec((1, 1, D), wsel)],
            out_specs=pl.BlockSpec((MOE_BLK * ROW_SUB, LANES), rows),
            scratch_shapes=[pltpu.VMEM((D, D_FF), BF16), pltpu.VMEM((D, D_FF), BF16), pltpu.VMEM((D_FF, D), BF16)]),
        out_shape=jax.ShapeDtypeStruct(xs.shape, F32),
        compiler_params=pltpu.CompilerParams(dimension_semantics=("arbitrary",)),
        name="experts",
    )(block_expert, n_used, row_block, valid_rows, xs, wgu, perm, bg, bl, wd, bd)


def _combine_kernel(dest_ref, dest_next_ref, ys_hbm, x1_ref, rg_ref, g2_ref, b2_ref, o_ref, buf, sem, *, alpha):
    i = pl.program_id(0)
    last = pl.num_programs(0) - 1
    tm = o_ref.shape[0]
    slot = i % 2
    other = 1 - slot

    def gather(ids_ref, t, to_slot):
        for k in range(TOP_K):
            _tile_copy(ys_hbm, ids_ref[t * TOP_K + k], buf.at[to_slot, k], t, sem.at[to_slot]).start(priority=k % 2)

    def wait_slot(s):
        for k in range(TOP_K):
            pltpu.make_async_copy(ys_hbm.at[pl.ds(0, tm * ROW_SUB), :], buf.at[s, k], sem.at[s]).wait()

    @pl.when(i == 0)
    def _():
        def first(t, c):
            gather(dest_ref, t, slot)
            return c
        lax.fori_loop(0, tm, first, 0)

    wait_slot(slot)

    gs = min(COMB_GROUP, tm)

    def group(g, c):
        r0 = pl.multiple_of(g * gs, gs)
        for j in range(gs):
            gather(dest_next_ref, r0 + j, other)
        rows = pl.ds(r0, gs)
        gate = rg_ref[rows, :]
        y = alpha * x1_ref[rows, :]
        for k in range(TOP_K):
            tiles = buf.at[slot, k, pl.ds(pl.multiple_of(r0 * ROW_SUB, gs * ROW_SUB), gs * ROW_SUB), :]
            y = y + gate[:, k:k + 1] * _load_row_tiles(tiles)
        o_ref[rows, :] = _layer_norm(y, g2_ref[...], b2_ref[...])
        return c

    lax.fori_loop(0, tm // gs, group, 0)

    @pl.when(i == last)
    def _():
        wait_slot(other)


def _combine(dest, ys, x1, rg, g2, b2, alpha):
    N, D = x1.shape
    tm = min(COMB_TM, N)
    n = N // tm
    tok = lambda w: pl.BlockSpec((tm, w), lambda i: (i, 0))
    full = lambda a: pl.BlockSpec(a.shape, lambda i: (0,) * a.ndim)
    ids = lambda f: pl.BlockSpec((tm * TOP_K,), f, memory_space=pltpu.SMEM)
    return pl.pallas_call(
        functools.partial(_combine_kernel, alpha=alpha),
        grid=(n,),
        in_specs=[ids(lambda i: (i,)), ids(lambda i: (jnp.minimum(i + 1, n - 1),)),
                  pl.BlockSpec(memory_space=pl.ANY), tok(D), tok(rg.shape[1]), full(g2), full(b2)],
        out_specs=tok(D),
        out_shape=jax.ShapeDtypeStruct((N, D), F32),
        scratch_shapes=[pltpu.VMEM((2, TOP_K, tm * ROW_SUB, LANES), F32), pltpu.SemaphoreType.DMA((2,))],
        compiler_params=pltpu.CompilerParams(dimension_semantics=("arbitrary",)),
        name="combine",
    )(dest, dest, ys, x1, rg, g2, b2)


def _rope_freq():
    half = ROPE_DIM // 2
    inv = (np.float32(ROPE_THETA) ** (-np.arange(half, dtype=np.float32) * np.float32(2.0 / ROPE_DIM))).astype(np.float32)
    return inv[:, None]


def _overlap_t(T):
    nc = T // CMP_STRIDE
    c0 = np.arange(nc) * CMP_STRIDE
    j0 = np.arange(SEL_LANES) * SEL_BLOCK
    ov = (c0[None, :] < j0[:, None] + SEL_BLOCK) & (c0[None, :] + CMP_BLOCK > j0[:, None])
    ov &= (np.arange(nc) < nc - 1)[None, :] & (np.arange(SEL_LANES) < T // SEL_BLOCK)[:, None]
    return ov.astype(np.float32)


def _deinterleave_perm():
    w = 2 * LANES
    p = np.zeros((w, w), np.float32)
    p[np.arange(0, w, 2), np.arange(w // 2)] = 1.0
    p[np.arange(1, w, 2), w // 2 + np.arange(w // 2)] = 1.0
    return p


def _token_major(t, B, T):
    return t.reshape(B, N_KV_GROUPS, HEAD_DIM, T).transpose(0, 1, 3, 2)


def _layer(x, positions, w_in, conv_w, cmp_pos_k, cmp_w1_k, cmp_w2_k, cmp_pos_v, cmp_w1_v, cmp_w2_v,
           w_up_conv, w_up_nsa, w_o, ln1_g, ln1_b, w_router, b_router, w_gate_up, b_gate_up,
           w_down, b_down, ln2_g, ln2_b, alpha):
    B, T, D = x.shape
    G, R = N_KV_GROUPS, HEADS_PER_GROUP
    N = B * T
    assert D == D_MODEL and T % SEL_BLOCK == 0 and T // SEL_BLOCK <= SEL_LANES

    c0 = 3 * CONV_CH
    c1 = c0 + NSA_WIDTH
    c2 = c1 + 6 * KV_WIDTH
    c3 = c2 + 3 * N_HEADS
    wc = w_in[:, :c0].astype(BF16)
    wqt = w_in[:, c0:c1].T.astype(BF16)
    wkvt = w_in[:, c1:c2].T.astype(BF16)
    gcols = np.zeros((G * GATE_ROWS,), np.int64)
    gmask = np.zeros((G * GATE_ROWS,), np.float32)
    for g in range(G):
        for br in range(3):
            for r in range(R):
                gcols[g * GATE_ROWS + br * R + r] = br * N_HEADS + g * R + r
                gmask[g * GATE_ROWS + br * R + r] = 1.0
    wgt = (w_in[:, c2:c3][:, gcols] * gmask).T.astype(BF16)
    wmg = w_in[:, c3:].astype(BF16)
    cw = conv_w.reshape(CONV_K, CONV_CH)
    pos_row = positions.astype(F32)[:, None, :]

    ya_pre, qt, kvt, gates = _proj(x, pos_row, wc, wqt, wkvt, wgt, cw, jnp.asarray(_rope_freq()))

    nc = T // CMP_STRIDE
    to_chunks = lambda t: _token_major(t, B, T).reshape(B * G, nc, CMP_STRIDE * HEAD_DIM)
    xin = jnp.stack([to_chunks(kvt[:, 0:KV_WIDTH]), to_chunks(kvt[:, KV_WIDTH:2 * KV_WIDTH])])
    w1 = jnp.stack([cmp_w1_k, cmp_w1_v])
    w2 = jnp.stack([cmp_w2_k, cmp_w2_v])
    pos_flat = jnp.stack([cmp_pos_k.reshape(1, -1), cmp_pos_v.reshape(1, -1)])
    pos_flat = jnp.pad(pos_flat, ((0, 0), (0, 7), (0, 0)))
    kcmp, kcmp_t = _compress(xin, w1, w2, w2.transpose(0, 2, 1), pos_flat)

    o_c, sel = _cmp_attn(qt, kcmp, kcmp_t, gates, jnp.asarray(_overlap_t(T)))

    ks = _token_major(kvt[:, 2 * KV_WIDTH:3 * KV_WIDTH], B, T)
    kw = _token_major(kvt[:, 4 * KV_WIDTH:5 * KV_WIDTH], B, T)
    onehot = (np.arange(T)[:, None] // SEL_BLOCK == np.arange(SEL_LANES)[None, :]).astype(np.float32)
    kaug = jnp.concatenate([ks, jnp.broadcast_to(jnp.asarray(onehot, BF16), (B, G, T, SEL_LANES))], axis=-1)
    o_s = _sel_attn(qt, sel, kaug, kvt, 3 * G, gates)
    o_w = _win_attn(qt, kw, kvt, 5 * G, gates)

    sub = min(MERGE_SUB, T)
    tri = jnp.asarray(np.triu(np.ones((sub, sub), np.float32), 1), BF16)
    assert N % MOE_BLK == 0
    x1, ri, rg, cnt, xs = _merge(x.reshape(N, D), ya_pre.reshape(N, CONV_CH), o_c, o_s, o_w, wmg,
                             w_up_conv.astype(BF16), w_up_nsa.astype(BF16), w_o.astype(BF16),
                             ln1_g[None, :], ln1_b[None, :], w_router.T, b_router[:, None], tri, alpha)

    counts = cnt[:, 0]
    blocks = (counts + MOE_BLK - 1) // MOE_BLK
    bend = jnp.cumsum(blocks).astype(I32)
    bstart = bend - blocks
    n_blocks = (N * TOP_K) // MOE_BLK + N_EXPERTS
    n_used = bend[-1:]
    b = jnp.minimum(jnp.arange(n_blocks, dtype=I32), n_used - 1)
    block_expert = jnp.minimum(jnp.sum((b[:, None] >= bend[None, :]).astype(I32), axis=1), N_EXPERTS - 1)
    first = jnp.sum(jnp.where(block_expert[:, None] == jnp.arange(N_EXPERTS)[None, :], bstart[None, :], 0), axis=1)
    row_block = block_expert * (N // MOE_BLK) + (b - first)
    count = jnp.sum(jnp.where(block_expert[:, None] == jnp.arange(N_EXPERTS)[None, :], counts[None, :], 0), axis=1)
    valid_rows = jnp.clip(count - (b - first) * MOE_BLK, 0, MOE_BLK).astype(I32)
    dest = (ri[:TOP_K] * N + ri[TOP_K:]).T.reshape(N * TOP_K)

    ys = _experts(block_expert, n_used, row_block, valid_rows, xs, w_gate_up, jnp.asarray(_deinterleave_perm(), BF16),
                  b_gate_up[:, None, 0::2], b_gate_up[:, None, 1::2], w_down, b_down[:, None, :])
    out = _combine(dest, ys, x1, rg.T, ln2_g[None, :], ln2_b[None, :], alpha)
    return out.reshape(B, T, D)


def kernel(x, positions, w_in, conv_w, cmp_pos_k, cmp_w1_k, cmp_w2_k, cmp_pos_v, cmp_w1_v, cmp_w2_v, w_up_conv, w_up_nsa, w_o, ln1_g, ln1_b, w_router, b_router, w_gate_up, b_gate_up, w_down, b_down, ln2_g, ln2_b):
    depth = w_in.shape[0]
    alpha = float((2 * depth) ** 0.25)
    h = x
    for l in range(depth):
        h = _layer(h, positions, w_in[l], conv_w[l], cmp_pos_k[l], cmp_w1_k[l], cmp_w2_k[l],
                   cmp_pos_v[l], cmp_w1_v[l], cmp_w2_v[l], w_up_conv[l], w_up_nsa[l], w_o[l],
                   ln1_g[l], ln1_b[l], w_router[l], b_router[l], w_gate_up[l], b_gate_up[l],
                   w_down[l], b_down[l], ln2_g[l], ln2_b[l], alpha)
    return h
```

```python
import functools

import numpy as np
import jax
import jax.numpy as jnp
from jax import lax
from jax.experimental import pallas as pl
from jax.experimental.pallas import tpu as pltpu

F32 = jnp.float32
BF16 = jnp.bfloat16
I32 = jnp.int32

D_MODEL = 1024
CONV_CH = 512
CONV_K = 3
N_HEADS = 8
N_KV_GROUPS = 2
HEADS_PER_GROUP = N_HEADS // N_KV_GROUPS
HEAD_DIM = 64
NSA_WIDTH = N_HEADS * HEAD_DIM
KV_WIDTH = N_KV_GROUPS * HEAD_DIM
ROPE_DIM = HEAD_DIM // 4
ROPE_THETA = 500000.0
CMP_BLOCK = 32
CMP_STRIDE = 16
CMP_HIDDEN = 256
SEL_BLOCK = 64
N_SELECT = 16
WINDOW = 512
N_EXPERTS = 32
TOP_K = 4
D_FF = 1024
SWIGLU_LIMIT = 7.0
SWIGLU_ALPHA = 1.702
LN_EPS = 1e-5
NEG_INF = -1e30
FORCE_SCORE = 1e4
LOG2E = 1.4426950408889634

LANES = 128
SUBLANES = 8
ROW_SUB = D_MODEL // LANES
assert ROW_SUB == SUBLANES
TAKEN = float("-inf")
SEL_LANES = 64
SEL_MASK_BIAS = -32768.0
GROUP_W = HEADS_PER_GROUP * HEAD_DIM
GATE_ROWS = 16

PROJ_TM = 1024
CMP_TQ = 512
SEL_TQ = 512
SEL_KC = 512
SEL_SPLIT = 2
WIN_TQ = 512
WIN_SUB = 128
MERGE_TM = 512
MERGE_SUB = 256
MOE_BLK = 512
COMB_TM = 256
COMB_GROUP = 128


def _dot(a, b, precision=None):
    return jnp.dot(a, b, precision=precision, preferred_element_type=F32)


def _dot_tn(a, b):
    return lax.dot_general(a, b, (((0,), (0,)), ((), ())), preferred_element_type=F32)


def _dot_nt(a, b, precision=None):
    return lax.dot_general(a, b, (((1,), (1,)), ((), ())), precision=precision, preferred_element_type=F32)


def _proj_kernel(x_ref, pos_ref, wc_ref, wqt_ref, wkvt_ref, wgt_ref, cw_ref, freq_ref,
                 ya_ref, qt_ref, kvt_ref, gt_ref, carry_ref):
    ti = pl.program_id(1)
    tm = x_ref.shape[1]
    xb = x_ref[0].astype(BF16)

    pc = _dot(xb, wc_ref[...])
    xv = pc[:, :CONV_CH]
    bg = pc[:, CONV_CH:2 * CONV_CH]
    cg = pc[:, 2 * CONV_CH:]
    u = cg * xv

    @pl.when(ti == 0)
    def _():
        carry_ref[...] = jnp.zeros_like(carry_ref)

    prev = carry_ref[...]
    row = lax.broadcasted_iota(I32, u.shape, 0)
    u1 = jnp.where(row == 0, prev[7:8], pltpu.roll(u, 1, 0))
    u2 = jnp.where(row == 0, prev[6:7], jnp.where(row == 1, prev[7:8], pltpu.roll(u, 2, 0)))
    cw = cw_ref[...]
    conv = cw[2:3] * u + cw[1:2] * u1 + cw[0:1] * u2
    carry_ref[...] = u[tm - 8:]
    ya_ref[0] = (bg * conv).astype(BF16)

    ang = freq_ref[...] * pos_ref[0]
    cos = jnp.cos(ang)
    sin = jnp.sin(ang)
    half = ROPE_DIM // 2

    def rope_head(t):
        t1 = t[:half]
        t2 = t[half:ROPE_DIM]
        return [t1 * cos - t2 * sin, t2 * cos + t1 * sin, t[ROPE_DIM:]]

    def heads(t, rotate):
        out = []
        for h in range(t.shape[0] // HEAD_DIM):
            th = t[h * HEAD_DIM:(h + 1) * HEAD_DIM]
            out.extend(rope_head(th) if rotate(h) else [th])
        return jnp.concatenate(out, axis=0)

    qt = heads(_dot_nt(wqt_ref[...], xb), lambda h: True)
    qt_ref[0] = (qt * (HEAD_DIM ** -0.5 * LOG2E)).astype(BF16)
    kvt = heads(_dot_nt(wkvt_ref[...], xb), lambda h: (h // N_KV_GROUPS) % 2 == 0)
    kvt_ref[0] = kvt.astype(BF16)
    gt_ref[0] = jax.nn.sigmoid(_dot_nt(wgt_ref[...], xb))


def _proj(x, pos_row, wc, wqt, wkvt, wgt, cw, freq):
    B, T, D = x.shape
    tm = min(PROJ_TM, T)
    grid = (B, T // tm)
    full = lambda a: pl.BlockSpec(a.shape, lambda b, t: (0,) * a.ndim)
    tok = lambda w: pl.BlockSpec((1, tm, w), lambda b, t: (b, t, 0))
    feat = lambda r: pl.BlockSpec((1, r, tm), lambda b, t: (b, 0, t))
    n_gate = wgt.shape[0]
    return pl.pallas_call(
        _proj_kernel,
        grid=grid,
        in_specs=[tok(D), feat(1), full(wc), full(wqt), full(wkvt), full(wgt), full(cw), full(freq)],
        out_specs=[tok(CONV_CH), feat(NSA_WIDTH), feat(6 * KV_WIDTH), feat(n_gate)],
        out_shape=[jax.ShapeDtypeStruct((B, T, CONV_CH), BF16),
                   jax.ShapeDtypeStruct((B, NSA_WIDTH, T), BF16),
                   jax.ShapeDtypeStruct((B, 6 * KV_WIDTH, T), BF16),
                   jax.ShapeDtypeStruct((B, n_gate, T), F32)],
        scratch_shapes=[pltpu.VMEM((8, CONV_CH), F32)],
        compiler_params=pltpu.CompilerParams(dimension_semantics=("arbitrary", "arbitrary")),
        name="proj",
    )(x, pos_row, wc, wqt, wkvt, wgt, cw, freq)


def _compress_kernel(x_ref, w1_ref, w2_ref, w2t_ref, pos_ref, o_ref, ot_ref):
    xb = x_ref[0, 0]
    nc = xb.shape[0]
    w1 = w1_ref[0]
    w1b = w1.astype(BF16)
    half = CMP_STRIDE * HEAD_DIM
    a = _dot(xb, w1b[:half])
    b = _dot(xb, w1b[half:])
    b_next = pltpu.roll(b, nc - 1, 0)
    pb = _dot(pos_ref[0], w1, precision=lax.Precision.HIGHEST)[0:1]
    h = a + b_next + pb
    g = (0.5 * h * (1.0 + jnp.tanh(np.sqrt(2.0 / np.pi) * (h + 0.044715 * (h * h * h))))).astype(BF16)
    o_ref[0, 0] = _dot(g, w2_ref[0].astype(BF16)).astype(BF16)
    ot_ref[0, 0] = _dot_nt(w2t_ref[0].astype(BF16), g).astype(BF16)


def _compress(xin, w1, w2, w2t, pos):
    _, BG, nc, W = xin.shape
    per = lambda a: pl.BlockSpec((1,) + a.shape[1:], lambda s, i: (s, 0, 0))
    return pl.pallas_call(
        _compress_kernel,
        grid=(2, BG),
        in_specs=[pl.BlockSpec((1, 1, nc, W), lambda s, i: (s, i, 0, 0)), per(w1), per(w2), per(w2t), per(pos)],
        out_specs=[pl.BlockSpec((1, 1, nc, HEAD_DIM), lambda s, i: (s, i, 0, 0)),
                   pl.BlockSpec((1, 1, HEAD_DIM, nc), lambda s, i: (s, i, 0, 0))],
        out_shape=[jax.ShapeDtypeStruct((2, BG, nc, HEAD_DIM), BF16),
                   jax.ShapeDtypeStruct((2, BG, HEAD_DIM, nc), BF16)],
        name="compress",
    )(xin, w1, w2, w2t, pos)


def _head_lanes(qt):
    return jnp.concatenate([qt[r * HEAD_DIM:(r + 1) * HEAD_DIM] for r in range(HEADS_PER_GROUP)], axis=1)


def _gated_out(ot, scale, gates, branch, tq):
    rows = []
    for r in range(HEADS_PER_GROUP):
        c = branch * HEADS_PER_GROUP + r
        sl = slice(r * tq, (r + 1) * tq)
        rows.append(ot[:, sl] * (scale[:, sl] * gates[c:c + 1]))
    return jnp.concatenate(rows, axis=0).astype(BF16)


def _attn_specs(tq):
    G = N_KV_GROUPS
    qspec = pl.BlockSpec((1, GROUP_W, tq), lambda b, g, i: (b, g, i))
    gspec = pl.BlockSpec((1, GATE_ROWS, tq), lambda b, g, i: (b, g, i))
    bg4 = lambda a: pl.BlockSpec((1, 1) + a.shape[2:], lambda b, g, i: (b, g, 0, 0))
    return G, qspec, gspec, bg4


def _cmp_attn_kernel(q_ref, kc_ref, vct_ref, g_ref, ovt_ref, o_ref, sel_ref):
    qi = pl.program_id(2)
    tq = q_ref.shape[2]
    q0 = qi * tq
    qt = _head_lanes(q_ref[0])
    s = _dot(kc_ref[0, 0], qt)
    c = lax.broadcasted_iota(I32, s.shape, 0)
    t = q0 + (lax.broadcasted_iota(I32, (1, s.shape[1]), 1) & (tq - 1))
    valid = c <= (t - (CMP_BLOCK - 1)) // CMP_STRIDE
    sm = jnp.where(valid, s, NEG_INF)
    m = jnp.max(sm, axis=0, keepdims=True)
    p = jnp.where(valid, jnp.exp2(sm - m), 0.0)
    l = jnp.sum(p, axis=0, keepdims=True)
    inv = 1.0 / jnp.where(l > 0.0, l, 1.0)
    ot = _dot(vct_ref[0, 0], p.astype(BF16))
    o_ref[0] = _gated_out(ot, inv, g_ref[0], 0, tq)

    pn = p * inv
    psum = pn[:, 0:tq]
    for r in range(1, HEADS_PER_GROUP):
        psum = psum + pn[:, r * tq:(r + 1) * tq]
    imp = _dot(ovt_ref[...], psum, precision=lax.Precision.HIGHEST)
    j = lax.broadcasted_iota(I32, imp.shape, 0)
    cur = (q0 + lax.broadcasted_iota(I32, imp.shape, 1)) // SEL_BLOCK
    valid_b = j <= cur
    forced = (j == 0) | (j == cur) | (j == cur - 1)
    score = jnp.where(valid_b, jnp.where(forced, FORCE_SCORE, imp), NEG_INF)
    sub = SUBLANES
    groups = [score[a:a + sub] for a in range(0, SEL_LANES, sub)]
    jrow = lax.broadcasted_iota(I32, groups[0].shape, 0)
    ranks = [jnp.zeros(g_.shape, I32) for g_ in groups]
    for i in range(SEL_LANES):
        si = score[i:i + 1, :]
        for a, g_ in enumerate(groups):
            if a > i // sub:
                inc = jnp.where(si >= g_, 1, 0)
            elif a < i // sub:
                inc = jnp.where(si > g_, 1, 0)
            else:
                inc = jnp.where(jrow > i % sub, jnp.where(si >= g_, 1, 0), jnp.where(si > g_, 1, 0))
            ranks[a] = ranks[a] + inc
    rank = jnp.concatenate(ranks, axis=0)
    selected = (rank < N_SELECT) & valid_b
    sel_ref[0, 0] = jnp.where(selected, 0.0, SEL_MASK_BIAS).astype(BF16)


def _cmp_attn(qt, kcmp, vcmp_t, gates, ovt):
    B, _, T = qt.shape
    tq = min(CMP_TQ, T)
    G, qspec, gspec, _ = _attn_specs(tq)
    cspec = lambda a: pl.BlockSpec((1, 1) + a.shape[2:], lambda b, g, i: (0, b * G + g, 0, 0))
    vspec = lambda a: pl.BlockSpec((1, 1) + a.shape[2:], lambda b, g, i: (1, b * G + g, 0, 0))
    return pl.pallas_call(
        _cmp_attn_kernel,
        grid=(B, G, T // tq),
        in_specs=[qspec, cspec(kcmp), vspec(vcmp_t), gspec, pl.BlockSpec(ovt.shape, lambda b, g, i: (0, 0))],
        out_specs=[qspec, pl.BlockSpec((1, 1, SEL_LANES, tq), lambda b, g, i: (b, g, 0, i))],
        out_shape=[jax.ShapeDtypeStruct((B, NSA_WIDTH, T), BF16),
                   jax.ShapeDtypeStruct((B, G, SEL_LANES, T), BF16)],
        name="cmp_attn",
    )(qt, kcmp, vcmp_t, gates, ovt)


def _sel_attn_kernel(q_ref, sel_ref, kt_ref, hot_ref, vt_ref, g_ref, o_ref, *, kc):
    qi = pl.program_id(2)
    tq = q_ref.shape[2]
    q0 = qi * tq
    qt = q_ref[0]
    sb = sel_ref[0, 0]
    qa = jnp.concatenate(
        [jnp.concatenate([qt[r * HEAD_DIM:(r + 1) * HEAD_DIM], sb], axis=0) for r in range(HEADS_PER_GROUP)],
        axis=1)
    gw = qa.shape[1] // SEL_SPLIT
    qas = [qa[:, i * gw:(i + 1) * gw] for i in range(SEL_SPLIT)]

    def step(kstart, kn, carry, diagonal):
        keys = pl.ds(kstart, kn)
        kblk_t = jnp.concatenate([kt_ref[0, :, keys], hot_ref[:, keys]], axis=0)
        vblk = vt_ref[0, :, keys]
        ss = [_dot_tn(kblk_t, qg) for qg in qas]
        out = []
        for gi, (s, (m, l, acc)) in enumerate(zip(ss, carry)):
            if diagonal:
                row = lax.broadcasted_iota(I32, s.shape, 0)
                off = (gi * gw + lax.broadcasted_iota(I32, s.shape, 1)) & (tq - 1)
                s = jnp.where(row <= off, s, NEG_INF)
            mn = jnp.maximum(m, jnp.max(s, axis=0, keepdims=True))
            alpha = jnp.exp2(m - mn)
            p = jnp.exp2(s - mn)
            l = alpha * l + jnp.sum(p, axis=0, keepdims=True)
            acc = alpha * acc + _dot(vblk, p.astype(BF16))
            out.append((mn, l, acc))
        return tuple(out)

    init = tuple((jnp.full((1, gw), NEG_INF, F32), jnp.zeros((1, gw), F32), jnp.zeros((HEAD_DIM, gw), F32))
                 for _ in range(SEL_SPLIT))
    n_full = q0 // kc
    carry = lax.fori_loop(0, n_full, lambda i, cr: step(pl.multiple_of(i * kc, kc), kc, cr, False), init)
    carry = lax.fori_loop(n_full * (kc // tq), qi, lambda i, cr: step(pl.multiple_of(i * tq, tq), tq, cr, False), carry)
    carry = step(pl.multiple_of(q0, tq), tq, carry, True)
    l = jnp.concatenate([c[1] for c in carry], axis=1)
    acc = jnp.concatenate([c[2] for c in carry], axis=1)
    o_ref[0] = _gated_out(acc, 1.0 / l, g_ref[0], 1, tq)


def _sel_attn(qt, sel, kvt, k_row_block, v_row_block, hot_t, gates):
    B, _, T = qt.shape
    tq = min(SEL_TQ, T)
    kc = min(SEL_KC, T)
    G, qspec, gspec, bg4 = _attn_specs(tq)
    return pl.pallas_call(
        functools.partial(_sel_attn_kernel, kc=kc),
        grid=(B, G, T // tq),
        in_specs=[qspec, pl.BlockSpec((1, 1, SEL_LANES, tq), lambda b, g, i: (b, g, 0, i)),
                  pl.BlockSpec((1, HEAD_DIM, T), lambda b, g, i: (b, k_row_block + g, 0)),
                  pl.BlockSpec(hot_t.shape, lambda b, g, i: (0, 0)),
                  pl.BlockSpec((1, HEAD_DIM, T), lambda b, g, i: (b, v_row_block + g, 0)), gspec],
        out_specs=qspec,
        out_shape=jax.ShapeDtypeStruct((B, NSA_WIDTH, T), BF16),
        name="sel_attn",
    )(qt, sel, kvt, hot_t, kvt, gates)


def _win_attn_kernel(q_ref, kt_ref, vt_ref, g_ref, o_ref, *, span, sub):
    qi = pl.program_id(2)
    tq = q_ref.shape[2]
    T = kt_ref.shape[2]

    def tile(s0, interior):
        cs = slice(s0, s0 + sub)
        q0 = qi * tq + s0
        start = pl.multiple_of(jnp.clip(q0 + sub - span, 0, T - span), sub)
        qt = _head_lanes(q_ref[0, :, cs])
        s = _dot_tn(kt_ref[0, :, pl.ds(start, span)], qt)
        if interior:
            row = lax.broadcasted_iota(I32, (sub, s.shape[1]), 0)
            off = lax.broadcasted_iota(I32, (sub, s.shape[1]), 1) & (sub - 1)
            s = jnp.concatenate([jnp.where(row > off, s[:sub], NEG_INF), s[sub:span - sub],
                                 jnp.where(row <= off, s[span - sub:], NEG_INF)], axis=0)
        else:
            kp = start + lax.broadcasted_iota(I32, s.shape, 0)
            t = q0 + (lax.broadcasted_iota(I32, s.shape, 1) & (sub - 1))
            diff = t - kp
            s = jnp.where((diff >= 0) & (diff < WINDOW), s, NEG_INF)
        m = jnp.max(s, axis=0, keepdims=True)
        p = jnp.exp2(s - m)
        l = jnp.sum(p, axis=0, keepdims=True)
        ot = _dot(vt_ref[0, :, pl.ds(start, span)], p.astype(BF16))
        o_ref[0, :, cs] = _gated_out(ot, 1.0 / l, g_ref[0, :, cs], 2, sub)

    first_interior = -(-WINDOW // tq)
    if span == WINDOW + sub:
        @pl.when(qi >= first_interior)
        def _():
            for s0 in range(0, tq, sub):
                tile(s0, True)

    @pl.when((qi < first_interior) | (span != WINDOW + sub))
    def _():
        for s0 in range(0, tq, sub):
            tile(s0, False)


def _win_attn(qt, kvt, k_row_block, v_row_block, gates):
    B, _, T = qt.shape
    tq = min(WIN_TQ, T)
    sub = min(WIN_SUB, T)
    span = min(WINDOW + sub, T)
    G, qspec, gspec, bg4 = _attn_specs(tq)
    return pl.pallas_call(
        functools.partial(_win_attn_kernel, span=span, sub=sub),
        grid=(B, G, T // tq),
        in_specs=[qspec, pl.BlockSpec((1, HEAD_DIM, T), lambda b, g, i: (b, k_row_block + g, 0)),
                  pl.BlockSpec((1, HEAD_DIM, T), lambda b, g, i: (b, v_row_block + g, 0)), gspec],
        out_specs=qspec,
        out_shape=jax.ShapeDtypeStruct((B, NSA_WIDTH, T), BF16),
        name="win_attn",
    )(qt, kvt, kvt, gates)


def _store_row_tiles(ref, x, row0=0):
    rows = x.shape[0]
    for s in range(ROW_SUB):
        ref[pl.ds(row0 * ROW_SUB + s, rows, stride=ROW_SUB), :] = x[:, s * LANES:(s + 1) * LANES]


def _load_row_tiles(ref):
    rows = ref.shape[0] // ROW_SUB
    return jnp.concatenate([ref[pl.ds(s, rows, stride=ROW_SUB), :] for s in range(ROW_SUB)], axis=1)


def _tile_copy(src, si, dst, di, sem):
    return pltpu.make_async_copy(src.at[pl.ds(pl.multiple_of(si * ROW_SUB, ROW_SUB), ROW_SUB), :],
                                 dst.at[pl.ds(pl.multiple_of(di * ROW_SUB, ROW_SUB), ROW_SUB), :], sem)


def _layer_norm(h, g, b):
    mu = jnp.mean(h, axis=-1, keepdims=True)
    c = h - mu
    var = jnp.mean(c * c, axis=-1, keepdims=True)
    return c * lax.rsqrt(var + LN_EPS) * g + b


def _merge_kernel(x_ref, ya_ref, oc_ref, os_ref, ow_ref, wmg_ref, wuc_ref, wun_ref, wo_ref, g1_ref, b1_ref,
                  wr_ref, br_ref, tri_ref, x1_ref, ri_ref, rg_ref, cnt_ref, xs_hbm,
                  carry_ref, stage, dest_v, dest_s, copy_sem, row_sem, *, alpha, cap):
    i = pl.program_id(0)
    last = pl.num_programs(0) - 1
    sub = tri_ref.shape[0]
    tm = x_ref.shape[0]
    slot = i % 2
    nsub = tm // sub

    def id_copy(si):
        return pltpu.make_async_copy(dest_v.at[si], dest_s.at[si], copy_sem.at[si])

    def issue_rows(si, s):
        id_copy(si).wait()
        for t in range(sub):
            for k in range(TOP_K):
                _tile_copy(stage.at[s], si * sub + t, xs_hbm, dest_s[si, k, t], row_sem.at[s]).start(priority=k % 2)

    def wait_rows(s, rows):
        for k in range(TOP_K):
            n = rows * ROW_SUB
            pltpu.make_async_copy(stage.at[s, pl.ds(0, n), :], xs_hbm.at[pl.ds(0, n), :], row_sem.at[s]).wait()

    @pl.when(i == 0)
    def _():
        carry_ref[...] = jnp.zeros_like(carry_ref)
        stage[1, pl.ds((nsub - 1) * sub * ROW_SUB, sub * ROW_SUB), :] = jnp.zeros((sub * ROW_SUB, LANES), F32)
        spare = (N_EXPERTS * cap + lax.broadcasted_iota(I32, (TOP_K, sub), 0) * sub
                 + lax.broadcasted_iota(I32, (TOP_K, sub), 1))
        dest_v[nsub - 1] = spare
        id_copy(nsub - 1).start()

    wr = wr_ref[...]
    wr_hi = wr.astype(BF16)
    wr_hl = jnp.concatenate([wr_hi, (wr - wr_hi.astype(F32)).astype(BF16)], axis=0)
    total = carry_ref[...]
    for si in range(nsub):
        s0 = si * sub
        rs = slice(s0, s0 + sub)
        x = x_ref[rs, :]
        mg = _dot(x.astype(BF16), wmg_ref[...])
        if si == 0:
            issue_rows(nsub - 1, 1 - slot)
        else:
            issue_rows(si - 1, slot)
        y_a = _dot(ya_ref[rs, :], wuc_ref[...])
        o_nsa_t = (oc_ref[0, :, rs].astype(F32) + os_ref[0, :, rs].astype(F32)
                   + ow_ref[0, :, rs].astype(F32))
        y_b = _dot(o_nsa_t.T.astype(BF16), wun_ref[...])
        merged = jax.nn.sigmoid(mg[:, :D_MODEL]) * y_a + jax.nn.sigmoid(mg[:, D_MODEL:]) * y_b
        h = alpha * x + _dot(merged.astype(BF16), wo_ref[...])
        x1 = _layer_norm(h, g1_ref[...], b1_ref[...])
        x1_ref[rs, :] = x1
        _store_row_tiles(stage.at[slot], x1, s0)

        x_hi = x1.astype(BF16)
        x_lo = (x1 - x_hi.astype(F32)).astype(BF16)
        both = _dot_nt(wr_hl, x_hi)
        logits = both[:N_EXPERTS] + both[N_EXPERTS:] + _dot_nt(wr_hi, x_lo) + br_ref[...]
        expert = lax.broadcasted_iota(I32, logits.shape, 0).astype(F32)
        rem = logits
        vals, idxs, hots = [], [], []
        for _ in range(TOP_K):
            m = jnp.max(rem, axis=0, keepdims=True)
            idx = jnp.min(jnp.where(rem == m, expert, float(N_EXPERTS)), axis=0, keepdims=True)
            hot = expert == idx
            vals.append(m)
            idxs.append(idx)
            hots.append(hot)
            rem = jnp.where(hot, TAKEN, rem)
        es = [jnp.exp(v - vals[0]) for v in vals]
        den = es[0]
        for e in es[1:]:
            den = den + e
        chosen = hots[0]
        for hot in hots[1:]:
            chosen = chosen | hot
        chosen_f = jnp.where(chosen, 1.0, 0.0)

        before = _dot(chosen_f.astype(BF16), tri_ref[...]) + total
        total = total + jnp.sum(chosen_f, axis=1, keepdims=True)
        ranks = [jnp.sum(jnp.where(hot, before, 0.0), axis=0, keepdims=True) for hot in hots]
        ri_ref[:, rs] = jnp.concatenate(idxs + ranks, axis=0).astype(I32)
        rg_ref[:, rs] = jnp.concatenate([e / den for e in es] + [jnp.zeros_like(den)] * TOP_K, axis=0)

        dest_v[si] = jnp.concatenate(idxs, axis=0).astype(I32) * cap + jnp.concatenate(ranks, axis=0).astype(I32)
        id_copy(si).start()

    carry_ref[...] = total
    cnt_ref[...] = jnp.broadcast_to(total, cnt_ref.shape).astype(I32)

    @pl.when(i == 0)
    def _():
        wait_rows(1, sub)

    @pl.when(i > 0)
    def _():
        wait_rows(1 - slot, tm)

    @pl.when(i == last)
    def _():
        issue_rows(nsub - 1, slot)
        wait_rows(slot, tm)


def _merge(x, ya, oc, os_, ow, wmg, wuc, wun, wo, g1, b1, wr, br, tri, alpha):
    N, D = x.shape
    B, _, T = oc.shape
    tm = min(MERGE_TM, T)
    nt = T // tm
    tok = lambda w: pl.BlockSpec((tm, w), lambda i: (i, 0))
    feat = pl.BlockSpec((1, NSA_WIDTH, tm), lambda i: (i // nt, 0, i % nt))
    full = lambda a: pl.BlockSpec(a.shape, lambda i: (0,) * a.ndim)
    sub = tri.shape[0]
    return pl.pallas_call(
        functools.partial(_merge_kernel, alpha=alpha, cap=N),
        grid=(N // tm,),
        in_specs=[tok(D), tok(CONV_CH), feat, feat, feat,
                  full(wmg), full(wuc), full(wun), full(wo), full(g1), full(b1), full(wr), full(br), full(tri)],
        out_specs=[tok(D),
                   pl.BlockSpec((2 * TOP_K, tm), lambda i: (0, i)), pl.BlockSpec((2 * TOP_K, tm), lambda i: (0, i)),
                   pl.BlockSpec((N_EXPERTS, LANES), lambda i: (0, 0)), pl.BlockSpec(memory_space=pl.ANY)],
        out_shape=[jax.ShapeDtypeStruct((N, D), F32),
                   jax.ShapeDtypeStruct((2 * TOP_K, N), I32),
                   jax.ShapeDtypeStruct((2 * TOP_K, N), F32),
                   jax.ShapeDtypeStruct((N_EXPERTS, LANES), I32),
                   jax.ShapeDtypeStruct(((N_EXPERTS * N + TOP_K * sub) * ROW_SUB, LANES), F32)],
        scratch_shapes=[pltpu.VMEM((N_EXPERTS, 1), F32), pltpu.VMEM((2, tm * ROW_SUB, LANES), F32),
                        pltpu.VMEM((tm // sub, TOP_K, sub), I32), pltpu.SMEM((tm // sub, TOP_K, sub), I32),
                        pltpu.SemaphoreType.DMA((tm // sub,)), pltpu.SemaphoreType.DMA((2,))],
        compiler_params=pltpu.CompilerParams(dimension_semantics=("arbitrary",)),
        name="merge",
    )(x, ya, oc, os_, ow, wmg, wuc, wun, wo, g1, b1, wr, br, tri)


def _experts_kernel(be_ref, nu_ref, rb_ref, vr_ref, xs_ref, wgu_ref, perm_ref, bg_ref, bl_ref, wd_ref, bd_ref, ys_ref, wg_s, wl_s, wd_s):
    i = pl.program_id(0)

    @pl.when((i == 0) | (be_ref[i] != be_ref[jnp.maximum(i, 1) - 1]))
    def _():
        w = perm_ref.shape[0]
        for c in range(2 * D_FF // w):
            t = _dot(wgu_ref[0, :, c * w:(c + 1) * w].astype(BF16), perm_ref[...])
            wg_s[:, c * (w // 2):(c + 1) * (w // 2)] = t[:, :w // 2].astype(BF16)
            wl_s[:, c * (w // 2):(c + 1) * (w // 2)] = t[:, w // 2:].astype(BF16)
        wd_s[...] = wd_ref[0].astype(BF16)

    @pl.when(i < nu_ref[0])
    def _():
        row = lax.broadcasted_iota(I32, (MOE_BLK, 1), 0)
        xb = jnp.where(row < vr_ref[i], _load_row_tiles(xs_ref), 0.0).astype(BF16)
        x_glu = jnp.minimum(_dot(xb, wg_s[...]) + bg_ref[0], SWIGLU_LIMIT)
        x_lin = jnp.clip(_dot(xb, wl_s[...]) + bl_ref[0], -SWIGLU_LIMIT, SWIGLU_LIMIT)
        act = x_glu * jax.nn.sigmoid(SWIGLU_ALPHA * x_glu) * (x_lin + 1.0)
        _store_row_tiles(ys_ref, _dot(act.astype(BF16), wd_s[...]) + bd_ref[0])


def _experts(block_expert, n_used, row_block, valid_rows, xs, wgu, perm, bg, bl, wd, bd):
    D = D_MODEL
    n_blocks = block_expert.shape[0]
    rows = lambda i, be, nu, rb, vr: (rb[i], 0)
    wsel = lambda i, be, nu, rb, vr: (be[i], 0, 0)
    return pl.pallas_call(
        _experts_kernel,
        grid_spec=pltpu.PrefetchScalarGridSpec(
            num_scalar_prefetch=4,
            grid=(n_blocks,),
            in_specs=[pl.BlockSpec((MOE_BLK * ROW_SUB, LANES), rows),
                      pl.BlockSpec((1, D, 2 * D_FF), wsel),
                      pl.BlockSpec(perm.shape, lambda i, be, nu, rb, vr: (0, 0)),
                      pl.BlockSpec((1, 1, D_FF), wsel), pl.BlockSpec((1, 1, D_FF), wsel),
                      pl.BlockSpec((1, D_FF, D), wsel), pl.BlockSpec((1, 1, D), wsel)],
            out_specs=pl.BlockSpec((MOE_BLK * ROW_SUB, LANES), rows),
            scratch_shapes=[pltpu.VMEM((D, D_FF), BF16), pltpu.VMEM((D, D_FF), BF16), pltpu.VMEM((D_FF, D), BF16)]),
        out_shape=jax.ShapeDtypeStruct(xs.shape, F32),
        compiler_params=pltpu.CompilerParams(dimension_semantics=("arbitrary",)),
        name="experts",
    )(block_expert, n_used, row_block, valid_rows, xs, wgu, perm, bg, bl, wd, bd)


def _combine_kernel(dest_ref, dest_next_ref, ys_hbm, x1_ref, rg_ref, g2_ref, b2_ref, o_ref, buf, sem, *, alpha):
    i = pl.program_id(0)
    last = pl.num_programs(0) - 1
    tm = o_ref.shape[0]
    slot = i % 2
    other = 1 - slot

    def gather(ids_ref, t, to_slot):
        for k in range(TOP_K):
            _tile_copy(ys_hbm, ids_ref[t * TOP_K + k], buf.at[to_slot, k], t, sem.at[to_slot]).start(priority=k % 2)

    def wait_slot(s):
        for k in range(TOP_K):
            pltpu.make_async_copy(ys_hbm.at[pl.ds(0, tm * ROW_SUB), :], buf.at[s, k], sem.at[s]).wait()

    @pl.when(i == 0)
    def _():
        def first(t, c):
            gather(dest_ref, t, slot)
            return c
        lax.fori_loop(0, tm, first, 0)

    wait_slot(slot)

    gs = min(COMB_GROUP, tm)

    def group(g, c):
        r0 = pl.multiple_of(g * gs, gs)
        for j in range(gs):
            gather(dest_next_ref, r0 + j, other)
        rows = pl.ds(r0, gs)
        gate = rg_ref[rows, :]
        y = alpha * x1_ref[rows, :]
        for k in range(TOP_K):
            tiles = buf.at[slot, k, pl.ds(pl.multiple_of(r0 * ROW_SUB, gs * ROW_SUB), gs * ROW_SUB), :]
            y = y + gate[:, k:k + 1] * _load_row_tiles(tiles)
        o_ref[rows, :] = _layer_norm(y, g2_ref[...], b2_ref[...])
        return c

    lax.fori_loop(0, tm // gs, group, 0)

    @pl.when(i == last)
    def _():
        wait_slot(other)


def _combine(dest, ys, x1, rg, g2, b2, alpha):
    N, D = x1.shape
    tm = min(COMB_TM, N)
    n = N // tm
    tok = lambda w: pl.BlockSpec((tm, w), lambda i: (i, 0))
    full = lambda a: pl.BlockSpec(a.shape, lambda i: (0,) * a.ndim)
    ids = lambda f: pl.BlockSpec((tm * TOP_K,), f, memory_space=pltpu.SMEM)
    return pl.pallas_call(
        functools.partial(_combine_kernel, alpha=alpha),
        grid=(n,),
        in_specs=[ids(lambda i: (i,)), ids(lambda i: (jnp.minimum(i + 1, n - 1),)),
                  pl.BlockSpec(memory_space=pl.ANY), tok(D), tok(rg.shape[1]), full(g2), full(b2)],
        out_specs=tok(D),
        out_shape=jax.ShapeDtypeStruct((N, D), F32),
        scratch_shapes=[pltpu.VMEM((2, TOP_K, tm * ROW_SUB, LANES), F32), pltpu.SemaphoreType.DMA((2,))],
        compiler_params=pltpu.CompilerParams(dimension_semantics=("arbitrary",)),
        name="combine",
    )(dest, dest, ys, x1, rg, g2, b2)


def _rope_freq():
    half = ROPE_DIM // 2
    inv = (np.float32(ROPE_THETA) ** (-np.arange(half, dtype=np.float32) * np.float32(2.0 / ROPE_DIM))).astype(np.float32)
    return inv[:, None]


def _overlap_t(T):
    nc = T // CMP_STRIDE
    c0 = np.arange(nc) * CMP_STRIDE
    j0 = np.arange(SEL_LANES) * SEL_BLOCK
    ov = (c0[None, :] < j0[:, None] + SEL_BLOCK) & (c0[None, :] + CMP_BLOCK > j0[:, None])
    ov &= (np.arange(nc) < nc - 1)[None, :] & (np.arange(SEL_LANES) < T // SEL_BLOCK)[:, None]
    return ov.astype(np.float32)


def _deinterleave_perm():
    w = 2 * LANES
    p = np.zeros((w, w), np.float32)
    p[np.arange(0, w, 2), np.arange(w // 2)] = 1.0
    p[np.arange(1, w, 2), w // 2 + np.arange(w // 2)] = 1.0
    return p


def _token_major(t, B, T):
    return t.reshape(B, N_KV_GROUPS, HEAD_DIM, T).transpose(0, 1, 3, 2)


def _layer(x, positions, w_in, conv_w, cmp_pos_k, cmp_w1_k, cmp_w2_k, cmp_pos_v, cmp_w1_v, cmp_w2_v,
           w_up_conv, w_up_nsa, w_o, ln1_g, ln1_b, w_router, b_router, w_gate_up, b_gate_up,
           w_down, b_down, ln2_g, ln2_b, alpha):
    B, T, D = x.shape
    G, R = N_KV_GROUPS, HEADS_PER_GROUP
    N = B * T
    assert D == D_MODEL and T % SEL_BLOCK == 0 and T // SEL_BLOCK <= SEL_LANES

    c0 = 3 * CONV_CH
    c1 = c0 + NSA_WIDTH
    c2 = c1 + 6 * KV_WIDTH
    c3 = c2 + 3 * N_HEADS
    wc = w_in[:, :c0].astype(BF16)
    wqt = w_in[:, c0:c1].T.astype(BF16)
    wkvt = w_in[:, c1:c2].T.astype(BF16)
    gcols = np.zeros((G * GATE_ROWS,), np.int64)
    gmask = np.zeros((G * GATE_ROWS,), np.float32)
    for g in range(G):
        for br in range(3):
            for r in range(R):
                gcols[g * GATE_ROWS + br * R + r] = br * N_HEADS + g * R + r
                gmask[g * GATE_ROWS + br * R + r] = 1.0
    wgt = (w_in[:, c2:c3][:, gcols] * gmask).T.astype(BF16)
    wmg = w_in[:, c3:].astype(BF16)
    cw = conv_w.reshape(CONV_K, CONV_CH)
    pos_row = positions.astype(F32)[:, None, :]

    ya_pre, qt, kvt, gates = _proj(x, pos_row, wc, wqt, wkvt, wgt, cw, jnp.asarray(_rope_freq()))

    nc = T // CMP_STRIDE
    to_chunks = lambda t: _token_major(t, B, T).reshape(B * G, nc, CMP_STRIDE * HEAD_DIM)
    xin = jnp.stack([to_chunks(kvt[:, 0:KV_WIDTH]), to_chunks(kvt[:, KV_WIDTH:2 * KV_WIDTH])])
    w1 = jnp.stack([cmp_w1_k, cmp_w1_v])
    w2 = jnp.stack([cmp_w2_k, cmp_w2_v])
    pos_flat = jnp.stack([cmp_pos_k.reshape(1, -1), cmp_pos_v.reshape(1, -1)])
    pos_flat = jnp.pad(pos_flat, ((0, 0), (0, 7), (0, 0)))
    kcmp, kcmp_t = _compress(xin, w1, w2, w2.transpose(0, 2, 1), pos_flat)

    o_c, sel = _cmp_attn(qt, kcmp, kcmp_t, gates, jnp.asarray(_overlap_t(T)))

    hot_t = (np.arange(SEL_LANES)[:, None] == np.arange(T)[None, :] // SEL_BLOCK).astype(np.float32)
    o_s = _sel_attn(qt, sel, kvt, 2 * G, 3 * G, jnp.asarray(hot_t, BF16), gates)
    o_w = _win_attn(qt, kvt, 4 * G, 5 * G, gates)

    sub = min(MERGE_SUB, T)
    tri = jnp.asarray(np.triu(np.ones((sub, sub), np.float32), 1), BF16)
    assert N % MOE_BLK == 0
    x1, ri, rg, cnt, xs = _merge(x.reshape(N, D), ya_pre.reshape(N, CONV_CH), o_c, o_s, o_w, wmg,
                             w_up_conv.astype(BF16), w_up_nsa.astype(BF16), w_o.astype(BF16),
                             ln1_g[None, :], ln1_b[None, :], w_router.T, b_router[:, None], tri, alpha)

    counts = cnt[:, 0]
    blocks = (counts + MOE_BLK - 1) // MOE_BLK
    bend = jnp.cumsum(blocks).astype(I32)
    bstart = bend - blocks
    n_blocks = (N * TOP_K) // MOE_BLK + N_EXPERTS
    n_used = bend[-1:]
    b = jnp.minimum(jnp.arange(n_blocks, dtype=I32), n_used - 1)
    block_expert = jnp.minimum(jnp.sum((b[:, None] >= bend[None, :]).astype(I32), axis=1), N_EXPERTS - 1)
    first = jnp.sum(jnp.where(block_expert[:, None] == jnp.arange(N_EXPERTS)[None, :], bstart[None, :], 0), axis=1)
    row_block = block_expert * (N // MOE_BLK) + (b - first)
    count = jnp.sum(jnp.where(block_expert[:, None] == jnp.arange(N_EXPERTS)[None, :], counts[None, :], 0), axis=1)
    valid_rows = jnp.clip(count - (b - first) * MOE_BLK, 0, MOE_BLK).astype(I32)
    dest = (ri[:TOP_K] * N + ri[TOP_K:]).T.reshape(N * TOP_K)

    ys = _experts(block_expert, n_used, row_block, valid_rows, xs, w_gate_up, jnp.asarray(_deinterleave_perm(), BF16),
                  b_gate_up[:, None, 0::2], b_gate_up[:, None, 1::2], w_down, b_down[:, None, :])
    out = _combine(dest, ys, x1, rg.T, ln2_g[None, :], ln2_b[None, :], alpha)
    return out.reshape(B, T, D)


def kernel(x, positions, w_in, conv_w, cmp_pos_k, cmp_w1_k, cmp_w2_k, cmp_pos_v, cmp_w1_v, cmp_w2_v, w_up_conv, w_up_nsa, w_o, ln1_g, ln1_b, w_router, b_router, w_gate_up, b_gate_up, w_down, b_down, ln2_g, ln2_b):
    depth = w_in.shape[0]
    alpha = float((2 * depth) ** 0.25)
    h = x
    for l in range(depth):
        h = _layer(h, positions, w_in[l], conv_w[l], cmp_pos_k[l], cmp_w1_k[l], cmp_w2_k[l],
                   cmp_pos_v[l], cmp_w1_v[l], cmp_w2_v[l], w_up_conv[l], w_up_nsa[l], w_o[l],
                   ln1_g[l], ln1_b[l], w_router[l], b_router[l], w_gate_up[l], b_gate_up[l],
                   w_down[l], b_down[l], ln2_g[l], ln2_b[l], alpha)
    return h
```

```python
import functools

import numpy as np
import jax
import jax.numpy as jnp
from jax import lax
from jax.experimental import pallas as pl
from jax.experimental.pallas import tpu as pltpu

F32 = jnp.float32
BF16 = jnp.bfloat16
I32 = jnp.int32

D_MODEL = 1024
CONV_CH = 512
CONV_K = 3
N_HEADS = 8
N_KV_GROUPS = 2
HEADS_PER_GROUP = N_HEADS // N_KV_GROUPS
HEAD_DIM = 64
NSA_WIDTH = N_HEADS * HEAD_DIM
KV_WIDTH = N_KV_GROUPS * HEAD_DIM
ROPE_DIM = HEAD_DIM // 4
ROPE_THETA = 500000.0
CMP_BLOCK = 32
CMP_STRIDE = 16
CMP_HIDDEN = 256
SEL_BLOCK = 64
N_SELECT = 16
WINDOW = 512
N_EXPERTS = 32
TOP_K = 4
D_FF = 1024
SWIGLU_LIMIT = 7.0
SWIGLU_ALPHA = 1.702
LN_EPS = 1e-5
NEG_INF = -1e30
FORCE_SCORE = 1e4
LOG2E = 1.4426950408889634

LANES = 128
SUBLANES = 8
ROW_SUB = D_MODEL // LANES
assert ROW_SUB == SUBLANES
TAKEN = float("-inf")
SEL_LANES = 64
SEL_MASK_BIAS = -32768.0
GROUP_W = HEADS_PER_GROUP * HEAD_DIM
GATE_ROWS = 16

PROJ_TM = 1024
PROJ_SUB = 256
CMP_TQ = 512
SEL_TQ = 512
SEL_KC = 512
SEL_SPLIT = 2
WIN_TQ = 512
WIN_SUB = 128
MERGE_TM = 512
MERGE_SUB = 256
MOE_BLK = 512
COMB_TM = 256
COMB_GROUP = 128


def _dot(a, b, precision=None):
    return jnp.dot(a, b, precision=precision, preferred_element_type=F32)


def _dot_tn(a, b):
    return lax.dot_general(a, b, (((0,), (0,)), ((), ())), preferred_element_type=F32)


def _dot_nt(a, b, precision=None):
    return lax.dot_general(a, b, (((1,), (1,)), ((), ())), precision=precision, preferred_element_type=F32)


def _proj_kernel(x_ref, pos_ref, wc_ref, wqt_ref, wkvt_ref, wgt_ref, cw_ref, freq_ref,
                 ya_ref, qt_ref, kvt_ref, gt_ref, carry_ref):
    ti = pl.program_id(1)
    tm = x_ref.shape[1]
    sub = min(PROJ_SUB, tm)
    half = ROPE_DIM // 2

    @pl.when(ti == 0)
    def _():
        carry_ref[...] = jnp.zeros_like(carry_ref)

    prev = carry_ref[...]
    cw = cw_ref[...]
    for s0 in range(0, tm, sub):
        rs = slice(s0, s0 + sub)
        xb = x_ref[0, rs, :].astype(BF16)

        pc = _dot(xb, wc_ref[...])
        xv = pc[:, :CONV_CH]
        bg = pc[:, CONV_CH:2 * CONV_CH]
        cg = pc[:, 2 * CONV_CH:]
        u = cg * xv
        row = lax.broadcasted_iota(I32, u.shape, 0)
        u1 = jnp.where(row == 0, prev[7:8], pltpu.roll(u, 1, 0))
        u2 = jnp.where(row == 0, prev[6:7], jnp.where(row == 1, prev[7:8], pltpu.roll(u, 2, 0)))
        conv = cw[2:3] * u + cw[1:2] * u1 + cw[0:1] * u2
        prev = u[sub - 8:]
        ya_ref[0, rs, :] = (bg * conv).astype(BF16)

        ang = freq_ref[...] * pos_ref[0, :, rs]
        cos = jnp.cos(ang)
        sin = jnp.sin(ang)

        def rope_head(t):
            t1 = t[:half]
            t2 = t[half:ROPE_DIM]
            return [t1 * cos - t2 * sin, t2 * cos + t1 * sin, t[ROPE_DIM:]]

        def heads(t, rotate):
            out = []
            for h in range(t.shape[0] // HEAD_DIM):
                th = t[h * HEAD_DIM:(h + 1) * HEAD_DIM]
                out.extend(rope_head(th) if rotate(h) else [th])
            return jnp.concatenate(out, axis=0)

        qt = heads(_dot_nt(wqt_ref[...], xb), lambda h: True)
        qt_ref[0, :, rs] = (qt * (HEAD_DIM ** -0.5 * LOG2E)).astype(BF16)
        kvt = heads(_dot_nt(wkvt_ref[...], xb), lambda h: (h // N_KV_GROUPS) % 2 == 0)
        kvt_ref[0, :, rs] = kvt.astype(BF16)
        gt_ref[0, :, rs] = jax.nn.sigmoid(_dot_nt(wgt_ref[...], xb))
    carry_ref[...] = prev


def _proj(x, pos_row, wc, wqt, wkvt, wgt, cw, freq):
    B, T, D = x.shape
    tm = min(PROJ_TM, T)
    grid = (B, T // tm)
    full = lambda a: pl.BlockSpec(a.shape, lambda b, t: (0,) * a.ndim)
    tok = lambda w: pl.BlockSpec((1, tm, w), lambda b, t: (b, t, 0))
    feat = lambda r: pl.BlockSpec((1, r, tm), lambda b, t: (b, 0, t))
    n_gate = wgt.shape[0]
    return pl.pallas_call(
        _proj_kernel,
        grid=grid,
        in_specs=[tok(D), feat(1), full(wc), full(wqt), full(wkvt), full(wgt), full(cw), full(freq)],
        out_specs=[tok(CONV_CH), feat(NSA_WIDTH), feat(6 * KV_WIDTH), feat(n_gate)],
        out_shape=[jax.ShapeDtypeStruct((B, T, CONV_CH), BF16),
                   jax.ShapeDtypeStruct((B, NSA_WIDTH, T), BF16),
                   jax.ShapeDtypeStruct((B, 6 * KV_WIDTH, T), BF16),
                   jax.ShapeDtypeStruct((B, n_gate, T), F32)],
        scratch_shapes=[pltpu.VMEM((8, CONV_CH), F32)],
        compiler_params=pltpu.CompilerParams(dimension_semantics=("arbitrary", "arbitrary")),
        name="proj",
    )(x, pos_row, wc, wqt, wkvt, wgt, cw, freq)


def _compress_kernel(x_ref, w1_ref, w2_ref, w2t_ref, pos_ref, o_ref, ot_ref):
    xb = x_ref[0, 0]
    nc = xb.shape[0]
    w1 = w1_ref[0]
    w1b = w1.astype(BF16)
    half = CMP_STRIDE * HEAD_DIM
    a = _dot(xb, w1b[:half])
    b = _dot(xb, w1b[half:])
    b_next = pltpu.roll(b, nc - 1, 0)
    pb = _dot(pos_ref[0], w1, precision=lax.Precision.HIGHEST)[0:1]
    h = a + b_next + pb
    g = (0.5 * h * (1.0 + jnp.tanh(np.sqrt(2.0 / np.pi) * (h + 0.044715 * (h * h * h))))).astype(BF16)
    o_ref[0, 0] = _dot(g, w2_ref[0].astype(BF16)).astype(BF16)
    ot_ref[0, 0] = _dot_nt(w2t_ref[0].astype(BF16), g).astype(BF16)


def _compress(xin, w1, w2, w2t, pos):
    _, BG, nc, W = xin.shape
    per = lambda a: pl.BlockSpec((1,) + a.shape[1:], lambda s, i: (s, 0, 0))
    return pl.pallas_call(
        _compress_kernel,
        grid=(2, BG),
        in_specs=[pl.BlockSpec((1, 1, nc, W), lambda s, i: (s, i, 0, 0)), per(w1), per(w2), per(w2t), per(pos)],
        out_specs=[pl.BlockSpec((1, 1, nc, HEAD_DIM), lambda s, i: (s, i, 0, 0)),
                   pl.BlockSpec((1, 1, HEAD_DIM, nc), lambda s, i: (s, i, 0, 0))],
        out_shape=[jax.ShapeDtypeStruct((2, BG, nc, HEAD_DIM), BF16),
                   jax.ShapeDtypeStruct((2, BG, HEAD_DIM, nc), BF16)],
        name="compress",
    )(xin, w1, w2, w2t, pos)


def _head_lanes(qt):
    return jnp.concatenate([qt[r * HEAD_DIM:(r + 1) * HEAD_DIM] for r in range(HEADS_PER_GROUP)], axis=1)


def _gated_out(ot, scale, gates, branch, tq):
    rows = []
    for r in range(HEADS_PER_GROUP):
        c = branch * HEADS_PER_GROUP + r
        sl = slice(r * tq, (r + 1) * tq)
        rows.append(ot[:, sl] * (scale[:, sl] * gates[c:c + 1]))
    return jnp.concatenate(rows, axis=0).astype(BF16)


def _attn_specs(tq):
    G = N_KV_GROUPS
    qspec = pl.BlockSpec((1, GROUP_W, tq), lambda b, g, i: (b, g, i))
    gspec = pl.BlockSpec((1, GATE_ROWS, tq), lambda b, g, i: (b, g, i))
    bg4 = lambda a: pl.BlockSpec((1, 1) + a.shape[2:], lambda b, g, i: (b, g, 0, 0))
    return G, qspec, gspec, bg4


def _cmp_attn_kernel(q_ref, kc_ref, vct_ref, g_ref, ovt_ref, o_ref, sel_ref, score_ref, rank_ref):
    qi = pl.program_id(2)
    tq = q_ref.shape[2]
    q0 = qi * tq
    qt = _head_lanes(q_ref[0])
    s = _dot(kc_ref[0, 0], qt)
    c = lax.broadcasted_iota(I32, s.shape, 0)
    t = q0 + (lax.broadcasted_iota(I32, (1, s.shape[1]), 1) & (tq - 1))
    valid = c <= (t - (CMP_BLOCK - 1)) // CMP_STRIDE
    sm = jnp.where(valid, s, NEG_INF)
    m = jnp.max(sm, axis=0, keepdims=True)
    p = jnp.where(valid, jnp.exp2(sm - m), 0.0)
    l = jnp.sum(p, axis=0, keepdims=True)
    inv = 1.0 / jnp.where(l > 0.0, l, 1.0)
    ot = _dot(vct_ref[0, 0], p.astype(BF16))
    o_ref[0] = _gated_out(ot, inv, g_ref[0], 0, tq)

    pn = p * inv
    psum = pn[:, 0:tq]
    for r in range(1, HEADS_PER_GROUP):
        psum = psum + pn[:, r * tq:(r + 1) * tq]
    imp = _dot(ovt_ref[...], psum, precision=lax.Precision.HIGHEST)
    j = lax.broadcasted_iota(I32, imp.shape, 0)
    cur = (q0 + lax.broadcasted_iota(I32, imp.shape, 1)) // SEL_BLOCK
    valid_b = j <= cur
    forced = (j == 0) | (j == cur) | (j == cur - 1)
    score = jnp.where(valid_b, jnp.where(forced, FORCE_SCORE, imp), NEG_INF)
    sub = SUBLANES
    score_ref[...] = score
    rank_ref[...] = jnp.zeros(rank_ref.shape, I32)
    i_max = (q0 + tq - 1) // SEL_BLOCK
    for i0 in range(0, SEL_LANES, sub):
        @pl.when(i0 <= i_max)
        def _():
            sc = score_ref[...]
            groups = [sc[a:a + sub] for a in range(0, SEL_LANES, sub)]
            jrow = lax.broadcasted_iota(I32, groups[0].shape, 0)
            ranks = [jnp.zeros(g_.shape, I32) for g_ in groups]
            for i in range(i0, i0 + sub):
                si = sc[i:i + 1, :]
                for a, g_ in enumerate(groups):
                    if a > i // sub:
                        inc = jnp.where(si >= g_, 1, 0)
                    elif a < i // sub:
                        inc = jnp.where(si > g_, 1, 0)
                    else:
                        inc = jnp.where(jrow > i % sub, jnp.where(si >= g_, 1, 0), jnp.where(si > g_, 1, 0))
                    ranks[a] = ranks[a] + inc
            rank_ref[...] += jnp.concatenate(ranks, axis=0)
    selected = (rank_ref[...] < N_SELECT) & valid_b
    sel_ref[0, 0] = jnp.where(selected, 0.0, SEL_MASK_BIAS).astype(BF16)


def _cmp_attn(qt, kcmp, vcmp_t, gates, ovt):
    B, _, T = qt.shape
    tq = min(CMP_TQ, T)
    G, qspec, gspec, _ = _attn_specs(tq)
    cspec = lambda a: pl.BlockSpec((1, 1) + a.shape[2:], lambda b, g, i: (0, b * G + g, 0, 0))
    vspec = lambda a: pl.BlockSpec((1, 1) + a.shape[2:], lambda b, g, i: (1, b * G + g, 0, 0))
    return pl.pallas_call(
        _cmp_attn_kernel,
        grid=(B, G, T // tq),
        in_specs=[qspec, cspec(kcmp), vspec(vcmp_t), gspec, pl.BlockSpec(ovt.shape, lambda b, g, i: (0, 0))],
        out_specs=[qspec, pl.BlockSpec((1, 1, SEL_LANES, tq), lambda b, g, i: (b, g, 0, i))],
        out_shape=[jax.ShapeDtypeStruct((B, NSA_WIDTH, T), BF16),
                   jax.ShapeDtypeStruct((B, G, SEL_LANES, T), BF16)],
        scratch_shapes=[pltpu.VMEM((SEL_LANES, tq), F32), pltpu.VMEM((SEL_LANES, tq), I32)],
        name="cmp_attn",
    )(qt, kcmp, vcmp_t, gates, ovt)


def _sel_attn_kernel(q_ref, sel_ref, kt_ref, hot_ref, vt_ref, g_ref, o_ref, *, kc):
    qi = pl.program_id(2)
    tq = q_ref.shape[2]
    q0 = qi * tq
    qt = q_ref[0]
    sb = sel_ref[0, 0]
    qa = jnp.concatenate(
        [jnp.concatenate([qt[r * HEAD_DIM:(r + 1) * HEAD_DIM], sb], axis=0) for r in range(HEADS_PER_GROUP)],
        axis=1)
    gw = qa.shape[1] // SEL_SPLIT
    qas = [qa[:, i * gw:(i + 1) * gw] for i in range(SEL_SPLIT)]

    def step(kstart, kn, carry, diagonal):
        keys = pl.ds(kstart, kn)
        kblk_t = jnp.concatenate([kt_ref[0, :, keys], hot_ref[:, keys]], axis=0)
        vblk = vt_ref[0, :, keys]
        ss = [_dot_tn(kblk_t, qg) for qg in qas]
        out = []
        for gi, (s, (m, l, acc)) in enumerate(zip(ss, carry)):
            if diagonal:
                row = lax.broadcasted_iota(I32, s.shape, 0)
                off = (gi * gw + lax.broadcasted_iota(I32, s.shape, 1)) & (tq - 1)
                s = jnp.where(row <= off, s, NEG_INF)
            mn = jnp.maximum(m, jnp.max(s, axis=0, keepdims=True))
            alpha = jnp.exp2(m - mn)
            p = jnp.exp2(s - mn)
            l = alpha * l + jnp.sum(p, axis=0, keepdims=True)
            acc = alpha * acc + _dot(vblk, p.astype(BF16))
            out.append((mn, l, acc))
        return tuple(out)

    init = tuple((jnp.full((1, gw), NEG_INF, F32), jnp.zeros((1, gw), F32), jnp.zeros((HEAD_DIM, gw), F32))
                 for _ in range(SEL_SPLIT))
    n_full = q0 // kc
    carry = lax.fori_loop(0, n_full, lambda i, cr: step(pl.multiple_of(i * kc, kc), kc, cr, False), init)
    carry = lax.fori_loop(n_full * (kc // tq), qi, lambda i, cr: step(pl.multiple_of(i * tq, tq), tq, cr, False), carry)
    carry = step(pl.multiple_of(q0, tq), tq, carry, True)
    l = jnp.concatenate([c[1] for c in carry], axis=1)
    acc = jnp.concatenate([c[2] for c in carry], axis=1)
    o_ref[0] = _gated_out(acc, 1.0 / l, g_ref[0], 1, tq)


def _sel_attn(qt, sel, kvt, k_row_block, v_row_block, hot_t, gates):
    B, _, T = qt.shape
    tq = min(SEL_TQ, T)
    kc = min(SEL_KC, T)
    G, qspec, gspec, bg4 = _attn_specs(tq)
    return pl.pallas_call(
        functools.partial(_sel_attn_kernel, kc=kc),
        grid=(B, G, T // tq),
        in_specs=[qspec, pl.BlockSpec((1, 1, SEL_LANES, tq), lambda b, g, i: (b, g, 0, i)),
                  pl.BlockSpec((1, HEAD_DIM, T), lambda b, g, i: (b, k_row_block + g, 0)),
                  pl.BlockSpec(hot_t.shape, lambda b, g, i: (0, 0)),
                  pl.BlockSpec((1, HEAD_DIM, T), lambda b, g, i: (b, v_row_block + g, 0)), gspec],
        out_specs=qspec,
        out_shape=jax.ShapeDtypeStruct((B, NSA_WIDTH, T), BF16),
        name="sel_attn",
    )(qt, sel, kvt, hot_t, kvt, gates)


def _win_attn_kernel(q_ref, kt_ref, vt_ref, g_ref, o_ref, *, span, sub):
    qi = pl.program_id(2)
    tq = q_ref.shape[2]
    T = kt_ref.shape[2]

    def tile(s0, interior):
        cs = slice(s0, s0 + sub)
        q0 = qi * tq + s0
        start = pl.multiple_of(jnp.clip(q0 + sub - span, 0, T - span), sub)
        qt = _head_lanes(q_ref[0, :, cs])
        s = _dot_tn(kt_ref[0, :, pl.ds(start, span)], qt)
        if interior:
            row = lax.broadcasted_iota(I32, (sub, s.shape[1]), 0)
            off = lax.broadcasted_iota(I32, (sub, s.shape[1]), 1) & (sub - 1)
            s = jnp.concatenate([jnp.where(row > off, s[:sub], NEG_INF), s[sub:span - sub],
                                 jnp.where(row <= off, s[span - sub:], NEG_INF)], axis=0)
        else:
            kp = start + lax.broadcasted_iota(I32, s.shape, 0)
            t = q0 + (lax.broadcasted_iota(I32, s.shape, 1) & (sub - 1))
            diff = t - kp
            s = jnp.where((diff >= 0) & (diff < WINDOW), s, NEG_INF)
        m = jnp.max(s, axis=0, keepdims=True)
        p = jnp.exp2(s - m)
        l = jnp.sum(p, axis=0, keepdims=True)
        ot = _dot(vt_ref[0, :, pl.ds(start, span)], p.astype(BF16))
        o_ref[0, :, cs] = _gated_out(ot, 1.0 / l, g_ref[0, :, cs], 2, sub)

    first_interior = -(-WINDOW // tq)
    if span == WINDOW + sub:
        @pl.when(qi >= first_interior)
        def _():
            for s0 in range(0, tq, sub):
                tile(s0, True)

    @pl.when((qi < first_interior) | (span != WINDOW + sub))
    def _():
        for s0 in range(0, tq, sub):
            tile(s0, False)


def _win_attn(qt, kvt, k_row_block, v_row_block, gates):
    B, _, T = qt.shape
    tq = min(WIN_TQ, T)
    sub = min(WIN_SUB, T)
    span = min(WINDOW + sub, T)
    G, qspec, gspec, bg4 = _attn_specs(tq)
    return pl.pallas_call(
        functools.partial(_win_attn_kernel, span=span, sub=sub),
        grid=(B, G, T // tq),
        in_specs=[qspec, pl.BlockSpec((1, HEAD_DIM, T), lambda b, g, i: (b, k_row_block + g, 0)),
                  pl.BlockSpec((1, HEAD_DIM, T), lambda b, g, i: (b, v_row_block + g, 0)), gspec],
        out_specs=qspec,
        out_shape=jax.ShapeDtypeStruct((B, NSA_WIDTH, T), BF16),
        name="win_attn",
    )(qt, kvt, kvt, gates)


def _store_row_tiles(ref, x, row0=0):
    rows = x.shape[0]
    for s in range(ROW_SUB):
        ref[pl.ds(row0 * ROW_SUB + s, rows, stride=ROW_SUB), :] = x[:, s * LANES:(s + 1) * LANES]


def _load_row_tiles(ref):
    rows = ref.shape[0] // ROW_SUB
    return jnp.concatenate([ref[pl.ds(s, rows, stride=ROW_SUB), :] for s in range(ROW_SUB)], axis=1)


def _tile_copy(src, si, dst, di, sem):
    return pltpu.make_async_copy(src.at[pl.ds(pl.multiple_of(si * ROW_SUB, ROW_SUB), ROW_SUB), :],
                                 dst.at[pl.ds(pl.multiple_of(di * ROW_SUB, ROW_SUB), ROW_SUB), :], sem)


def _layer_norm(h, g, b):
    mu = jnp.mean(h, axis=-1, keepdims=True)
    c = h - mu
    var = jnp.mean(c * c, axis=-1, keepdims=True)
    return c * lax.rsqrt(var + LN_EPS) * g + b


def _merge_kernel(x_ref, ya_ref, oc_ref, os_ref, ow_ref, wmg_ref, wuc_ref, wun_ref, wo_ref, g1_ref, b1_ref,
                  wr_ref, br_ref, tri_ref, x1_ref, ri_ref, rg_ref, cnt_ref, xs_hbm,
                  carry_ref, stage, dest_v, dest_s, copy_sem, row_sem, *, alpha, cap):
    i = pl.program_id(0)
    last = pl.num_programs(0) - 1
    sub = tri_ref.shape[0]
    tm = x_ref.shape[0]
    slot = i % 2
    nsub = tm // sub

    def id_copy(si):
        return pltpu.make_async_copy(dest_v.at[si], dest_s.at[si], copy_sem.at[si])

    def issue_rows(si, s):
        id_copy(si).wait()
        for t in range(sub):
            for k in range(TOP_K):
                _tile_copy(stage.at[s], si * sub + t, xs_hbm, dest_s[si, k, t], row_sem.at[s]).start(priority=k % 2)

    def wait_rows(s, rows):
        for k in range(TOP_K):
            n = rows * ROW_SUB
            pltpu.make_async_copy(stage.at[s, pl.ds(0, n), :], xs_hbm.at[pl.ds(0, n), :], row_sem.at[s]).wait()

    @pl.when(i == 0)
    def _():
        carry_ref[...] = jnp.zeros_like(carry_ref)
        stage[1, pl.ds((nsub - 1) * sub * ROW_SUB, sub * ROW_SUB), :] = jnp.zeros((sub * ROW_SUB, LANES), F32)
        spare = (N_EXPERTS * cap + lax.broadcasted_iota(I32, (TOP_K, sub), 0) * sub
                 + lax.broadcasted_iota(I32, (TOP_K, sub), 1))
        dest_v[nsub - 1] = spare
        id_copy(nsub - 1).start()

    wr = wr_ref[...]
    wr_hi = wr.astype(BF16)
    wr_hl = jnp.concatenate([wr_hi, (wr - wr_hi.astype(F32)).astype(BF16)], axis=0)
    total = carry_ref[...]
    for si in range(nsub):
        s0 = si * sub
        rs = slice(s0, s0 + sub)
        x = x_ref[rs, :]
        mg = _dot(x.astype(BF16), wmg_ref[...])
        if si == 0:
            issue_rows(nsub - 1, 1 - slot)
        else:
            issue_rows(si - 1, slot)
        y_a = _dot(ya_ref[rs, :], wuc_ref[...])
        o_nsa_t = (oc_ref[0, :, rs].astype(F32) + os_ref[0, :, rs].astype(F32)
                   + ow_ref[0, :, rs].astype(F32))
        y_b = _dot(o_nsa_t.T.astype(BF16), wun_ref[...])
        merged = jax.nn.sigmoid(mg[:, :D_MODEL]) * y_a + jax.nn.sigmoid(mg[:, D_MODEL:]) * y_b
        h = alpha * x + _dot(merged.astype(BF16), wo_ref[...])
        x1 = _layer_norm(h, g1_ref[...], b1_ref[...])
        x1_ref[rs, :] = x1
        _store_row_tiles(stage.at[slot], x1, s0)

        x_hi = x1.astype(BF16)
        x_lo = (x1 - x_hi.astype(F32)).astype(BF16)
        both = _dot_nt(wr_hl, x_hi)
        logits = both[:N_EXPERTS] + both[N_EXPERTS:] + _dot_nt(wr_hi, x_lo) + br_ref[...]
        expert = lax.broadcasted_iota(I32, logits.shape, 0).astype(F32)
        rem = logits
        vals, idxs, hots = [], [], []
        for _ in range(TOP_K):
            m = jnp.max(rem, axis=0, keepdims=True)
            idx = jnp.min(jnp.where(rem == m, expert, float(N_EXPERTS)), axis=0, keepdims=True)
            hot = expert == idx
            vals.append(m)
            idxs.append(idx)
            hots.append(hot)
            rem = jnp.where(hot, TAKEN, rem)
        es = [jnp.exp(v - vals[0]) for v in vals]
        den = es[0]
        for e in es[1:]:
            den = den + e
        chosen = hots[0]
        for hot in hots[1:]:
            chosen = chosen | hot
        chosen_f = jnp.where(chosen, 1.0, 0.0)

        before = _dot(chosen_f.astype(BF16), tri_ref[...]) + total
        total = total + jnp.sum(chosen_f, axis=1, keepdims=True)
        ranks = [jnp.sum(jnp.where(hot, before, 0.0), axis=0, keepdims=True) for hot in hots]
        ri_ref[:, rs] = jnp.concatenate(idxs + ranks, axis=0).astype(I32)
        rg_ref[:, rs] = jnp.concatenate([e / den for e in es] + [jnp.zeros_like(den)] * TOP_K, axis=0)

        dest_v[si] = jnp.concatenate(idxs, axis=0).astype(I32) * cap + jnp.concatenate(ranks, axis=0).astype(I32)
        id_copy(si).start()

    carry_ref[...] = total
    cnt_ref[...] = jnp.broadcast_to(total, cnt_ref.shape).astype(I32)

    @pl.when(i == 0)
    def _():
        wait_rows(1, sub)

    @pl.when(i > 0)
    def _():
        wait_rows(1 - slot, tm)

    @pl.when(i == last)
    def _():
        issue_rows(nsub - 1, slot)
        wait_rows(slot, tm)


def _merge(x, ya, oc, os_, ow, wmg, wuc, wun, wo, g1, b1, wr, br, tri, alpha):
    N, D = x.shape
    B, _, T = oc.shape
    tm = min(MERGE_TM, T)
    nt = T // tm
    tok = lambda w: pl.BlockSpec((tm, w), lambda i: (i, 0))
    feat = pl.BlockSpec((1, NSA_WIDTH, tm), lambda i: (i // nt, 0, i % nt))
    full = lambda a: pl.BlockSpec(a.shape, lambda i: (0,) * a.ndim)
    sub = tri.shape[0]
    return pl.pallas_call(
        functools.partial(_merge_kernel, alpha=alpha, cap=N),
        grid=(N // tm,),
        in_specs=[tok(D), tok(CONV_CH), feat, feat, feat,
                  full(wmg), full(wuc), full(wun), full(wo), full(g1), full(b1), full(wr), full(br), full(tri)],
        out_specs=[tok(D),
                   pl.BlockSpec((2 * TOP_K, tm), lambda i: (0, i)), pl.BlockSpec((2 * TOP_K, tm), lambda i: (0, i)),
                   pl.BlockSpec((N_EXPERTS, LANES), lambda i: (0, 0)), pl.BlockSpec(memory_space=pl.ANY)],
        out_shape=[jax.ShapeDtypeStruct((N, D), F32),
                   jax.ShapeDtypeStruct((2 * TOP_K, N), I32),
                   jax.ShapeDtypeStruct((2 * TOP_K, N), F32),
                   jax.ShapeDtypeStruct((N_EXPERTS, LANES), I32),
                   jax.ShapeDtypeStruct(((N_EXPERTS * N + TOP_K * sub) * ROW_SUB, LANES), F32)],
        scratch_shapes=[pltpu.VMEM((N_EXPERTS, 1), F32), pltpu.VMEM((2, tm * ROW_SUB, LANES), F32),
                        pltpu.VMEM((tm // sub, TOP_K, sub), I32), pltpu.SMEM((tm // sub, TOP_K, sub), I32),
                        pltpu.SemaphoreType.DMA((tm // sub,)), pltpu.SemaphoreType.DMA((2,))],
        compiler_params=pltpu.CompilerParams(dimension_semantics=("arbitrary",)),
        name="merge",
    )(x, ya, oc, os_, ow, wmg, wuc, wun, wo, g1, b1, wr, br, tri)


def _experts_kernel(be_ref, nu_ref, rb_ref, vr_ref, xs_ref, wgu_ref, perm_ref, bg_ref, bl_ref, wd_ref, bd_ref, ys_ref, wg_s, wl_s, wd_s):
    i = pl.program_id(0)

    @pl.when((i == 0) | (be_ref[i] != be_ref[jnp.maximum(i, 1) - 1]))
    def _():
        w = perm_ref.shape[0]
        for c in range(2 * D_FF // w):
            t = _dot(wgu_ref[0, :, c * w:(c + 1) * w].astype(BF16), perm_ref[...])
            wg_s[:, c * (w // 2):(c + 1) * (w // 2)] = t[:, :w // 2].astype(BF16)
            wl_s[:, c * (w // 2):(c + 1) * (w // 2)] = t[:, w // 2:].astype(BF16)
        wd_s[...] = wd_ref[0].astype(BF16)

    @pl.when(i < nu_ref[0])
    def _():
        row = lax.broadcasted_iota(I32, (MOE_BLK, 1), 0)
        xb = jnp.where(row < vr_ref[i], _load_row_tiles(xs_ref), 0.0).astype(BF16)
        x_glu = jnp.minimum(_dot(xb, wg_s[...]) + bg_ref[0], SWIGLU_LIMIT)
        x_lin = jnp.clip(_dot(xb, wl_s[...]) + bl_ref[0], -SWIGLU_LIMIT, SWIGLU_LIMIT)
        act = x_glu * jax.nn.sigmoid(SWIGLU_ALPHA * x_glu) * (x_lin + 1.0)
        _store_row_tiles(ys_ref, _dot(act.astype(BF16), wd_s[...]) + bd_ref[0])


def _experts(block_expert, n_used, row_block, valid_rows, xs, wgu, perm, bg, bl, wd, bd):
    D = D_MODEL
    n_blocks = block_expert.shape[0]
    rows = lambda i, be, nu, rb, vr: (rb[i], 0)
    wsel = lambda i, be, nu, rb, vr: (be[i], 0, 0)
    return pl.pallas_call(
        _experts_kernel,
        grid_spec=pltpu.PrefetchScalarGridSpec(
            num_scalar_prefetch=4,
            grid=(n_blocks,),
            in_specs=[pl.BlockSpec((MOE_BLK * ROW_SUB, LANES), rows),
                      pl.BlockSpec((1, D, 2 * D_FF), wsel),
                      pl.BlockSpec(perm.shape, lambda i, be, nu, rb, vr: (0, 0)),
                      pl.BlockSpec((1, 1, D_FF), wsel), pl.BlockSpec((1, 1, D_FF), wsel),
                      pl.BlockSpec((1, D_FF, D), wsel), pl.BlockSpec((1, 1, D), wsel)],
            out_specs=pl.BlockSpec((MOE_BLK * ROW_SUB, LANES), rows),
            scratch_shapes=[pltpu.VMEM((D, D_FF), BF16), pltpu.VMEM((D, D_FF), BF16), pltpu.VMEM((D_FF, D), BF16)]),
        out_shape=jax.ShapeDtypeStruct(xs.shape, F32),
        compiler_params=pltpu.CompilerParams(dimension_semantics=("arbitrary",)),
        name="experts",
    )(block_expert, n_used, row_block, valid_rows, xs, wgu, perm, bg, bl, wd, bd)


def _combine_kernel(dest_ref, dest_next_ref, ys_hbm, x1_ref, rg_ref, g2_ref, b2_ref, o_ref, buf, sem, *, alpha):
    i = pl.program_id(0)
    last = pl.num_programs(0) - 1
    tm = o_ref.shape[0]
    slot = i % 2
    other = 1 - slot

    def gather(ids_ref, t, to_slot):
        for k in range(TOP_K):
            _tile_copy(ys_hbm, ids_ref[t * TOP_K + k], buf.at[to_slot, k], t, sem.at[to_slot]).start(priority=k % 2)

    def wait_slot(s):
        for k in range(TOP_K):
            pltpu.make_async_copy(ys_hbm.at[pl.ds(0, tm * ROW_SUB), :], buf.at[s, k], sem.at[s]).wait()

    @pl.when(i == 0)
    def _():
        def first(t, c):
            gather(dest_ref, t, slot)
            return c
        lax.fori_loop(0, tm, first, 0)

    wait_slot(slot)

    gs = min(COMB_GROUP, tm)

    def group(g, c):
        r0 = pl.multiple_of(g * gs, gs)
        for j in range(gs):
            gather(dest_next_ref, r0 + j, other)
        rows = pl.ds(r0, gs)
        gate = rg_ref[rows, :]
        y = alpha * x1_ref[rows, :]
        for k in range(TOP_K):
            tiles = buf.at[slot, k, pl.ds(pl.multiple_of(r0 * ROW_SUB, gs * ROW_SUB), gs * ROW_SUB), :]
            y = y + gate[:, k:k + 1] * _load_row_tiles(tiles)
        o_ref[rows, :] = _layer_norm(y, g2_ref[...], b2_ref[...])
        return c

    lax.fori_loop(0, tm // gs, group, 0)

    @pl.when(i == last)
    def _():
        wait_slot(other)


def _combine(dest, ys, x1, rg, g2, b2, alpha):
    N, D = x1.shape
    tm = min(COMB_TM, N)
    n = N // tm
    tok = lambda w: pl.BlockSpec((tm, w), lambda i: (i, 0))
    full = lambda a: pl.BlockSpec(a.shape, lambda i: (0,) * a.ndim)
    ids = lambda f: pl.BlockSpec((tm * TOP_K,), f, memory_space=pltpu.SMEM)
    return pl.pallas_call(
        functools.partial(_combine_kernel, alpha=alpha),
        grid=(n,),
        in_specs=[ids(lambda i: (i,)), ids(lambda i: (jnp.minimum(i + 1, n - 1),)),
                  pl.BlockSpec(memory_space=pl.ANY), tok(D), tok(rg.shape[1]), full(g2), full(b2)],
        out_specs=tok(D),
        out_shape=jax.ShapeDtypeStruct((N, D), F32),
        scratch_shapes=[pltpu.VMEM((2, TOP_K, tm * ROW_SUB, LANES), F32), pltpu.SemaphoreType.DMA((2,))],
        compiler_params=pltpu.CompilerParams(dimension_semantics=("arbitrary",)),
        name="combine",
    )(dest, dest, ys, x1, rg, g2, b2)


def _rope_freq():
    half = ROPE_DIM // 2
    inv = (np.float32(ROPE_THETA) ** (-np.arange(half, dtype=np.float32) * np.float32(2.0 / ROPE_DIM))).astype(np.float32)
    return inv[:, None]


def _overlap_t(T):
    nc = T // CMP_STRIDE
    c0 = np.arange(nc) * CMP_STRIDE
    j0 = np.arange(SEL_LANES) * SEL_BLOCK
    ov = (c0[None, :] < j0[:, None] + SEL_BLOCK) & (c0[None, :] + CMP_BLOCK > j0[:, None])
    ov &= (np.arange(nc) < nc - 1)[None, :] & (np.arange(SEL_LANES) < T // SEL_BLOCK)[:, None]
    return ov.astype(np.float32)


def _deinterleave_perm():
    w = 2 * LANES
    p = np.zeros((w, w), np.float32)
    p[np.arange(0, w, 2), np.arange(w // 2)] = 1.0
    p[np.arange(1, w, 2), w // 2 + np.arange(w // 2)] = 1.0
    return p


def _token_major(t, B, T):
    return t.reshape(B, N_KV_GROUPS, HEAD_DIM, T).transpose(0, 1, 3, 2)


def _layer(x, positions, w_in, conv_w, cmp_pos_k, cmp_w1_k, cmp_w2_k, cmp_pos_v, cmp_w1_v, cmp_w2_v,
           w_up_conv, w_up_nsa, w_o, ln1_g, ln1_b, w_router, b_router, w_gate_up, b_gate_up,
           w_down, b_down, ln2_g, ln2_b, alpha):
    B, T, D = x.shape
    G, R = N_KV_GROUPS, HEADS_PER_GROUP
    N = B * T
    assert D == D_MODEL and T % SEL_BLOCK == 0 and T // SEL_BLOCK <= SEL_LANES

    c0 = 3 * CONV_CH
    c1 = c0 + NSA_WIDTH
    c2 = c1 + 6 * KV_WIDTH
    c3 = c2 + 3 * N_HEADS
    wc = w_in[:, :c0].astype(BF16)
    wqt = w_in[:, c0:c1].T.astype(BF16)
    wkvt = w_in[:, c1:c2].T.astype(BF16)
    gcols = np.zeros((G * GATE_ROWS,), np.int64)
    gmask = np.zeros((G * GATE_ROWS,), np.float32)
    for g in range(G):
        for br in range(3):
            for r in range(R):
                gcols[g * GATE_ROWS + br * R + r] = br * N_HEADS + g * R + r
                gmask[g * GATE_ROWS + br * R + r] = 1.0
    wgt = (w_in[:, c2:c3][:, gcols] * gmask).T.astype(BF16)
    wmg = w_in[:, c3:].astype(BF16)
    cw = conv_w.reshape(CONV_K, CONV_CH)
    pos_row = positions.astype(F32)[:, None, :]

    ya_pre, qt, kvt, gates = _proj(x, pos_row, wc, wqt, wkvt, wgt, cw, jnp.asarray(_rope_freq()))

    nc = T // CMP_STRIDE
    to_chunks = lambda t: _token_major(t, B, T).reshape(B * G, nc, CMP_STRIDE * HEAD_DIM)
    xin = jnp.stack([to_chunks(kvt[:, 0:KV_WIDTH]), to_chunks(kvt[:, KV_WIDTH:2 * KV_WIDTH])])
    w1 = jnp.stack([cmp_w1_k, cmp_w1_v])
    w2 = jnp.stack([cmp_w2_k, cmp_w2_v])
    pos_flat = jnp.stack([cmp_pos_k.reshape(1, -1), cmp_pos_v.reshape(1, -1)])
    pos_flat = jnp.pad(pos_flat, ((0, 0), (0, 7), (0, 0)))
    kcmp, kcmp_t = _compress(xin, w1, w2, w2.transpose(0, 2, 1), pos_flat)

    o_c, sel = _cmp_attn(qt, kcmp, kcmp_t, gates, jnp.asarray(_overlap_t(T)))

    hot_t = (np.arange(SEL_LANES)[:, None] == np.arange(T)[None, :] // SEL_BLOCK).astype(np.float32)
    o_s = _sel_attn(qt, sel, kvt, 2 * G, 3 * G, jnp.asarray(hot_t, BF16), gates)
    o_w = _win_attn(qt, kvt, 4 * G, 5 * G, gates)

    sub = min(MERGE_SUB, T)
    tri = jnp.asarray(np.triu(np.ones((sub, sub), np.float32), 1), BF16)
    assert N % MOE_BLK == 0
    x1, ri, rg, cnt, xs = _merge(x.reshape(N, D), ya_pre.reshape(N, CONV_CH), o_c, o_s, o_w, wmg,
                             w_up_conv.astype(BF16), w_up_nsa.astype(BF16), w_o.astype(BF16),
                             ln1_g[None, :], ln1_b[None, :], w_router.T, b_router[:, None], tri, alpha)

    counts = cnt[:, 0]
    blocks = (counts + MOE_BLK - 1) // MOE_BLK
    bend = jnp.cumsum(blocks).astype(I32)
    bstart = bend - blocks
    n_blocks = (N * TOP_K) // MOE_BLK + N_EXPERTS
    n_used = bend[-1:]
    b = jnp.minimum(jnp.arange(n_blocks, dtype=I32), n_used - 1)
    block_expert = jnp.minimum(jnp.sum((b[:, None] >= bend[None, :]).astype(I32), axis=1), N_EXPERTS - 1)
    first = jnp.sum(jnp.where(block_expert[:, None] == jnp.arange(N_EXPERTS)[None, :], bstart[None, :], 0), axis=1)
    row_block = block_expert * (N // MOE_BLK) + (b - first)
    count = jnp.sum(jnp.where(block_expert[:, None] == jnp.arange(N_EXPERTS)[None, :], counts[None, :], 0), axis=1)
    valid_rows = jnp.clip(count - (b - first) * MOE_BLK, 0, MOE_BLK).astype(I32)
    dest = (ri[:TOP_K] * N + ri[TOP_K:]).T.reshape(N * TOP_K)

    ys = _experts(block_expert, n_used, row_block, valid_rows, xs, w_gate_up, jnp.asarray(_deinterleave_perm(), BF16),
                  b_gate_up[:, None, 0::2], b_gate_up[:, None, 1::2], w_down, b_down[:, None, :])
    out = _combine(dest, ys, x1, rg.T, ln2_g[None, :], ln2_b[None, :], alpha)
    return out.reshape(B, T, D)


def kernel(x, positions, w_in, conv_w, cmp_pos_k, cmp_w1_k, cmp_w2_k, cmp_pos_v, cmp_w1_v, cmp_w2_v, w_up_conv, w_up_nsa, w_o, ln1_g, ln1_b, w_router, b_router, w_gate_up, b_gate_up, w_down, b_down, ln2_g, ln2_b):
    depth = w_in.shape[0]
    alpha = float((2 * depth) ** 0.25)
    h = x
    for l in range(depth):
        h = _layer(h, positions, w_in[l], conv_w[l], cmp_pos_k[l], cmp_w1_k[l], cmp_w2_k[l],
                   cmp_pos_v[l], cmp_w1_v[l], cmp_w2_v[l], w_up_conv[l], w_up_nsa[l], w_o[l],
                   ln1_g[l], ln1_b[l], w_router[l], b_router[l], w_gate_up[l], b_gate_up[l],
                   w_down[l], b_down[l], ln2_g[l], ln2_b[l], alpha)
    return h
```

```python
import functools

import numpy as np
import jax
import jax.numpy as jnp
from jax import lax
from jax.experimental import pallas as pl
from jax.experimental.pallas import tpu as pltpu

F32 = jnp.float32
BF16 = jnp.bfloat16
I32 = jnp.int32

D_MODEL = 1024
CONV_CH = 512
CONV_K = 3
N_HEADS = 8
N_KV_GROUPS = 2
HEADS_PER_GROUP = N_HEADS // N_KV_GROUPS
HEAD_DIM = 64
NSA_WIDTH = N_HEADS * HEAD_DIM
KV_WIDTH = N_KV_GROUPS * HEAD_DIM
ROPE_DIM = HEAD_DIM // 4
ROPE_THETA = 500000.0
CMP_BLOCK = 32
CMP_STRIDE = 16
CMP_HIDDEN = 256
SEL_BLOCK = 64
N_SELECT = 16
WINDOW = 512
N_EXPERTS = 32
TOP_K = 4
D_FF = 1024
SWIGLU_LIMIT = 7.0
SWIGLU_ALPHA = 1.702
LN_EPS = 1e-5
NEG_INF = -1e30
FORCE_SCORE = 1e4
LOG2E = 1.4426950408889634

LANES = 128
SUBLANES = 8
ROW_SUB = D_MODEL // LANES
assert ROW_SUB == SUBLANES
TAKEN = float("-inf")
SEL_LANES = 64
SEL_MASK_BIAS = -32768.0
GROUP_W = HEADS_PER_GROUP * HEAD_DIM
GATE_ROWS = 16

PROJ_TM = 1024
PROJ_SUB = 256
CMP_TQ = 512
SEL_TQ = 512
SEL_KC = 512
SEL_SPLIT = 2
WIN_TQ = 1024
WIN_SUB = 128
MERGE_TM = 512
MERGE_SUB = 256
MOE_BLK = 512
COMB_TM = 512
COMB_GROUP = 128


def _dot(a, b, precision=None):
    return jnp.dot(a, b, precision=precision, preferred_element_type=F32)


def _dot_tn(a, b):
    return lax.dot_general(a, b, (((0,), (0,)), ((), ())), preferred_element_type=F32)


def _dot_nt(a, b, precision=None):
    return lax.dot_general(a, b, (((1,), (1,)), ((), ())), precision=precision, preferred_element_type=F32)


def _proj_kernel(x_ref, pos_ref, wc_ref, wqt_ref, wkvt_ref, wgt_ref, cw_ref, freq_ref,
                 ya_ref, qt_ref, kvt_ref, gt_ref, carry_ref):
    ti = pl.program_id(1)
    tm = x_ref.shape[1]
    sub = min(PROJ_SUB, tm)
    half = ROPE_DIM // 2

    @pl.when(ti == 0)
    def _():
        carry_ref[...] = jnp.zeros_like(carry_ref)

    prev = carry_ref[...]
    cw = cw_ref[...]
    for s0 in range(0, tm, sub):
        rs = slice(s0, s0 + sub)
        xb = x_ref[0, rs, :].astype(BF16)

        pc = _dot(xb, wc_ref[...])
        xv = pc[:, :CONV_CH]
        bg = pc[:, CONV_CH:2 * CONV_CH]
        cg = pc[:, 2 * CONV_CH:]
        u = cg * xv
        row = lax.broadcasted_iota(I32, u.shape, 0)
        u1 = jnp.where(row == 0, prev[7:8], pltpu.roll(u, 1, 0))
        u2 = jnp.where(row == 0, prev[6:7], jnp.where(row == 1, prev[7:8], pltpu.roll(u, 2, 0)))
        conv = cw[2:3] * u + cw[1:2] * u1 + cw[0:1] * u2
        prev = u[sub - 8:]
        ya_ref[0, rs, :] = (bg * conv).astype(BF16)

        ang = freq_ref[...] * pos_ref[0, :, rs]
        cos = jnp.cos(ang)
        sin = jnp.sin(ang)

        def rope_head(t):
            t1 = t[:half]
            t2 = t[half:ROPE_DIM]
            return [t1 * cos - t2 * sin, t2 * cos + t1 * sin, t[ROPE_DIM:]]

        def heads(t, rotate):
            out = []
            for h in range(t.shape[0] // HEAD_DIM):
                th = t[h * HEAD_DIM:(h + 1) * HEAD_DIM]
                out.extend(rope_head(th) if rotate(h) else [th])
            return jnp.concatenate(out, axis=0)

        qt = heads(_dot_nt(wqt_ref[...], xb), lambda h: True)
        qt_ref[0, :, rs] = (qt * (HEAD_DIM ** -0.5 * LOG2E)).astype(BF16)
        kvt = heads(_dot_nt(wkvt_ref[...], xb), lambda h: (h // N_KV_GROUPS) % 2 == 0)
        kvt_ref[0, :, rs] = kvt.astype(BF16)
        gt_ref[0, :, rs] = jax.nn.sigmoid(_dot_nt(wgt_ref[...], xb))
    carry_ref[...] = prev


def _proj(x, pos_row, wc, wqt, wkvt, wgt, cw, freq):
    B, T, D = x.shape
    tm = min(PROJ_TM, T)
    grid = (B, T // tm)
    full = lambda a: pl.BlockSpec(a.shape, lambda b, t: (0,) * a.ndim)
    tok = lambda w: pl.BlockSpec((1, tm, w), lambda b, t: (b, t, 0))
    feat = lambda r: pl.BlockSpec((1, r, tm), lambda b, t: (b, 0, t))
    n_gate = wgt.shape[0]
    return pl.pallas_call(
        _proj_kernel,
        grid=grid,
        in_specs=[tok(D), feat(1), full(wc), full(wqt), full(wkvt), full(wgt), full(cw), full(freq)],
        out_specs=[tok(CONV_CH), feat(NSA_WIDTH), feat(6 * KV_WIDTH), feat(n_gate)],
        out_shape=[jax.ShapeDtypeStruct((B, T, CONV_CH), BF16),
                   jax.ShapeDtypeStruct((B, NSA_WIDTH, T), BF16),
                   jax.ShapeDtypeStruct((B, 6 * KV_WIDTH, T), BF16),
                   jax.ShapeDtypeStruct((B, n_gate, T), F32)],
        scratch_shapes=[pltpu.VMEM((8, CONV_CH), F32)],
        compiler_params=pltpu.CompilerParams(dimension_semantics=("arbitrary", "arbitrary")),
        name="proj",
    )(x, pos_row, wc, wqt, wkvt, wgt, cw, freq)


def _compress_kernel(x_ref, w1_ref, w2_ref, w2t_ref, pos_ref, o_ref, ot_ref):
    xb = x_ref[0, 0]
    nc = xb.shape[0]
    w1 = w1_ref[0]
    w1b = w1.astype(BF16)
    half = CMP_STRIDE * HEAD_DIM
    a = _dot(xb, w1b[:half])
    b = _dot(xb, w1b[half:])
    b_next = pltpu.roll(b, nc - 1, 0)
    pb = _dot(pos_ref[0], w1, precision=lax.Precision.HIGHEST)[0:1]
    h = a + b_next + pb
    g = (0.5 * h * (1.0 + jnp.tanh(np.sqrt(2.0 / np.pi) * (h + 0.044715 * (h * h * h))))).astype(BF16)
    o_ref[0, 0] = _dot(g, w2_ref[0].astype(BF16)).astype(BF16)
    ot_ref[0, 0] = _dot_nt(w2t_ref[0].astype(BF16), g).astype(BF16)


def _compress(xin, w1, w2, w2t, pos):
    _, BG, nc, W = xin.shape
    per = lambda a: pl.BlockSpec((1,) + a.shape[1:], lambda s, i: (s, 0, 0))
    return pl.pallas_call(
        _compress_kernel,
        grid=(2, BG),
        in_specs=[pl.BlockSpec((1, 1, nc, W), lambda s, i: (s, i, 0, 0)), per(w1), per(w2), per(w2t), per(pos)],
        out_specs=[pl.BlockSpec((1, 1, nc, HEAD_DIM), lambda s, i: (s, i, 0, 0)),
                   pl.BlockSpec((1, 1, HEAD_DIM, nc), lambda s, i: (s, i, 0, 0))],
        out_shape=[jax.ShapeDtypeStruct((2, BG, nc, HEAD_DIM), BF16),
                   jax.ShapeDtypeStruct((2, BG, HEAD_DIM, nc), BF16)],
        name="compress",
    )(xin, w1, w2, w2t, pos)


def _head_lanes(qt):
    return jnp.concatenate([qt[r * HEAD_DIM:(r + 1) * HEAD_DIM] for r in range(HEADS_PER_GROUP)], axis=1)


def _gated_out(ot, scale, gates, branch, tq):
    rows = []
    for r in range(HEADS_PER_GROUP):
        c = branch * HEADS_PER_GROUP + r
        sl = slice(r * tq, (r + 1) * tq)
        rows.append(ot[:, sl] * (scale[:, sl] * gates[c:c + 1]))
    return jnp.concatenate(rows, axis=0).astype(BF16)


def _attn_specs(tq):
    G = N_KV_GROUPS
    qspec = pl.BlockSpec((1, GROUP_W, tq), lambda b, g, i: (b, g, i))
    gspec = pl.BlockSpec((1, GATE_ROWS, tq), lambda b, g, i: (b, g, i))
    bg4 = lambda a: pl.BlockSpec((1, 1) + a.shape[2:], lambda b, g, i: (b, g, 0, 0))
    return G, qspec, gspec, bg4


def _cmp_attn_kernel(q_ref, kc_ref, vct_ref, g_ref, ovt_ref, o_ref, sel_ref, score_ref, rank_ref):
    qi = pl.program_id(2)
    tq = q_ref.shape[2]
    q0 = qi * tq
    qt = _head_lanes(q_ref[0])
    s = _dot(kc_ref[0, 0], qt)
    c = lax.broadcasted_iota(I32, s.shape, 0)
    t = q0 + (lax.broadcasted_iota(I32, (1, s.shape[1]), 1) & (tq - 1))
    valid = c <= (t - (CMP_BLOCK - 1)) // CMP_STRIDE
    sm = jnp.where(valid, s, NEG_INF)
    m = jnp.max(sm, axis=0, keepdims=True)
    p = jnp.where(valid, jnp.exp2(sm - m), 0.0)
    l = jnp.sum(p, axis=0, keepdims=True)
    inv = 1.0 / jnp.where(l > 0.0, l, 1.0)
    ot = _dot(vct_ref[0, 0], p.astype(BF16))
    o_ref[0] = _gated_out(ot, inv, g_ref[0], 0, tq)

    pn = p * inv
    psum = pn[:, 0:tq]
    for r in range(1, HEADS_PER_GROUP):
        psum = psum + pn[:, r * tq:(r + 1) * tq]
    imp = _dot(ovt_ref[...], psum, precision=lax.Precision.HIGHEST)
    j = lax.broadcasted_iota(I32, imp.shape, 0)
    cur = (q0 + lax.broadcasted_iota(I32, imp.shape, 1)) // SEL_BLOCK
    valid_b = j <= cur
    forced = (j == 0) | (j == cur) | (j == cur - 1)
    score = jnp.where(valid_b, jnp.where(forced, FORCE_SCORE, imp), NEG_INF)
    sub = SUBLANES
    score_ref[...] = score
    rank_ref[...] = jnp.zeros(rank_ref.shape, I32)
    i_max = (q0 + tq - 1) // SEL_BLOCK
    for i0 in range(0, SEL_LANES, sub):
        @pl.when(i0 <= i_max)
        def _():
            sc = score_ref[...]
            groups = [sc[a:a + sub] for a in range(0, SEL_LANES, sub)]
            jrow = lax.broadcasted_iota(I32, groups[0].shape, 0)
            ranks = [jnp.zeros(g_.shape, I32) for g_ in groups]
            for i in range(i0, i0 + sub):
                si = sc[i:i + 1, :]
                for a, g_ in enumerate(groups):
                    if a > i // sub:
                        inc = jnp.where(si >= g_, 1, 0)
                    elif a < i // sub:
                        inc = jnp.where(si > g_, 1, 0)
                    else:
                        inc = jnp.where(jrow > i % sub, jnp.where(si >= g_, 1, 0), jnp.where(si > g_, 1, 0))
                    ranks[a] = ranks[a] + inc
            rank_ref[...] += jnp.concatenate(ranks, axis=0)
    selected = (rank_ref[...] < N_SELECT) & valid_b
    sel_ref[0, 0] = jnp.where(selected, 0.0, SEL_MASK_BIAS).astype(BF16)


def _cmp_attn(qt, kcmp, vcmp_t, gates, ovt):
    B, _, T = qt.shape
    tq = min(CMP_TQ, T)
    G, qspec, gspec, _ = _attn_specs(tq)
    cspec = lambda a: pl.BlockSpec((1, 1) + a.shape[2:], lambda b, g, i: (0, b * G + g, 0, 0))
    vspec = lambda a: pl.BlockSpec((1, 1) + a.shape[2:], lambda b, g, i: (1, b * G + g, 0, 0))
    return pl.pallas_call(
        _cmp_attn_kernel,
        grid=(B, G, T // tq),
        in_specs=[qspec, cspec(kcmp), vspec(vcmp_t), gspec, pl.BlockSpec(ovt.shape, lambda b, g, i: (0, 0))],
        out_specs=[qspec, pl.BlockSpec((1, 1, SEL_LANES, tq), lambda b, g, i: (b, g, 0, i))],
        out_shape=[jax.ShapeDtypeStruct((B, NSA_WIDTH, T), BF16),
                   jax.ShapeDtypeStruct((B, G, SEL_LANES, T), BF16)],
        scratch_shapes=[pltpu.VMEM((SEL_LANES, tq), F32), pltpu.VMEM((SEL_LANES, tq), I32)],
        name="cmp_attn",
    )(qt, kcmp, vcmp_t, gates, ovt)


def _sel_attn_kernel(q_ref, sel_ref, kt_ref, hot_ref, vt_ref, g_ref, o_ref, *, kc):
    qi = pl.program_id(2)
    tq = q_ref.shape[2]
    q0 = qi * tq
    qt = q_ref[0]
    sb = sel_ref[0, 0]
    qa = jnp.concatenate(
        [jnp.concatenate([qt[r * HEAD_DIM:(r + 1) * HEAD_DIM], sb], axis=0) for r in range(HEADS_PER_GROUP)],
        axis=1)
    gw = qa.shape[1] // SEL_SPLIT
    qas = [qa[:, i * gw:(i + 1) * gw] for i in range(SEL_SPLIT)]

    def step(kstart, kn, carry, diagonal):
        keys = pl.ds(kstart, kn)
        kblk_t = jnp.concatenate([kt_ref[0, :, keys], hot_ref[:, keys]], axis=0)
        vblk = vt_ref[0, :, keys]
        ss = [_dot_tn(kblk_t, qg) for qg in qas]
        out = []
        for gi, (s, (m, l, acc)) in enumerate(zip(ss, carry)):
            if diagonal:
                row = lax.broadcasted_iota(I32, s.shape, 0)
                off = (gi * gw + lax.broadcasted_iota(I32, s.shape, 1)) & (tq - 1)
                s = jnp.where(row <= off, s, NEG_INF)
            mn = jnp.maximum(m, jnp.max(s, axis=0, keepdims=True))
            alpha = jnp.exp2(m - mn)
            p = jnp.exp2(s - mn)
            l = alpha * l + jnp.sum(p, axis=0, keepdims=True)
            acc = alpha * acc + _dot(vblk, p.astype(BF16))
            out.append((mn, l, acc))
        return tuple(out)

    init = tuple((jnp.full((1, gw), NEG_INF, F32), jnp.zeros((1, gw), F32), jnp.zeros((HEAD_DIM, gw), F32))
                 for _ in range(SEL_SPLIT))
    n_full = q0 // kc
    carry = lax.fori_loop(0, n_full, lambda i, cr: step(pl.multiple_of(i * kc, kc), kc, cr, False), init)
    carry = lax.fori_loop(n_full * (kc // tq), qi, lambda i, cr: step(pl.multiple_of(i * tq, tq), tq, cr, False), carry)
    carry = step(pl.multiple_of(q0, tq), tq, carry, True)
    l = jnp.concatenate([c[1] for c in carry], axis=1)
    acc = jnp.concatenate([c[2] for c in carry], axis=1)
    o_ref[0] = _gated_out(acc, 1.0 / l, g_ref[0], 1, tq)


def _sel_attn(qt, sel, kvt, k_row_block, v_row_block, hot_t, gates):
    B, _, T = qt.shape
    tq = min(SEL_TQ, T)
    kc = min(SEL_KC, T)
    G, qspec, gspec, bg4 = _attn_specs(tq)
    return pl.pallas_call(
        functools.partial(_sel_attn_kernel, kc=kc),
        grid=(B, G, T // tq),
        in_specs=[qspec, pl.BlockSpec((1, 1, SEL_LANES, tq), lambda b, g, i: (b, g, 0, i)),
                  pl.BlockSpec((1, HEAD_DIM, T), lambda b, g, i: (b, k_row_block + g, 0)),
                  pl.BlockSpec(hot_t.shape, lambda b, g, i: (0, 0)),
                  pl.BlockSpec((1, HEAD_DIM, T), lambda b, g, i: (b, v_row_block + g, 0)), gspec],
        out_specs=qspec,
        out_shape=jax.ShapeDtypeStruct((B, NSA_WIDTH, T), BF16),
        name="sel_attn",
    )(qt, sel, kvt, hot_t, kvt, gates)


def _win_attn_kernel(q_ref, kt_ref, vt_ref, g_ref, o_ref, *, span, sub):
    qi = pl.program_id(2)
    tq = q_ref.shape[2]
    T = kt_ref.shape[2]

    def tile(s0, interior):
        cs = slice(s0, s0 + sub)
        q0 = qi * tq + s0
        start = pl.multiple_of(jnp.clip(q0 + sub - span, 0, T - span), sub)
        qt = _head_lanes(q_ref[0, :, cs])
        s = _dot_tn(kt_ref[0, :, pl.ds(start, span)], qt)
        if interior:
            row = lax.broadcasted_iota(I32, (sub, s.shape[1]), 0)
            off = lax.broadcasted_iota(I32, (sub, s.shape[1]), 1) & (sub - 1)
            s = jnp.concatenate([jnp.where(row > off, s[:sub], NEG_INF), s[sub:span - sub],
                                 jnp.where(row <= off, s[span - sub:], NEG_INF)], axis=0)
        else:
            kp = start + lax.broadcasted_iota(I32, s.shape, 0)
            t = q0 + (lax.broadcasted_iota(I32, s.shape, 1) & (sub - 1))
            diff = t - kp
            s = jnp.where((diff >= 0) & (diff < WINDOW), s, NEG_INF)
        m = jnp.max(s, axis=0, keepdims=True)
        p = jnp.exp2(s - m)
        l = jnp.sum(p, axis=0, keepdims=True)
        ot = _dot(vt_ref[0, :, pl.ds(start, span)], p.astype(BF16))
        o_ref[0, :, cs] = _gated_out(ot, 1.0 / l, g_ref[0, :, cs], 2, sub)

    first_interior = -(-WINDOW // tq)
    if span == WINDOW + sub:
        @pl.when(qi >= first_interior)
        def _():
            for s0 in range(0, tq, sub):
                tile(s0, True)

    @pl.when((qi < first_interior) | (span != WINDOW + sub))
    def _():
        for s0 in range(0, tq, sub):
            tile(s0, first_interior == 1 and s0 >= WINDOW and span == WINDOW + sub)


def _win_attn(qt, kvt, k_row_block, v_row_block, gates):
    B, _, T = qt.shape
    tq = min(WIN_TQ, T)
    sub = min(WIN_SUB, T)
    span = min(WINDOW + sub, T)
    G, qspec, gspec, bg4 = _attn_specs(tq)
    return pl.pallas_call(
        functools.partial(_win_attn_kernel, span=span, sub=sub),
        grid=(B, G, T // tq),
        in_specs=[qspec, pl.BlockSpec((1, HEAD_DIM, T), lambda b, g, i: (b, k_row_block + g, 0)),
                  pl.BlockSpec((1, HEAD_DIM, T), lambda b, g, i: (b, v_row_block + g, 0)), gspec],
        out_specs=qspec,
        out_shape=jax.ShapeDtypeStruct((B, NSA_WIDTH, T), BF16),
        name="win_attn",
    )(qt, kvt, kvt, gates)


def _store_row_tiles(ref, x, row0=0):
    rows = x.shape[0]
    for s in range(ROW_SUB):
        ref[pl.ds(row0 * ROW_SUB + s, rows, stride=ROW_SUB), :] = x[:, s * LANES:(s + 1) * LANES]


def _load_row_tiles(ref):
    rows = ref.shape[0] // ROW_SUB
    return jnp.concatenate([ref[pl.ds(s, rows, stride=ROW_SUB), :] for s in range(ROW_SUB)], axis=1)


def _tile_copy(src, si, dst, di, sem):
    return pltpu.make_async_copy(src.at[pl.ds(pl.multiple_of(si * ROW_SUB, ROW_SUB), ROW_SUB), :],
                                 dst.at[pl.ds(pl.multiple_of(di * ROW_SUB, ROW_SUB), ROW_SUB), :], sem)


def _layer_norm(h, g, b):
    mu = jnp.mean(h, axis=-1, keepdims=True)
    c = h - mu
    var = jnp.mean(c * c, axis=-1, keepdims=True)
    return c * lax.rsqrt(var + LN_EPS) * g + b


def _merge_kernel(x_ref, ya_ref, oc_ref, os_ref, ow_ref, wmg_ref, wuc_ref, wun_ref, wo_ref, g1_ref, b1_ref,
                  wr_ref, br_ref, tri_ref, x1_ref, ri_ref, rg_ref, cnt_ref, xs_hbm,
                  carry_ref, stage, dest_v, dest_s, copy_sem, row_sem, *, alpha, cap):
    i = pl.program_id(0)
    last = pl.num_programs(0) - 1
    sub = tri_ref.shape[0]
    tm = x_ref.shape[0]
    slot = i % 2
    nsub = tm // sub

    def id_copy(si):
        return pltpu.make_async_copy(dest_v.at[si], dest_s.at[si], copy_sem.at[si])

    def issue_rows(si, s):
        id_copy(si).wait()
        for t in range(sub):
            for k in range(TOP_K):
                _tile_copy(stage.at[s], si * sub + t, xs_hbm, dest_s[si, k, t], row_sem.at[s]).start(priority=k % 2)

    def wait_rows(s, rows):
        for k in range(TOP_K):
            n = rows * ROW_SUB
            pltpu.make_async_copy(stage.at[s, pl.ds(0, n), :], xs_hbm.at[pl.ds(0, n), :], row_sem.at[s]).wait()

    @pl.when(i == 0)
    def _():
        carry_ref[...] = jnp.zeros_like(carry_ref)
        stage[1, pl.ds((nsub - 1) * sub * ROW_SUB, sub * ROW_SUB), :] = jnp.zeros((sub * ROW_SUB, LANES), F32)
        spare = (N_EXPERTS * cap + lax.broadcasted_iota(I32, (TOP_K, sub), 0) * sub
                 + lax.broadcasted_iota(I32, (TOP_K, sub), 1))
        dest_v[nsub - 1] = spare
        id_copy(nsub - 1).start()

    wr = wr_ref[...]
    wr_hi = wr.astype(BF16)
    wr_hl = jnp.concatenate([wr_hi, (wr - wr_hi.astype(F32)).astype(BF16)], axis=0)
    total = carry_ref[...]
    for si in range(nsub):
        s0 = si * sub
        rs = slice(s0, s0 + sub)
        x = x_ref[rs, :]
        mg = _dot(x.astype(BF16), wmg_ref[...])
        if si == 0:
            issue_rows(nsub - 1, 1 - slot)
        else:
            issue_rows(si - 1, slot)
        y_a = _dot(ya_ref[rs, :], wuc_ref[...])
        o_nsa_t = (oc_ref[0, :, rs].astype(F32) + os_ref[0, :, rs].astype(F32)
                   + ow_ref[0, :, rs].astype(F32))
        y_b = _dot(o_nsa_t.T.astype(BF16), wun_ref[...])
        merged = jax.nn.sigmoid(mg[:, :D_MODEL]) * y_a + jax.nn.sigmoid(mg[:, D_MODEL:]) * y_b
        h = alpha * x + _dot(merged.astype(BF16), wo_ref[...])
        x1 = _layer_norm(h, g1_ref[...], b1_ref[...])
        x1_ref[rs, :] = x1
        _store_row_tiles(stage.at[slot], x1, s0)

        x_hi = x1.astype(BF16)
        x_lo = (x1 - x_hi.astype(F32)).astype(BF16)
        both = _dot_nt(wr_hl, x_hi)
        logits = both[:N_EXPERTS] + both[N_EXPERTS:] + _dot_nt(wr_hi, x_lo) + br_ref[...]
        expert = lax.broadcasted_iota(I32, logits.shape, 0).astype(F32)
        rem = logits
        vals, idxs, hots = [], [], []
        for _ in range(TOP_K):
            m = jnp.max(rem, axis=0, keepdims=True)
            idx = jnp.min(jnp.where(rem == m, expert, float(N_EXPERTS)), axis=0, keepdims=True)
            hot = expert == idx
            vals.append(m)
            idxs.append(idx)
            hots.append(hot)
            rem = jnp.where(hot, TAKEN, rem)
        es = [jnp.exp(v - vals[0]) for v in vals]
        den = es[0]
        for e in es[1:]:
            den = den + e
        chosen = hots[0]
        for hot in hots[1:]:
            chosen = chosen | hot
        chosen_f = jnp.where(chosen, 1.0, 0.0)

        before = _dot(chosen_f.astype(BF16), tri_ref[...]) + total
        total = total + jnp.sum(chosen_f, axis=1, keepdims=True)
        ranks = [jnp.sum(jnp.where(hot, before, 0.0), axis=0, keepdims=True) for hot in hots]
        ri_ref[:, rs] = jnp.concatenate(idxs + ranks, axis=0).astype(I32)
        rg_ref[:, rs] = jnp.concatenate([e / den for e in es] + [jnp.zeros_like(den)] * TOP_K, axis=0)

        dest_v[si] = jnp.concatenate(idxs, axis=0).astype(I32) * cap + jnp.concatenate(ranks, axis=0).astype(I32)
        id_copy(si).start()

    carry_ref[...] = total
    cnt_ref[...] = jnp.broadcast_to(total, cnt_ref.shape).astype(I32)

    @pl.when(i == 0)
    def _():
        wait_rows(1, sub)

    @pl.when(i > 0)
    def _():
        wait_rows(1 - slot, tm)

    @pl.when(i == last)
    def _():
        issue_rows(nsub - 1, slot)
        wait_rows(slot, tm)


def _merge(x, ya, oc, os_, ow, wmg, wuc, wun, wo, g1, b1, wr, br, tri, alpha):
    N, D = x.shape
    B, _, T = oc.shape
    tm = min(MERGE_TM, T)
    nt = T // tm
    tok = lambda w: pl.BlockSpec((tm, w), lambda i: (i, 0))
    feat = pl.BlockSpec((1, NSA_WIDTH, tm), lambda i: (i // nt, 0, i % nt))
    full = lambda a: pl.BlockSpec(a.shape, lambda i: (0,) * a.ndim)
    sub = tri.shape[0]
    return pl.pallas_call(
        functools.partial(_merge_kernel, alpha=alpha, cap=N),
        grid=(N // tm,),
        in_specs=[tok(D), tok(CONV_CH), feat, feat, feat,
                  full(wmg), full(wuc), full(wun), full(wo), full(g1), full(b1), full(wr), full(br), full(tri)],
        out_specs=[tok(D),
                   pl.BlockSpec((2 * TOP_K, tm), lambda i: (0, i)), pl.BlockSpec((2 * TOP_K, tm), lambda i: (0, i)),
                   pl.BlockSpec((N_EXPERTS, LANES), lambda i: (0, 0)), pl.BlockSpec(memory_space=pl.ANY)],
        out_shape=[jax.ShapeDtypeStruct((N, D), F32),
                   jax.ShapeDtypeStruct((2 * TOP_K, N), I32),
                   jax.ShapeDtypeStruct((2 * TOP_K, N), F32),
                   jax.ShapeDtypeStruct((N_EXPERTS, LANES), I32),
                   jax.ShapeDtypeStruct(((N_EXPERTS * N + TOP_K * sub) * ROW_SUB, LANES), F32)],
        scratch_shapes=[pltpu.VMEM((N_EXPERTS, 1), F32), pltpu.VMEM((2, tm * ROW_SUB, LANES), F32),
                        pltpu.VMEM((tm // sub, TOP_K, sub), I32), pltpu.SMEM((tm // sub, TOP_K, sub), I32),
                        pltpu.SemaphoreType.DMA((tm // sub,)), pltpu.SemaphoreType.DMA((2,))],
        compiler_params=pltpu.CompilerParams(dimension_semantics=("arbitrary",)),
        name="merge",
    )(x, ya, oc, os_, ow, wmg, wuc, wun, wo, g1, b1, wr, br, tri)


def _experts_kernel(be_ref, nu_ref, rb_ref, vr_ref, xs_ref, wgu_ref, perm_ref, bg_ref, bl_ref, wd_ref, bd_ref, ys_ref, wg_s, wl_s, wd_s):
    i = pl.program_id(0)

    @pl.when((i == 0) | (be_ref[i] != be_ref[jnp.maximum(i, 1) - 1]))
    def _():
        w = perm_ref.shape[0]
        for c in range(2 * D_FF // w):
            t = _dot(wgu_ref[0, :, c * w:(c + 1) * w].astype(BF16), perm_ref[...])
            wg_s[:, c * (w // 2):(c + 1) * (w // 2)] = t[:, :w // 2].astype(BF16)
            wl_s[:, c * (w // 2):(c + 1) * (w // 2)] = t[:, w // 2:].astype(BF16)
        wd_s[...] = wd_ref[0].astype(BF16)

    @pl.when(i < nu_ref[0])
    def _():
        row = lax.broadcasted_iota(I32, (MOE_BLK, 1), 0)
        xb = jnp.where(row < vr_ref[i], _load_row_tiles(xs_ref), 0.0).astype(BF16)
        x_glu = jnp.minimum(_dot(xb, wg_s[...]) + bg_ref[0], SWIGLU_LIMIT)
        x_lin = jnp.clip(_dot(xb, wl_s[...]) + bl_ref[0], -SWIGLU_LIMIT, SWIGLU_LIMIT)
        act = x_glu * jax.nn.sigmoid(SWIGLU_ALPHA * x_glu) * (x_lin + 1.0)
        _store_row_tiles(ys_ref, _dot(act.astype(BF16), wd_s[...]) + bd_ref[0])


def _experts(block_expert, n_used, row_block, valid_rows, xs, wgu, perm, bg, bl, wd, bd):
    D = D_MODEL
    n_blocks = block_expert.shape[0]
    rows = lambda i, be, nu, rb, vr: (rb[i], 0)
    wsel = lambda i, be, nu, rb, vr: (be[i], 0, 0)
    return pl.pallas_call(
        _experts_kernel,
        grid_spec=pltpu.PrefetchScalarGridSpec(
            num_scalar_prefetch=4,
            grid=(n_blocks,),
            in_specs=[pl.BlockSpec((MOE_BLK * ROW_SUB, LANES), rows),
                      pl.BlockSpec((1, D, 2 * D_FF), wsel),
                      pl.BlockSpec(perm.shape, lambda i, be, nu, rb, vr: (0, 0)),
                      pl.BlockSpec((1, 1, D_FF), wsel), pl.BlockSpec((1, 1, D_FF), wsel),
                      pl.BlockSpec((1, D_FF, D), wsel), pl.BlockSpec((1, 1, D), wsel)],
            out_specs=pl.BlockSpec((MOE_BLK * ROW_SUB, LANES), rows),
            scratch_shapes=[pltpu.VMEM((D, D_FF), BF16), pltpu.VMEM((D, D_FF), BF16), pltpu.VMEM((D_FF, D), BF16)]),
        out_shape=jax.ShapeDtypeStruct(xs.shape, F32),
        compiler_params=pltpu.CompilerParams(dimension_semantics=("arbitrary",)),
        name="experts",
    )(block_expert, n_used, row_block, valid_rows, xs, wgu, perm, bg, bl, wd, bd)


def _combine_kernel(dest_ref, dest_next_ref, ys_hbm, x1_ref, rg_ref, g2_ref, b2_ref, o_ref, buf, sem, *, alpha):
    i = pl.program_id(0)
    last = pl.num_programs(0) - 1
    tm = o_ref.shape[0]
    slot = i % 2
    other = 1 - slot

    def gather(ids_ref, t, to_slot):
        for k in range(TOP_K):
            _tile_copy(ys_hbm, ids_ref[t * TOP_K + k], buf.at[to_slot, k], t, sem.at[to_slot]).start(priority=k % 2)

    def wait_slot(s):
        for k in range(TOP_K):
            pltpu.make_async_copy(ys_hbm.at[pl.ds(0, tm * ROW_SUB), :], buf.at[s, k], sem.at[s]).wait()

    @pl.when(i == 0)
    def _():
        def first(t, c):
            gather(dest_ref, t, slot)
            return c
        lax.fori_loop(0, tm, first, 0)

    wait_slot(slot)

    gs = min(COMB_GROUP, tm)

    def group(g, c):
        r0 = pl.multiple_of(g * gs, gs)
        for j in range(gs):
            gather(dest_next_ref, r0 + j, other)
        rows = pl.ds(r0, gs)
        gate = rg_ref[rows, :]
        y = alpha * x1_ref[rows, :]
        for k in range(TOP_K):
            tiles = buf.at[slot, k, pl.ds(pl.multiple_of(r0 * ROW_SUB, gs * ROW_SUB), gs * ROW_SUB), :]
            y = y + gate[:, k:k + 1] * _load_row_tiles(tiles)
        o_ref[rows, :] = _layer_norm(y, g2_ref[...], b2_ref[...])
        return c

    lax.fori_loop(0, tm // gs, group, 0)

    @pl.when(i == last)
    def _():
        wait_slot(other)


def _combine(dest, ys, x1, rg, g2, b2, alpha):
    N, D = x1.shape
    tm = min(COMB_TM, N)
    n = N // tm
    tok = lambda w: pl.BlockSpec((tm, w), lambda i: (i, 0))
    full = lambda a: pl.BlockSpec(a.shape, lambda i: (0,) * a.ndim)
    ids = lambda f: pl.BlockSpec((tm * TOP_K,), f, memory_space=pltpu.SMEM)
    return pl.pallas_call(
        functools.partial(_combine_kernel, alpha=alpha),
        grid=(n,),
        in_specs=[ids(lambda i: (i,)), ids(lambda i: (jnp.minimum(i + 1, n - 1),)),
                  pl.BlockSpec(memory_space=pl.ANY), tok(D), tok(rg.shape[1]), full(g2), full(b2)],
        out_specs=tok(D),
        out_shape=jax.ShapeDtypeStruct((N, D), F32),
        scratch_shapes=[pltpu.VMEM((2, TOP_K, tm * ROW_SUB, LANES), F32), pltpu.SemaphoreType.DMA((2,))],
        compiler_params=pltpu.CompilerParams(dimension_semantics=("arbitrary",)),
        name="combine",
    )(dest, dest, ys, x1, rg, g2, b2)


def _rope_freq():
    half = ROPE_DIM // 2
    inv = (np.float32(ROPE_THETA) ** (-np.arange(half, dtype=np.float32) * np.float32(2.0 / ROPE_DIM))).astype(np.float32)
    return inv[:, None]


def _overlap_t(T):
    nc = T // CMP_STRIDE
    c0 = np.arange(nc) * CMP_STRIDE
    j0 = np.arange(SEL_LANES) * SEL_BLOCK
    ov = (c0[None, :] < j0[:, None] + SEL_BLOCK) & (c0[None, :] + CMP_BLOCK > j0[:, None])
    ov &= (np.arange(nc) < nc - 1)[None, :] & (np.arange(SEL_LANES) < T // SEL_BLOCK)[:, None]
    return ov.astype(np.float32)


def _deinterleave_perm():
    w = 2 * LANES
    p = np.zeros((w, w), np.float32)
    p[np.arange(0, w, 2), np.arange(w // 2)] = 1.0
    p[np.arange(1, w, 2), w // 2 + np.arange(w // 2)] = 1.0
    return p


def _token_major(t, B, T):
    return t.reshape(B, N_KV_GROUPS, HEAD_DIM, T).transpose(0, 1, 3, 2)


def _layer(x, positions, w_in, conv_w, cmp_pos_k, cmp_w1_k, cmp_w2_k, cmp_pos_v, cmp_w1_v, cmp_w2_v,
           w_up_conv, w_up_nsa, w_o, ln1_g, ln1_b, w_router, b_router, w_gate_up, b_gate_up,
           w_down, b_down, ln2_g, ln2_b, alpha):
    B, T, D = x.shape
    G, R = N_KV_GROUPS, HEADS_PER_GROUP
    N = B * T
    assert D == D_MODEL and T % SEL_BLOCK == 0 and T // SEL_BLOCK <= SEL_LANES

    c0 = 3 * CONV_CH
    c1 = c0 + NSA_WIDTH
    c2 = c1 + 6 * KV_WIDTH
    c3 = c2 + 3 * N_HEADS
    wc = w_in[:, :c0].astype(BF16)
    wqt = w_in[:, c0:c1].T.astype(BF16)
    wkvt = w_in[:, c1:c2].T.astype(BF16)
    gcols = np.zeros((G * GATE_ROWS,), np.int64)
    gmask = np.zeros((G * GATE_ROWS,), np.float32)
    for g in range(G):
        for br in range(3):
            for r in range(R):
                gcols[g * GATE_ROWS + br * R + r] = br * N_HEADS + g * R + r
                gmask[g * GATE_ROWS + br * R + r] = 1.0
    wgt = (w_in[:, c2:c3][:, gcols] * gmask).T.astype(BF16)
    wmg = w_in[:, c3:].astype(BF16)
    cw = conv_w.reshape(CONV_K, CONV_CH)
    pos_row = positions.astype(F32)[:, None, :]

    ya_pre, qt, kvt, gates = _proj(x, pos_row, wc, wqt, wkvt, wgt, cw, jnp.asarray(_rope_freq()))

    nc = T // CMP_STRIDE
    to_chunks = lambda t: _token_major(t, B, T).reshape(B * G, nc, CMP_STRIDE * HEAD_DIM)
    xin = jnp.stack([to_chunks(kvt[:, 0:KV_WIDTH]), to_chunks(kvt[:, KV_WIDTH:2 * KV_WIDTH])])
    w1 = jnp.stack([cmp_w1_k, cmp_w1_v])
    w2 = jnp.stack([cmp_w2_k, cmp_w2_v])
    pos_flat = jnp.stack([cmp_pos_k.reshape(1, -1), cmp_pos_v.reshape(1, -1)])
    pos_flat = jnp.pad(pos_flat, ((0, 0), (0, 7), (0, 0)))
    kcmp, kcmp_t = _compress(xin, w1, w2, w2.transpose(0, 2, 1), pos_flat)

    o_c, sel = _cmp_attn(qt, kcmp, kcmp_t, gates, jnp.asarray(_overlap_t(T)))

    hot_t = (np.arange(SEL_LANES)[:, None] == np.arange(T)[None, :] // SEL_BLOCK).astype(np.float32)
    o_s = _sel_attn(qt, sel, kvt, 2 * G, 3 * G, jnp.asarray(hot_t, BF16), gates)
    o_w = _win_attn(qt, kvt, 4 * G, 5 * G, gates)

    sub = min(MERGE_SUB, T)
    tri = jnp.asarray(np.triu(np.ones((sub, sub), np.float32), 1), BF16)
    assert N % MOE_BLK == 0
    x1, ri, rg, cnt, xs = _merge(x.reshape(N, D), ya_pre.reshape(N, CONV_CH), o_c, o_s, o_w, wmg,
                             w_up_conv.astype(BF16), w_up_nsa.astype(BF16), w_o.astype(BF16),
                             ln1_g[None, :], ln1_b[None, :], w_router.T, b_router[:, None], tri, alpha)

    counts = cnt[:, 0]
    blocks = (counts + MOE_BLK - 1) // MOE_BLK
    bend = jnp.cumsum(blocks).astype(I32)
    bstart = bend - blocks
    n_blocks = (N * TOP_K) // MOE_BLK + N_EXPERTS
    n_used = bend[-1:]
    b = jnp.minimum(jnp.arange(n_blocks, dtype=I32), n_used - 1)
    block_expert = jnp.minimum(jnp.sum((b[:, None] >= bend[None, :]).astype(I32), axis=1), N_EXPERTS - 1)
    first = jnp.sum(jnp.where(block_expert[:, None] == jnp.arange(N_EXPERTS)[None, :], bstart[None, :], 0), axis=1)
    row_block = block_expert * (N // MOE_BLK) + (b - first)
    count = jnp.sum(jnp.where(block_expert[:, None] == jnp.arange(N_EXPERTS)[None, :], counts[None, :], 0), axis=1)
    valid_rows = jnp.clip(count - (b - first) * MOE_BLK, 0, MOE_BLK).astype(I32)
    dest = (ri[:TOP_K] * N + ri[TOP_K:]).T.reshape(N * TOP_K)

    ys = _experts(block_expert, n_used, row_block, valid_rows, xs, w_gate_up, jnp.asarray(_deinterleave_perm(), BF16),
                  b_gate_up[:, None, 0::2], b_gate_up[:, None, 1::2], w_down, b_down[:, None, :])
    out = _combine(dest, ys, x1, rg.T, ln2_g[None, :], ln2_b[None, :], alpha)
    return out.reshape(B, T, D)


def kernel(x, positions, w_in, conv_w, cmp_pos_k, cmp_w1_k, cmp_w2_k, cmp_pos_v, cmp_w1_v, cmp_w2_v, w_up_conv, w_up_nsa, w_o, ln1_g, ln1_b, w_router, b_router, w_gate_up, b_gate_up, w_down, b_down, ln2_g, ln2_b):
    depth = w_in.shape[0]
    alpha = float((2 * depth) ** 0.25)
    h = x
    for l in range(depth):
        h = _layer(h, positions, w_in[l], conv_w[l], cmp_pos_k[l], cmp_w1_k[l], cmp_w2_k[l],
                   cmp_pos_v[l], cmp_w1_v[l], cmp_w2_v[l], w_up_conv[l], w_up_nsa[l], w_o[l],
                   ln1_g[l], ln1_b[l], w_router[l], b_router[l], w_gate_up[l], b_gate_up[l],
                   w_down[l], b_down[l], ln2_g[l], ln2_b[l], alpha)
    return h
```

```python
import functools

import numpy as np
import jax
import jax.numpy as jnp
from jax import lax
from jax.experimental import pallas as pl
from jax.experimental.pallas import tpu as pltpu

F32 = jnp.float32
BF16 = jnp.bfloat16
I32 = jnp.int32

D_MODEL = 1024
CONV_CH = 512
CONV_K = 3
N_HEADS = 8
N_KV_GROUPS = 2
HEADS_PER_GROUP = N_HEADS // N_KV_GROUPS
HEAD_DIM = 64
NSA_WIDTH = N_HEADS * HEAD_DIM
KV_WIDTH = N_KV_GROUPS * HEAD_DIM
ROPE_DIM = HEAD_DIM // 4
ROPE_THETA = 500000.0
CMP_BLOCK = 32
CMP_STRIDE = 16
CMP_HIDDEN = 256
SEL_BLOCK = 64
N_SELECT = 16
WINDOW = 512
N_EXPERTS = 32
TOP_K = 4
D_FF = 1024
SWIGLU_LIMIT = 7.0
SWIGLU_ALPHA = 1.702
LN_EPS = 1e-5
NEG_INF = -1e30
FORCE_SCORE = 1e4
LOG2E = 1.4426950408889634

LANES = 128
SUBLANES = 8
ROW_SUB = D_MODEL // LANES
assert ROW_SUB == SUBLANES
TAKEN = float("-inf")
SEL_LANES = 64
SEL_MASK_BIAS = -32768.0
GROUP_W = HEADS_PER_GROUP * HEAD_DIM
GATE_ROWS = 16

PROJ_TM = 1024
PROJ_SUB = 256
CMP_TQ = 512
SEL_TQ = 512
SEL_KC = 512
SEL_SPLIT = 2
WIN_TQ = 1024
WIN_SUB = 128
MERGE_TM = 512
MERGE_SUB = 256
MOE_BLK = 512
COMB_TM = 512
COMB_GROUP = 128


def _dot(a, b, precision=None):
    return jnp.dot(a, b, precision=precision, preferred_element_type=F32)


def _dot_tn(a, b):
    return lax.dot_general(a, b, (((0,), (0,)), ((), ())), preferred_element_type=F32)


def _dot_nt(a, b, precision=None):
    return lax.dot_general(a, b, (((1,), (1,)), ((), ())), precision=precision, preferred_element_type=F32)


def _proj_kernel(x_ref, pos_ref, wc_ref, wqt_ref, wkvt_ref, wgt_ref, cw_ref, freq_ref,
                 ya_ref, qt_ref, kvt_ref, gt_ref, carry_ref):
    ti = pl.program_id(1)
    tm = x_ref.shape[1]
    sub = min(PROJ_SUB, tm)
    half = ROPE_DIM // 2

    @pl.when(ti == 0)
    def _():
        carry_ref[...] = jnp.zeros_like(carry_ref)

    prev = carry_ref[...]
    cw = cw_ref[...]
    for s0 in range(0, tm, sub):
        rs = slice(s0, s0 + sub)
        xb = x_ref[0, rs, :].astype(BF16)

        pc = _dot(xb, wc_ref[...])
        xv = pc[:, :CONV_CH]
        bg = pc[:, CONV_CH:2 * CONV_CH]
        cg = pc[:, 2 * CONV_CH:]
        u = cg * xv
        row = lax.broadcasted_iota(I32, u.shape, 0)
        u1 = jnp.where(row == 0, prev[7:8], pltpu.roll(u, 1, 0))
        u2 = jnp.where(row == 0, prev[6:7], jnp.where(row == 1, prev[7:8], pltpu.roll(u, 2, 0)))
        conv = cw[2:3] * u + cw[1:2] * u1 + cw[0:1] * u2
        prev = u[sub - 8:]
        ya_ref[0, rs, :] = (bg * conv).astype(BF16)

        ang = freq_ref[...] * pos_ref[0, :, rs]
        cos = jnp.cos(ang)
        sin = jnp.sin(ang)

        def rope_head(t):
            t1 = t[:half]
            t2 = t[half:ROPE_DIM]
            return [t1 * cos - t2 * sin, t2 * cos + t1 * sin, t[ROPE_DIM:]]

        def heads(t, rotate):
            out = []
            for h in range(t.shape[0] // HEAD_DIM):
                th = t[h * HEAD_DIM:(h + 1) * HEAD_DIM]
                out.extend(rope_head(th) if rotate(h) else [th])
            return jnp.concatenate(out, axis=0)

        qt = heads(_dot_nt(wqt_ref[...], xb), lambda h: True)
        qt_ref[0, :, rs] = (qt * (HEAD_DIM ** -0.5 * LOG2E)).astype(BF16)
        kvt = heads(_dot_nt(wkvt_ref[...], xb), lambda h: (h // N_KV_GROUPS) % 2 == 0)
        kvt_ref[0, :, rs] = kvt.astype(BF16)
        gt_ref[0, :, rs] = jax.nn.sigmoid(_dot_nt(wgt_ref[...], xb))
    carry_ref[...] = prev


def _proj(x, pos_row, wc, wqt, wkvt, wgt, cw, freq):
    B, T, D = x.shape
    tm = min(PROJ_TM, T)
    grid = (B, T // tm)
    full = lambda a: pl.BlockSpec(a.shape, lambda b, t: (0,) * a.ndim)
    tok = lambda w: pl.BlockSpec((1, tm, w), lambda b, t: (b, t, 0))
    feat = lambda r: pl.BlockSpec((1, r, tm), lambda b, t: (b, 0, t))
    n_gate = wgt.shape[0]
    return pl.pallas_call(
        _proj_kernel,
        grid=grid,
        in_specs=[tok(D), feat(1), full(wc), full(wqt), full(wkvt), full(wgt), full(cw), full(freq)],
        out_specs=[tok(CONV_CH), feat(NSA_WIDTH), feat(6 * KV_WIDTH), feat(n_gate)],
        out_shape=[jax.ShapeDtypeStruct((B, T, CONV_CH), BF16),
                   jax.ShapeDtypeStruct((B, NSA_WIDTH, T), BF16),
                   jax.ShapeDtypeStruct((B, 6 * KV_WIDTH, T), BF16),
                   jax.ShapeDtypeStruct((B, n_gate, T), F32)],
        scratch_shapes=[pltpu.VMEM((8, CONV_CH), F32)],
        compiler_params=pltpu.CompilerParams(dimension_semantics=("arbitrary", "arbitrary")),
        name="proj",
    )(x, pos_row, wc, wqt, wkvt, wgt, cw, freq)


def _compress_kernel(x_ref, w1_ref, w2_ref, w2t_ref, pos_ref, o_ref, ot_ref):
    xb = x_ref[0, 0]
    nc = xb.shape[0]
    w1 = w1_ref[0]
    w1b = w1.astype(BF16)
    half = CMP_STRIDE * HEAD_DIM
    a = _dot(xb, w1b[:half])
    b = _dot(xb, w1b[half:])
    b_next = pltpu.roll(b, nc - 1, 0)
    pb = _dot(pos_ref[0], w1, precision=lax.Precision.HIGHEST)[0:1]
    h = a + b_next + pb
    g = (0.5 * h * (1.0 + jnp.tanh(np.sqrt(2.0 / np.pi) * (h + 0.044715 * (h * h * h))))).astype(BF16)
    o_ref[0, 0] = _dot(g, w2_ref[0].astype(BF16)).astype(BF16)
    ot_ref[0, 0] = _dot_nt(w2t_ref[0].astype(BF16), g).astype(BF16)


def _compress(xin, w1, w2, w2t, pos):
    _, BG, nc, W = xin.shape
    per = lambda a: pl.BlockSpec((1,) + a.shape[1:], lambda s, i: (s, 0, 0))
    return pl.pallas_call(
        _compress_kernel,
        grid=(2, BG),
        in_specs=[pl.BlockSpec((1, 1, nc, W), lambda s, i: (s, i, 0, 0)), per(w1), per(w2), per(w2t), per(pos)],
        out_specs=[pl.BlockSpec((1, 1, nc, HEAD_DIM), lambda s, i: (s, i, 0, 0)),
                   pl.BlockSpec((1, 1, HEAD_DIM, nc), lambda s, i: (s, i, 0, 0))],
        out_shape=[jax.ShapeDtypeStruct((2, BG, nc, HEAD_DIM), BF16),
                   jax.ShapeDtypeStruct((2, BG, HEAD_DIM, nc), BF16)],
        name="compress",
    )(xin, w1, w2, w2t, pos)


def _head_lanes(qt):
    return jnp.concatenate([qt[r * HEAD_DIM:(r + 1) * HEAD_DIM] for r in range(HEADS_PER_GROUP)], axis=1)


def _gated_out(ot, scale, gates, branch, tq):
    rows = []
    for r in range(HEADS_PER_GROUP):
        c = branch * HEADS_PER_GROUP + r
        sl = slice(r * tq, (r + 1) * tq)
        rows.append(ot[:, sl] * (scale[:, sl] * gates[c:c + 1]))
    return jnp.concatenate(rows, axis=0).astype(BF16)


def _attn_specs(tq):
    G = N_KV_GROUPS
    qspec = pl.BlockSpec((1, GROUP_W, tq), lambda b, g, i: (b, g, i))
    gspec = pl.BlockSpec((1, GATE_ROWS, tq), lambda b, g, i: (b, g, i))
    bg4 = lambda a: pl.BlockSpec((1, 1) + a.shape[2:], lambda b, g, i: (b, g, 0, 0))
    return G, qspec, gspec, bg4


def _cmp_attn_kernel(q_ref, kc_ref, vct_ref, g_ref, ovt_ref, o_ref, sel_ref, score_ref, rank_ref):
    qi = pl.program_id(2)
    tq = q_ref.shape[2]
    q0 = qi * tq
    qt = _head_lanes(q_ref[0])
    s = _dot(kc_ref[0, 0], qt)
    c = lax.broadcasted_iota(I32, s.shape, 0)
    t = q0 + (lax.broadcasted_iota(I32, (1, s.shape[1]), 1) & (tq - 1))
    valid = c <= (t - (CMP_BLOCK - 1)) // CMP_STRIDE
    sm = jnp.where(valid, s, NEG_INF)
    m = jnp.max(sm, axis=0, keepdims=True)
    p = jnp.where(valid, jnp.exp2(sm - m), 0.0)
    l = jnp.sum(p, axis=0, keepdims=True)
    inv = 1.0 / jnp.where(l > 0.0, l, 1.0)
    ot = _dot(vct_ref[0, 0], p.astype(BF16))
    o_ref[0] = _gated_out(ot, inv, g_ref[0], 0, tq)

    pn = p * inv
    psum = pn[:, 0:tq]
    for r in range(1, HEADS_PER_GROUP):
        psum = psum + pn[:, r * tq:(r + 1) * tq]
    imp = _dot(ovt_ref[...], psum, precision=lax.Precision.HIGHEST)
    j = lax.broadcasted_iota(I32, imp.shape, 0)
    cur = (q0 + lax.broadcasted_iota(I32, imp.shape, 1)) // SEL_BLOCK
    valid_b = j <= cur
    forced = (j == 0) | (j == cur) | (j == cur - 1)
    score = jnp.where(valid_b, jnp.where(forced, FORCE_SCORE, imp), NEG_INF)
    sub = SUBLANES
    score_ref[...] = score
    rank_ref[...] = jnp.zeros(rank_ref.shape, I32)
    i_max = (q0 + tq - 1) // SEL_BLOCK
    for i0 in range(0, SEL_LANES, sub):
        @pl.when(i0 <= i_max)
        def _():
            sc = score_ref[...]
            groups = [sc[a:a + sub] for a in range(0, SEL_LANES, sub)]
            jrow = lax.broadcasted_iota(I32, groups[0].shape, 0)
            ranks = [jnp.zeros(g_.shape, I32) for g_ in groups]
            for i in range(i0, i0 + sub):
                si = sc[i:i + 1, :]
                for a, g_ in enumerate(groups):
                    if a > i // sub:
                        inc = jnp.where(si >= g_, 1, 0)
                    elif a < i // sub:
                        inc = jnp.where(si > g_, 1, 0)
                    else:
                        inc = jnp.where(jrow > i % sub, jnp.where(si >= g_, 1, 0), jnp.where(si > g_, 1, 0))
                    ranks[a] = ranks[a] + inc
            rank_ref[...] += jnp.concatenate(ranks, axis=0)
    selected = (rank_ref[...] < N_SELECT) & valid_b
    sel_ref[0, 0] = jnp.where(selected, 0.0, SEL_MASK_BIAS).astype(BF16)


def _cmp_attn(qt, kcmp, vcmp_t, gates, ovt):
    B, _, T = qt.shape
    tq = min(CMP_TQ, T)
    G, qspec, gspec, _ = _attn_specs(tq)
    cspec = lambda a: pl.BlockSpec((1, 1) + a.shape[2:], lambda b, g, i: (0, b * G + g, 0, 0))
    vspec = lambda a: pl.BlockSpec((1, 1) + a.shape[2:], lambda b, g, i: (1, b * G + g, 0, 0))
    return pl.pallas_call(
        _cmp_attn_kernel,
        grid=(B, G, T // tq),
        in_specs=[qspec, cspec(kcmp), vspec(vcmp_t), gspec, pl.BlockSpec(ovt.shape, lambda b, g, i: (0, 0))],
        out_specs=[qspec, pl.BlockSpec((1, 1, SEL_LANES, tq), lambda b, g, i: (b, g, 0, i))],
        out_shape=[jax.ShapeDtypeStruct((B, NSA_WIDTH, T), BF16),
                   jax.ShapeDtypeStruct((B, G, SEL_LANES, T), BF16)],
        scratch_shapes=[pltpu.VMEM((SEL_LANES, tq), F32), pltpu.VMEM((SEL_LANES, tq), I32)],
        name="cmp_attn",
    )(qt, kcmp, vcmp_t, gates, ovt)


def _sel_attn_kernel(q_ref, sel_ref, kt_ref, hot_ref, vt_ref, g_ref, o_ref, *, kc):
    qi = pl.program_id(2)
    tq = q_ref.shape[2]
    q0 = qi * tq
    qt = q_ref[0]
    sb = sel_ref[0, 0]
    qa = jnp.concatenate(
        [jnp.concatenate([qt[r * HEAD_DIM:(r + 1) * HEAD_DIM], sb], axis=0) for r in range(HEADS_PER_GROUP)],
        axis=1)
    gw = qa.shape[1] // SEL_SPLIT
    qas = [qa[:, i * gw:(i + 1) * gw] for i in range(SEL_SPLIT)]

    def step(kstart, kn, carry, diagonal):
        keys = pl.ds(kstart, kn)
        kblk_t = jnp.concatenate([kt_ref[0, :, keys], hot_ref[:, keys]], axis=0)
        vblk = vt_ref[0, :, keys]
        ss = [_dot_tn(kblk_t, qg) for qg in qas]
        out = []
        for gi, (s, (m, l, acc)) in enumerate(zip(ss, carry)):
            if diagonal:
                row = lax.broadcasted_iota(I32, s.shape, 0)
                off = (gi * gw + lax.broadcasted_iota(I32, s.shape, 1)) & (tq - 1)
                s = jnp.where(row <= off, s, NEG_INF)
            mn = jnp.maximum(m, jnp.max(s, axis=0, keepdims=True))
            alpha = jnp.exp2(m - mn)
            p = jnp.exp2(s - mn)
            l = alpha * l + jnp.sum(p, axis=0, keepdims=True)
            acc = alpha * acc + _dot(vblk, p.astype(BF16))
            out.append((mn, l, acc))
        return tuple(out)

    init = tuple((jnp.full((1, gw), NEG_INF, F32), jnp.zeros((1, gw), F32), jnp.zeros((HEAD_DIM, gw), F32))
                 for _ in range(SEL_SPLIT))
    n_full = q0 // kc
    carry = lax.fori_loop(0, n_full, lambda i, cr: step(pl.multiple_of(i * kc, kc), kc, cr, False), init)
    carry = lax.fori_loop(n_full * (kc // tq), qi, lambda i, cr: step(pl.multiple_of(i * tq, tq), tq, cr, False), carry)
    carry = step(pl.multiple_of(q0, tq), tq, carry, True)
    l = jnp.concatenate([c[1] for c in carry], axis=1)
    acc = jnp.concatenate([c[2] for c in carry], axis=1)
    o_ref[0] = _gated_out(acc, 1.0 / l, g_ref[0], 1, tq)


def _sel_attn(qt, sel, kvt, k_row_block, v_row_block, hot_t, gates):
    B, _, T = qt.shape
    tq = min(SEL_TQ, T)
    kc = min(SEL_KC, T)
    G, qspec, gspec, bg4 = _attn_specs(tq)
    return pl.pallas_call(
        functools.partial(_sel_attn_kernel, kc=kc),
        grid=(B, G, T // tq),
        in_specs=[qspec, pl.BlockSpec((1, 1, SEL_LANES, tq), lambda b, g, i: (b, g, 0, i)),
                  pl.BlockSpec((1, HEAD_DIM, T), lambda b, g, i: (b, k_row_block + g, 0)),
                  pl.BlockSpec(hot_t.shape, lambda b, g, i: (0, 0)),
                  pl.BlockSpec((1, HEAD_DIM, T), lambda b, g, i: (b, v_row_block + g, 0)), gspec],
        out_specs=qspec,
        out_shape=jax.ShapeDtypeStruct((B, NSA_WIDTH, T), BF16),
        name="sel_attn",
    )(qt, sel, kvt, hot_t, kvt, gates)


def _win_attn_kernel(q_ref, kt_ref, vt_ref, g_ref, o_ref, *, span, sub):
    qi = pl.program_id(2)
    tq = q_ref.shape[2]
    T = kt_ref.shape[2]

    def tile(s0, interior):
        cs = slice(s0, s0 + sub)
        q0 = qi * tq + s0
        start = pl.multiple_of(jnp.clip(q0 + sub - span, 0, T - span), sub)
        qt = _head_lanes(q_ref[0, :, cs])
        s = _dot_tn(kt_ref[0, :, pl.ds(start, span)], qt)
        if interior:
            row = lax.broadcasted_iota(I32, (sub, s.shape[1]), 0)
            off = lax.broadcasted_iota(I32, (sub, s.shape[1]), 1) & (sub - 1)
            s = jnp.concatenate([jnp.where(row > off, s[:sub], NEG_INF), s[sub:span - sub],
                                 jnp.where(row <= off, s[span - sub:], NEG_INF)], axis=0)
        else:
            kp = start + lax.broadcasted_iota(I32, s.shape, 0)
            t = q0 + (lax.broadcasted_iota(I32, s.shape, 1) & (sub - 1))
            diff = t - kp
            s = jnp.where((diff >= 0) & (diff < WINDOW), s, NEG_INF)
        m = jnp.max(s, axis=0, keepdims=True)
        p = jnp.exp2(s - m)
        l = jnp.sum(p, axis=0, keepdims=True)
        ot = _dot(vt_ref[0, :, pl.ds(start, span)], p.astype(BF16))
        o_ref[0, :, cs] = _gated_out(ot, 1.0 / l, g_ref[0, :, cs], 2, sub)

    first_interior = -(-WINDOW // tq)
    if span == WINDOW + sub:
        @pl.when(qi >= first_interior)
        def _():
            for s0 in range(0, tq, sub):
                tile(s0, True)

    @pl.when((qi < first_interior) | (span != WINDOW + sub))
    def _():
        for s0 in range(0, tq, sub):
            tile(s0, first_interior == 1 and s0 >= WINDOW and span == WINDOW + sub)


def _win_attn(qt, kvt, k_row_block, v_row_block, gates):
    B, _, T = qt.shape
    tq = min(WIN_TQ, T)
    sub = min(WIN_SUB, T)
    span = min(WINDOW + sub, T)
    G, qspec, gspec, bg4 = _attn_specs(tq)
    return pl.pallas_call(
        functools.partial(_win_attn_kernel, span=span, sub=sub),
        grid=(B, G, T // tq),
        in_specs=[qspec, pl.BlockSpec((1, HEAD_DIM, T), lambda b, g, i: (b, k_row_block + g, 0)),
                  pl.BlockSpec((1, HEAD_DIM, T), lambda b, g, i: (b, v_row_block + g, 0)), gspec],
        out_specs=qspec,
        out_shape=jax.ShapeDtypeStruct((B, NSA_WIDTH, T), BF16),
        name="win_attn",
    )(qt, kvt, kvt, gates)


def _store_row_tiles(ref, x, row0=0):
    rows = x.shape[0]
    for s in range(ROW_SUB):
        ref[pl.ds(row0 * ROW_SUB + s, rows, stride=ROW_SUB), :] = x[:, s * LANES:(s + 1) * LANES]


def _load_row_tiles(ref, rows=None):
    rows = ref.shape[0] // ROW_SUB if rows is None else rows
    return jnp.concatenate([ref[pl.ds(s, rows, stride=ROW_SUB), :] for s in range(ROW_SUB)], axis=1)


def _tile_copy(src, si, dst, di, sem):
    return pltpu.make_async_copy(src.at[pl.ds(pl.multiple_of(si * ROW_SUB, ROW_SUB), ROW_SUB), :],
                                 dst.at[pl.ds(pl.multiple_of(di * ROW_SUB, ROW_SUB), ROW_SUB), :], sem)


def _layer_norm(h, g, b):
    mu = jnp.mean(h, axis=-1, keepdims=True)
    c = h - mu
    var = jnp.mean(c * c, axis=-1, keepdims=True)
    return c * lax.rsqrt(var + LN_EPS) * g + b


def _merge_kernel(x_ref, ya_ref, oc_ref, os_ref, ow_ref, wmg_ref, wuc_ref, wun_ref, wo_ref, g1_ref, b1_ref,
                  wr_ref, br_ref, tri_ref, x1_ref, ri_ref, rg_ref, cnt_ref, xs_hbm,
                  carry_ref, stage, dest_v, dest_s, copy_sem, row_sem, *, alpha, cap):
    i = pl.program_id(0)
    last = pl.num_programs(0) - 1
    sub = tri_ref.shape[0]
    tm = x_ref.shape[0]
    slot = i % 2
    nsub = tm // sub

    def id_copy(si):
        return pltpu.make_async_copy(dest_v.at[si], dest_s.at[si], copy_sem.at[si])

    def issue_rows(si, s):
        id_copy(si).wait()
        for t in range(sub):
            for k in range(TOP_K):
                _tile_copy(stage.at[s], si * sub + t, xs_hbm, dest_s[si, k, t], row_sem.at[s]).start(priority=k % 2)

    def wait_rows(s, rows):
        for k in range(TOP_K):
            n = rows * ROW_SUB
            pltpu.make_async_copy(stage.at[s, pl.ds(0, n), :], xs_hbm.at[pl.ds(0, n), :], row_sem.at[s]).wait()

    @pl.when(i == 0)
    def _():
        carry_ref[...] = jnp.zeros_like(carry_ref)
        stage[1, pl.ds((nsub - 1) * sub * ROW_SUB, sub * ROW_SUB), :] = jnp.zeros((sub * ROW_SUB, LANES), F32)
        spare = (N_EXPERTS * cap + lax.broadcasted_iota(I32, (TOP_K, sub), 0) * sub
                 + lax.broadcasted_iota(I32, (TOP_K, sub), 1))
        dest_v[nsub - 1] = spare
        id_copy(nsub - 1).start()

    wr = wr_ref[...]
    wr_hi = wr.astype(BF16)
    wr_hl = jnp.concatenate([wr_hi, (wr - wr_hi.astype(F32)).astype(BF16)], axis=0)
    total = carry_ref[...]
    for si in range(nsub):
        s0 = si * sub
        rs = slice(s0, s0 + sub)
        x = x_ref[rs, :]
        mg = _dot(x.astype(BF16), wmg_ref[...])
        if si == 0:
            issue_rows(nsub - 1, 1 - slot)
        else:
            issue_rows(si - 1, slot)
        y_a = _dot(ya_ref[rs, :], wuc_ref[...])
        o_nsa_t = (oc_ref[0, :, rs].astype(F32) + os_ref[0, :, rs].astype(F32)
                   + ow_ref[0, :, rs].astype(F32))
        y_b = _dot(o_nsa_t.T.astype(BF16), wun_ref[...])
        merged = jax.nn.sigmoid(mg[:, :D_MODEL]) * y_a + jax.nn.sigmoid(mg[:, D_MODEL:]) * y_b
        h = alpha * x + _dot(merged.astype(BF16), wo_ref[...])
        x1 = _layer_norm(h, g1_ref[...], b1_ref[...])
        x1_ref[rs, :] = x1
        _store_row_tiles(stage.at[slot], x1, s0)

        x_hi = x1.astype(BF16)
        x_lo = (x1 - x_hi.astype(F32)).astype(BF16)
        both = _dot_nt(wr_hl, x_hi)
        logits = both[:N_EXPERTS] + both[N_EXPERTS:] + _dot_nt(wr_hi, x_lo) + br_ref[...]
        expert = lax.broadcasted_iota(I32, logits.shape, 0).astype(F32)
        rem = logits
        vals, idxs, hots = [], [], []
        for _ in range(TOP_K):
            m = jnp.max(rem, axis=0, keepdims=True)
            idx = jnp.min(jnp.where(rem == m, expert, float(N_EXPERTS)), axis=0, keepdims=True)
            hot = expert == idx
            vals.append(m)
            idxs.append(idx)
            hots.append(hot)
            rem = jnp.where(hot, TAKEN, rem)
        es = [jnp.exp(v - vals[0]) for v in vals]
        den = es[0]
        for e in es[1:]:
            den = den + e
        chosen = hots[0]
        for hot in hots[1:]:
            chosen = chosen | hot
        chosen_f = jnp.where(chosen, 1.0, 0.0)

        before = _dot(chosen_f.astype(BF16), tri_ref[...]) + total
        total = total + jnp.sum(chosen_f, axis=1, keepdims=True)
        ranks = [jnp.sum(jnp.where(hot, before, 0.0), axis=0, keepdims=True) for hot in hots]
        ri_ref[:, rs] = jnp.concatenate(idxs + ranks, axis=0).astype(I32)
        rg_ref[:, rs] = jnp.concatenate([e / den for e in es] + [jnp.zeros_like(den)] * TOP_K, axis=0)

        dest_v[si] = jnp.concatenate(idxs, axis=0).astype(I32) * cap + jnp.concatenate(ranks, axis=0).astype(I32)
        id_copy(si).start()

    carry_ref[...] = total
    cnt_ref[...] = jnp.broadcast_to(total, cnt_ref.shape).astype(I32)

    @pl.when(i == 0)
    def _():
        wait_rows(1, sub)

    @pl.when(i > 0)
    def _():
        wait_rows(1 - slot, tm)

    @pl.when(i == last)
    def _():
        issue_rows(nsub - 1, slot)
        wait_rows(slot, tm)


def _merge(x, ya, oc, os_, ow, wmg, wuc, wun, wo, g1, b1, wr, br, tri, alpha):
    N, D = x.shape
    B, _, T = oc.shape
    tm = min(MERGE_TM, T)
    nt = T // tm
    tok = lambda w: pl.BlockSpec((tm, w), lambda i: (i, 0))
    feat = pl.BlockSpec((1, NSA_WIDTH, tm), lambda i: (i // nt, 0, i % nt))
    full = lambda a: pl.BlockSpec(a.shape, lambda i: (0,) * a.ndim)
    sub = tri.shape[0]
    return pl.pallas_call(
        functools.partial(_merge_kernel, alpha=alpha, cap=N),
        grid=(N // tm,),
        in_specs=[tok(D), tok(CONV_CH), feat, feat, feat,
                  full(wmg), full(wuc), full(wun), full(wo), full(g1), full(b1), full(wr), full(br), full(tri)],
        out_specs=[tok(D),
                   pl.BlockSpec((2 * TOP_K, tm), lambda i: (0, i)), pl.BlockSpec((2 * TOP_K, tm), lambda i: (0, i)),
                   pl.BlockSpec((N_EXPERTS, LANES), lambda i: (0, 0)), pl.BlockSpec(memory_space=pl.ANY)],
        out_shape=[jax.ShapeDtypeStruct((N, D), F32),
                   jax.ShapeDtypeStruct((2 * TOP_K, N), I32),
                   jax.ShapeDtypeStruct((2 * TOP_K, N), F32),
                   jax.ShapeDtypeStruct((N_EXPERTS, LANES), I32),
                   jax.ShapeDtypeStruct(((N_EXPERTS * N + TOP_K * sub) * ROW_SUB, LANES), F32)],
        scratch_shapes=[pltpu.VMEM((N_EXPERTS, 1), F32), pltpu.VMEM((2, tm * ROW_SUB, LANES), F32),
                        pltpu.VMEM((tm // sub, TOP_K, sub), I32), pltpu.SMEM((tm // sub, TOP_K, sub), I32),
                        pltpu.SemaphoreType.DMA((tm // sub,)), pltpu.SemaphoreType.DMA((2,))],
        compiler_params=pltpu.CompilerParams(dimension_semantics=("arbitrary",)),
        name="merge",
    )(x, ya, oc, os_, ow, wmg, wuc, wun, wo, g1, b1, wr, br, tri)


def _experts_kernel(be_ref, nu_ref, rb_ref, vr_ref, xs_ref, wgu_ref, perm_ref, bg_ref, bl_ref, wd_ref, bd_ref, ys_ref, wg_s, wl_s, wd_s):
    i = pl.program_id(0)

    @pl.when((i == 0) | (be_ref[i] != be_ref[jnp.maximum(i, 1) - 1]))
    def _():
        w = perm_ref.shape[0]
        for c in range(2 * D_FF // w):
            t = _dot(wgu_ref[0, :, c * w:(c + 1) * w].astype(BF16), perm_ref[...])
            wg_s[:, c * (w // 2):(c + 1) * (w // 2)] = t[:, :w // 2].astype(BF16)
            wl_s[:, c * (w // 2):(c + 1) * (w // 2)] = t[:, w // 2:].astype(BF16)
        wd_s[...] = wd_ref[0].astype(BF16)

    used = i < nu_ref[0]
    valid = vr_ref[i]

    def ffn(rows):
        row = lax.broadcasted_iota(I32, (rows, 1), 0)
        xb = jnp.where(row < valid, _load_row_tiles(xs_ref, rows), 0.0).astype(BF16)
        x_glu = jnp.minimum(_dot(xb, wg_s[...]) + bg_ref[0], SWIGLU_LIMIT)
        x_lin = jnp.clip(_dot(xb, wl_s[...]) + bl_ref[0], -SWIGLU_LIMIT, SWIGLU_LIMIT)
        act = x_glu * jax.nn.sigmoid(SWIGLU_ALPHA * x_glu) * (x_lin + 1.0)
        _store_row_tiles(ys_ref, _dot(act.astype(BF16), wd_s[...]) + bd_ref[0])

    half = MOE_BLK // 2
    pl.when(used & (valid > half))(functools.partial(ffn, MOE_BLK))
    pl.when(used & (valid <= half))(functools.partial(ffn, half))


def _experts(block_expert, n_used, row_block, valid_rows, xs, wgu, perm, bg, bl, wd, bd):
    D = D_MODEL
    n_blocks = block_expert.shape[0]
    rows = lambda i, be, nu, rb, vr: (rb[i], 0)
    wsel = lambda i, be, nu, rb, vr: (be[i], 0, 0)
    return pl.pallas_call(
        _experts_kernel,
        grid_spec=pltpu.PrefetchScalarGridSpec(
            num_scalar_prefetch=4,
            grid=(n_blocks,),
            in_specs=[pl.BlockSpec((MOE_BLK * ROW_SUB, LANES), rows),
                      pl.BlockSpec((1, D, 2 * D_FF), wsel),
                      pl.BlockSpec(perm.shape, lambda i, be, nu, rb, vr: (0, 0)),
                      pl.BlockSpec((1, 1, D_FF), wsel), pl.BlockSpec((1, 1, D_FF), wsel),
                      pl.BlockSpec((1, D_FF, D), wsel), pl.BlockSpec((1, 1, D), wsel)],
            out_specs=pl.BlockSpec((MOE_BLK * ROW_SUB, LANES), rows),
            scratch_shapes=[pltpu.VMEM((D, D_FF), BF16), pltpu.VMEM((D, D_FF), BF16), pltpu.VMEM((D_FF, D), BF16)]),
        out_shape=jax.ShapeDtypeStruct(xs.shape, F32),
        compiler_params=pltpu.CompilerParams(dimension_semantics=("arbitrary",)),
        name="experts",
    )(block_expert, n_used, row_block, valid_rows, xs, wgu, perm, bg, bl, wd, bd)


def _combine_kernel(dest_ref, dest_next_ref, ys_hbm, x1_ref, rg_ref, g2_ref, b2_ref, o_ref, buf, sem, *, alpha):
    i = pl.program_id(0)
    last = pl.num_programs(0) - 1
    tm = o_ref.shape[0]
    slot = i % 2
    other = 1 - slot

    def gather(ids_ref, t, to_slot):
        for k in range(TOP_K):
            _tile_copy(ys_hbm, ids_ref[t * TOP_K + k], buf.at[to_slot, k], t, sem.at[to_slot]).start(priority=k % 2)

    def wait_slot(s):
        for k in range(TOP_K):
            pltpu.make_async_copy(ys_hbm.at[pl.ds(0, tm * ROW_SUB), :], buf.at[s, k], sem.at[s]).wait()

    @pl.when(i == 0)
    def _():
        def first(t, c):
            gather(dest_ref, t, slot)
            return c
        lax.fori_loop(0, tm, first, 0)

    wait_slot(slot)

    gs = min(COMB_GROUP, tm)

    def group(g, c):
        r0 = pl.multiple_of(g * gs, gs)
        for j in range(gs):
            gather(dest_next_ref, r0 + j, other)
        rows = pl.ds(r0, gs)
        gate = rg_ref[rows, :]
        y = alpha * x1_ref[rows, :]
        for k in range(TOP_K):
            tiles = buf.at[slot, k, pl.ds(pl.multiple_of(r0 * ROW_SUB, gs * ROW_SUB), gs * ROW_SUB), :]
            y = y + gate[:, k:k + 1] * _load_row_tiles(tiles)
        o_ref[rows, :] = _layer_norm(y, g2_ref[...], b2_ref[...])
        return c

    lax.fori_loop(0, tm // gs, group, 0)

    @pl.when(i == last)
    def _():
        wait_slot(other)


def _combine(dest, ys, x1, rg, g2, b2, alpha):
    N, D = x1.shape
    tm = min(COMB_TM, N)
    n = N // tm
    tok = lambda w: pl.BlockSpec((tm, w), lambda i: (i, 0))
    full = lambda a: pl.BlockSpec(a.shape, lambda i: (0,) * a.ndim)
    ids = lambda f: pl.BlockSpec((tm * TOP_K,), f, memory_space=pltpu.SMEM)
    return pl.pallas_call(
        functools.partial(_combine_kernel, alpha=alpha),
        grid=(n,),
        in_specs=[ids(lambda i: (i,)), ids(lambda i: (jnp.minimum(i + 1, n - 1),)),
                  pl.BlockSpec(memory_space=pl.ANY), tok(D), tok(rg.shape[1]), full(g2), full(b2)],
        out_specs=tok(D),
        out_shape=jax.ShapeDtypeStruct((N, D), F32),
        scratch_shapes=[pltpu.VMEM((2, TOP_K, tm * ROW_SUB, LANES), F32), pltpu.SemaphoreType.DMA((2,))],
        compiler_params=pltpu.CompilerParams(dimension_semantics=("arbitrary",)),
        name="combine",
    )(dest, dest, ys, x1, rg, g2, b2)


def _rope_freq():
    half = ROPE_DIM // 2
    inv = (np.float32(ROPE_THETA) ** (-np.arange(half, dtype=np.float32) * np.float32(2.0 / ROPE_DIM))).astype(np.float32)
    return inv[:, None]


def _overlap_t(T):
    nc = T // CMP_STRIDE
    c0 = np.arange(nc) * CMP_STRIDE
    j0 = np.arange(SEL_LANES) * SEL_BLOCK
    ov = (c0[None, :] < j0[:, None] + SEL_BLOCK) & (c0[None, :] + CMP_BLOCK > j0[:, None])
    ov &= (np.arange(nc) < nc - 1)[None, :] & (np.arange(SEL_LANES) < T // SEL_BLOCK)[:, None]
    return ov.astype(np.float32)


def _deinterleave_perm():
    w = 2 * LANES
    p = np.zeros((w, w), np.float32)
    p[np.arange(0, w, 2), np.arange(w // 2)] = 1.0
    p[np.arange(1, w, 2), w // 2 + np.arange(w // 2)] = 1.0
    return p


def _token_major(t, B, T):
    return t.reshape(B, N_KV_GROUPS, HEAD_DIM, T).transpose(0, 1, 3, 2)


def _layer(x, positions, w_in, conv_w, cmp_pos_k, cmp_w1_k, cmp_w2_k, cmp_pos_v, cmp_w1_v, cmp_w2_v,
           w_up_conv, w_up_nsa, w_o, ln1_g, ln1_b, w_router, b_router, w_gate_up, b_gate_up,
           w_down, b_down, ln2_g, ln2_b, alpha):
    B, T, D = x.shape
    G, R = N_KV_GROUPS, HEADS_PER_GROUP
    N = B * T
    assert D == D_MODEL and T % SEL_BLOCK == 0 and T // SEL_BLOCK <= SEL_LANES

    c0 = 3 * CONV_CH
    c1 = c0 + NSA_WIDTH
    c2 = c1 + 6 * KV_WIDTH
    c3 = c2 + 3 * N_HEADS
    wc = w_in[:, :c0].astype(BF16)
    wqt = w_in[:, c0:c1].T.astype(BF16)
    wkvt = w_in[:, c1:c2].T.astype(BF16)
    gcols = np.zeros((G * GATE_ROWS,), np.int64)
    gmask = np.zeros((G * GATE_ROWS,), np.float32)
    for g in range(G):
        for br in range(3):
            for r in range(R):
                gcols[g * GATE_ROWS + br * R + r] = br * N_HEADS + g * R + r
                gmask[g * GATE_ROWS + br * R + r] = 1.0
    wgt = (w_in[:, c2:c3][:, gcols] * gmask).T.astype(BF16)
    wmg = w_in[:, c3:].astype(BF16)
    cw = conv_w.reshape(CONV_K, CONV_CH)
    pos_row = positions.astype(F32)[:, None, :]

    ya_pre, qt, kvt, gates = _proj(x, pos_row, wc, wqt, wkvt, wgt, cw, jnp.asarray(_rope_freq()))

    nc = T // CMP_STRIDE
    to_chunks = lambda t: _token_major(t, B, T).reshape(B * G, nc, CMP_STRIDE * HEAD_DIM)
    xin = jnp.stack([to_chunks(kvt[:, 0:KV_WIDTH]), to_chunks(kvt[:, KV_WIDTH:2 * KV_WIDTH])])
    w1 = jnp.stack([cmp_w1_k, cmp_w1_v])
    w2 = jnp.stack([cmp_w2_k, cmp_w2_v])
    pos_flat = jnp.stack([cmp_pos_k.reshape(1, -1), cmp_pos_v.reshape(1, -1)])
    pos_flat = jnp.pad(pos_flat, ((0, 0), (0, 7), (0, 0)))
    kcmp, kcmp_t = _compress(xin, w1, w2, w2.transpose(0, 2, 1), pos_flat)

    o_c, sel = _cmp_attn(qt, kcmp, kcmp_t, gates, jnp.asarray(_overlap_t(T)))

    hot_t = (np.arange(SEL_LANES)[:, None] == np.arange(T)[None, :] // SEL_BLOCK).astype(np.float32)
    o_s = _sel_attn(qt, sel, kvt, 2 * G, 3 * G, jnp.asarray(hot_t, BF16), gates)
    o_w = _win_attn(qt, kvt, 4 * G, 5 * G, gates)

    sub = min(MERGE_SUB, T)
    tri = jnp.asarray(np.triu(np.ones((sub, sub), np.float32), 1), BF16)
    assert N % MOE_BLK == 0
    x1, ri, rg, cnt, xs = _merge(x.reshape(N, D), ya_pre.reshape(N, CONV_CH), o_c, o_s, o_w, wmg,
                             w_up_conv.astype(BF16), w_up_nsa.astype(BF16), w_o.astype(BF16),
                             ln1_g[None, :], ln1_b[None, :], w_router.T, b_router[:, None], tri, alpha)

    counts = cnt[:, 0]
    blocks = (counts + MOE_BLK - 1) // MOE_BLK
    bend = jnp.cumsum(blocks).astype(I32)
    bstart = bend - blocks
    n_blocks = (N * TOP_K) // MOE_BLK + N_EXPERTS
    n_used = bend[-1:]
    b = jnp.minimum(jnp.arange(n_blocks, dtype=I32), n_used - 1)
    block_expert = jnp.minimum(jnp.sum((b[:, None] >= bend[None, :]).astype(I32), axis=1), N_EXPERTS - 1)
    first = jnp.sum(jnp.where(block_expert[:, None] == jnp.arange(N_EXPERTS)[None, :], bstart[None, :], 0), axis=1)
    row_block = block_expert * (N // MOE_BLK) + (b - first)
    count = jnp.sum(jnp.where(block_expert[:, None] == jnp.arange(N_EXPERTS)[None, :], counts[None, :], 0), axis=1)
    valid_rows = jnp.clip(count - (b - first) * MOE_BLK, 0, MOE_BLK).astype(I32)
    dest = (ri[:TOP_K] * N + ri[TOP_K:]).T.reshape(N * TOP_K)

    ys = _experts(block_expert, n_used, row_block, valid_rows, xs, w_gate_up, jnp.asarray(_deinterleave_perm(), BF16),
                  b_gate_up[:, None, 0::2], b_gate_up[:, None, 1::2], w_down, b_down[:, None, :])
    out = _combine(dest, ys, x1, rg.T, ln2_g[None, :], ln2_b[None, :], alpha)
    return out.reshape(B, T, D)


def kernel(x, positions, w_in, conv_w, cmp_pos_k, cmp_w1_k, cmp_w2_k, cmp_pos_v, cmp_w1_v, cmp_w2_v, w_up_conv, w_up_nsa, w_o, ln1_g, ln1_b, w_router, b_router, w_gate_up, b_gate_up, w_down, b_down, ln2_g, ln2_b):
    depth = w_in.shape[0]
    alpha = float((2 * depth) ** 0.25)
    h = x
    for l in range(depth):
        h = _layer(h, positions, w_in[l], conv_w[l], cmp_pos_k[l], cmp_w1_k[l], cmp_w2_k[l],
                   cmp_pos_v[l], cmp_w1_v[l], cmp_w2_v[l], w_up_conv[l], w_up_nsa[l], w_o[l],
                   ln1_g[l], ln1_b[l], w_router[l], b_router[l], w_gate_up[l], b_gate_up[l],
                   w_down[l], b_down[l], ln2_g[l], ln2_b[l], alpha)
    return h
```

```python
import functools

import numpy as np
import jax
import jax.numpy as jnp
from jax import lax
from jax.experimental import pallas as pl
from jax.experimental.pallas import tpu as pltpu

F32 = jnp.float32
BF16 = jnp.bfloat16
I32 = jnp.int32

D_MODEL = 1024
CONV_CH = 512
CONV_K = 3
N_HEADS = 8
N_KV_GROUPS = 2
HEADS_PER_GROUP = N_HEADS // N_KV_GROUPS
HEAD_DIM = 64
NSA_WIDTH = N_HEADS * HEAD_DIM
KV_WIDTH = N_KV_GROUPS * HEAD_DIM
ROPE_DIM = HEAD_DIM // 4
ROPE_THETA = 500000.0
CMP_BLOCK = 32
CMP_STRIDE = 16
CMP_HIDDEN = 256
SEL_BLOCK = 64
N_SELECT = 16
WINDOW = 512
N_EXPERTS = 32
TOP_K = 4
D_FF = 1024
SWIGLU_LIMIT = 7.0
SWIGLU_ALPHA = 1.702
LN_EPS = 1e-5
NEG_INF = -1e30
FORCE_SCORE = 1e4
LOG2E = 1.4426950408889634

LANES = 128
SUBLANES = 8
ROW_SUB = D_MODEL // LANES
assert ROW_SUB == SUBLANES
TAKEN = float("-inf")
SEL_LANES = 64
SEL_MASK_BIAS = -float(2 ** 20)
GROUP_W = HEADS_PER_GROUP * HEAD_DIM
GATE_ROWS = 16

PROJ_TM = 1024
PROJ_SUB = 256
CMP_TQ = 512
SEL_TQ = 512
SEL_KC = 512
SEL_SPLIT = 2
WIN_TQ = 1024
WIN_SUB = 128
MERGE_TM = 512
MERGE_SUB = 256
MOE_BLK = 512
COMB_TM = 512
COMB_GROUP = 128


def _dot(a, b, precision=None):
    return jnp.dot(a, b, precision=precision, preferred_element_type=F32)


def _dot_tn(a, b):
    return lax.dot_general(a, b, (((0,), (0,)), ((), ())), preferred_element_type=F32)


def _dot_nt(a, b, precision=None):
    return lax.dot_general(a, b, (((1,), (1,)), ((), ())), precision=precision, preferred_element_type=F32)


def _proj_kernel(x_ref, pos_ref, wc_ref, wqt_ref, wkvt_ref, wgt_ref, cw_ref, freq_ref,
                 ya_ref, qt_ref, kvt_ref, gt_ref, carry_ref):
    ti = pl.program_id(1)
    tm = x_ref.shape[1]
    sub = min(PROJ_SUB, tm)
    half = ROPE_DIM // 2

    @pl.when(ti == 0)
    def _():
        carry_ref[...] = jnp.zeros_like(carry_ref)

    prev = carry_ref[...]
    cw = cw_ref[...]
    for s0 in range(0, tm, sub):
        rs = slice(s0, s0 + sub)
        xb = x_ref[0, rs, :].astype(BF16)

        pc = _dot(xb, wc_ref[...])
        xv = pc[:, :CONV_CH]
        bg = pc[:, CONV_CH:2 * CONV_CH]
        cg = pc[:, 2 * CONV_CH:]
        u = cg * xv
        row = lax.broadcasted_iota(I32, u.shape, 0)
        u1 = jnp.where(row == 0, prev[7:8], pltpu.roll(u, 1, 0))
        u2 = jnp.where(row == 0, prev[6:7], jnp.where(row == 1, prev[7:8], pltpu.roll(u, 2, 0)))
        conv = cw[2:3] * u + cw[1:2] * u1 + cw[0:1] * u2
        prev = u[sub - 8:]
        ya_ref[0, rs, :] = (bg * conv).astype(BF16)

        ang = freq_ref[...] * pos_ref[0, :, rs]
        cos = jnp.cos(ang)
        sin = jnp.sin(ang)

        def rope_head(t):
            t1 = t[:half]
            t2 = t[half:ROPE_DIM]
            return [t1 * cos - t2 * sin, t2 * cos + t1 * sin, t[ROPE_DIM:]]

        def heads(t, rotate):
            out = []
            for h in range(t.shape[0] // HEAD_DIM):
                th = t[h * HEAD_DIM:(h + 1) * HEAD_DIM]
                out.extend(rope_head(th) if rotate(h) else [th])
            return jnp.concatenate(out, axis=0)

        qt = heads(_dot_nt(wqt_ref[...], xb), lambda h: True)
        qt_ref[0, :, rs] = (qt * (HEAD_DIM ** -0.5 * LOG2E)).astype(BF16)
        kvt = heads(_dot_nt(wkvt_ref[...], xb), lambda h: (h // N_KV_GROUPS) % 2 == 0)
        kvt_ref[0, :, rs] = kvt.astype(BF16)
        gt_ref[0, :, rs] = jax.nn.sigmoid(_dot_nt(wgt_ref[...], xb))
    carry_ref[...] = prev


def _proj(x, pos_row, wc, wqt, wkvt, wgt, cw, freq):
    B, T, D = x.shape
    tm = min(PROJ_TM, T)
    grid = (B, T // tm)
    full = lambda a: pl.BlockSpec(a.shape, lambda b, t: (0,) * a.ndim)
    tok = lambda w: pl.BlockSpec((1, tm, w), lambda b, t: (b, t, 0))
    feat = lambda r: pl.BlockSpec((1, r, tm), lambda b, t: (b, 0, t))
    n_gate = wgt.shape[0]
    return pl.pallas_call(
        _proj_kernel,
        grid=grid,
        in_specs=[tok(D), feat(1), full(wc), full(wqt), full(wkvt), full(wgt), full(cw), full(freq)],
        out_specs=[tok(CONV_CH), feat(NSA_WIDTH), feat(6 * KV_WIDTH), feat(n_gate)],
        out_shape=[jax.ShapeDtypeStruct((B, T, CONV_CH), BF16),
                   jax.ShapeDtypeStruct((B, NSA_WIDTH, T), BF16),
                   jax.ShapeDtypeStruct((B, 6 * KV_WIDTH, T), BF16),
                   jax.ShapeDtypeStruct((B, n_gate, T), F32)],
        scratch_shapes=[pltpu.VMEM((8, CONV_CH), F32)],
        compiler_params=pltpu.CompilerParams(dimension_semantics=("arbitrary", "arbitrary")),
        name="proj",
    )(x, pos_row, wc, wqt, wkvt, wgt, cw, freq)


def _compress_kernel(x_ref, w1_ref, w2_ref, w2t_ref, pos_ref, o_ref, ot_ref):
    xb = x_ref[0, 0]
    nc = xb.shape[0]
    w1 = w1_ref[0]
    w1b = w1.astype(BF16)
    half = CMP_STRIDE * HEAD_DIM
    a = _dot(xb, w1b[:half])
    b = _dot(xb, w1b[half:])
    b_next = pltpu.roll(b, nc - 1, 0)
    pb = _dot(pos_ref[0], w1, precision=lax.Precision.HIGHEST)[0:1]
    h = a + b_next + pb
    g = (0.5 * h * (1.0 + jnp.tanh(np.sqrt(2.0 / np.pi) * (h + 0.044715 * (h * h * h))))).astype(BF16)
    o_ref[0, 0] = _dot(g, w2_ref[0].astype(BF16)).astype(BF16)
    ot_ref[0, 0] = _dot_nt(w2t_ref[0].astype(BF16), g).astype(BF16)


def _compress(xin, w1, w2, w2t, pos):
    _, BG, nc, W = xin.shape
    per = lambda a: pl.BlockSpec((1,) + a.shape[1:], lambda s, i: (s, 0, 0))
    return pl.pallas_call(
        _compress_kernel,
        grid=(2, BG),
        in_specs=[pl.BlockSpec((1, 1, nc, W), lambda s, i: (s, i, 0, 0)), per(w1), per(w2), per(w2t), per(pos)],
        out_specs=[pl.BlockSpec((1, 1, nc, HEAD_DIM), lambda s, i: (s, i, 0, 0)),
                   pl.BlockSpec((1, 1, HEAD_DIM, nc), lambda s, i: (s, i, 0, 0))],
        out_shape=[jax.ShapeDtypeStruct((2, BG, nc, HEAD_DIM), BF16),
                   jax.ShapeDtypeStruct((2, BG, HEAD_DIM, nc), BF16)],
        name="compress",
    )(xin, w1, w2, w2t, pos)


def _head_lanes(qt):
    return jnp.concatenate([qt[r * HEAD_DIM:(r + 1) * HEAD_DIM] for r in range(HEADS_PER_GROUP)], axis=1)


def _gated_out(ot, scale, gates, branch, tq):
    rows = []
    for r in range(HEADS_PER_GROUP):
        c = branch * HEADS_PER_GROUP + r
        sl = slice(r * tq, (r + 1) * tq)
        rows.append(ot[:, sl] * (scale[:, sl] * gates[c:c + 1]))
    return jnp.concatenate(rows, axis=0).astype(BF16)


def _attn_specs(tq):
    G = N_KV_GROUPS
    qspec = pl.BlockSpec((1, GROUP_W, tq), lambda b, g, i: (b, g, i))
    gspec = pl.BlockSpec((1, GATE_ROWS, tq), lambda b, g, i: (b, g, i))
    return G, qspec, gspec


def _cmp_attn_kernel(q_ref, kc_ref, vct_ref, g_ref, ovt_ref, o_ref, sel_ref, score_ref, rank_ref):
    qi = pl.program_id(2)
    tq = q_ref.shape[2]
    q0 = qi * tq
    qt = _head_lanes(q_ref[0])
    s = _dot(kc_ref[0, 0], qt)
    c = lax.broadcasted_iota(I32, s.shape, 0)
    t = q0 + (lax.broadcasted_iota(I32, (1, s.shape[1]), 1) & (tq - 1))
    valid = c <= (t - (CMP_BLOCK - 1)) // CMP_STRIDE
    sm = jnp.where(valid, s, NEG_INF)
    m = jnp.max(sm, axis=0, keepdims=True)
    p = jnp.where(valid, jnp.exp2(sm - m), 0.0)
    l = jnp.sum(p, axis=0, keepdims=True)
    inv = 1.0 / jnp.where(l > 0.0, l, 1.0)
    ot = _dot(vct_ref[0, 0], p.astype(BF16))
    o_ref[0] = _gated_out(ot, inv, g_ref[0], 0, tq)

    pn = p * inv
    psum = pn[:, 0:tq]
    for r in range(1, HEADS_PER_GROUP):
        psum = psum + pn[:, r * tq:(r + 1) * tq]
    imp = _dot(ovt_ref[...], psum, precision=lax.Precision.HIGHEST)
    j = lax.broadcasted_iota(I32, imp.shape, 0)
    cur = (q0 + lax.broadcasted_iota(I32, imp.shape, 1)) // SEL_BLOCK
    valid_b = j <= cur
    forced = (j == 0) | (j == cur) | (j == cur - 1)
    score = jnp.where(valid_b, jnp.where(forced, FORCE_SCORE, imp), NEG_INF)
    sub = SUBLANES
    score_ref[...] = score
    rank_ref[...] = jnp.zeros(rank_ref.shape, I32)
    i_max = (q0 + tq - 1) // SEL_BLOCK
    for i0 in range(0, SEL_LANES, sub):
        @pl.when(i0 <= i_max)
        def _():
            sc = score_ref[...]
            groups = [sc[a:a + sub] for a in range(0, SEL_LANES, sub)]
            jrow = lax.broadcasted_iota(I32, groups[0].shape, 0)
            ranks = [jnp.zeros(g_.shape, I32) for g_ in groups]
            for i in range(i0, i0 + sub):
                si = sc[i:i + 1, :]
                for a, g_ in enumerate(groups):
                    if a > i // sub:
                        inc = jnp.where(si >= g_, 1, 0)
                    elif a < i // sub:
                        inc = jnp.where(si > g_, 1, 0)
                    else:
                        inc = jnp.where(jrow > i % sub, jnp.where(si >= g_, 1, 0), jnp.where(si > g_, 1, 0))
                    ranks[a] = ranks[a] + inc
            rank_ref[...] += jnp.concatenate(ranks, axis=0)
    selected = (rank_ref[...] < N_SELECT) & valid_b
    sel_ref[0, 0] = jnp.where(selected, 0.0, SEL_MASK_BIAS).astype(BF16)


def _cmp_attn(qt, kcmp, vcmp_t, gates, ovt):
    B, _, T = qt.shape
    tq = min(CMP_TQ, T)
    G, qspec, gspec = _attn_specs(tq)
    cspec = lambda a: pl.BlockSpec((1, 1) + a.shape[2:], lambda b, g, i: (0, b * G + g, 0, 0))
    vspec = lambda a: pl.BlockSpec((1, 1) + a.shape[2:], lambda b, g, i: (1, b * G + g, 0, 0))
    return pl.pallas_call(
        _cmp_attn_kernel,
        grid=(B, G, T // tq),
        in_specs=[qspec, cspec(kcmp), vspec(vcmp_t), gspec, pl.BlockSpec(ovt.shape, lambda b, g, i: (0, 0))],
        out_specs=[qspec, pl.BlockSpec((1, 1, SEL_LANES, tq), lambda b, g, i: (b, g, 0, i))],
        out_shape=[jax.ShapeDtypeStruct((B, NSA_WIDTH, T), BF16),
                   jax.ShapeDtypeStruct((B, G, SEL_LANES, T), BF16)],
        scratch_shapes=[pltpu.VMEM((SEL_LANES, tq), F32), pltpu.VMEM((SEL_LANES, tq), I32)],
        name="cmp_attn",
    )(qt, kcmp, vcmp_t, gates, ovt)


def _sel_attn_kernel(q_ref, sel_ref, kt_ref, hot_ref, vt_ref, g_ref, o_ref, *, kc):
    qi = pl.program_id(2)
    tq = q_ref.shape[2]
    q0 = qi * tq
    qt = q_ref[0]
    sb = sel_ref[0, 0]
    qa = jnp.concatenate(
        [jnp.concatenate([qt[r * HEAD_DIM:(r + 1) * HEAD_DIM], sb], axis=0) for r in range(HEADS_PER_GROUP)],
        axis=1)
    gw = qa.shape[1] // SEL_SPLIT
    qas = [qa[:, i * gw:(i + 1) * gw] for i in range(SEL_SPLIT)]

    def step(kstart, kn, carry, diagonal):
        keys = pl.ds(kstart, kn)
        kblk_t = jnp.concatenate([kt_ref[0, :, keys], hot_ref[:, keys]], axis=0)
        vblk = vt_ref[0, :, keys]
        ss = [_dot_tn(kblk_t, qg) for qg in qas]
        out = []
        for gi, (s, (m, l, acc)) in enumerate(zip(ss, carry)):
            if diagonal:
                row = lax.broadcasted_iota(I32, s.shape, 0)
                off = (gi * gw + lax.broadcasted_iota(I32, s.shape, 1)) & (tq - 1)
                s = jnp.where(row <= off, s, NEG_INF)
            mn = jnp.maximum(m, jnp.max(s, axis=0, keepdims=True))
            alpha = jnp.exp2(m - mn)
            p = jnp.exp2(s - mn)
            l = alpha * l + jnp.sum(p, axis=0, keepdims=True)
            acc = alpha * acc + _dot(vblk, p.astype(BF16))
            out.append((mn, l, acc))
        return tuple(out)

    init = tuple((jnp.full((1, gw), NEG_INF, F32), jnp.zeros((1, gw), F32), jnp.zeros((HEAD_DIM, gw), F32))
                 for _ in range(SEL_SPLIT))
    n_full = q0 // kc
    carry = lax.fori_loop(0, n_full, lambda i, cr: step(pl.multiple_of(i * kc, kc), kc, cr, False), init)
    carry = lax.fori_loop(n_full * (kc // tq), qi, lambda i, cr: step(pl.multiple_of(i * tq, tq), tq, cr, False), carry)
    carry = step(pl.multiple_of(q0, tq), tq, carry, True)
    l = jnp.concatenate([c[1] for c in carry], axis=1)
    acc = jnp.concatenate([c[2] for c in carry], axis=1)
    o_ref[0] = _gated_out(acc, 1.0 / l, g_ref[0], 1, tq)


def _sel_attn(qt, sel, kvt, k_row_block, v_row_block, hot_t, gates):
    B, _, T = qt.shape
    tq = min(SEL_TQ, T)
    kc = min(SEL_KC, T)
    G, qspec, gspec = _attn_specs(tq)
    return pl.pallas_call(
        functools.partial(_sel_attn_kernel, kc=kc),
        grid=(B, G, T // tq),
        in_specs=[qspec, pl.BlockSpec((1, 1, SEL_LANES, tq), lambda b, g, i: (b, g, 0, i)),
                  pl.BlockSpec((1, HEAD_DIM, T), lambda b, g, i: (b, k_row_block + g, 0)),
                  pl.BlockSpec(hot_t.shape, lambda b, g, i: (0, 0)),
                  pl.BlockSpec((1, HEAD_DIM, T), lambda b, g, i: (b, v_row_block + g, 0)), gspec],
        out_specs=qspec,
        out_shape=jax.ShapeDtypeStruct((B, NSA_WIDTH, T), BF16),
        name="sel_attn",
    )(qt, sel, kvt, hot_t, kvt, gates)


def _win_attn_kernel(q_ref, kt_ref, vt_ref, g_ref, o_ref, *, span, sub):
    qi = pl.program_id(2)
    tq = q_ref.shape[2]
    T = kt_ref.shape[2]

    def tile(s0, interior):
        cs = slice(s0, s0 + sub)
        q0 = qi * tq + s0
        start = pl.multiple_of(jnp.clip(q0 + sub - span, 0, T - span), sub)
        qt = _head_lanes(q_ref[0, :, cs])
        s = _dot_tn(kt_ref[0, :, pl.ds(start, span)], qt)
        if interior:
            row = lax.broadcasted_iota(I32, (sub, s.shape[1]), 0)
            off = lax.broadcasted_iota(I32, (sub, s.shape[1]), 1) & (sub - 1)
            s = jnp.concatenate([jnp.where(row > off, s[:sub], NEG_INF), s[sub:span - sub],
                                 jnp.where(row <= off, s[span - sub:], NEG_INF)], axis=0)
        else:
            kp = start + lax.broadcasted_iota(I32, s.shape, 0)
            t = q0 + (lax.broadcasted_iota(I32, s.shape, 1) & (sub - 1))
            diff = t - kp
            s = jnp.where((diff >= 0) & (diff < WINDOW), s, NEG_INF)
        m = jnp.max(s, axis=0, keepdims=True)
        p = jnp.exp2(s - m)
        l = jnp.sum(p, axis=0, keepdims=True)
        ot = _dot(vt_ref[0, :, pl.ds(start, span)], p.astype(BF16))
        o_ref[0, :, cs] = _gated_out(ot, 1.0 / l, g_ref[0, :, cs], 2, sub)

    first_interior = -(-WINDOW // tq)
    if span == WINDOW + sub:
        @pl.when(qi >= first_interior)
        def _():
            for s0 in range(0, tq, sub):
                tile(s0, True)

    @pl.when((qi < first_interior) | (span != WINDOW + sub))
    def _():
        for s0 in range(0, tq, sub):
            tile(s0, first_interior == 1 and s0 >= WINDOW and span == WINDOW + sub)


def _win_attn(qt, kvt, k_row_block, v_row_block, gates):
    B, _, T = qt.shape
    tq = min(WIN_TQ, T)
    sub = min(WIN_SUB, T)
    span = min(WINDOW + sub, T)
    G, qspec, gspec = _attn_specs(tq)
    return pl.pallas_call(
        functools.partial(_win_attn_kernel, span=span, sub=sub),
        grid=(B, G, T // tq),
        in_specs=[qspec, pl.BlockSpec((1, HEAD_DIM, T), lambda b, g, i: (b, k_row_block + g, 0)),
                  pl.BlockSpec((1, HEAD_DIM, T), lambda b, g, i: (b, v_row_block + g, 0)), gspec],
        out_specs=qspec,
        out_shape=jax.ShapeDtypeStruct((B, NSA_WIDTH, T), BF16),
        name="win_attn",
    )(qt, kvt, kvt, gates)


def _store_row_tiles(ref, x, row0=0):
    rows = x.shape[0]
    for s in range(ROW_SUB):
        ref[pl.ds(row0 * ROW_SUB + s, rows, stride=ROW_SUB), :] = x[:, s * LANES:(s + 1) * LANES]


def _load_row_tiles(ref):
    rows = ref.shape[0] // ROW_SUB
    return jnp.concatenate([ref[pl.ds(s, rows, stride=ROW_SUB), :] for s in range(ROW_SUB)], axis=1)


def _tile_copy(src, si, dst, di, sem):
    return pltpu.make_async_copy(src.at[pl.ds(pl.multiple_of(si * ROW_SUB, ROW_SUB), ROW_SUB), :],
                                 dst.at[pl.ds(pl.multiple_of(di * ROW_SUB, ROW_SUB), ROW_SUB), :], sem)


def _layer_norm(h, g, b):
    mu = jnp.mean(h, axis=-1, keepdims=True)
    c = h - mu
    var = jnp.mean(c * c, axis=-1, keepdims=True)
    return c * lax.rsqrt(var + LN_EPS) * g + b


def _merge_kernel(x_ref, ya_ref, oc_ref, os_ref, ow_ref, wmg_ref, wuc_ref, wun_ref, wo_ref, g1_ref, b1_ref,
                  wr_ref, br_ref, tri_ref, x1_ref, ri_ref, rg_ref, cnt_ref, xs_hbm,
                  carry_ref, stage, dest_v, dest_s, copy_sem, row_sem, *, alpha, cap):
    i = pl.program_id(0)
    last = pl.num_programs(0) - 1
    sub = tri_ref.shape[0]
    tm = x_ref.shape[0]
    slot = i % 2
    nsub = tm // sub

    def id_copy(si):
        return pltpu.make_async_copy(dest_v.at[si], dest_s.at[si], copy_sem.at[si])

    def issue_rows(si, s):
        id_copy(si).wait()
        for t in range(sub):
            for k in range(TOP_K):
                _tile_copy(stage.at[s], si * sub + t, xs_hbm, dest_s[si, k, t], row_sem.at[s]).start(priority=k % 2)

    def wait_rows(s, rows):
        for k in range(TOP_K):
            n = rows * ROW_SUB
            pltpu.make_async_copy(stage.at[s, pl.ds(0, n), :], xs_hbm.at[pl.ds(0, n), :], row_sem.at[s]).wait()

    @pl.when(i == 0)
    def _():
        carry_ref[...] = jnp.zeros_like(carry_ref)
        stage[1, pl.ds((nsub - 1) * sub * ROW_SUB, sub * ROW_SUB), :] = jnp.zeros((sub * ROW_SUB, LANES), F32)
        spare = (N_EXPERTS * cap + lax.broadcasted_iota(I32, (TOP_K, sub), 0) * sub
                 + lax.broadcasted_iota(I32, (TOP_K, sub), 1))
        dest_v[nsub - 1] = spare
        id_copy(nsub - 1).start()

    wr = wr_ref[...]
    wr_hi = wr.astype(BF16)
    wr_hl = jnp.concatenate([wr_hi, (wr - wr_hi.astype(F32)).astype(BF16)], axis=0)
    total = carry_ref[...]
    for si in range(nsub):
        s0 = si * sub
        rs = slice(s0, s0 + sub)
        x = x_ref[rs, :]
        mg = _dot(x.astype(BF16), wmg_ref[...])
        if si == 0:
            issue_rows(nsub - 1, 1 - slot)
        else:
            issue_rows(si - 1, slot)
        y_a = _dot(ya_ref[rs, :], wuc_ref[...])
        o_nsa_t = (oc_ref[0, :, rs].astype(F32) + os_ref[0, :, rs].astype(F32)
                   + ow_ref[0, :, rs].astype(F32))
        y_b = _dot(o_nsa_t.T.astype(BF16), wun_ref[...])
        merged = jax.nn.sigmoid(mg[:, :D_MODEL]) * y_a + jax.nn.sigmoid(mg[:, D_MODEL:]) * y_b
        h = alpha * x + _dot(merged.astype(BF16), wo_ref[...])
        x1 = _layer_norm(h, g1_ref[...], b1_ref[...])
        x1_ref[rs, :] = x1
        _store_row_tiles(stage.at[slot], x1, s0)

        x_hi = x1.astype(BF16)
        x_lo = (x1 - x_hi.astype(F32)).astype(BF16)
        both = _dot_nt(wr_hl, x_hi)
        logits = both[:N_EXPERTS] + both[N_EXPERTS:] + _dot_nt(wr_hi, x_lo) + br_ref[...]
        expert = lax.broadcasted_iota(I32, logits.shape, 0).astype(F32)
        rem = logits
        vals, idxs, hots = [], [], []
        for _ in range(TOP_K):
            m = jnp.max(rem, axis=0, keepdims=True)
            idx = jnp.min(jnp.where(rem == m, expert, float(N_EXPERTS)), axis=0, keepdims=True)
            hot = expert == idx
            vals.append(m)
            idxs.append(idx)
            hots.append(hot)
            rem = jnp.where(hot, TAKEN, rem)
        es = [jnp.exp(v - vals[0]) for v in vals]
        den = es[0]
        for e in es[1:]:
            den = den + e
        chosen = hots[0]
        for hot in hots[1:]:
            chosen = chosen | hot
        chosen_f = jnp.where(chosen, 1.0, 0.0)

        before = _dot(chosen_f.astype(BF16), tri_ref[...]) + total
        total = total + jnp.sum(chosen_f, axis=1, keepdims=True)
        ranks = [jnp.sum(jnp.where(hot, before, 0.0), axis=0, keepdims=True) for hot in hots]
        ri_ref[:, rs] = jnp.concatenate(idxs + ranks, axis=0).astype(I32)
        rg_ref[:, rs] = jnp.concatenate([e / den for e in es] + [jnp.zeros_like(den)] * TOP_K, axis=0)

        dest_v[si] = jnp.concatenate(idxs, axis=0).astype(I32) * cap + jnp.concatenate(ranks, axis=0).astype(I32)
        id_copy(si).start()

    carry_ref[...] = total
    cnt_ref[...] = jnp.broadcast_to(total, cnt_ref.shape).astype(I32)

    @pl.when(i == 0)
    def _():
        wait_rows(1, sub)

    @pl.when(i > 0)
    def _():
        wait_rows(1 - slot, tm)

    @pl.when(i == last)
    def _():
        issue_rows(nsub - 1, slot)
        wait_rows(slot, tm)


def _merge(x, ya, oc, os_, ow, wmg, wuc, wun, wo, g1, b1, wr, br, tri, alpha):
    N, D = x.shape
    B, _, T = oc.shape
    tm = min(MERGE_TM, T)
    nt = T // tm
    tok = lambda w: pl.BlockSpec((tm, w), lambda i: (i, 0))
    feat = pl.BlockSpec((1, NSA_WIDTH, tm), lambda i: (i // nt, 0, i % nt))
    full = lambda a: pl.BlockSpec(a.shape, lambda i: (0,) * a.ndim)
    sub = tri.shape[0]
    return pl.pallas_call(
        functools.partial(_merge_kernel, alpha=alpha, cap=N),
        grid=(N // tm,),
        in_specs=[tok(D), tok(CONV_CH), feat, feat, feat,
                  full(wmg), full(wuc), full(wun), full(wo), full(g1), full(b1), full(wr), full(br), full(tri)],
        out_specs=[tok(D),
                   pl.BlockSpec((2 * TOP_K, tm), lambda i: (0, i)), pl.BlockSpec((2 * TOP_K, tm), lambda i: (0, i)),
                   pl.BlockSpec((N_EXPERTS, LANES), lambda i: (0, 0)), pl.BlockSpec(memory_space=pl.ANY)],
        out_shape=[jax.ShapeDtypeStruct((N, D), F32),
                   jax.ShapeDtypeStruct((2 * TOP_K, N), I32),
                   jax.ShapeDtypeStruct((2 * TOP_K, N), F32),
                   jax.ShapeDtypeStruct((N_EXPERTS, LANES), I32),
                   jax.ShapeDtypeStruct(((N_EXPERTS * N + TOP_K * sub) * ROW_SUB, LANES), F32)],
        scratch_shapes=[pltpu.VMEM((N_EXPERTS, 1), F32), pltpu.VMEM((2, tm * ROW_SUB, LANES), F32),
                        pltpu.VMEM((tm // sub, TOP_K, sub), I32), pltpu.SMEM((tm // sub, TOP_K, sub), I32),
                        pltpu.SemaphoreType.DMA((tm // sub,)), pltpu.SemaphoreType.DMA((2,))],
        compiler_params=pltpu.CompilerParams(dimension_semantics=("arbitrary",)),
        name="merge",
    )(x, ya, oc, os_, ow, wmg, wuc, wun, wo, g1, b1, wr, br, tri)


def _experts_kernel(be_ref, nu_ref, rb_ref, vr_ref, xs_ref, wgu_ref, perm_ref, bg_ref, bl_ref, wd_ref, bd_ref, ys_ref, wg_s, wl_s, wd_s):
    i = pl.program_id(0)

    @pl.when((i == 0) | (be_ref[i] != be_ref[jnp.maximum(i, 1) - 1]))
    def _():
        w = perm_ref.shape[0]
        for c in range(2 * D_FF // w):
            t = _dot(wgu_ref[0, :, c * w:(c + 1) * w].astype(BF16), perm_ref[...])
            wg_s[:, c * (w // 2):(c + 1) * (w // 2)] = t[:, :w // 2].astype(BF16)
            wl_s[:, c * (w // 2):(c + 1) * (w // 2)] = t[:, w // 2:].astype(BF16)
        wd_s[...] = wd_ref[0].astype(BF16)

    @pl.when(i < nu_ref[0])
    def _():
        row = lax.broadcasted_iota(I32, (MOE_BLK, 1), 0)
        xb = jnp.where(row < vr_ref[i], _load_row_tiles(xs_ref), 0.0).astype(BF16)
        x_glu = jnp.minimum(_dot(xb, wg_s[...]) + bg_ref[0], SWIGLU_LIMIT)
        x_lin = jnp.clip(_dot(xb, wl_s[...]) + bl_ref[0], -SWIGLU_LIMIT, SWIGLU_LIMIT)
        act = x_glu * jax.nn.sigmoid(SWIGLU_ALPHA * x_glu) * (x_lin + 1.0)
        _store_row_tiles(ys_ref, _dot(act.astype(BF16), wd_s[...]) + bd_ref[0])


def _experts(block_expert, n_used, row_block, valid_rows, xs, wgu, perm, bg, bl, wd, bd):
    D = D_MODEL
    n_blocks = block_expert.shape[0]
    rows = lambda i, be, nu, rb, vr: (rb[i], 0)
    wsel = lambda i, be, nu, rb, vr: (be[i], 0, 0)
    return pl.pallas_call(
        _experts_kernel,
        grid_spec=pltpu.PrefetchScalarGridSpec(
            num_scalar_prefetch=4,
            grid=(n_blocks,),
            in_specs=[pl.BlockSpec((MOE_BLK * ROW_SUB, LANES), rows),
                      pl.BlockSpec((1, D, 2 * D_FF), wsel),
                      pl.BlockSpec(perm.shape, lambda i, be, nu, rb, vr: (0, 0)),
                      pl.BlockSpec((1, 1, D_FF), wsel), pl.BlockSpec((1, 1, D_FF), wsel),
                      pl.BlockSpec((1, D_FF, D), wsel), pl.BlockSpec((1, 1, D), wsel)],
            out_specs=pl.BlockSpec((MOE_BLK * ROW_SUB, LANES), rows),
            scratch_shapes=[pltpu.VMEM((D, D_FF), BF16), pltpu.VMEM((D, D_FF), BF16), pltpu.VMEM((D_FF, D), BF16)]),
        out_shape=jax.ShapeDtypeStruct(xs.shape, F32),
        compiler_params=pltpu.CompilerParams(dimension_semantics=("arbitrary",)),
        name="experts",
    )(block_expert, n_used, row_block, valid_rows, xs, wgu, perm, bg, bl, wd, bd)


def _combine_kernel(dest_ref, dest_next_ref, ys_hbm, x1_ref, rg_ref, g2_ref, b2_ref, o_ref, buf, sem, *, alpha):
    i = pl.program_id(0)
    last = pl.num_programs(0) - 1
    tm = o_ref.shape[0]
    slot = i % 2
    other = 1 - slot

    def gather(ids_ref, t, to_slot):
        for k in range(TOP_K):
            _tile_copy(ys_hbm, ids_ref[t * TOP_K + k], buf.at[to_slot, k], t, sem.at[to_slot]).start(priority=k % 2)

    def wait_slot(s):
        for k in range(TOP_K):
            pltpu.make_async_copy(ys_hbm.at[pl.ds(0, tm * ROW_SUB), :], buf.at[s, k], sem.at[s]).wait()

    @pl.when(i == 0)
    def _():
        def first(t, c):
            gather(dest_ref, t, slot)
            return c
        lax.fori_loop(0, tm, first, 0)

    wait_slot(slot)

    gs = min(COMB_GROUP, tm)

    def group(g, c):
        r0 = pl.multiple_of(g * gs, gs)
        for j in range(gs):
            gather(dest_next_ref, r0 + j, other)
        rows = pl.ds(r0, gs)
        gate = rg_ref[rows, :]
        y = alpha * x1_ref[rows, :]
        for k in range(TOP_K):
            tiles = buf.at[slot, k, pl.ds(pl.multiple_of(r0 * ROW_SUB, gs * ROW_SUB), gs * ROW_SUB), :]
            y = y + gate[:, k:k + 1] * _load_row_tiles(tiles)
        o_ref[rows, :] = _layer_norm(y, g2_ref[...], b2_ref[...])
        return c

    lax.fori_loop(0, tm // gs, group, 0)

    @pl.when(i == last)
    def _():
        wait_slot(other)


def _combine(dest, ys, x1, rg, g2, b2, alpha):
    N, D = x1.shape
    tm = min(COMB_TM, N)
    n = N // tm
    tok = lambda w: pl.BlockSpec((tm, w), lambda i: (i, 0))
    full = lambda a: pl.BlockSpec(a.shape, lambda i: (0,) * a.ndim)
    ids = lambda f: pl.BlockSpec((tm * TOP_K,), f, memory_space=pltpu.SMEM)
    return pl.pallas_call(
        functools.partial(_combine_kernel, alpha=alpha),
        grid=(n,),
        in_specs=[ids(lambda i: (i,)), ids(lambda i: (jnp.minimum(i + 1, n - 1),)),
                  pl.BlockSpec(memory_space=pl.ANY), tok(D), tok(rg.shape[1]), full(g2), full(b2)],
        out_specs=tok(D),
        out_shape=jax.ShapeDtypeStruct((N, D), F32),
        scratch_shapes=[pltpu.VMEM((2, TOP_K, tm * ROW_SUB, LANES), F32), pltpu.SemaphoreType.DMA((2,))],
        compiler_params=pltpu.CompilerParams(dimension_semantics=("arbitrary",)),
        name="combine",
    )(dest, dest, ys, x1, rg, g2, b2)


def _rope_freq():
    half = ROPE_DIM // 2
    inv = (np.float32(ROPE_THETA) ** (-np.arange(half, dtype=np.float32) * np.float32(2.0 / ROPE_DIM))).astype(np.float32)
    return inv[:, None]


def _overlap_t(T):
    nc = T // CMP_STRIDE
    c0 = np.arange(nc) * CMP_STRIDE
    j0 = np.arange(SEL_LANES) * SEL_BLOCK
    ov = (c0[None, :] < j0[:, None] + SEL_BLOCK) & (c0[None, :] + CMP_BLOCK > j0[:, None])
    ov &= (np.arange(nc) < nc - 1)[None, :] & (np.arange(SEL_LANES) < T // SEL_BLOCK)[:, None]
    return ov.astype(np.float32)


def _deinterleave_perm():
    w = 2 * LANES
    p = np.zeros((w, w), np.float32)
    p[np.arange(0, w, 2), np.arange(w // 2)] = 1.0
    p[np.arange(1, w, 2), w // 2 + np.arange(w // 2)] = 1.0
    return p


def _token_major(t, B, T):
    return t.reshape(B, N_KV_GROUPS, HEAD_DIM, T).transpose(0, 1, 3, 2)


def _layer(x, positions, w_in, conv_w, cmp_pos_k, cmp_w1_k, cmp_w2_k, cmp_pos_v, cmp_w1_v, cmp_w2_v,
           w_up_conv, w_up_nsa, w_o, ln1_g, ln1_b, w_router, b_router, w_gate_up, b_gate_up,
           w_down, b_down, ln2_g, ln2_b, alpha):
    B, T, D = x.shape
    G, R = N_KV_GROUPS, HEADS_PER_GROUP
    N = B * T
    assert D == D_MODEL and T % SEL_BLOCK == 0 and T // SEL_BLOCK <= SEL_LANES

    c0 = 3 * CONV_CH
    c1 = c0 + NSA_WIDTH
    c2 = c1 + 6 * KV_WIDTH
    c3 = c2 + 3 * N_HEADS
    wc = w_in[:, :c0].astype(BF16)
    wqt = w_in[:, c0:c1].T.astype(BF16)
    wkvt = w_in[:, c1:c2].T.astype(BF16)
    gcols = np.zeros((G * GATE_ROWS,), np.int64)
    gmask = np.zeros((G * GATE_ROWS,), np.float32)
    for g in range(G):
        for br in range(3):
            for r in range(R):
                gcols[g * GATE_ROWS + br * R + r] = br * N_HEADS + g * R + r
                gmask[g * GATE_ROWS + br * R + r] = 1.0
    wgt = (w_in[:, c2:c3][:, gcols] * gmask).T.astype(BF16)
    wmg = w_in[:, c3:].astype(BF16)
    cw = conv_w.reshape(CONV_K, CONV_CH)
    pos_row = positions.astype(F32)[:, None, :]

    ya_pre, qt, kvt, gates = _proj(x, pos_row, wc, wqt, wkvt, wgt, cw, jnp.asarray(_rope_freq()))

    nc = T // CMP_STRIDE
    to_chunks = lambda t: _token_major(t, B, T).reshape(B * G, nc, CMP_STRIDE * HEAD_DIM)
    xin = jnp.stack([to_chunks(kvt[:, 0:KV_WIDTH]), to_chunks(kvt[:, KV_WIDTH:2 * KV_WIDTH])])
    w1 = jnp.stack([cmp_w1_k, cmp_w1_v])
    w2 = jnp.stack([cmp_w2_k, cmp_w2_v])
    pos_flat = jnp.stack([cmp_pos_k.reshape(1, -1), cmp_pos_v.reshape(1, -1)])
    pos_flat = jnp.pad(pos_flat, ((0, 0), (0, 7), (0, 0)))
    kcmp, kcmp_t = _compress(xin, w1, w2, w2.transpose(0, 2, 1), pos_flat)

    o_c, sel = _cmp_attn(qt, kcmp, kcmp_t, gates, jnp.asarray(_overlap_t(T)))

    hot_t = (np.arange(SEL_LANES)[:, None] == np.arange(T)[None, :] // SEL_BLOCK).astype(np.float32)
    o_s = _sel_attn(qt, sel, kvt, 2 * G, 3 * G, jnp.asarray(hot_t, BF16), gates)
    o_w = _win_attn(qt, kvt, 4 * G, 5 * G, gates)

    sub = min(MERGE_SUB, T)
    tri = jnp.asarray(np.triu(np.ones((sub, sub), np.float32), 1), BF16)
    assert N % MOE_BLK == 0
    x1, ri, rg, cnt, xs = _merge(x.reshape(N, D), ya_pre.reshape(N, CONV_CH), o_c, o_s, o_w, wmg,
                             w_up_conv.astype(BF16), w_up_nsa.astype(BF16), w_o.astype(BF16),
                             ln1_g[None, :], ln1_b[None, :], w_router.T, b_router[:, None], tri, alpha)

    counts = cnt[:, 0]
    blocks = (counts + MOE_BLK - 1) // MOE_BLK
    bend = jnp.cumsum(blocks).astype(I32)
    bstart = bend - blocks
    n_blocks = (N * TOP_K) // MOE_BLK + N_EXPERTS
    n_used = bend[-1:]
    b = jnp.minimum(jnp.arange(n_blocks, dtype=I32), n_used - 1)
    block_expert = jnp.minimum(jnp.sum((b[:, None] >= bend[None, :]).astype(I32), axis=1), N_EXPERTS - 1)
    first = jnp.sum(jnp.where(block_expert[:, None] == jnp.arange(N_EXPERTS)[None, :], bstart[None, :], 0), axis=1)
    row_block = block_expert * (N // MOE_BLK) + (b - first)
    count = jnp.sum(jnp.where(block_expert[:, None] == jnp.arange(N_EXPERTS)[None, :], counts[None, :], 0), axis=1)
    valid_rows = jnp.clip(count - (b - first) * MOE_BLK, 0, MOE_BLK).astype(I32)
    dest = (ri[:TOP_K] * N + ri[TOP_K:]).T.reshape(N * TOP_K)

    ys = _experts(block_expert, n_used, row_block, valid_rows, xs, w_gate_up, jnp.asarray(_deinterleave_perm(), BF16),
                  b_gate_up[:, None, 0::2], b_gate_up[:, None, 1::2], w_down, b_down[:, None, :])
    out = _combine(dest, ys, x1, rg.T, ln2_g[None, :], ln2_b[None, :], alpha)
    return out.reshape(B, T, D)


def kernel(x, positions, w_in, conv_w, cmp_pos_k, cmp_w1_k, cmp_w2_k, cmp_pos_v, cmp_w1_v, cmp_w2_v, w_up_conv, w_up_nsa, w_o, ln1_g, ln1_b, w_router, b_router, w_gate_up, b_gate_up, w_down, b_down, ln2_g, ln2_b):
    depth = w_in.shape[0]
    alpha = float((2 * depth) ** 0.25)
    h = x
    for l in range(depth):
        h = _layer(h, positions, w_in[l], conv_w[l], cmp_pos_k[l], cmp_w1_k[l], cmp_w2_k[l],
                   cmp_pos_v[l], cmp_w1_v[l], cmp_w2_v[l], w_up_conv[l], w_up_nsa[l], w_o[l],
                   ln1_g[l], ln1_b[l], w_router[l], b_router[l], w_gate_up[l], b_gate_up[l],
                   w_down[l], b_down[l], ln2_g[l], ln2_b[l], alpha)
    return h
```

```python
import functools

import numpy as np
import jax
import jax.numpy as jnp
from jax import lax
from jax.experimental import pallas as pl
from jax.experimental.pallas import tpu as pltpu

F32 = jnp.float32
BF16 = jnp.bfloat16
I32 = jnp.int32

D_MODEL = 1024
CONV_CH = 512
CONV_K = 3
N_HEADS = 8
N_KV_GROUPS = 2
HEADS_PER_GROUP = N_HEADS // N_KV_GROUPS
HEAD_DIM = 64
NSA_WIDTH = N_HEADS * HEAD_DIM
KV_WIDTH = N_KV_GROUPS * HEAD_DIM
ROPE_DIM = HEAD_DIM // 4
ROPE_THETA = 500000.0
CMP_BLOCK = 32
CMP_STRIDE = 16
CMP_HIDDEN = 256
SEL_BLOCK = 64
N_SELECT = 16
WINDOW = 512
N_EXPERTS = 32
TOP_K = 4
D_FF = 1024
SWIGLU_LIMIT = 7.0
SWIGLU_ALPHA = 1.702
LN_EPS = 1e-5
NEG_INF = -1e30
FORCE_SCORE = 1e4
LOG2E = 1.4426950408889634

LANES = 128
SUBLANES = 8
ROW_SUB = D_MODEL // LANES
assert ROW_SUB == SUBLANES
TAKEN = float("-inf")
SEL_LANES = 64
SEL_MASK_BIAS = -float(2 ** 20)
GROUP_W = HEADS_PER_GROUP * HEAD_DIM
GATE_ROWS = 16

PROJ_TM = 1024
PROJ_SUB = 256
CMP_TQ = 512
SEL_TQ = 512
SEL_KC = 512
SEL_SPLIT = 2
WIN_TQ = 1024
WIN_SUB = 128
MERGE_TM = 512
MERGE_SUB = 256
MOE_BLK = 512
COMB_TM = 512
COMB_GROUP = 128


def _dot(a, b, precision=None):
    return jnp.dot(a, b, precision=precision, preferred_element_type=F32)


def _dot_tn(a, b):
    return lax.dot_general(a, b, (((0,), (0,)), ((), ())), preferred_element_type=F32)


def _dot_nt(a, b, precision=None):
    return lax.dot_general(a, b, (((1,), (1,)), ((), ())), precision=precision, preferred_element_type=F32)


def _proj_kernel(x_ref, pos_ref, wc_ref, wqt_ref, wkvt_ref, wgt_ref, cw_ref, freq_ref,
                 ya_ref, qt_ref, kvt_ref, gt_ref, ctok_ref, carry_ref):
    ti = pl.program_id(1)
    tm = x_ref.shape[1]
    sub = min(PROJ_SUB, tm)
    half = ROPE_DIM // 2

    @pl.when(ti == 0)
    def _():
        carry_ref[...] = jnp.zeros_like(carry_ref)

    prev = carry_ref[...]
    cw = cw_ref[...]
    for s0 in range(0, tm, sub):
        rs = slice(s0, s0 + sub)
        xb = x_ref[0, rs, :].astype(BF16)

        pc = _dot(xb, wc_ref[...])
        xv = pc[:, :CONV_CH]
        bg = pc[:, CONV_CH:2 * CONV_CH]
        cg = pc[:, 2 * CONV_CH:]
        u = cg * xv
        row = lax.broadcasted_iota(I32, u.shape, 0)
        u1 = jnp.where(row == 0, prev[7:8], pltpu.roll(u, 1, 0))
        u2 = jnp.where(row == 0, prev[6:7], jnp.where(row == 1, prev[7:8], pltpu.roll(u, 2, 0)))
        conv = cw[2:3] * u + cw[1:2] * u1 + cw[0:1] * u2
        prev = u[sub - 8:]
        ya_ref[0, rs, :] = (bg * conv).astype(BF16)

        ang = freq_ref[...] * pos_ref[0, :, rs]
        cos = jnp.cos(ang)
        sin = jnp.sin(ang)

        def rope_head(t):
            t1 = t[:half]
            t2 = t[half:ROPE_DIM]
            return [t1 * cos - t2 * sin, t2 * cos + t1 * sin, t[ROPE_DIM:]]

        def heads(t, rotate):
            out = []
            for h in range(t.shape[0] // HEAD_DIM):
                th = t[h * HEAD_DIM:(h + 1) * HEAD_DIM]
                out.extend(rope_head(th) if rotate(h) else [th])
            return jnp.concatenate(out, axis=0)

        qt = heads(_dot_nt(wqt_ref[...], xb), lambda h: True)
        qt_ref[0, :, rs] = (qt * (HEAD_DIM ** -0.5 * LOG2E)).astype(BF16)
        kvt = heads(_dot_nt(wkvt_ref[...], xb), lambda h: (h // N_KV_GROUPS) % 2 == 0)
        kvt_ref[0, :, rs] = kvt.astype(BF16)
        for s in range(2):
            ctok_ref[0, s, rs, :] = kvt[s * KV_WIDTH:(s + 1) * KV_WIDTH].T
        gt_ref[0, :, rs] = jax.nn.sigmoid(_dot_nt(wgt_ref[...], xb))
    carry_ref[...] = prev


def _proj(x, pos_row, wc, wqt, wkvt, wgt, cw, freq):
    B, T, D = x.shape
    tm = min(PROJ_TM, T)
    grid = (B, T // tm)
    full = lambda a: pl.BlockSpec(a.shape, lambda b, t: (0,) * a.ndim)
    tok = lambda w: pl.BlockSpec((1, tm, w), lambda b, t: (b, t, 0))
    feat = lambda r: pl.BlockSpec((1, r, tm), lambda b, t: (b, 0, t))
    n_gate = wgt.shape[0]
    return pl.pallas_call(
        _proj_kernel,
        grid=grid,
        in_specs=[tok(D), feat(1), full(wc), full(wqt), full(wkvt), full(wgt), full(cw), full(freq)],
        out_specs=[tok(CONV_CH), feat(NSA_WIDTH), feat(6 * KV_WIDTH), feat(n_gate),
                   pl.BlockSpec((1, 2, tm, KV_WIDTH), lambda b, t: (b, 0, t, 0))],
        out_shape=[jax.ShapeDtypeStruct((B, T, CONV_CH), BF16),
                   jax.ShapeDtypeStruct((B, NSA_WIDTH, T), BF16),
                   jax.ShapeDtypeStruct((B, 6 * KV_WIDTH, T), BF16),
                   jax.ShapeDtypeStruct((B, n_gate, T), F32),
                   jax.ShapeDtypeStruct((B, 2, T, KV_WIDTH), F32)],
        scratch_shapes=[pltpu.VMEM((8, CONV_CH), F32)],
        compiler_params=pltpu.CompilerParams(dimension_semantics=("arbitrary", "arbitrary")),
        name="proj",
    )(x, pos_row, wc, wqt, wkvt, wgt, cw, freq)


def _compress_kernel(x_ref, w1_ref, w2_ref, w2t_ref, pos_ref, o_ref, ot_ref):
    nc = x_ref.shape[2] // CMP_STRIDE
    half = CMP_STRIDE * HEAD_DIM
    for s in range(2):
        parts = [x_ref[0, s, pl.ds(l, nc, stride=CMP_STRIDE), :].astype(BF16) for l in range(CMP_STRIDE)]
        w1 = w1_ref[s]
        w1b = w1.astype(BF16)
        pb = _dot(pos_ref[s], w1, precision=lax.Precision.HIGHEST)[0:1]
        for g in range(N_KV_GROUPS):
            lo = g * HEAD_DIM
            xb = jnp.concatenate([p[:, lo:lo + HEAD_DIM] for p in parts], axis=1)
            a = _dot(xb, w1b[:half])
            b = _dot(xb, w1b[half:])
            b_next = pltpu.roll(b, nc - 1, 0)
            h = a + b_next + pb
            act = (0.5 * h * (1.0 + jnp.tanh(np.sqrt(2.0 / np.pi) * (h + 0.044715 * (h * h * h))))).astype(BF16)
            o_ref[s, 0, g] = _dot(act, w2_ref[s].astype(BF16)).astype(BF16)
            ot_ref[s, 0, g] = _dot_nt(w2t_ref[s].astype(BF16), act).astype(BF16)


def _compress(ctok, w1, w2, w2t, pos):
    B, _, T, W = ctok.shape
    nc = T // CMP_STRIDE
    G = N_KV_GROUPS
    full = lambda a: pl.BlockSpec(a.shape, lambda b: (0,) * a.ndim)
    return pl.pallas_call(
        _compress_kernel,
        grid=(B,),
        in_specs=[pl.BlockSpec((1, 2, T, W), lambda b: (b, 0, 0, 0)), full(w1), full(w2), full(w2t), full(pos)],
        out_specs=[pl.BlockSpec((2, 1, G, nc, HEAD_DIM), lambda b: (0, b, 0, 0, 0)),
                   pl.BlockSpec((2, 1, G, HEAD_DIM, nc), lambda b: (0, b, 0, 0, 0))],
        out_shape=[jax.ShapeDtypeStruct((2, B, G, nc, HEAD_DIM), BF16),
                   jax.ShapeDtypeStruct((2, B, G, HEAD_DIM, nc), BF16)],
        name="compress",
    )(ctok, w1, w2, w2t, pos)


def _head_lanes(qt):
    return jnp.concatenate([qt[r * HEAD_DIM:(r + 1) * HEAD_DIM] for r in range(HEADS_PER_GROUP)], axis=1)


def _gated_out(ot, scale, gates, branch, tq):
    rows = []
    for r in range(HEADS_PER_GROUP):
        c = branch * HEADS_PER_GROUP + r
        sl = slice(r * tq, (r + 1) * tq)
        rows.append(ot[:, sl] * (scale[:, sl] * gates[c:c + 1]))
    return jnp.concatenate(rows, axis=0).astype(BF16)


def _attn_specs(tq):
    G = N_KV_GROUPS
    qspec = pl.BlockSpec((1, GROUP_W, tq), lambda b, g, i: (b, g, i))
    gspec = pl.BlockSpec((1, GATE_ROWS, tq), lambda b, g, i: (b, g, i))
    return G, qspec, gspec


def _cmp_attn_kernel(q_ref, kc_ref, vct_ref, g_ref, ovt_ref, o_ref, sel_ref, score_ref, rank_ref):
    qi = pl.program_id(2)
    tq = q_ref.shape[2]
    q0 = qi * tq
    qt = _head_lanes(q_ref[0])
    s = _dot(kc_ref[0, 0], qt)
    c = lax.broadcasted_iota(I32, s.shape, 0)
    t = q0 + (lax.broadcasted_iota(I32, (1, s.shape[1]), 1) & (tq - 1))
    valid = c <= (t - (CMP_BLOCK - 1)) // CMP_STRIDE
    sm = jnp.where(valid, s, NEG_INF)
    m = jnp.max(sm, axis=0, keepdims=True)
    p = jnp.where(valid, jnp.exp2(sm - m), 0.0)
    l = jnp.sum(p, axis=0, keepdims=True)
    inv = 1.0 / jnp.where(l > 0.0, l, 1.0)
    ot = _dot(vct_ref[0, 0], p.astype(BF16))
    o_ref[0] = _gated_out(ot, inv, g_ref[0], 0, tq)

    pn = p * inv
    psum = pn[:, 0:tq]
    for r in range(1, HEADS_PER_GROUP):
        psum = psum + pn[:, r * tq:(r + 1) * tq]
    imp = _dot(ovt_ref[...], psum, precision=lax.Precision.HIGHEST)
    j = lax.broadcasted_iota(I32, imp.shape, 0)
    cur = (q0 + lax.broadcasted_iota(I32, imp.shape, 1)) // SEL_BLOCK
    valid_b = j <= cur
    forced = (j == 0) | (j == cur) | (j == cur - 1)
    score = jnp.where(valid_b, jnp.where(forced, FORCE_SCORE, imp), NEG_INF)
    sub = SUBLANES
    score_ref[...] = score
    rank_ref[...] = jnp.zeros(rank_ref.shape, I32)
    i_max = (q0 + tq - 1) // SEL_BLOCK
    for i0 in range(0, SEL_LANES, sub):
        @pl.when(i0 <= i_max)
        def _():
            sc = score_ref[...]
            groups = [sc[a:a + sub] for a in range(0, SEL_LANES, sub)]
            jrow = lax.broadcasted_iota(I32, groups[0].shape, 0)
            ranks = [jnp.zeros(g_.shape, I32) for g_ in groups]
            for i in range(i0, i0 + sub):
                si = sc[i:i + 1, :]
                for a, g_ in enumerate(groups):
                    if a > i // sub:
                        inc = jnp.where(si >= g_, 1, 0)
                    elif a < i // sub:
                        inc = jnp.where(si > g_, 1, 0)
                    else:
                        inc = jnp.where(jrow > i % sub, jnp.where(si >= g_, 1, 0), jnp.where(si > g_, 1, 0))
                    ranks[a] = ranks[a] + inc
            rank_ref[...] += jnp.concatenate(ranks, axis=0)
    selected = (rank_ref[...] < N_SELECT) & valid_b
    sel_ref[0, 0] = jnp.where(selected, 0.0, SEL_MASK_BIAS).astype(BF16)


def _cmp_attn(qt, kcmp, vcmp_t, gates, ovt):
    B, _, T = qt.shape
    tq = min(CMP_TQ, T)
    G, qspec, gspec = _attn_specs(tq)
    cspec = lambda a: pl.BlockSpec((1, 1) + a.shape[2:], lambda b, g, i: (0, b * G + g, 0, 0))
    vspec = lambda a: pl.BlockSpec((1, 1) + a.shape[2:], lambda b, g, i: (1, b * G + g, 0, 0))
    return pl.pallas_call(
        _cmp_attn_kernel,
        grid=(B, G, T // tq),
        in_specs=[qspec, cspec(kcmp), vspec(vcmp_t), gspec, pl.BlockSpec(ovt.shape, lambda b, g, i: (0, 0))],
        out_specs=[qspec, pl.BlockSpec((1, 1, SEL_LANES, tq), lambda b, g, i: (b, g, 0, i))],
        out_shape=[jax.ShapeDtypeStruct((B, NSA_WIDTH, T), BF16),
                   jax.ShapeDtypeStruct((B, G, SEL_LANES, T), BF16)],
        scratch_shapes=[pltpu.VMEM((SEL_LANES, tq), F32), pltpu.VMEM((SEL_LANES, tq), I32)],
        name="cmp_attn",
    )(qt, kcmp, vcmp_t, gates, ovt)


def _sel_attn_kernel(q_ref, sel_ref, kt_ref, hot_ref, vt_ref, g_ref, o_ref, *, kc):
    qi = pl.program_id(2)
    tq = q_ref.shape[2]
    q0 = qi * tq
    qt = q_ref[0]
    sb = sel_ref[0, 0]
    qa = jnp.concatenate(
        [jnp.concatenate([qt[r * HEAD_DIM:(r + 1) * HEAD_DIM], sb], axis=0) for r in range(HEADS_PER_GROUP)],
        axis=1)
    gw = qa.shape[1] // SEL_SPLIT
    qas = [qa[:, i * gw:(i + 1) * gw] for i in range(SEL_SPLIT)]

    def step(kstart, kn, carry, diagonal):
        keys = pl.ds(kstart, kn)
        kblk_t = jnp.concatenate([kt_ref[0, :, keys], hot_ref[:, keys]], axis=0)
        vblk = vt_ref[0, :, keys]
        ss = [_dot_tn(kblk_t, qg) for qg in qas]
        out = []
        for gi, (s, (m, l, acc)) in enumerate(zip(ss, carry)):
            if diagonal:
                row = lax.broadcasted_iota(I32, s.shape, 0)
                off = (gi * gw + lax.broadcasted_iota(I32, s.shape, 1)) & (tq - 1)
                s = jnp.where(row <= off, s, NEG_INF)
            mn = jnp.maximum(m, jnp.max(s, axis=0, keepdims=True))
            alpha = jnp.exp2(m - mn)
            p = jnp.exp2(s - mn)
            l = alpha * l + jnp.sum(p, axis=0, keepdims=True)
            acc = alpha * acc + _dot(vblk, p.astype(BF16))
            out.append((mn, l, acc))
        return tuple(out)

    init = tuple((jnp.full((1, gw), NEG_INF, F32), jnp.zeros((1, gw), F32), jnp.zeros((HEAD_DIM, gw), F32))
                 for _ in range(SEL_SPLIT))
    n_full = q0 // kc
    carry = lax.fori_loop(0, n_full, lambda i, cr: step(pl.multiple_of(i * kc, kc), kc, cr, False), init)
    carry = lax.fori_loop(n_full * (kc // tq), qi, lambda i, cr: step(pl.multiple_of(i * tq, tq), tq, cr, False), carry)
    carry = step(pl.multiple_of(q0, tq), tq, carry, True)
    l = jnp.concatenate([c[1] for c in carry], axis=1)
    acc = jnp.concatenate([c[2] for c in carry], axis=1)
    o_ref[0] = _gated_out(acc, 1.0 / l, g_ref[0], 1, tq)


def _sel_attn(qt, sel, kvt, k_row_block, v_row_block, hot_t, gates):
    B, _, T = qt.shape
    tq = min(SEL_TQ, T)
    kc = min(SEL_KC, T)
    G, qspec, gspec = _attn_specs(tq)
    return pl.pallas_call(
        functools.partial(_sel_attn_kernel, kc=kc),
        grid=(B, G, T // tq),
        in_specs=[qspec, pl.BlockSpec((1, 1, SEL_LANES, tq), lambda b, g, i: (b, g, 0, i)),
                  pl.BlockSpec((1, HEAD_DIM, T), lambda b, g, i: (b, k_row_block + g, 0)),
                  pl.BlockSpec(hot_t.shape, lambda b, g, i: (0, 0)),
                  pl.BlockSpec((1, HEAD_DIM, T), lambda b, g, i: (b, v_row_block + g, 0)), gspec],
        out_specs=qspec,
        out_shape=jax.ShapeDtypeStruct((B, NSA_WIDTH, T), BF16),
        name="sel_attn",
    )(qt, sel, kvt, hot_t, kvt, gates)


def _win_attn_kernel(q_ref, kt_ref, vt_ref, g_ref, o_ref, *, span, sub):
    qi = pl.program_id(2)
    tq = q_ref.shape[2]
    T = kt_ref.shape[2]

    def tile(s0, interior):
        cs = slice(s0, s0 + sub)
        q0 = qi * tq + s0
        start = pl.multiple_of(jnp.clip(q0 + sub - span, 0, T - span), sub)
        qt = _head_lanes(q_ref[0, :, cs])
        s = _dot_tn(kt_ref[0, :, pl.ds(start, span)], qt)
        if interior:
            row = lax.broadcasted_iota(I32, (sub, s.shape[1]), 0)
            off = lax.broadcasted_iota(I32, (sub, s.shape[1]), 1) & (sub - 1)
            s = jnp.concatenate([jnp.where(row > off, s[:sub], NEG_INF), s[sub:span - sub],
                                 jnp.where(row <= off, s[span - sub:], NEG_INF)], axis=0)
        else:
            kp = start + lax.broadcasted_iota(I32, s.shape, 0)
            t = q0 + (lax.broadcasted_iota(I32, s.shape, 1) & (sub - 1))
            diff = t - kp
            s = jnp.where((diff >= 0) & (diff < WINDOW), s, NEG_INF)
        m = jnp.max(s, axis=0, keepdims=True)
        p = jnp.exp2(s - m)
        l = jnp.sum(p, axis=0, keepdims=True)
        ot = _dot(vt_ref[0, :, pl.ds(start, span)], p.astype(BF16))
        o_ref[0, :, cs] = _gated_out(ot, 1.0 / l, g_ref[0, :, cs], 2, sub)

    first_interior = -(-WINDOW // tq)
    if span == WINDOW + sub:
        @pl.when(qi >= first_interior)
        def _():
            for s0 in range(0, tq, sub):
                tile(s0, True)

    @pl.when((qi < first_interior) | (span != WINDOW + sub))
    def _():
        for s0 in range(0, tq, sub):
            tile(s0, first_interior == 1 and s0 >= WINDOW and span == WINDOW + sub)


def _win_attn(qt, kvt, k_row_block, v_row_block, gates):
    B, _, T = qt.shape
    tq = min(WIN_TQ, T)
    sub = min(WIN_SUB, T)
    span = min(WINDOW + sub, T)
    G, qspec, gspec = _attn_specs(tq)
    return pl.pallas_call(
        functools.partial(_win_attn_kernel, span=span, sub=sub),
        grid=(B, G, T // tq),
        in_specs=[qspec, pl.BlockSpec((1, HEAD_DIM, T), lambda b, g, i: (b, k_row_block + g, 0)),
                  pl.BlockSpec((1, HEAD_DIM, T), lambda b, g, i: (b, v_row_block + g, 0)), gspec],
        out_specs=qspec,
        out_shape=jax.ShapeDtypeStruct((B, NSA_WIDTH, T), BF16),
        name="win_attn",
    )(qt, kvt, kvt, gates)


def _store_row_tiles(ref, x, row0=0):
    rows = x.shape[0]
    for s in range(ROW_SUB):
        ref[pl.ds(row0 * ROW_SUB + s, rows, stride=ROW_SUB), :] = x[:, s * LANES:(s + 1) * LANES]


def _load_row_tiles(ref):
    rows = ref.shape[0] // ROW_SUB
    return jnp.concatenate([ref[pl.ds(s, rows, stride=ROW_SUB), :] for s in range(ROW_SUB)], axis=1)


def _tile_copy(src, si, dst, di, sem):
    return pltpu.make_async_copy(src.at[pl.ds(pl.multiple_of(si * ROW_SUB, ROW_SUB), ROW_SUB), :],
                                 dst.at[pl.ds(pl.multiple_of(di * ROW_SUB, ROW_SUB), ROW_SUB), :], sem)


def _layer_norm(h, g, b):
    mu = jnp.mean(h, axis=-1, keepdims=True)
    c = h - mu
    var = jnp.mean(c * c, axis=-1, keepdims=True)
    return c * lax.rsqrt(var + LN_EPS) * g + b


def _merge_kernel(x_ref, ya_ref, oc_ref, os_ref, ow_ref, wmg_ref, wuc_ref, wun_ref, wo_ref, g1_ref, b1_ref,
                  wr_ref, br_ref, tri_ref, x1_ref, ri_ref, rg_ref, cnt_ref, xs_hbm,
                  carry_ref, stage, dest_v, dest_s, copy_sem, row_sem, *, alpha, cap):
    i = pl.program_id(0)
    last = pl.num_programs(0) - 1
    sub = tri_ref.shape[0]
    tm = x_ref.shape[0]
    slot = i % 2
    nsub = tm // sub

    def id_copy(si):
        return pltpu.make_async_copy(dest_v.at[si], dest_s.at[si], copy_sem.at[si])

    def issue_rows(si, s):
        id_copy(si).wait()
        for t in range(sub):
            for k in range(TOP_K):
                _tile_copy(stage.at[s], si * sub + t, xs_hbm, dest_s[si, k, t], row_sem.at[s]).start(priority=k % 2)

    def wait_rows(s, rows):
        for k in range(TOP_K):
            n = rows * ROW_SUB
            pltpu.make_async_copy(stage.at[s, pl.ds(0, n), :], xs_hbm.at[pl.ds(0, n), :], row_sem.at[s]).wait()

    @pl.when(i == 0)
    def _():
        carry_ref[...] = jnp.zeros_like(carry_ref)
        stage[1, pl.ds((nsub - 1) * sub * ROW_SUB, sub * ROW_SUB), :] = jnp.zeros((sub * ROW_SUB, LANES), F32)
        spare = (N_EXPERTS * cap + lax.broadcasted_iota(I32, (TOP_K, sub), 0) * sub
                 + lax.broadcasted_iota(I32, (TOP_K, sub), 1))
        dest_v[nsub - 1] = spare
        id_copy(nsub - 1).start()

    wr = wr_ref[...]
    wr_hi = wr.astype(BF16)
    wr_hl = jnp.concatenate([wr_hi, (wr - wr_hi.astype(F32)).astype(BF16)], axis=0)
    total = carry_ref[...]
    for si in range(nsub):
        s0 = si * sub
        rs = slice(s0, s0 + sub)
        x = x_ref[rs, :]
        mg = _dot(x.astype(BF16), wmg_ref[...])
        if si == 0:
            issue_rows(nsub - 1, 1 - slot)
        else:
            issue_rows(si - 1, slot)
        y_a = _dot(ya_ref[rs, :], wuc_ref[...])
        o_nsa_t = (oc_ref[0, :, rs].astype(F32) + os_ref[0, :, rs].astype(F32)
                   + ow_ref[0, :, rs].astype(F32))
        y_b = _dot(o_nsa_t.T.astype(BF16), wun_ref[...])
        merged = jax.nn.sigmoid(mg[:, :D_MODEL]) * y_a + jax.nn.sigmoid(mg[:, D_MODEL:]) * y_b
        h = alpha * x + _dot(merged.astype(BF16), wo_ref[...])
        x1 = _layer_norm(h, g1_ref[...], b1_ref[...])
        x1_ref[rs, :] = x1
        _store_row_tiles(stage.at[slot], x1, s0)

        x_hi = x1.astype(BF16)
        x_lo = (x1 - x_hi.astype(F32)).astype(BF16)
        both = _dot_nt(wr_hl, x_hi)
        logits = both[:N_EXPERTS] + both[N_EXPERTS:] + _dot_nt(wr_hi, x_lo) + br_ref[...]
        expert = lax.broadcasted_iota(I32, logits.shape, 0).astype(F32)
        rem = logits
        vals, idxs, hots = [], [], []
        for _ in range(TOP_K):
            m = jnp.max(rem, axis=0, keepdims=True)
            idx = jnp.min(jnp.where(rem == m, expert, float(N_EXPERTS)), axis=0, keepdims=True)
            hot = expert == idx
            vals.append(m)
            idxs.append(idx)
            hots.append(hot)
            rem = jnp.where(hot, TAKEN, rem)
        es = [jnp.exp(v - vals[0]) for v in vals]
        den = es[0]
        for e in es[1:]:
            den = den + e
        chosen = hots[0]
        for hot in hots[1:]:
            chosen = chosen | hot
        chosen_f = jnp.where(chosen, 1.0, 0.0)

        before = _dot(chosen_f.astype(BF16), tri_ref[...]) + total
        total = total + jnp.sum(chosen_f, axis=1, keepdims=True)
        ranks = [jnp.sum(jnp.where(hot, before, 0.0), axis=0, keepdims=True) for hot in hots]
        ri_ref[:, rs] = jnp.concatenate(idxs + ranks, axis=0).astype(I32)
        rg_ref[:, rs] = jnp.concatenate([e / den for e in es] + [jnp.zeros_like(den)] * TOP_K, axis=0)

        dest_v[si] = jnp.concatenate(idxs, axis=0).astype(I32) * cap + jnp.concatenate(ranks, axis=0).astype(I32)
        id_copy(si).start()

    carry_ref[...] = total
    cnt_ref[...] = jnp.broadcast_to(total, cnt_ref.shape).astype(I32)

    @pl.when(i == 0)
    def _():
        wait_rows(1, sub)

    @pl.when(i > 0)
    def _():
        wait_rows(1 - slot, tm)

    @pl.when(i == last)
    def _():
        issue_rows(nsub - 1, slot)
        wait_rows(slot, tm)


def _merge(x, ya, oc, os_, ow, wmg, wuc, wun, wo, g1, b1, wr, br, tri, alpha):
    N, D = x.shape
    B, _, T = oc.shape
    tm = min(MERGE_TM, T)
    nt = T // tm
    tok = lambda w: pl.BlockSpec((tm, w), lambda i: (i, 0))
    feat = pl.BlockSpec((1, NSA_WIDTH, tm), lambda i: (i // nt, 0, i % nt))
    full = lambda a: pl.BlockSpec(a.shape, lambda i: (0,) * a.ndim)
    sub = tri.shape[0]
    return pl.pallas_call(
        functools.partial(_merge_kernel, alpha=alpha, cap=N),
        grid=(N // tm,),
        in_specs=[tok(D), tok(CONV_CH), feat, feat, feat,
                  full(wmg), full(wuc), full(wun), full(wo), full(g1), full(b1), full(wr), full(br), full(tri)],
        out_specs=[tok(D),
                   pl.BlockSpec((2 * TOP_K, tm), lambda i: (0, i)), pl.BlockSpec((2 * TOP_K, tm), lambda i: (0, i)),
                   pl.BlockSpec((N_EXPERTS, LANES), lambda i: (0, 0)), pl.BlockSpec(memory_space=pl.ANY)],
        out_shape=[jax.ShapeDtypeStruct((N, D), F32),
                   jax.ShapeDtypeStruct((2 * TOP_K, N), I32),
                   jax.ShapeDtypeStruct((2 * TOP_K, N), F32),
                   jax.ShapeDtypeStruct((N_EXPERTS, LANES), I32),
                   jax.ShapeDtypeStruct(((N_EXPERTS * N + TOP_K * sub) * ROW_SUB, LANES), F32)],
        scratch_shapes=[pltpu.VMEM((N_EXPERTS, 1), F32), pltpu.VMEM((2, tm * ROW_SUB, LANES), F32),
                        pltpu.VMEM((tm // sub, TOP_K, sub), I32), pltpu.SMEM((tm // sub, TOP_K, sub), I32),
                        pltpu.SemaphoreType.DMA((tm // sub,)), pltpu.SemaphoreType.DMA((2,))],
        compiler_params=pltpu.CompilerParams(dimension_semantics=("arbitrary",)),
        name="merge",
    )(x, ya, oc, os_, ow, wmg, wuc, wun, wo, g1, b1, wr, br, tri)


def _experts_kernel(be_ref, nu_ref, rb_ref, vr_ref, xs_ref, wgu_ref, perm_ref, bg_ref, bl_ref, wd_ref, bd_ref, ys_ref, wg_s, wl_s, wd_s):
    i = pl.program_id(0)

    @pl.when((i == 0) | (be_ref[i] != be_ref[jnp.maximum(i, 1) - 1]))
    def _():
        w = perm_ref.shape[0]
        for c in range(2 * D_FF // w):
            t = _dot(wgu_ref[0, :, c * w:(c + 1) * w].astype(BF16), perm_ref[...])
            wg_s[:, c * (w // 2):(c + 1) * (w // 2)] = t[:, :w // 2].astype(BF16)
            wl_s[:, c * (w // 2):(c + 1) * (w // 2)] = t[:, w // 2:].astype(BF16)
        wd_s[...] = wd_ref[0].astype(BF16)

    @pl.when(i < nu_ref[0])
    def _():
        row = lax.broadcasted_iota(I32, (MOE_BLK, 1), 0)
        xb = jnp.where(row < vr_ref[i], _load_row_tiles(xs_ref), 0.0).astype(BF16)
        x_glu = jnp.minimum(_dot(xb, wg_s[...]) + bg_ref[0], SWIGLU_LIMIT)
        x_lin = jnp.clip(_dot(xb, wl_s[...]) + bl_ref[0], -SWIGLU_LIMIT, SWIGLU_LIMIT)
        act = x_glu * jax.nn.sigmoid(SWIGLU_ALPHA * x_glu) * (x_lin + 1.0)
        _store_row_tiles(ys_ref, _dot(act.astype(BF16), wd_s[...]) + bd_ref[0])


def _experts(block_expert, n_used, row_block, valid_rows, xs, wgu, perm, bg, bl, wd, bd):
    D = D_MODEL
    n_blocks = block_expert.shape[0]
    rows = lambda i, be, nu, rb, vr: (rb[i], 0)
    wsel = lambda i, be, nu, rb, vr: (be[i], 0, 0)
    return pl.pallas_call(
        _experts_kernel,
        grid_spec=pltpu.PrefetchScalarGridSpec(
            num_scalar_prefetch=4,
            grid=(n_blocks,),
            in_specs=[pl.BlockSpec((MOE_BLK * ROW_SUB, LANES), rows),
                      pl.BlockSpec((1, D, 2 * D_FF), wsel),
                      pl.BlockSpec(perm.shape, lambda i, be, nu, rb, vr: (0, 0)),
                      pl.BlockSpec((1, 1, D_FF), wsel), pl.BlockSpec((1, 1, D_FF), wsel),
                      pl.BlockSpec((1, D_FF, D), wsel), pl.BlockSpec((1, 1, D), wsel)],
            out_specs=pl.BlockSpec((MOE_BLK * ROW_SUB, LANES), rows),
            scratch_shapes=[pltpu.VMEM((D, D_FF), BF16), pltpu.VMEM((D, D_FF), BF16), pltpu.VMEM((D_FF, D), BF16)]),
        out_shape=jax.ShapeDtypeStruct(xs.shape, F32),
        compiler_params=pltpu.CompilerParams(dimension_semantics=("arbitrary",)),
        name="experts",
    )(block_expert, n_used, row_block, valid_rows, xs, wgu, perm, bg, bl, wd, bd)


def _combine_kernel(dest_ref, dest_next_ref, ys_hbm, x1_ref, rg_ref, g2_ref, b2_ref, o_ref, buf, sem, *, alpha):
    i = pl.program_id(0)
    last = pl.num_programs(0) - 1
    tm = o_ref.shape[0]
    slot = i % 2
    other = 1 - slot

    def gather(ids_ref, t, to_slot):
        for k in range(TOP_K):
            _tile_copy(ys_hbm, ids_ref[t * TOP_K + k], buf.at[to_slot, k], t, sem.at[to_slot]).start(priority=k % 2)

    def wait_slot(s):
        for k in range(TOP_K):
            pltpu.make_async_copy(ys_hbm.at[pl.ds(0, tm * ROW_SUB), :], buf.at[s, k], sem.at[s]).wait()

    @pl.when(i == 0)
    def _():
        def first(t, c):
            gather(dest_ref, t, slot)
            return c
        lax.fori_loop(0, tm, first, 0)

    wait_slot(slot)

    gs = min(COMB_GROUP, tm)

    def group(g, c):
        r0 = pl.multiple_of(g * gs, gs)
        for j in range(gs):
            gather(dest_next_ref, r0 + j, other)
        rows = pl.ds(r0, gs)
        gate = rg_ref[rows, :]
        y = alpha * x1_ref[rows, :]
        for k in range(TOP_K):
            tiles = buf.at[slot, k, pl.ds(pl.multiple_of(r0 * ROW_SUB, gs * ROW_SUB), gs * ROW_SUB), :]
            y = y + gate[:, k:k + 1] * _load_row_tiles(tiles)
        o_ref[rows, :] = _layer_norm(y, g2_ref[...], b2_ref[...])
        return c

    lax.fori_loop(0, tm // gs, group, 0)

    @pl.when(i == last)
    def _():
        wait_slot(other)


def _combine(dest, ys, x1, rg, g2, b2, alpha):
    N, D = x1.shape
    tm = min(COMB_TM, N)
    n = N // tm
    tok = lambda w: pl.BlockSpec((tm, w), lambda i: (i, 0))
    full = lambda a: pl.BlockSpec(a.shape, lambda i: (0,) * a.ndim)
    ids = lambda f: pl.BlockSpec((tm * TOP_K,), f, memory_space=pltpu.SMEM)
    return pl.pallas_call(
        functools.partial(_combine_kernel, alpha=alpha),
        grid=(n,),
        in_specs=[ids(lambda i: (i,)), ids(lambda i: (jnp.minimum(i + 1, n - 1),)),
                  pl.BlockSpec(memory_space=pl.ANY), tok(D), tok(rg.shape[1]), full(g2), full(b2)],
        out_specs=tok(D),
        out_shape=jax.ShapeDtypeStruct((N, D), F32),
        scratch_shapes=[pltpu.VMEM((2, TOP_K, tm * ROW_SUB, LANES), F32), pltpu.SemaphoreType.DMA((2,))],
        compiler_params=pltpu.CompilerParams(dimension_semantics=("arbitrary",)),
        name="combine",
    )(dest, dest, ys, x1, rg, g2, b2)


def _rope_freq():
    half = ROPE_DIM // 2
    inv = (np.float32(ROPE_THETA) ** (-np.arange(half, dtype=np.float32) * np.float32(2.0 / ROPE_DIM))).astype(np.float32)
    return inv[:, None]


def _overlap_t(T):
    nc = T // CMP_STRIDE
    c0 = np.arange(nc) * CMP_STRIDE
    j0 = np.arange(SEL_LANES) * SEL_BLOCK
    ov = (c0[None, :] < j0[:, None] + SEL_BLOCK) & (c0[None, :] + CMP_BLOCK > j0[:, None])
    ov &= (np.arange(nc) < nc - 1)[None, :] & (np.arange(SEL_LANES) < T // SEL_BLOCK)[:, None]
    return ov.astype(np.float32)


def _deinterleave_perm():
    w = 2 * LANES
    p = np.zeros((w, w), np.float32)
    p[np.arange(0, w, 2), np.arange(w // 2)] = 1.0
    p[np.arange(1, w, 2), w // 2 + np.arange(w // 2)] = 1.0
    return p


def _layer(x, positions, w_in, conv_w, cmp_pos_k, cmp_w1_k, cmp_w2_k, cmp_pos_v, cmp_w1_v, cmp_w2_v,
           w_up_conv, w_up_nsa, w_o, ln1_g, ln1_b, w_router, b_router, w_gate_up, b_gate_up,
           w_down, b_down, ln2_g, ln2_b, alpha):
    B, T, D = x.shape
    G, R = N_KV_GROUPS, HEADS_PER_GROUP
    N = B * T
    assert D == D_MODEL and T % SEL_BLOCK == 0 and T // SEL_BLOCK <= SEL_LANES

    c0 = 3 * CONV_CH
    c1 = c0 + NSA_WIDTH
    c2 = c1 + 6 * KV_WIDTH
    c3 = c2 + 3 * N_HEADS
    wc = w_in[:, :c0].astype(BF16)
    wqt = w_in[:, c0:c1].T.astype(BF16)
    wkvt = w_in[:, c1:c2].T.astype(BF16)
    gcols = np.zeros((G * GATE_ROWS,), np.int64)
    gmask = np.zeros((G * GATE_ROWS,), np.float32)
    for g in range(G):
        for br in range(3):
            for r in range(R):
                gcols[g * GATE_ROWS + br * R + r] = br * N_HEADS + g * R + r
                gmask[g * GATE_ROWS + br * R + r] = 1.0
    wgt = (w_in[:, c2:c3][:, gcols] * gmask).T.astype(BF16)
    wmg = w_in[:, c3:].astype(BF16)
    cw = conv_w.reshape(CONV_K, CONV_CH)
    pos_row = positions.astype(F32)[:, None, :]

    ya_pre, qt, kvt, gates, ctok = _proj(x, pos_row, wc, wqt, wkvt, wgt, cw, jnp.asarray(_rope_freq()))

    nc = T // CMP_STRIDE
    w1 = jnp.stack([cmp_w1_k, cmp_w1_v])
    w2 = jnp.stack([cmp_w2_k, cmp_w2_v])
    pos_flat = jnp.stack([cmp_pos_k.reshape(1, -1), cmp_pos_v.reshape(1, -1)])
    pos_flat = jnp.pad(pos_flat, ((0, 0), (0, 7), (0, 0)))
    kcmp, kcmp_t = _compress(ctok, w1, w2, w2.transpose(0, 2, 1), pos_flat)
    kcmp = kcmp.reshape(2, B * G, nc, HEAD_DIM)
    kcmp_t = kcmp_t.reshape(2, B * G, HEAD_DIM, nc)

    o_c, sel = _cmp_attn(qt, kcmp, kcmp_t, gates, jnp.asarray(_overlap_t(T)))

    hot_t = (np.arange(SEL_LANES)[:, None] == np.arange(T)[None, :] // SEL_BLOCK).astype(np.float32)
    o_s = _sel_attn(qt, sel, kvt, 2 * G, 3 * G, jnp.asarray(hot_t, BF16), gates)
    o_w = _win_attn(qt, kvt, 4 * G, 5 * G, gates)

    sub = min(MERGE_SUB, T)
    tri = jnp.asarray(np.triu(np.ones((sub, sub), np.float32), 1), BF16)
    assert N % MOE_BLK == 0
    x1, ri, rg, cnt, xs = _merge(x.reshape(N, D), ya_pre.reshape(N, CONV_CH), o_c, o_s, o_w, wmg,
                             w_up_conv.astype(BF16), w_up_nsa.astype(BF16), w_o.astype(BF16),
                             ln1_g[None, :], ln1_b[None, :], w_router.T, b_router[:, None], tri, alpha)

    counts = cnt[:, 0]
    blocks = (counts + MOE_BLK - 1) // MOE_BLK
    bend = jnp.cumsum(blocks).astype(I32)
    bstart = bend - blocks
    n_blocks = (N * TOP_K) // MOE_BLK + N_EXPERTS
    n_used = bend[-1:]
    b = jnp.minimum(jnp.arange(n_blocks, dtype=I32), n_used - 1)
    block_expert = jnp.minimum(jnp.sum((b[:, None] >= bend[None, :]).astype(I32), axis=1), N_EXPERTS - 1)
    first = jnp.sum(jnp.where(block_expert[:, None] == jnp.arange(N_EXPERTS)[None, :], bstart[None, :], 0), axis=1)
    row_block = block_expert * (N // MOE_BLK) + (b - first)
    count = jnp.sum(jnp.where(block_expert[:, None] == jnp.arange(N_EXPERTS)[None, :], counts[None, :], 0), axis=1)
    valid_rows = jnp.clip(count - (b - first) * MOE_BLK, 0, MOE_BLK).astype(I32)
    dest = (ri[:TOP_K] * N + ri[TOP_K:]).T.reshape(N * TOP_K)

    ys = _experts(block_expert, n_used, row_block, valid_rows, xs, w_gate_up, jnp.asarray(_deinterleave_perm(), BF16),
                  b_gate_up[:, None, 0::2], b_gate_up[:, None, 1::2], w_down, b_down[:, None, :])
    out = _combine(dest, ys, x1, rg.T, ln2_g[None, :], ln2_b[None, :], alpha)
    return out.reshape(B, T, D)


def kernel(x, positions, w_in, conv_w, cmp_pos_k, cmp_w1_k, cmp_w2_k, cmp_pos_v, cmp_w1_v, cmp_w2_v, w_up_conv, w_up_nsa, w_o, ln1_g, ln1_b, w_router, b_router, w_gate_up, b_gate_up, w_down, b_down, ln2_g, ln2_b):
    depth = w_in.shape[0]
    alpha = float((2 * depth) ** 0.25)
    h = x
    for l in range(depth):
        h = _layer(h, positions, w_in[l], conv_w[l], cmp_pos_k[l], cmp_w1_k[l], cmp_w2_k[l],
                   cmp_pos_v[l], cmp_w1_v[l], cmp_w2_v[l], w_up_conv[l], w_up_nsa[l], w_o[l],
                   ln1_g[l], ln1_b[l], w_router[l], b_router[l], w_gate_up[l], b_gate_up[l],
                   w_down[l], b_down[l], ln2_g[l], ln2_b[l], alpha)
    return h
```

```python
import functools

import numpy as np
import jax
import jax.numpy as jnp
from jax import lax
from jax.experimental import pallas as pl
from jax.experimental.pallas import tpu as pltpu

F32 = jnp.float32
BF16 = jnp.bfloat16
I32 = jnp.int32

D_MODEL = 1024
CONV_CH = 512
CONV_K = 3
N_HEADS = 8
N_KV_GROUPS = 2
HEADS_PER_GROUP = N_HEADS // N_KV_GROUPS
HEAD_DIM = 64
NSA_WIDTH = N_HEADS * HEAD_DIM
KV_WIDTH = N_KV_GROUPS * HEAD_DIM
ROPE_DIM = HEAD_DIM // 4
ROPE_THETA = 500000.0
CMP_BLOCK = 32
CMP_STRIDE = 16
CMP_HIDDEN = 256
SEL_BLOCK = 64
N_SELECT = 16
WINDOW = 512
N_EXPERTS = 32
TOP_K = 4
D_FF = 1024
SWIGLU_LIMIT = 7.0
SWIGLU_ALPHA = 1.702
LN_EPS = 1e-5
NEG_INF = -1e30
FORCE_SCORE = 1e4
LOG2E = 1.4426950408889634

LANES = 128
SUBLANES = 8
ROW_SUB = D_MODEL // LANES
assert ROW_SUB == SUBLANES
TAKEN = float("-inf")
SEL_LANES = 64
SEL_MASK_BIAS = -float(2 ** 20)
GROUP_W = HEADS_PER_GROUP * HEAD_DIM
GATE_ROWS = 16

PROJ_TM = 1024
PROJ_SUB = 256
CMP_TQ = 512
SEL_TQ = 512
SEL_KC = 512
SEL_SPLIT = 2
WIN_TQ = 1024
WIN_SUB = 128
MERGE_TM = 512
MERGE_SUB = 256
MOE_BLK = 512
COMB_TM = 512
COMB_GROUP = 256


def _dot(a, b, precision=None):
    return jnp.dot(a, b, precision=precision, preferred_element_type=F32)


def _dot_tn(a, b):
    return lax.dot_general(a, b, (((0,), (0,)), ((), ())), preferred_element_type=F32)


def _dot_nt(a, b, precision=None):
    return lax.dot_general(a, b, (((1,), (1,)), ((), ())), precision=precision, preferred_element_type=F32)


def _proj_kernel(x_ref, pos_ref, wc_ref, wqt_ref, wkvt_ref, wgt_ref, cw_ref, freq_ref,
                 ya_ref, qt_ref, kvt_ref, gt_ref, ctok_ref, carry_ref):
    ti = pl.program_id(1)
    tm = x_ref.shape[1]
    sub = min(PROJ_SUB, tm)
    half = ROPE_DIM // 2

    @pl.when(ti == 0)
    def _():
        carry_ref[...] = jnp.zeros_like(carry_ref)

    prev = carry_ref[...]
    cw = cw_ref[...]
    for s0 in range(0, tm, sub):
        rs = slice(s0, s0 + sub)
        xb = x_ref[0, rs, :].astype(BF16)

        pc = _dot(xb, wc_ref[...])
        xv = pc[:, :CONV_CH]
        bg = pc[:, CONV_CH:2 * CONV_CH]
        cg = pc[:, 2 * CONV_CH:]
        u = cg * xv
        row = lax.broadcasted_iota(I32, u.shape, 0)
        u1 = jnp.where(row == 0, prev[7:8], pltpu.roll(u, 1, 0))
        u2 = jnp.where(row == 0, prev[6:7], jnp.where(row == 1, prev[7:8], pltpu.roll(u, 2, 0)))
        conv = cw[2:3] * u + cw[1:2] * u1 + cw[0:1] * u2
        prev = u[sub - 8:]
        ya_ref[0, rs, :] = (bg * conv).astype(BF16)

        ang = freq_ref[...] * pos_ref[0, :, rs]
        cos = jnp.cos(ang)
        sin = jnp.sin(ang)

        def rope_head(t):
            t1 = t[:half]
            t2 = t[half:ROPE_DIM]
            return [t1 * cos - t2 * sin, t2 * cos + t1 * sin, t[ROPE_DIM:]]

        def heads(t, rotate):
            out = []
            for h in range(t.shape[0] // HEAD_DIM):
                th = t[h * HEAD_DIM:(h + 1) * HEAD_DIM]
                out.extend(rope_head(th) if rotate(h) else [th])
            return jnp.concatenate(out, axis=0)

        qt = heads(_dot_nt(wqt_ref[...], xb), lambda h: True)
        qt_ref[0, :, rs] = (qt * (HEAD_DIM ** -0.5 * LOG2E)).astype(BF16)
        kvt = heads(_dot_nt(wkvt_ref[...], xb), lambda h: (h // N_KV_GROUPS) % 2 == 0)
        kvt_ref[0, :, rs] = kvt.astype(BF16)
        for s in range(2):
            ctok_ref[0, s, rs, :] = kvt[s * KV_WIDTH:(s + 1) * KV_WIDTH].T
        gt_ref[0, :, rs] = jax.nn.sigmoid(_dot_nt(wgt_ref[...], xb))
    carry_ref[...] = prev


def _proj(x, pos_row, wc, wqt, wkvt, wgt, cw, freq):
    B, T, D = x.shape
    tm = min(PROJ_TM, T)
    grid = (B, T // tm)
    full = lambda a: pl.BlockSpec(a.shape, lambda b, t: (0,) * a.ndim)
    tok = lambda w: pl.BlockSpec((1, tm, w), lambda b, t: (b, t, 0))
    feat = lambda r: pl.BlockSpec((1, r, tm), lambda b, t: (b, 0, t))
    n_gate = wgt.shape[0]
    return pl.pallas_call(
        _proj_kernel,
        grid=grid,
        in_specs=[tok(D), feat(1), full(wc), full(wqt), full(wkvt), full(wgt), full(cw), full(freq)],
        out_specs=[tok(CONV_CH), feat(NSA_WIDTH), feat(6 * KV_WIDTH), feat(n_gate),
                   pl.BlockSpec((1, 2, tm, KV_WIDTH), lambda b, t: (b, 0, t, 0))],
        out_shape=[jax.ShapeDtypeStruct((B, T, CONV_CH), BF16),
                   jax.ShapeDtypeStruct((B, NSA_WIDTH, T), BF16),
                   jax.ShapeDtypeStruct((B, 6 * KV_WIDTH, T), BF16),
                   jax.ShapeDtypeStruct((B, n_gate, T), F32),
                   jax.ShapeDtypeStruct((B, 2, T, KV_WIDTH), F32)],
        scratch_shapes=[pltpu.VMEM((8, CONV_CH), F32)],
        compiler_params=pltpu.CompilerParams(dimension_semantics=("arbitrary", "arbitrary")),
        name="proj",
    )(x, pos_row, wc, wqt, wkvt, wgt, cw, freq)


def _compress_kernel(x_ref, w1_ref, w2_ref, w2t_ref, pos_ref, o_ref, ot_ref):
    nc = x_ref.shape[2] // CMP_STRIDE
    half = CMP_STRIDE * HEAD_DIM
    for s in range(2):
        parts = [x_ref[0, s, pl.ds(l, nc, stride=CMP_STRIDE), :].astype(BF16) for l in range(CMP_STRIDE)]
        w1 = w1_ref[s]
        w1b = w1.astype(BF16)
        pb = _dot(pos_ref[s], w1, precision=lax.Precision.HIGHEST)[0:1]
        for g in range(N_KV_GROUPS):
            lo = g * HEAD_DIM
            xb = jnp.concatenate([p[:, lo:lo + HEAD_DIM] for p in parts], axis=1)
            a = _dot(xb, w1b[:half])
            b = _dot(xb, w1b[half:])
            b_next = pltpu.roll(b, nc - 1, 0)
            h = a + b_next + pb
            act = (0.5 * h * (1.0 + jnp.tanh(np.sqrt(2.0 / np.pi) * (h + 0.044715 * (h * h * h))))).astype(BF16)
            o_ref[s, 0, g] = _dot(act, w2_ref[s].astype(BF16)).astype(BF16)
            ot_ref[s, 0, g] = _dot_nt(w2t_ref[s].astype(BF16), act).astype(BF16)


def _compress(ctok, w1, w2, w2t, pos):
    B, _, T, W = ctok.shape
    nc = T // CMP_STRIDE
    G = N_KV_GROUPS
    full = lambda a: pl.BlockSpec(a.shape, lambda b: (0,) * a.ndim)
    return pl.pallas_call(
        _compress_kernel,
        grid=(B,),
        in_specs=[pl.BlockSpec((1, 2, T, W), lambda b: (b, 0, 0, 0)), full(w1), full(w2), full(w2t), full(pos)],
        out_specs=[pl.BlockSpec((2, 1, G, nc, HEAD_DIM), lambda b: (0, b, 0, 0, 0)),
                   pl.BlockSpec((2, 1, G, HEAD_DIM, nc), lambda b: (0, b, 0, 0, 0))],
        out_shape=[jax.ShapeDtypeStruct((2, B, G, nc, HEAD_DIM), BF16),
                   jax.ShapeDtypeStruct((2, B, G, HEAD_DIM, nc), BF16)],
        name="compress",
    )(ctok, w1, w2, w2t, pos)


def _head_lanes(qt):
    return jnp.concatenate([qt[r * HEAD_DIM:(r + 1) * HEAD_DIM] for r in range(HEADS_PER_GROUP)], axis=1)


def _gated_out(ot, scale, gates, branch, tq):
    rows = []
    for r in range(HEADS_PER_GROUP):
        c = branch * HEADS_PER_GROUP + r
        sl = slice(r * tq, (r + 1) * tq)
        rows.append(ot[:, sl] * (scale[:, sl] * gates[c:c + 1]))
    return jnp.concatenate(rows, axis=0).astype(BF16)


def _attn_specs(tq):
    G = N_KV_GROUPS
    qspec = pl.BlockSpec((1, GROUP_W, tq), lambda b, g, i: (b, g, i))
    gspec = pl.BlockSpec((1, GATE_ROWS, tq), lambda b, g, i: (b, g, i))
    return G, qspec, gspec


def _cmp_attn_kernel(q_ref, kc_ref, vct_ref, g_ref, ovt_ref, o_ref, sel_ref, score_ref, rank_ref):
    qi = pl.program_id(2)
    tq = q_ref.shape[2]
    q0 = qi * tq
    qt = _head_lanes(q_ref[0])
    s = _dot(kc_ref[0, 0], qt)
    c = lax.broadcasted_iota(I32, s.shape, 0)
    t = q0 + (lax.broadcasted_iota(I32, (1, s.shape[1]), 1) & (tq - 1))
    valid = c <= (t - (CMP_BLOCK - 1)) // CMP_STRIDE
    sm = jnp.where(valid, s, NEG_INF)
    m = jnp.max(sm, axis=0, keepdims=True)
    p = jnp.where(valid, jnp.exp2(sm - m), 0.0)
    l = jnp.sum(p, axis=0, keepdims=True)
    inv = 1.0 / jnp.where(l > 0.0, l, 1.0)
    ot = _dot(vct_ref[0, 0], p.astype(BF16))
    o_ref[0] = _gated_out(ot, inv, g_ref[0], 0, tq)

    pn = p * inv
    psum = pn[:, 0:tq]
    for r in range(1, HEADS_PER_GROUP):
        psum = psum + pn[:, r * tq:(r + 1) * tq]
    imp = _dot(ovt_ref[...], psum, precision=lax.Precision.HIGHEST)
    j = lax.broadcasted_iota(I32, imp.shape, 0)
    cur = (q0 + lax.broadcasted_iota(I32, imp.shape, 1)) // SEL_BLOCK
    valid_b = j <= cur
    forced = (j == 0) | (j == cur) | (j == cur - 1)
    score = jnp.where(valid_b, jnp.where(forced, FORCE_SCORE, imp), NEG_INF)
    sub = SUBLANES
    score_ref[...] = score
    rank_ref[...] = jnp.zeros(rank_ref.shape, I32)
    i_max = (q0 + tq - 1) // SEL_BLOCK
    for i0 in range(0, SEL_LANES, sub):
        @pl.when(i0 <= i_max)
        def _():
            sc = score_ref[...]
            groups = [sc[a:a + sub] for a in range(0, SEL_LANES, sub)]
            jrow = lax.broadcasted_iota(I32, groups[0].shape, 0)
            ranks = [jnp.zeros(g_.shape, I32) for g_ in groups]
            for i in range(i0, i0 + sub):
                si = sc[i:i + 1, :]
                for a, g_ in enumerate(groups):
                    if a > i // sub:
                        inc = jnp.where(si >= g_, 1, 0)
                    elif a < i // sub:
                        inc = jnp.where(si > g_, 1, 0)
                    else:
                        inc = jnp.where(jrow > i % sub, jnp.where(si >= g_, 1, 0), jnp.where(si > g_, 1, 0))
                    ranks[a] = ranks[a] + inc
            rank_ref[...] += jnp.concatenate(ranks, axis=0)
    selected = (rank_ref[...] < N_SELECT) & valid_b
    sel_ref[0, 0] = jnp.where(selected, 0.0, SEL_MASK_BIAS).astype(BF16)


def _cmp_attn(qt, kcmp, vcmp_t, gates, ovt):
    B, _, T = qt.shape
    tq = min(CMP_TQ, T)
    G, qspec, gspec = _attn_specs(tq)
    cspec = lambda a: pl.BlockSpec((1, 1) + a.shape[2:], lambda b, g, i: (0, b * G + g, 0, 0))
    vspec = lambda a: pl.BlockSpec((1, 1) + a.shape[2:], lambda b, g, i: (1, b * G + g, 0, 0))
    return pl.pallas_call(
        _cmp_attn_kernel,
        grid=(B, G, T // tq),
        in_specs=[qspec, cspec(kcmp), vspec(vcmp_t), gspec, pl.BlockSpec(ovt.shape, lambda b, g, i: (0, 0))],
        out_specs=[qspec, pl.BlockSpec((1, 1, SEL_LANES, tq), lambda b, g, i: (b, g, 0, i))],
        out_shape=[jax.ShapeDtypeStruct((B, NSA_WIDTH, T), BF16),
                   jax.ShapeDtypeStruct((B, G, SEL_LANES, T), BF16)],
        scratch_shapes=[pltpu.VMEM((SEL_LANES, tq), F32), pltpu.VMEM((SEL_LANES, tq), I32)],
        name="cmp_attn",
    )(qt, kcmp, vcmp_t, gates, ovt)


def _sel_attn_kernel(q_ref, sel_ref, kt_ref, hot_ref, vt_ref, g_ref, o_ref, ka_s, *, kc):
    qi = pl.program_id(2)
    tq = q_ref.shape[2]
    q0 = qi * tq

    @pl.when(qi == 0)
    def _():
        for c0 in range(0, ka_s.shape[0], kc):
            ka_t = jnp.concatenate([kt_ref[0, :, c0:c0 + kc], hot_ref[:, c0:c0 + kc]], axis=0)
            ka_s[c0:c0 + kc, :] = ka_t.astype(F32).T.astype(BF16)

    qt = q_ref[0]
    sb = sel_ref[0, 0]
    qa = jnp.concatenate(
        [jnp.concatenate([qt[r * HEAD_DIM:(r + 1) * HEAD_DIM], sb], axis=0) for r in range(HEADS_PER_GROUP)],
        axis=1)
    gw = qa.shape[1] // SEL_SPLIT
    qas = [qa[:, i * gw:(i + 1) * gw] for i in range(SEL_SPLIT)]

    def step(kstart, kn, carry, diagonal):
        keys = pl.ds(kstart, kn)
        kblk = ka_s[keys, :]
        vblk = vt_ref[0, :, keys]
        ss = [_dot(kblk, qg) for qg in qas]
        out = []
        for gi, (s, (m, l, acc)) in enumerate(zip(ss, carry)):
            if diagonal:
                row = lax.broadcasted_iota(I32, s.shape, 0)
                off = (gi * gw + lax.broadcasted_iota(I32, s.shape, 1)) & (tq - 1)
                s = jnp.where(row <= off, s, NEG_INF)
            mn = jnp.maximum(m, jnp.max(s, axis=0, keepdims=True))
            alpha = jnp.exp2(m - mn)
            p = jnp.exp2(s - mn)
            l = alpha * l + jnp.sum(p, axis=0, keepdims=True)
            acc = alpha * acc + _dot(vblk, p.astype(BF16))
            out.append((mn, l, acc))
        return tuple(out)

    init = tuple((jnp.full((1, gw), NEG_INF, F32), jnp.zeros((1, gw), F32), jnp.zeros((HEAD_DIM, gw), F32))
                 for _ in range(SEL_SPLIT))
    n_full = q0 // kc
    carry = lax.fori_loop(0, n_full, lambda i, cr: step(pl.multiple_of(i * kc, kc), kc, cr, False), init)
    carry = lax.fori_loop(n_full * (kc // tq), qi, lambda i, cr: step(pl.multiple_of(i * tq, tq), tq, cr, False), carry)
    carry = step(pl.multiple_of(q0, tq), tq, carry, True)
    l = jnp.concatenate([c[1] for c in carry], axis=1)
    acc = jnp.concatenate([c[2] for c in carry], axis=1)
    o_ref[0] = _gated_out(acc, 1.0 / l, g_ref[0], 1, tq)


def _sel_attn(qt, sel, kvt, k_row_block, v_row_block, hot_t, gates):
    B, _, T = qt.shape
    tq = min(SEL_TQ, T)
    kc = min(SEL_KC, T)
    G, qspec, gspec = _attn_specs(tq)
    return pl.pallas_call(
        functools.partial(_sel_attn_kernel, kc=kc),
        grid=(B, G, T // tq),
        in_specs=[qspec, pl.BlockSpec((1, 1, SEL_LANES, tq), lambda b, g, i: (b, g, 0, i)),
                  pl.BlockSpec((1, HEAD_DIM, T), lambda b, g, i: (b, k_row_block + g, 0)),
                  pl.BlockSpec(hot_t.shape, lambda b, g, i: (0, 0)),
                  pl.BlockSpec((1, HEAD_DIM, T), lambda b, g, i: (b, v_row_block + g, 0)), gspec],
        out_specs=qspec,
        out_shape=jax.ShapeDtypeStruct((B, NSA_WIDTH, T), BF16),
        scratch_shapes=[pltpu.VMEM((T, HEAD_DIM + SEL_LANES), BF16)],
        name="sel_attn",
    )(qt, sel, kvt, hot_t, kvt, gates)


def _win_attn_kernel(q_ref, kt_ref, vt_ref, g_ref, o_ref, *, span, sub):
    qi = pl.program_id(2)
    tq = q_ref.shape[2]
    T = kt_ref.shape[2]

    def tile(s0, interior):
        cs = slice(s0, s0 + sub)
        q0 = qi * tq + s0
        start = pl.multiple_of(jnp.clip(q0 + sub - span, 0, T - span), sub)
        qt = _head_lanes(q_ref[0, :, cs])
        s = _dot_tn(kt_ref[0, :, pl.ds(start, span)], qt)
        if interior:
            row = lax.broadcasted_iota(I32, (sub, s.shape[1]), 0)
            off = lax.broadcasted_iota(I32, (sub, s.shape[1]), 1) & (sub - 1)
            s = jnp.concatenate([jnp.where(row > off, s[:sub], NEG_INF), s[sub:span - sub],
                                 jnp.where(row <= off, s[span - sub:], NEG_INF)], axis=0)
        else:
            kp = start + lax.broadcasted_iota(I32, s.shape, 0)
            t = q0 + (lax.broadcasted_iota(I32, s.shape, 1) & (sub - 1))
            diff = t - kp
            s = jnp.where((diff >= 0) & (diff < WINDOW), s, NEG_INF)
        m = jnp.max(s, axis=0, keepdims=True)
        p = jnp.exp2(s - m)
        l = jnp.sum(p, axis=0, keepdims=True)
        ot = _dot(vt_ref[0, :, pl.ds(start, span)], p.astype(BF16))
        o_ref[0, :, cs] = _gated_out(ot, 1.0 / l, g_ref[0, :, cs], 2, sub)

    first_interior = -(-WINDOW // tq)
    if span == WINDOW + sub:
        @pl.when(qi >= first_interior)
        def _():
            for s0 in range(0, tq, sub):
                tile(s0, True)

    @pl.when((qi < first_interior) | (span != WINDOW + sub))
    def _():
        for s0 in range(0, tq, sub):
            tile(s0, first_interior == 1 and s0 >= WINDOW and span == WINDOW + sub)


def _win_attn(qt, kvt, k_row_block, v_row_block, gates):
    B, _, T = qt.shape
    tq = min(WIN_TQ, T)
    sub = min(WIN_SUB, T)
    span = min(WINDOW + sub, T)
    G, qspec, gspec = _attn_specs(tq)
    return pl.pallas_call(
        functools.partial(_win_attn_kernel, span=span, sub=sub),
        grid=(B, G, T // tq),
        in_specs=[qspec, pl.BlockSpec((1, HEAD_DIM, T), lambda b, g, i: (b, k_row_block + g, 0)),
                  pl.BlockSpec((1, HEAD_DIM, T), lambda b, g, i: (b, v_row_block + g, 0)), gspec],
        out_specs=qspec,
        out_shape=jax.ShapeDtypeStruct((B, NSA_WIDTH, T), BF16),
        name="win_attn",
    )(qt, kvt, kvt, gates)


def _store_row_tiles(ref, x, row0=0):
    rows = x.shape[0]
    for s in range(ROW_SUB):
        ref[pl.ds(row0 * ROW_SUB + s, rows, stride=ROW_SUB), :] = x[:, s * LANES:(s + 1) * LANES]


def _load_row_tiles(ref):
    rows = ref.shape[0] // ROW_SUB
    return jnp.concatenate([ref[pl.ds(s, rows, stride=ROW_SUB), :] for s in range(ROW_SUB)], axis=1)


def _tile_copy(src, si, dst, di, sem):
    return pltpu.make_async_copy(src.at[pl.ds(pl.multiple_of(si * ROW_SUB, ROW_SUB), ROW_SUB), :],
                                 dst.at[pl.ds(pl.multiple_of(di * ROW_SUB, ROW_SUB), ROW_SUB), :], sem)


def _layer_norm(h, g, b):
    mu = jnp.mean(h, axis=-1, keepdims=True)
    c = h - mu
    var = jnp.mean(c * c, axis=-1, keepdims=True)
    return c * lax.rsqrt(var + LN_EPS) * g + b


def _merge_kernel(x_ref, ya_ref, oc_ref, os_ref, ow_ref, wmg_ref, wuc_ref, wun_ref, wo_ref, g1_ref, b1_ref,
                  wr_ref, br_ref, tri_ref, x1_ref, ri_ref, rg_ref, cnt_ref, xs_hbm,
                  carry_ref, stage, dest_v, dest_s, copy_sem, row_sem, *, alpha, cap):
    i = pl.program_id(0)
    last = pl.num_programs(0) - 1
    sub = tri_ref.shape[0]
    tm = x_ref.shape[0]
    slot = i % 2
    nsub = tm // sub

    def id_copy(si):
        return pltpu.make_async_copy(dest_v.at[si], dest_s.at[si], copy_sem.at[si])

    def issue_rows(si, s):
        id_copy(si).wait()
        for t in range(sub):
            for k in range(TOP_K):
                _tile_copy(stage.at[s], si * sub + t, xs_hbm, dest_s[si, k, t], row_sem.at[s]).start(priority=k % 2)

    def wait_rows(s, rows):
        for k in range(TOP_K):
            n = rows * ROW_SUB
            pltpu.make_async_copy(stage.at[s, pl.ds(0, n), :], xs_hbm.at[pl.ds(0, n), :], row_sem.at[s]).wait()

    @pl.when(i == 0)
    def _():
        carry_ref[...] = jnp.zeros_like(carry_ref)
        stage[1, pl.ds((nsub - 1) * sub * ROW_SUB, sub * ROW_SUB), :] = jnp.zeros((sub * ROW_SUB, LANES), F32)
        spare = (N_EXPERTS * cap + lax.broadcasted_iota(I32, (TOP_K, sub), 0) * sub
                 + lax.broadcasted_iota(I32, (TOP_K, sub), 1))
        dest_v[nsub - 1] = spare
        id_copy(nsub - 1).start()

    wr = wr_ref[...]
    wr_hi = wr.astype(BF16)
    wr_hl = jnp.concatenate([wr_hi, (wr - wr_hi.astype(F32)).astype(BF16)], axis=0)
    total = carry_ref[...]
    for si in range(nsub):
        s0 = si * sub
        rs = slice(s0, s0 + sub)
        x = x_ref[rs, :]
        mg = _dot(x.astype(BF16), wmg_ref[...])
        if si == 0:
            issue_rows(nsub - 1, 1 - slot)
        else:
            issue_rows(si - 1, slot)
        y_a = _dot(ya_ref[rs, :], wuc_ref[...])
        o_nsa_t = (oc_ref[0, :, rs].astype(F32) + os_ref[0, :, rs].astype(F32)
                   + ow_ref[0, :, rs].astype(F32))
        y_b = _dot(o_nsa_t.T.astype(BF16), wun_ref[...])
        merged = jax.nn.sigmoid(mg[:, :D_MODEL]) * y_a + jax.nn.sigmoid(mg[:, D_MODEL:]) * y_b
        h = alpha * x + _dot(merged.astype(BF16), wo_ref[...])
        x1 = _layer_norm(h, g1_ref[...], b1_ref[...])
        x1_ref[rs, :] = x1
        _store_row_tiles(stage.at[slot], x1, s0)

        x_hi = x1.astype(BF16)
        x_lo = (x1 - x_hi.astype(F32)).astype(BF16)
        both = _dot_nt(wr_hl, x_hi)
        logits = both[:N_EXPERTS] + both[N_EXPERTS:] + _dot_nt(wr_hi, x_lo) + br_ref[...]
        expert = lax.broadcasted_iota(I32, logits.shape, 0).astype(F32)
        rem = logits
        vals, idxs, hots = [], [], []
        for _ in range(TOP_K):
            m = jnp.max(rem, axis=0, keepdims=True)
            idx = jnp.min(jnp.where(rem == m, expert, float(N_EXPERTS)), axis=0, keepdims=True)
            hot = expert == idx
            vals.append(m)
            idxs.append(idx)
            hots.append(hot)
            rem = jnp.where(hot, TAKEN, rem)
        es = [jnp.exp(v - vals[0]) for v in vals]
        den = es[0]
        for e in es[1:]:
            den = den + e
        chosen = hots[0]
        for hot in hots[1:]:
            chosen = chosen | hot
        chosen_f = jnp.where(chosen, 1.0, 0.0)

        before = _dot(chosen_f.astype(BF16), tri_ref[...]) + total
        total = total + jnp.sum(chosen_f, axis=1, keepdims=True)
        ranks = [jnp.sum(jnp.where(hot, before, 0.0), axis=0, keepdims=True) for hot in hots]
        ri_ref[:, rs] = jnp.concatenate(idxs + ranks, axis=0).astype(I32)
        rg_ref[:, rs] = jnp.concatenate([e / den for e in es] + [jnp.zeros_like(den)] * TOP_K, axis=0)

        dest_v[si] = jnp.concatenate(idxs, axis=0).astype(I32) * cap + jnp.concatenate(ranks, axis=0).astype(I32)
        id_copy(si).start()

    carry_ref[...] = total
    cnt_ref[...] = jnp.broadcast_to(total, cnt_ref.shape).astype(I32)

    @pl.when(i == 0)
    def _():
        wait_rows(1, sub)

    @pl.when(i > 0)
    def _():
        wait_rows(1 - slot, tm)

    @pl.when(i == last)
    def _():
        issue_rows(nsub - 1, slot)
        wait_rows(slot, tm)


def _merge(x, ya, oc, os_, ow, wmg, wuc, wun, wo, g1, b1, wr, br, tri, alpha):
    N, D = x.shape
    B, _, T = oc.shape
    tm = min(MERGE_TM, T)
    nt = T // tm
    tok = lambda w: pl.BlockSpec((tm, w), lambda i: (i, 0))
    feat = pl.BlockSpec((1, NSA_WIDTH, tm), lambda i: (i // nt, 0, i % nt))
    full = lambda a: pl.BlockSpec(a.shape, lambda i: (0,) * a.ndim)
    sub = tri.shape[0]
    return pl.pallas_call(
        functools.partial(_merge_kernel, alpha=alpha, cap=N),
        grid=(N // tm,),
        in_specs=[tok(D), tok(CONV_CH), feat, feat, feat,
                  full(wmg), full(wuc), full(wun), full(wo), full(g1), full(b1), full(wr), full(br), full(tri)],
        out_specs=[tok(D),
                   pl.BlockSpec((2 * TOP_K, tm), lambda i: (0, i)), pl.BlockSpec((2 * TOP_K, tm), lambda i: (0, i)),
                   pl.BlockSpec((N_EXPERTS, LANES), lambda i: (0, 0)), pl.BlockSpec(memory_space=pl.ANY)],
        out_shape=[jax.ShapeDtypeStruct((N, D), F32),
                   jax.ShapeDtypeStruct((2 * TOP_K, N), I32),
                   jax.ShapeDtypeStruct((2 * TOP_K, N), F32),
                   jax.ShapeDtypeStruct((N_EXPERTS, LANES), I32),
                   jax.ShapeDtypeStruct(((N_EXPERTS * N + TOP_K * sub) * ROW_SUB, LANES), F32)],
        scratch_shapes=[pltpu.VMEM((N_EXPERTS, 1), F32), pltpu.VMEM((2, tm * ROW_SUB, LANES), F32),
                        pltpu.VMEM((tm // sub, TOP_K, sub), I32), pltpu.SMEM((tm // sub, TOP_K, sub), I32),
                        pltpu.SemaphoreType.DMA((tm // sub,)), pltpu.SemaphoreType.DMA((2,))],
        compiler_params=pltpu.CompilerParams(dimension_semantics=("arbitrary",)),
        name="merge",
    )(x, ya, oc, os_, ow, wmg, wuc, wun, wo, g1, b1, wr, br, tri)


def _experts_kernel(be_ref, nu_ref, rb_ref, vr_ref, xs_ref, wgu_ref, perm_ref, bg_ref, bl_ref, wd_ref, bd_ref, ys_ref, wg_s, wl_s, wd_s):
    i = pl.program_id(0)

    @pl.when((i == 0) | (be_ref[i] != be_ref[jnp.maximum(i, 1) - 1]))
    def _():
        w = perm_ref.shape[0]
        for c in range(2 * D_FF // w):
            t = _dot(wgu_ref[0, :, c * w:(c + 1) * w].astype(BF16), perm_ref[...])
            wg_s[:, c * (w // 2):(c + 1) * (w // 2)] = t[:, :w // 2].astype(BF16)
            wl_s[:, c * (w // 2):(c + 1) * (w // 2)] = t[:, w // 2:].astype(BF16)
        wd_s[...] = wd_ref[0].astype(BF16)

    @pl.when(i < nu_ref[0])
    def _():
        row = lax.broadcasted_iota(I32, (MOE_BLK, 1), 0)
        xb = jnp.where(row < vr_ref[i], _load_row_tiles(xs_ref), 0.0).astype(BF16)
        x_glu = jnp.minimum(_dot(xb, wg_s[...]) + bg_ref[0], SWIGLU_LIMIT)
        x_lin = jnp.clip(_dot(xb, wl_s[...]) + bl_ref[0], -SWIGLU_LIMIT, SWIGLU_LIMIT)
        act = x_glu * jax.nn.sigmoid(SWIGLU_ALPHA * x_glu) * (x_lin + 1.0)
        _store_row_tiles(ys_ref, _dot(act.astype(BF16), wd_s[...]) + bd_ref[0])


def _experts(block_expert, n_used, row_block, valid_rows, xs, wgu, perm, bg, bl, wd, bd):
    D = D_MODEL
    n_blocks = block_expert.shape[0]
    rows = lambda i, be, nu, rb, vr: (rb[i], 0)
    wsel = lambda i, be, nu, rb, vr: (be[i], 0, 0)
    return pl.pallas_call(
        _experts_kernel,
        grid_spec=pltpu.PrefetchScalarGridSpec(
            num_scalar_prefetch=4,
            grid=(n_blocks,),
            in_specs=[pl.BlockSpec((MOE_BLK * ROW_SUB, LANES), rows),
                      pl.BlockSpec((1, D, 2 * D_FF), wsel),
                      pl.BlockSpec(perm.shape, lambda i, be, nu, rb, vr: (0, 0)),
                      pl.BlockSpec((1, 1, D_FF), wsel), pl.BlockSpec((1, 1, D_FF), wsel),
                      pl.BlockSpec((1, D_FF, D), wsel), pl.BlockSpec((1, 1, D), wsel)],
            out_specs=pl.BlockSpec((MOE_BLK * ROW_SUB, LANES), rows),
            scratch_shapes=[pltpu.VMEM((D, D_FF), BF16), pltpu.VMEM((D, D_FF), BF16), pltpu.VMEM((D_FF, D), BF16)]),
        out_shape=jax.ShapeDtypeStruct(xs.shape, F32),
        compiler_params=pltpu.CompilerParams(dimension_semantics=("arbitrary",)),
        name="experts",
    )(block_expert, n_used, row_block, valid_rows, xs, wgu, perm, bg, bl, wd, bd)


def _combine_kernel(dest_ref, dest_next_ref, ys_hbm, x1_ref, rg_ref, g2_ref, b2_ref, o_ref, buf, sem, *, alpha):
    i = pl.program_id(0)
    last = pl.num_programs(0) - 1
    tm = o_ref.shape[0]
    slot = i % 2
    other = 1 - slot

    def gather(ids_ref, t, to_slot):
        for k in range(TOP_K):
            _tile_copy(ys_hbm, ids_ref[t * TOP_K + k], buf.at[to_slot, k], t, sem.at[to_slot]).start(priority=k % 2)

    def wait_slot(s):
        for k in range(TOP_K):
            pltpu.make_async_copy(ys_hbm.at[pl.ds(0, tm * ROW_SUB), :], buf.at[s, k], sem.at[s]).wait()

    @pl.when(i == 0)
    def _():
        def first(t, c):
            gather(dest_ref, t, slot)
            return c
        lax.fori_loop(0, tm, first, 0)

    wait_slot(slot)

    gs = min(COMB_GROUP, tm)

    def group(g, c):
        r0 = pl.multiple_of(g * gs, gs)
        for j in range(gs):
            gather(dest_next_ref, r0 + j, other)
        rows = pl.ds(r0, gs)
        gate = rg_ref[rows, :]
        y = alpha * x1_ref[rows, :]
        for k in range(TOP_K):
            tiles = buf.at[slot, k, pl.ds(pl.multiple_of(r0 * ROW_SUB, gs * ROW_SUB), gs * ROW_SUB), :]
            y = y + gate[:, k:k + 1] * _load_row_tiles(tiles)
        o_ref[rows, :] = _layer_norm(y, g2_ref[...], b2_ref[...])
        return c

    lax.fori_loop(0, tm // gs, group, 0)

    @pl.when(i == last)
    def _():
        wait_slot(other)


def _combine(dest, ys, x1, rg, g2, b2, alpha):
    N, D = x1.shape
    tm = min(COMB_TM, N)
    n = N // tm
    tok = lambda w: pl.BlockSpec((tm, w), lambda i: (i, 0))
    full = lambda a: pl.BlockSpec(a.shape, lambda i: (0,) * a.ndim)
    ids = lambda f: pl.BlockSpec((tm * TOP_K,), f, memory_space=pltpu.SMEM)
    return pl.pallas_call(
        functools.partial(_combine_kernel, alpha=alpha),
        grid=(n,),
        in_specs=[ids(lambda i: (i,)), ids(lambda i: (jnp.minimum(i + 1, n - 1),)),
                  pl.BlockSpec(memory_space=pl.ANY), tok(D), tok(rg.shape[1]), full(g2), full(b2)],
        out_specs=tok(D),
        out_shape=jax.ShapeDtypeStruct((N, D), F32),
        scratch_shapes=[pltpu.VMEM((2, TOP_K, tm * ROW_SUB, LANES), F32), pltpu.SemaphoreType.DMA((2,))],
        compiler_params=pltpu.CompilerParams(dimension_semantics=("arbitrary",)),
        name="combine",
    )(dest, dest, ys, x1, rg, g2, b2)


def _rope_freq():
    half = ROPE_DIM // 2
    inv = (np.float32(ROPE_THETA) ** (-np.arange(half, dtype=np.float32) * np.float32(2.0 / ROPE_DIM))).astype(np.float32)
    return inv[:, None]


def _overlap_t(T):
    nc = T // CMP_STRIDE
    c0 = np.arange(nc) * CMP_STRIDE
    j0 = np.arange(SEL_LANES) * SEL_BLOCK
    ov = (c0[None, :] < j0[:, None] + SEL_BLOCK) & (c0[None, :] + CMP_BLOCK > j0[:, None])
    ov &= (np.arange(nc) < nc - 1)[None, :] & (np.arange(SEL_LANES) < T // SEL_BLOCK)[:, None]
    return ov.astype(np.float32)


def _deinterleave_perm():
    w = 2 * LANES
    p = np.zeros((w, w), np.float32)
    p[np.arange(0, w, 2), np.arange(w // 2)] = 1.0
    p[np.arange(1, w, 2), w // 2 + np.arange(w // 2)] = 1.0
    return p


def _layer(x, positions, w_in, conv_w, cmp_pos_k, cmp_w1_k, cmp_w2_k, cmp_pos_v, cmp_w1_v, cmp_w2_v,
           w_up_conv, w_up_nsa, w_o, ln1_g, ln1_b, w_router, b_router, w_gate_up, b_gate_up,
           w_down, b_down, ln2_g, ln2_b, alpha):
    B, T, D = x.shape
    G, R = N_KV_GROUPS, HEADS_PER_GROUP
    N = B * T
    assert D == D_MODEL and T % SEL_BLOCK == 0 and T // SEL_BLOCK <= SEL_LANES

    c0 = 3 * CONV_CH
    c1 = c0 + NSA_WIDTH
    c2 = c1 + 6 * KV_WIDTH
    c3 = c2 + 3 * N_HEADS
    wc = w_in[:, :c0].astype(BF16)
    wqt = w_in[:, c0:c1].T.astype(BF16)
    wkvt = w_in[:, c1:c2].T.astype(BF16)
    gcols = np.zeros((G * GATE_ROWS,), np.int64)
    gmask = np.zeros((G * GATE_ROWS,), np.float32)
    for g in range(G):
        for br in range(3):
            for r in range(R):
                gcols[g * GATE_ROWS + br * R + r] = br * N_HEADS + g * R + r
                gmask[g * GATE_ROWS + br * R + r] = 1.0
    wgt = (w_in[:, c2:c3][:, gcols] * gmask).T.astype(BF16)
    wmg = w_in[:, c3:].astype(BF16)
    cw = conv_w.reshape(CONV_K, CONV_CH)
    pos_row = positions.astype(F32)[:, None, :]

    ya_pre, qt, kvt, gates, ctok = _proj(x, pos_row, wc, wqt, wkvt, wgt, cw, jnp.asarray(_rope_freq()))

    nc = T // CMP_STRIDE
    w1 = jnp.stack([cmp_w1_k, cmp_w1_v])
    w2 = jnp.stack([cmp_w2_k, cmp_w2_v])
    pos_flat = jnp.stack([cmp_pos_k.reshape(1, -1), cmp_pos_v.reshape(1, -1)])
    pos_flat = jnp.pad(pos_flat, ((0, 0), (0, 7), (0, 0)))
    kcmp, kcmp_t = _compress(ctok, w1, w2, w2.transpose(0, 2, 1), pos_flat)
    kcmp = kcmp.reshape(2, B * G, nc, HEAD_DIM)
    kcmp_t = kcmp_t.reshape(2, B * G, HEAD_DIM, nc)

    o_c, sel = _cmp_attn(qt, kcmp, kcmp_t, gates, jnp.asarray(_overlap_t(T)))

    hot_t = (np.arange(SEL_LANES)[:, None] == np.arange(T)[None, :] // SEL_BLOCK).astype(np.float32)
    o_s = _sel_attn(qt, sel, kvt, 2 * G, 3 * G, jnp.asarray(hot_t, BF16), gates)
    o_w = _win_attn(qt, kvt, 4 * G, 5 * G, gates)

    sub = min(MERGE_SUB, T)
    tri = jnp.asarray(np.triu(np.ones((sub, sub), np.float32), 1), BF16)
    assert N % MOE_BLK == 0
    x1, ri, rg, cnt, xs = _merge(x.reshape(N, D), ya_pre.reshape(N, CONV_CH), o_c, o_s, o_w, wmg,
                             w_up_conv.astype(BF16), w_up_nsa.astype(BF16), w_o.astype(BF16),
                             ln1_g[None, :], ln1_b[None, :], w_router.T, b_router[:, None], tri, alpha)

    counts = cnt[:, 0]
    blocks = (counts + MOE_BLK - 1) // MOE_BLK
    bend = jnp.cumsum(blocks).astype(I32)
    bstart = bend - blocks
    n_blocks = (N * TOP_K) // MOE_BLK + N_EXPERTS
    n_used = bend[-1:]
    b = jnp.minimum(jnp.arange(n_blocks, dtype=I32), n_used - 1)
    block_expert = jnp.minimum(jnp.sum((b[:, None] >= bend[None, :]).astype(I32), axis=1), N_EXPERTS - 1)
    first = jnp.sum(jnp.where(block_expert[:, None] == jnp.arange(N_EXPERTS)[None, :], bstart[None, :], 0), axis=1)
    row_block = block_expert * (N // MOE_BLK) + (b - first)
    count = jnp.sum(jnp.where(block_expert[:, None] == jnp.arange(N_EXPERTS)[None, :], counts[None, :], 0), axis=1)
    valid_rows = jnp.clip(count - (b - first) * MOE_BLK, 0, MOE_BLK).astype(I32)
    dest = (ri[:TOP_K] * N + ri[TOP_K:]).T.reshape(N * TOP_K)

    ys = _experts(block_expert, n_used, row_block, valid_rows, xs, w_gate_up, jnp.asarray(_deinterleave_perm(), BF16),
                  b_gate_up[:, None, 0::2], b_gate_up[:, None, 1::2], w_down, b_down[:, None, :])
    out = _combine(dest, ys, x1, rg.T, ln2_g[None, :], ln2_b[None, :], alpha)
    return out.reshape(B, T, D)


def kernel(x, positions, w_in, conv_w, cmp_pos_k, cmp_w1_k, cmp_w2_k, cmp_pos_v, cmp_w1_v, cmp_w2_v, w_up_conv, w_up_nsa, w_o, ln1_g, ln1_b, w_router, b_router, w_gate_up, b_gate_up, w_down, b_down, ln2_g, ln2_b):
    depth = w_in.shape[0]
    alpha = float((2 * depth) ** 0.25)
    h = x
    for l in range(depth):
        h = _layer(h, positions, w_in[l], conv_w[l], cmp_pos_k[l], cmp_w1_k[l], cmp_w2_k[l],
                   cmp_pos_v[l], cmp_w1_v[l], cmp_w2_v[l], w_up_conv[l], w_up_nsa[l], w_o[l],
                   ln1_g[l], ln1_b[l], w_router[l], b_router[l], w_gate_up[l], b_gate_up[l],
                   w_down[l], b_down[l], ln2_g[l], ln2_b[l], alpha)
    return h
```

```python
import functools

import numpy as np
import jax
import jax.numpy as jnp
from jax import lax
from jax.experimental import pallas as pl
from jax.experimental.pallas import tpu as pltpu

F32 = jnp.float32
BF16 = jnp.bfloat16
I32 = jnp.int32

D_MODEL = 1024
CONV_CH = 512
CONV_K = 3
N_HEADS = 8
N_KV_GROUPS = 2
HEADS_PER_GROUP = N_HEADS // N_KV_GROUPS
HEAD_DIM = 64
NSA_WIDTH = N_HEADS * HEAD_DIM
KV_WIDTH = N_KV_GROUPS * HEAD_DIM
ROPE_DIM = HEAD_DIM // 4
ROPE_THETA = 500000.0
CMP_BLOCK = 32
CMP_STRIDE = 16
CMP_HIDDEN = 256
SEL_BLOCK = 64
N_SELECT = 16
WINDOW = 512
N_EXPERTS = 32
TOP_K = 4
D_FF = 1024
SWIGLU_LIMIT = 7.0
SWIGLU_ALPHA = 1.702
LN_EPS = 1e-5
NEG_INF = -1e30
FORCE_SCORE = 1e4
LOG2E = 1.4426950408889634

LANES = 128
SUBLANES = 8
ROW_SUB = D_MODEL // LANES
assert ROW_SUB == SUBLANES
TAKEN = float("-inf")
SEL_LANES = 64
SEL_MASK_BIAS = -float(2 ** 20)
GROUP_W = HEADS_PER_GROUP * HEAD_DIM
GATE_ROWS = 16

PROJ_TM = 1024
PROJ_SUB = 256
CMP_TQ = 512
SEL_TQ = 512
SEL_KC = 512
SEL_SPLIT = 2
WIN_TQ = 1024
WIN_SUB = 128
MERGE_TM = 512
MERGE_SUB = 256
MOE_BLK = 512
COMB_TM = 512
COMB_GROUP = 128


def _dot(a, b, precision=None):
    return jnp.dot(a, b, precision=precision, preferred_element_type=F32)


def _dot_tn(a, b):
    return lax.dot_general(a, b, (((0,), (0,)), ((), ())), preferred_element_type=F32)


def _dot_nt(a, b, precision=None):
    return lax.dot_general(a, b, (((1,), (1,)), ((), ())), precision=precision, preferred_element_type=F32)


def _proj_kernel(x_ref, pos_ref, wc_ref, wqt_ref, wkvt_ref, wgt_ref, cw_ref, freq_ref,
                 ya_ref, qt_ref, kvt_ref, gt_ref, ctok_ref, carry_ref):
    ti = pl.program_id(1)
    tm = x_ref.shape[1]
    sub = min(PROJ_SUB, tm)
    half = ROPE_DIM // 2

    @pl.when(ti == 0)
    def _():
        carry_ref[...] = jnp.zeros_like(carry_ref)

    prev = carry_ref[...]
    cw = cw_ref[...]
    for s0 in range(0, tm, sub):
        rs = slice(s0, s0 + sub)
        xb = x_ref[0, rs, :].astype(BF16)

        pc = _dot(xb, wc_ref[...])
        xv = pc[:, :CONV_CH]
        bg = pc[:, CONV_CH:2 * CONV_CH]
        cg = pc[:, 2 * CONV_CH:]
        u = cg * xv
        row = lax.broadcasted_iota(I32, u.shape, 0)
        u1 = jnp.where(row == 0, prev[7:8], pltpu.roll(u, 1, 0))
        u2 = jnp.where(row == 0, prev[6:7], jnp.where(row == 1, prev[7:8], pltpu.roll(u, 2, 0)))
        conv = cw[2:3] * u + cw[1:2] * u1 + cw[0:1] * u2
        prev = u[sub - 8:]
        ya_ref[0, rs, :] = (bg * conv).astype(BF16)

        ang = freq_ref[...] * pos_ref[0, :, rs]
        cos = jnp.cos(ang)
        sin = jnp.sin(ang)

        def rope_head(t):
            t1 = t[:half]
            t2 = t[half:ROPE_DIM]
            return [t1 * cos - t2 * sin, t2 * cos + t1 * sin, t[ROPE_DIM:]]

        def heads(t, rotate):
            out = []
            for h in range(t.shape[0] // HEAD_DIM):
                th = t[h * HEAD_DIM:(h + 1) * HEAD_DIM]
                out.extend(rope_head(th) if rotate(h) else [th])
            return jnp.concatenate(out, axis=0)

        qt = heads(_dot_nt(wqt_ref[...], xb), lambda h: True)
        qt_ref[0, :, rs] = (qt * (HEAD_DIM ** -0.5 * LOG2E)).astype(BF16)
        kvt = heads(_dot_nt(wkvt_ref[...], xb), lambda h: (h // N_KV_GROUPS) % 2 == 0)
        kvt_ref[0, :, rs] = kvt.astype(BF16)
        for s in range(2):
            ctok_ref[0, s, rs, :] = kvt[s * KV_WIDTH:(s + 1) * KV_WIDTH].T
        gt_ref[0, :, rs] = jax.nn.sigmoid(_dot_nt(wgt_ref[...], xb))
    carry_ref[...] = prev


def _proj(x, pos_row, wc, wqt, wkvt, wgt, cw, freq):
    B, T, D = x.shape
    tm = min(PROJ_TM, T)
    grid = (B, T // tm)
    full = lambda a: pl.BlockSpec(a.shape, lambda b, t: (0,) * a.ndim)
    tok = lambda w: pl.BlockSpec((1, tm, w), lambda b, t: (b, t, 0))
    feat = lambda r: pl.BlockSpec((1, r, tm), lambda b, t: (b, 0, t))
    n_gate = wgt.shape[0]
    return pl.pallas_call(
        _proj_kernel,
        grid=grid,
        in_specs=[tok(D), feat(1), full(wc), full(wqt), full(wkvt), full(wgt), full(cw), full(freq)],
        out_specs=[tok(CONV_CH), feat(NSA_WIDTH), feat(6 * KV_WIDTH), feat(n_gate),
                   pl.BlockSpec((1, 2, tm, KV_WIDTH), lambda b, t: (b, 0, t, 0))],
        out_shape=[jax.ShapeDtypeStruct((B, T, CONV_CH), BF16),
                   jax.ShapeDtypeStruct((B, NSA_WIDTH, T), BF16),
                   jax.ShapeDtypeStruct((B, 6 * KV_WIDTH, T), BF16),
                   jax.ShapeDtypeStruct((B, n_gate, T), F32),
                   jax.ShapeDtypeStruct((B, 2, T, KV_WIDTH), F32)],
        scratch_shapes=[pltpu.VMEM((8, CONV_CH), F32)],
        compiler_params=pltpu.CompilerParams(dimension_semantics=("arbitrary", "arbitrary")),
        name="proj",
    )(x, pos_row, wc, wqt, wkvt, wgt, cw, freq)


def _compress_kernel(x_ref, w1_ref, w2_ref, w2t_ref, pos_ref, o_ref, ot_ref):
    nc = x_ref.shape[2] // CMP_STRIDE
    half = CMP_STRIDE * HEAD_DIM
    for s in range(2):
        parts = [x_ref[0, s, pl.ds(l, nc, stride=CMP_STRIDE), :].astype(BF16) for l in range(CMP_STRIDE)]
        w1 = w1_ref[s]
        w1b = w1.astype(BF16)
        pb = _dot(pos_ref[s], w1, precision=lax.Precision.HIGHEST)[0:1]
        for g in range(N_KV_GROUPS):
            lo = g * HEAD_DIM
            xb = jnp.concatenate([p[:, lo:lo + HEAD_DIM] for p in parts], axis=1)
            a = _dot(xb, w1b[:half])
            b = _dot(xb, w1b[half:])
            b_next = pltpu.roll(b, nc - 1, 0)
            h = a + b_next + pb
            act = (0.5 * h * (1.0 + jnp.tanh(np.sqrt(2.0 / np.pi) * (h + 0.044715 * (h * h * h))))).astype(BF16)
            o_ref[s, 0, g] = _dot(act, w2_ref[s].astype(BF16)).astype(BF16)
            ot_ref[s, 0, g] = _dot_nt(w2t_ref[s].astype(BF16), act).astype(BF16)


def _compress(ctok, w1, w2, w2t, pos):
    B, _, T, W = ctok.shape
    nc = T // CMP_STRIDE
    G = N_KV_GROUPS
    full = lambda a: pl.BlockSpec(a.shape, lambda b: (0,) * a.ndim)
    return pl.pallas_call(
        _compress_kernel,
        grid=(B,),
        in_specs=[pl.BlockSpec((1, 2, T, W), lambda b: (b, 0, 0, 0)), full(w1), full(w2), full(w2t), full(pos)],
        out_specs=[pl.BlockSpec((2, 1, G, nc, HEAD_DIM), lambda b: (0, b, 0, 0, 0)),
                   pl.BlockSpec((2, 1, G, HEAD_DIM, nc), lambda b: (0, b, 0, 0, 0))],
        out_shape=[jax.ShapeDtypeStruct((2, B, G, nc, HEAD_DIM), BF16),
                   jax.ShapeDtypeStruct((2, B, G, HEAD_DIM, nc), BF16)],
        name="compress",
    )(ctok, w1, w2, w2t, pos)


def _head_lanes(qt):
    return jnp.concatenate([qt[r * HEAD_DIM:(r + 1) * HEAD_DIM] for r in range(HEADS_PER_GROUP)], axis=1)


def _gated_out(ot, scale, gates, branch, tq):
    rows = []
    for r in range(HEADS_PER_GROUP):
        c = branch * HEADS_PER_GROUP + r
        sl = slice(r * tq, (r + 1) * tq)
        rows.append(ot[:, sl] * (scale[:, sl] * gates[c:c + 1]))
    return jnp.concatenate(rows, axis=0).astype(BF16)


def _attn_specs(tq):
    G = N_KV_GROUPS
    qspec = pl.BlockSpec((1, GROUP_W, tq), lambda b, g, i: (b, g, i))
    gspec = pl.BlockSpec((1, GATE_ROWS, tq), lambda b, g, i: (b, g, i))
    return G, qspec, gspec


def _cmp_attn_kernel(q_ref, kc_ref, vct_ref, g_ref, ovt_ref, o_ref, sel_ref, score_ref, rank_ref):
    qi = pl.program_id(2)
    tq = q_ref.shape[2]
    q0 = qi * tq
    qt = _head_lanes(q_ref[0])
    s = _dot(kc_ref[0, 0], qt)
    c = lax.broadcasted_iota(I32, s.shape, 0)
    t = q0 + (lax.broadcasted_iota(I32, (1, s.shape[1]), 1) & (tq - 1))
    valid = c <= (t - (CMP_BLOCK - 1)) // CMP_STRIDE
    sm = jnp.where(valid, s, NEG_INF)
    m = jnp.max(sm, axis=0, keepdims=True)
    p = jnp.where(valid, jnp.exp2(sm - m), 0.0)
    l = jnp.sum(p, axis=0, keepdims=True)
    inv = 1.0 / jnp.where(l > 0.0, l, 1.0)
    ot = _dot(vct_ref[0, 0], p.astype(BF16))
    o_ref[0] = _gated_out(ot, inv, g_ref[0], 0, tq)

    pn = p * inv
    psum = pn[:, 0:tq]
    for r in range(1, HEADS_PER_GROUP):
        psum = psum + pn[:, r * tq:(r + 1) * tq]
    imp = _dot(ovt_ref[...], psum, precision=lax.Precision.HIGHEST)
    j = lax.broadcasted_iota(I32, imp.shape, 0)
    cur = (q0 + lax.broadcasted_iota(I32, imp.shape, 1)) // SEL_BLOCK
    valid_b = j <= cur
    forced = (j == 0) | (j == cur) | (j == cur - 1)
    score = jnp.where(valid_b, jnp.where(forced, FORCE_SCORE, imp), NEG_INF)
    sub = SUBLANES
    score_ref[...] = score
    rank_ref[...] = jnp.zeros(rank_ref.shape, I32)
    i_max = (q0 + tq - 1) // SEL_BLOCK
    for i0 in range(0, SEL_LANES, sub):
        @pl.when(i0 <= i_max)
        def _():
            sc = score_ref[...]
            groups = [sc[a:a + sub] for a in range(0, SEL_LANES, sub)]
            jrow = lax.broadcasted_iota(I32, groups[0].shape, 0)
            ranks = [jnp.zeros(g_.shape, I32) for g_ in groups]
            for i in range(i0, i0 + sub):
                si = sc[i:i + 1, :]
                for a, g_ in enumerate(groups):
                    if a > i // sub:
                        inc = jnp.where(si >= g_, 1, 0)
                    elif a < i // sub:
                        inc = jnp.where(si > g_, 1, 0)
                    else:
                        inc = jnp.where(jrow > i % sub, jnp.where(si >= g_, 1, 0), jnp.where(si > g_, 1, 0))
                    ranks[a] = ranks[a] + inc
            rank_ref[...] += jnp.concatenate(ranks, axis=0)
    selected = (rank_ref[...] < N_SELECT) & valid_b
    sel_ref[0, 0] = jnp.where(selected, 0.0, SEL_MASK_BIAS).astype(BF16)


def _cmp_attn(qt, kcmp, vcmp_t, gates, ovt):
    B, _, T = qt.shape
    tq = min(CMP_TQ, T)
    G, qspec, gspec = _attn_specs(tq)
    cspec = lambda a: pl.BlockSpec((1, 1) + a.shape[2:], lambda b, g, i: (0, b * G + g, 0, 0))
    vspec = lambda a: pl.BlockSpec((1, 1) + a.shape[2:], lambda b, g, i: (1, b * G + g, 0, 0))
    return pl.pallas_call(
        _cmp_attn_kernel,
        grid=(B, G, T // tq),
        in_specs=[qspec, cspec(kcmp), vspec(vcmp_t), gspec, pl.BlockSpec(ovt.shape, lambda b, g, i: (0, 0))],
        out_specs=[qspec, pl.BlockSpec((1, 1, SEL_LANES, tq), lambda b, g, i: (b, g, 0, i))],
        out_shape=[jax.ShapeDtypeStruct((B, NSA_WIDTH, T), BF16),
                   jax.ShapeDtypeStruct((B, G, SEL_LANES, T), BF16)],
        scratch_shapes=[pltpu.VMEM((SEL_LANES, tq), F32), pltpu.VMEM((SEL_LANES, tq), I32)],
        name="cmp_attn",
    )(qt, kcmp, vcmp_t, gates, ovt)


def _sel_attn_kernel(q_ref, sel_ref, kt_ref, hot_ref, vt_ref, g_ref, o_ref, *, kc):
    qi = pl.program_id(2)
    tq = q_ref.shape[2]
    q0 = qi * tq
    qt = q_ref[0]
    sb = sel_ref[0, 0]
    qa = jnp.concatenate(
        [jnp.concatenate([qt[r * HEAD_DIM:(r + 1) * HEAD_DIM], sb], axis=0) for r in range(HEADS_PER_GROUP)],
        axis=1)
    gw = qa.shape[1] // SEL_SPLIT
    qas = [qa[:, i * gw:(i + 1) * gw] for i in range(SEL_SPLIT)]

    def step(kstart, kn, carry, diagonal):
        keys = pl.ds(kstart, kn)
        kblk_t = jnp.concatenate([kt_ref[0, :, keys], hot_ref[:, keys]], axis=0)
        vblk = vt_ref[0, :, keys]
        ss = [_dot_tn(kblk_t, qg) for qg in qas]
        out = []
        for gi, (s, (m, l, acc)) in enumerate(zip(ss, carry)):
            if diagonal:
                row = lax.broadcasted_iota(I32, s.shape, 0)
                off = (gi * gw + lax.broadcasted_iota(I32, s.shape, 1)) & (tq - 1)
                s = jnp.where(row <= off, s, NEG_INF)
            mn = jnp.maximum(m, jnp.max(s, axis=0, keepdims=True))
            alpha = jnp.exp2(m - mn)
            p = jnp.exp2(s - mn)
            l = alpha * l + jnp.sum(p, axis=0, keepdims=True)
            acc = alpha * acc + _dot(vblk, p.astype(BF16))
            out.append((mn, l, acc))
        return tuple(out)

    init = tuple((jnp.full((1, gw), NEG_INF, F32), jnp.zeros((1, gw), F32), jnp.zeros((HEAD_DIM, gw), F32))
                 for _ in range(SEL_SPLIT))
    n_full = q0 // kc
    carry = lax.fori_loop(0, n_full, lambda i, cr: step(pl.multiple_of(i * kc, kc), kc, cr, False), init)
    carry = lax.fori_loop(n_full * (kc // tq), qi, lambda i, cr: step(pl.multiple_of(i * tq, tq), tq, cr, False), carry)
    carry = step(pl.multiple_of(q0, tq), tq, carry, True)
    l = jnp.concatenate([c[1] for c in carry], axis=1)
    acc = jnp.concatenate([c[2] for c in carry], axis=1)
    o_ref[0] = _gated_out(acc, 1.0 / l, g_ref[0], 1, tq)


def _sel_attn(qt, sel, kvt, k_row_block, v_row_block, hot_t, gates):
    B, _, T = qt.shape
    tq = min(SEL_TQ, T)
    kc = min(SEL_KC, T)
    G, qspec, gspec = _attn_specs(tq)
    return pl.pallas_call(
        functools.partial(_sel_attn_kernel, kc=kc),
        grid=(B, G, T // tq),
        in_specs=[qspec, pl.BlockSpec((1, 1, SEL_LANES, tq), lambda b, g, i: (b, g, 0, i)),
                  pl.BlockSpec((1, HEAD_DIM, T), lambda b, g, i: (b, k_row_block + g, 0)),
                  pl.BlockSpec(hot_t.shape, lambda b, g, i: (0, 0)),
                  pl.BlockSpec((1, HEAD_DIM, T), lambda b, g, i: (b, v_row_block + g, 0)), gspec],
        out_specs=qspec,
        out_shape=jax.ShapeDtypeStruct((B, NSA_WIDTH, T), BF16),
        name="sel_attn",
    )(qt, sel, kvt, hot_t, kvt, gates)


def _win_attn_kernel(q_ref, kt_ref, vt_ref, g_ref, o_ref, *, span, sub):
    qi = pl.program_id(2)
    tq = q_ref.shape[2]
    T = kt_ref.shape[2]

    def tile(s0, interior):
        cs = slice(s0, s0 + sub)
        q0 = qi * tq + s0
        start = pl.multiple_of(jnp.clip(q0 + sub - span, 0, T - span), sub)
        qt = _head_lanes(q_ref[0, :, cs])
        s = _dot_tn(kt_ref[0, :, pl.ds(start, span)], qt)
        if interior:
            row = lax.broadcasted_iota(I32, (sub, s.shape[1]), 0)
            off = lax.broadcasted_iota(I32, (sub, s.shape[1]), 1) & (sub - 1)
            s = jnp.concatenate([jnp.where(row > off, s[:sub], NEG_INF), s[sub:span - sub],
                                 jnp.where(row <= off, s[span - sub:], NEG_INF)], axis=0)
        else:
            kp = start + lax.broadcasted_iota(I32, s.shape, 0)
            t = q0 + (lax.broadcasted_iota(I32, s.shape, 1) & (sub - 1))
            diff = t - kp
            s = jnp.where((diff >= 0) & (diff < WINDOW), s, NEG_INF)
        m = jnp.max(s, axis=0, keepdims=True)
        p = jnp.exp2(s - m)
        l = jnp.sum(p, axis=0, keepdims=True)
        ot = _dot(vt_ref[0, :, pl.ds(start, span)], p.astype(BF16))
        o_ref[0, :, cs] = _gated_out(ot, 1.0 / l, g_ref[0, :, cs], 2, sub)

    first_interior = -(-WINDOW // tq)
    if span == WINDOW + sub:
        @pl.when(qi >= first_interior)
        def _():
            for s0 in range(0, tq, sub):
                tile(s0, True)

    @pl.when((qi < first_interior) | (span != WINDOW + sub))
    def _():
        for s0 in range(0, tq, sub):
            tile(s0, first_interior == 1 and s0 >= WINDOW and span == WINDOW + sub)


def _win_attn(qt, kvt, k_row_block, v_row_block, gates):
    B, _, T = qt.shape
    tq = min(WIN_TQ, T)
    sub = min(WIN_SUB, T)
    span = min(WINDOW + sub, T)
    G, qspec, gspec = _attn_specs(tq)
    return pl.pallas_call(
        functools.partial(_win_attn_kernel, span=span, sub=sub),
        grid=(B, G, T // tq),
        in_specs=[qspec, pl.BlockSpec((1, HEAD_DIM, T), lambda b, g, i: (b, k_row_block + g, 0)),
                  pl.BlockSpec((1, HEAD_DIM, T), lambda b, g, i: (b, v_row_block + g, 0)), gspec],
        out_specs=qspec,
        out_shape=jax.ShapeDtypeStruct((B, NSA_WIDTH, T), BF16),
        name="win_attn",
    )(qt, kvt, kvt, gates)


def _store_row_tiles(ref, x, row0=0):
    rows = x.shape[0]
    for s in range(ROW_SUB):
        ref[pl.ds(row0 * ROW_SUB + s, rows, stride=ROW_SUB), :] = x[:, s * LANES:(s + 1) * LANES]


def _load_row_tiles(ref):
    rows = ref.shape[0] // ROW_SUB
    return jnp.concatenate([ref[pl.ds(s, rows, stride=ROW_SUB), :] for s in range(ROW_SUB)], axis=1)


def _tile_copy(src, si, dst, di, sem):
    return pltpu.make_async_copy(src.at[pl.ds(pl.multiple_of(si * ROW_SUB, ROW_SUB), ROW_SUB), :],
                                 dst.at[pl.ds(pl.multiple_of(di * ROW_SUB, ROW_SUB), ROW_SUB), :], sem)


def _layer_norm(h, g, b):
    mu = jnp.mean(h, axis=-1, keepdims=True)
    c = h - mu
    var = jnp.mean(c * c, axis=-1, keepdims=True)
    return c * lax.rsqrt(var + LN_EPS) * g + b


def _merge_kernel(x_ref, ya_ref, oc_ref, os_ref, ow_ref, wmg_ref, wuc_ref, wun_ref, wo_ref, g1_ref, b1_ref,
                  wr_ref, br_ref, tri_ref, x1_ref, ri_ref, rg_ref, cnt_ref, xs_hbm,
                  carry_ref, stage, dest_v, dest_s, copy_sem, row_sem, *, alpha, cap):
    i = pl.program_id(0)
    last = pl.num_programs(0) - 1
    sub = tri_ref.shape[0]
    tm = x_ref.shape[0]
    slot = i % 2
    nsub = tm // sub

    def id_copy(si):
        return pltpu.make_async_copy(dest_v.at[si], dest_s.at[si], copy_sem.at[si])

    def issue_rows(si, s):
        id_copy(si).wait()
        for t in range(sub):
            for k in range(TOP_K):
                _tile_copy(stage.at[s], si * sub + t, xs_hbm, dest_s[si, k, t], row_sem.at[s]).start(priority=k % 2)

    def wait_rows(s, rows):
        for k in range(TOP_K):
            n = rows * ROW_SUB
            pltpu.make_async_copy(stage.at[s, pl.ds(0, n), :], xs_hbm.at[pl.ds(0, n), :], row_sem.at[s]).wait()

    @pl.when(i == 0)
    def _():
        carry_ref[...] = jnp.zeros_like(carry_ref)
        stage[1, pl.ds((nsub - 1) * sub * ROW_SUB, sub * ROW_SUB), :] = jnp.zeros((sub * ROW_SUB, LANES), F32)
        spare = (N_EXPERTS * cap + lax.broadcasted_iota(I32, (TOP_K, sub), 0) * sub
                 + lax.broadcasted_iota(I32, (TOP_K, sub), 1))
        dest_v[nsub - 1] = spare
        id_copy(nsub - 1).start()

    wr = wr_ref[...]
    wr_hi = wr.astype(BF16)
    wr_hl = jnp.concatenate([wr_hi, (wr - wr_hi.astype(F32)).astype(BF16)], axis=0)
    total = carry_ref[...]
    for si in range(nsub):
        s0 = si * sub
        rs = slice(s0, s0 + sub)
        x = x_ref[rs, :]
        mg = _dot(x.astype(BF16), wmg_ref[...])
        if si == 0:
            issue_rows(nsub - 1, 1 - slot)
        else:
            issue_rows(si - 1, slot)
        y_a = _dot(ya_ref[rs, :], wuc_ref[...])
        o_nsa_t = (oc_ref[0, :, rs].astype(F32) + os_ref[0, :, rs].astype(F32)
                   + ow_ref[0, :, rs].astype(F32))
        y_b = _dot(o_nsa_t.T.astype(BF16), wun_ref[...])
        merged = jax.nn.sigmoid(mg[:, :D_MODEL]) * y_a + jax.nn.sigmoid(mg[:, D_MODEL:]) * y_b
        h = alpha * x + _dot(merged.astype(BF16), wo_ref[...])
        x1 = _layer_norm(h, g1_ref[...], b1_ref[...])
        x1_ref[rs, :] = x1
        _store_row_tiles(stage.at[slot], x1, s0)

        x_hi = x1.astype(BF16)
        x_lo = (x1 - x_hi.astype(F32)).astype(BF16)
        both = _dot_nt(wr_hl, x_hi)
        logits = both[:N_EXPERTS] + both[N_EXPERTS:] + _dot_nt(wr_hi, x_lo) + br_ref[...]
        expert = lax.broadcasted_iota(I32, logits.shape, 0).astype(F32)
        rem = logits
        vals, idxs, hots = [], [], []
        for _ in range(TOP_K):
            m = jnp.max(rem, axis=0, keepdims=True)
            idx = jnp.min(jnp.where(rem == m, expert, float(N_EXPERTS)), axis=0, keepdims=True)
            hot = expert == idx
            vals.append(m)
            idxs.append(idx)
            hots.append(hot)
            rem = jnp.where(hot, TAKEN, rem)
        es = [jnp.exp(v - vals[0]) for v in vals]
        den = es[0]
        for e in es[1:]:
            den = den + e
        chosen = hots[0]
        for hot in hots[1:]:
            chosen = chosen | hot
        chosen_f = jnp.where(chosen, 1.0, 0.0)

        before = _dot(chosen_f.astype(BF16), tri_ref[...]) + total
        total = total + jnp.sum(chosen_f, axis=1, keepdims=True)
        ranks = [jnp.sum(jnp.where(hot, before, 0.0), axis=0, keepdims=True) for hot in hots]
        ri_ref[:, rs] = jnp.concatenate(idxs + ranks, axis=0).astype(I32)
        rg_ref[:, rs] = jnp.concatenate([e / den for e in es] + [jnp.zeros_like(den)] * TOP_K, axis=0)

        dest_v[si] = jnp.concatenate(idxs, axis=0).astype(I32) * cap + jnp.concatenate(ranks, axis=0).astype(I32)
        id_copy(si).start()

    carry_ref[...] = total
    cnt_ref[...] = jnp.broadcast_to(total, cnt_ref.shape).astype(I32)

    @pl.when(i == 0)
    def _():
        wait_rows(1, sub)

    @pl.when(i > 0)
    def _():
        wait_rows(1 - slot, tm)

    @pl.when(i == last)
    def _():
        issue_rows(nsub - 1, slot)
        wait_rows(slot, tm)


def _merge(x, ya, oc, os_, ow, wmg, wuc, wun, wo, g1, b1, wr, br, tri, alpha):
    N, D = x.shape
    B, _, T = oc.shape
    tm = min(MERGE_TM, T)
    nt = T // tm
    tok = lambda w: pl.BlockSpec((tm, w), lambda i: (i, 0))
    feat = pl.BlockSpec((1, NSA_WIDTH, tm), lambda i: (i // nt, 0, i % nt))
    full = lambda a: pl.BlockSpec(a.shape, lambda i: (0,) * a.ndim)
    sub = tri.shape[0]
    return pl.pallas_call(
        functools.partial(_merge_kernel, alpha=alpha, cap=N),
        grid=(N // tm,),
        in_specs=[tok(D), tok(CONV_CH), feat, feat, feat,
                  full(wmg), full(wuc), full(wun), full(wo), full(g1), full(b1), full(wr), full(br), full(tri)],
        out_specs=[tok(D),
                   pl.BlockSpec((2 * TOP_K, tm), lambda i: (0, i)), pl.BlockSpec((2 * TOP_K, tm), lambda i: (0, i)),
                   pl.BlockSpec((N_EXPERTS, LANES), lambda i: (0, 0)), pl.BlockSpec(memory_space=pl.ANY)],
        out_shape=[jax.ShapeDtypeStruct((N, D), F32),
                   jax.ShapeDtypeStruct((2 * TOP_K, N), I32),
                   jax.ShapeDtypeStruct((2 * TOP_K, N), F32),
                   jax.ShapeDtypeStruct((N_EXPERTS, LANES), I32),
                   jax.ShapeDtypeStruct(((N_EXPERTS * N + TOP_K * sub) * ROW_SUB, LANES), F32)],
        scratch_shapes=[pltpu.VMEM((N_EXPERTS, 1), F32), pltpu.VMEM((2, tm * ROW_SUB, LANES), F32),
                        pltpu.VMEM((tm // sub, TOP_K, sub), I32), pltpu.SMEM((tm // sub, TOP_K, sub), I32),
                        pltpu.SemaphoreType.DMA((tm // sub,)), pltpu.SemaphoreType.DMA((2,))],
        compiler_params=pltpu.CompilerParams(dimension_semantics=("arbitrary",)),
        name="merge",
    )(x, ya, oc, os_, ow, wmg, wuc, wun, wo, g1, b1, wr, br, tri)


def _experts_kernel(be_ref, nu_ref, rb_ref, vr_ref, nxt_ref, slot_ref, xs_ref, wgu_hbm, perm_ref, bg_ref, bl_ref,
                    wd_hbm, bd_ref, ys_ref, wgu_buf, wd_buf, wsem, wg_s, wl_s, wd_s):
    i = pl.program_id(0)
    slot = slot_ref[i]

    def fetch(e, s):
        return (pltpu.make_async_copy(wgu_hbm.at[e], wgu_buf.at[s], wsem.at[0, s]),
                pltpu.make_async_copy(wd_hbm.at[e], wd_buf.at[s], wsem.at[1, s]))

    @pl.when(i == 0)
    def _():
        for c in fetch(be_ref[0], slot):
            c.start()

    @pl.when((i == 0) | (be_ref[i] != be_ref[jnp.maximum(i, 1) - 1]))
    def _():
        for c in fetch(be_ref[i], slot):
            c.wait()

        @pl.when(nxt_ref[i] >= 0)
        def _():
            for c in fetch(nxt_ref[i], 1 - slot):
                c.start()

        w = perm_ref.shape[0]
        for c in range(2 * D_FF // w):
            t = _dot(wgu_buf[slot, :, c * w:(c + 1) * w].astype(BF16), perm_ref[...])
            wg_s[:, c * (w // 2):(c + 1) * (w // 2)] = t[:, :w // 2].astype(BF16)
            wl_s[:, c * (w // 2):(c + 1) * (w // 2)] = t[:, w // 2:].astype(BF16)
        wd_s[...] = wd_buf[slot].astype(BF16)

    @pl.when(i < nu_ref[0])
    def _():
        row = lax.broadcasted_iota(I32, (MOE_BLK, 1), 0)
        xb = jnp.where(row < vr_ref[i], _load_row_tiles(xs_ref), 0.0).astype(BF16)
        x_glu = jnp.minimum(_dot(xb, wg_s[...]) + bg_ref[0], SWIGLU_LIMIT)
        x_lin = jnp.clip(_dot(xb, wl_s[...]) + bl_ref[0], -SWIGLU_LIMIT, SWIGLU_LIMIT)
        act = x_glu * jax.nn.sigmoid(SWIGLU_ALPHA * x_glu) * (x_lin + 1.0)
        _store_row_tiles(ys_ref, _dot(act.astype(BF16), wd_s[...]) + bd_ref[0])


def _experts(block_expert, n_used, row_block, valid_rows, next_expert, slot, xs, wgu, perm, bg, bl, wd, bd):
    D = D_MODEL
    n_blocks = block_expert.shape[0]
    rows = lambda i, be, nu, rb, vr, nx, sl: (rb[i], 0)
    wsel = lambda i, be, nu, rb, vr, nx, sl: (be[i], 0, 0)
    hbm = pl.BlockSpec(memory_space=pl.ANY)
    return pl.pallas_call(
        _experts_kernel,
        grid_spec=pltpu.PrefetchScalarGridSpec(
            num_scalar_prefetch=6,
            grid=(n_blocks,),
            in_specs=[pl.BlockSpec((MOE_BLK * ROW_SUB, LANES), rows),
                      hbm,
                      pl.BlockSpec(perm.shape, lambda i, be, nu, rb, vr, nx, sl: (0, 0)),
                      pl.BlockSpec((1, 1, D_FF), wsel), pl.BlockSpec((1, 1, D_FF), wsel),
                      hbm, pl.BlockSpec((1, 1, D), wsel)],
            out_specs=pl.BlockSpec((MOE_BLK * ROW_SUB, LANES), rows),
            scratch_shapes=[pltpu.VMEM((2, D, 2 * D_FF), F32), pltpu.VMEM((2, D_FF, D), F32), pltpu.SemaphoreType.DMA((2, 2)),
                            pltpu.VMEM((D, D_FF), BF16), pltpu.VMEM((D, D_FF), BF16), pltpu.VMEM((D_FF, D), BF16)]),
        out_shape=jax.ShapeDtypeStruct(xs.shape, F32),
        compiler_params=pltpu.CompilerParams(dimension_semantics=("arbitrary",)),
        name="experts",
    )(block_expert, n_used, row_block, valid_rows, next_expert, slot, xs, wgu, perm, bg, bl, wd, bd)


def _combine_kernel(dest_ref, dest_next_ref, ys_hbm, x1_ref, rg_ref, g2_ref, b2_ref, o_ref, buf, sem, *, alpha):
    i = pl.program_id(0)
    last = pl.num_programs(0) - 1
    tm = o_ref.shape[0]
    slot = i % 2
    other = 1 - slot

    def gather(ids_ref, t, to_slot):
        for k in range(TOP_K):
            _tile_copy(ys_hbm, ids_ref[t * TOP_K + k], buf.at[to_slot, k], t, sem.at[to_slot]).start(priority=k % 2)

    def wait_slot(s):
        for k in range(TOP_K):
            pltpu.make_async_copy(ys_hbm.at[pl.ds(0, tm * ROW_SUB), :], buf.at[s, k], sem.at[s]).wait()

    @pl.when(i == 0)
    def _():
        def first(t, c):
            gather(dest_ref, t, slot)
            return c
        lax.fori_loop(0, tm, first, 0)

    wait_slot(slot)

    gs = min(COMB_GROUP, tm)

    def group(g, c):
        r0 = pl.multiple_of(g * gs, gs)
        for j in range(gs):
            gather(dest_next_ref, r0 + j, other)
        rows = pl.ds(r0, gs)
        gate = rg_ref[rows, :]
        y = alpha * x1_ref[rows, :]
        for k in range(TOP_K):
            tiles = buf.at[slot, k, pl.ds(pl.multiple_of(r0 * ROW_SUB, gs * ROW_SUB), gs * ROW_SUB), :]
            y = y + gate[:, k:k + 1] * _load_row_tiles(tiles)
        o_ref[rows, :] = _layer_norm(y, g2_ref[...], b2_ref[...])
        return c

    lax.fori_loop(0, tm // gs, group, 0)

    @pl.when(i == last)
    def _():
        wait_slot(other)


def _combine(dest, ys, x1, rg, g2, b2, alpha):
    N, D = x1.shape
    tm = min(COMB_TM, N)
    n = N // tm
    tok = lambda w: pl.BlockSpec((tm, w), lambda i: (i, 0))
    full = lambda a: pl.BlockSpec(a.shape, lambda i: (0,) * a.ndim)
    ids = lambda f: pl.BlockSpec((tm * TOP_K,), f, memory_space=pltpu.SMEM)
    return pl.pallas_call(
        functools.partial(_combine_kernel, alpha=alpha),
        grid=(n,),
        in_specs=[ids(lambda i: (i,)), ids(lambda i: (jnp.minimum(i + 1, n - 1),)),
                  pl.BlockSpec(memory_space=pl.ANY), tok(D), tok(rg.shape[1]), full(g2), full(b2)],
        out_specs=tok(D),
        out_shape=jax.ShapeDtypeStruct((N, D), F32),
        scratch_shapes=[pltpu.VMEM((2, TOP_K, tm * ROW_SUB, LANES), F32), pltpu.SemaphoreType.DMA((2,))],
        compiler_params=pltpu.CompilerParams(dimension_semantics=("arbitrary",)),
        name="combine",
    )(dest, dest, ys, x1, rg, g2, b2)


def _rope_freq():
    half = ROPE_DIM // 2
    inv = (np.float32(ROPE_THETA) ** (-np.arange(half, dtype=np.float32) * np.float32(2.0 / ROPE_DIM))).astype(np.float32)
    return inv[:, None]


def _overlap_t(T):
    nc = T // CMP_STRIDE
    c0 = np.arange(nc) * CMP_STRIDE
    j0 = np.arange(SEL_LANES) * SEL_BLOCK
    ov = (c0[None, :] < j0[:, None] + SEL_BLOCK) & (c0[None, :] + CMP_BLOCK > j0[:, None])
    ov &= (np.arange(nc) < nc - 1)[None, :] & (np.arange(SEL_LANES) < T // SEL_BLOCK)[:, None]
    return ov.astype(np.float32)


def _deinterleave_perm():
    w = 2 * LANES
    p = np.zeros((w, w), np.float32)
    p[np.arange(0, w, 2), np.arange(w // 2)] = 1.0
    p[np.arange(1, w, 2), w // 2 + np.arange(w // 2)] = 1.0
    return p


def _layer(x, positions, w_in, conv_w, cmp_pos_k, cmp_w1_k, cmp_w2_k, cmp_pos_v, cmp_w1_v, cmp_w2_v,
           w_up_conv, w_up_nsa, w_o, ln1_g, ln1_b, w_router, b_router, w_gate_up, b_gate_up,
           w_down, b_down, ln2_g, ln2_b, alpha):
    B, T, D = x.shape
    G, R = N_KV_GROUPS, HEADS_PER_GROUP
    N = B * T
    assert D == D_MODEL and T % SEL_BLOCK == 0 and T // SEL_BLOCK <= SEL_LANES

    c0 = 3 * CONV_CH
    c1 = c0 + NSA_WIDTH
    c2 = c1 + 6 * KV_WIDTH
    c3 = c2 + 3 * N_HEADS
    wc = w_in[:, :c0].astype(BF16)
    wqt = w_in[:, c0:c1].T.astype(BF16)
    wkvt = w_in[:, c1:c2].T.astype(BF16)
    gcols = np.zeros((G * GATE_ROWS,), np.int64)
    gmask = np.zeros((G * GATE_ROWS,), np.float32)
    for g in range(G):
        for br in range(3):
            for r in range(R):
                gcols[g * GATE_ROWS + br * R + r] = br * N_HEADS + g * R + r
                gmask[g * GATE_ROWS + br * R + r] = 1.0
    wgt = (w_in[:, c2:c3][:, gcols] * gmask).T.astype(BF16)
    wmg = w_in[:, c3:].astype(BF16)
    cw = conv_w.reshape(CONV_K, CONV_CH)
    pos_row = positions.astype(F32)[:, None, :]

    ya_pre, qt, kvt, gates, ctok = _proj(x, pos_row, wc, wqt, wkvt, wgt, cw, jnp.asarray(_rope_freq()))

    nc = T // CMP_STRIDE
    w1 = jnp.stack([cmp_w1_k, cmp_w1_v])
    w2 = jnp.stack([cmp_w2_k, cmp_w2_v])
    pos_flat = jnp.stack([cmp_pos_k.reshape(1, -1), cmp_pos_v.reshape(1, -1)])
    pos_flat = jnp.pad(pos_flat, ((0, 0), (0, 7), (0, 0)))
    kcmp, kcmp_t = _compress(ctok, w1, w2, w2.transpose(0, 2, 1), pos_flat)
    kcmp = kcmp.reshape(2, B * G, nc, HEAD_DIM)
    kcmp_t = kcmp_t.reshape(2, B * G, HEAD_DIM, nc)

    o_c, sel = _cmp_attn(qt, kcmp, kcmp_t, gates, jnp.asarray(_overlap_t(T)))

    hot_t = (np.arange(SEL_LANES)[:, None] == np.arange(T)[None, :] // SEL_BLOCK).astype(np.float32)
    o_s = _sel_attn(qt, sel, kvt, 2 * G, 3 * G, jnp.asarray(hot_t, BF16), gates)
    o_w = _win_attn(qt, kvt, 4 * G, 5 * G, gates)

    sub = min(MERGE_SUB, T)
    tri = jnp.asarray(np.triu(np.ones((sub, sub), np.float32), 1), BF16)
    assert N % MOE_BLK == 0
    x1, ri, rg, cnt, xs = _merge(x.reshape(N, D), ya_pre.reshape(N, CONV_CH), o_c, o_s, o_w, wmg,
                             w_up_conv.astype(BF16), w_up_nsa.astype(BF16), w_o.astype(BF16),
                             ln1_g[None, :], ln1_b[None, :], w_router.T, b_router[:, None], tri, alpha)

    counts = cnt[:, 0]
    blocks = (counts + MOE_BLK - 1) // MOE_BLK
    bend = jnp.cumsum(blocks).astype(I32)
    bstart = bend - blocks
    n_blocks = (N * TOP_K) // MOE_BLK + N_EXPERTS
    n_used = bend[-1:]
    b = jnp.minimum(jnp.arange(n_blocks, dtype=I32), n_used - 1)
    block_expert = jnp.minimum(jnp.sum((b[:, None] >= bend[None, :]).astype(I32), axis=1), N_EXPERTS - 1)
    first = jnp.sum(jnp.where(block_expert[:, None] == jnp.arange(N_EXPERTS)[None, :], bstart[None, :], 0), axis=1)
    row_block = block_expert * (N // MOE_BLK) + (b - first)
    count = jnp.sum(jnp.where(block_expert[:, None] == jnp.arange(N_EXPERTS)[None, :], counts[None, :], 0), axis=1)
    valid_rows = jnp.clip(count - (b - first) * MOE_BLK, 0, MOE_BLK).astype(I32)
    dest = (ri[:TOP_K] * N + ri[TOP_K:]).T.reshape(N * TOP_K)
    e_ids = jnp.arange(N_EXPERTS, dtype=I32)
    later = (blocks > 0)[None, :] & (e_ids[None, :] > e_ids[:, None])
    next_tab = jnp.min(jnp.where(later, e_ids[None, :], N_EXPERTS), axis=1)
    next_tab = jnp.where(next_tab < N_EXPERTS, next_tab, -1)
    pick = block_expert[:, None] == e_ids[None, :]
    next_expert = jnp.sum(jnp.where(pick, next_tab[None, :], 0), axis=1).astype(I32)
    order = jnp.cumsum((blocks > 0).astype(I32)) - 1
    slot = (jnp.sum(jnp.where(pick, order[None, :], 0), axis=1) % 2).astype(I32)

    ys = _experts(block_expert, n_used, row_block, valid_rows, next_expert, slot, xs, w_gate_up, jnp.asarray(_deinterleave_perm(), BF16),
                  b_gate_up[:, None, 0::2], b_gate_up[:, None, 1::2], w_down, b_down[:, None, :])
    out = _combine(dest, ys, x1, rg.T, ln2_g[None, :], ln2_b[None, :], alpha)
    return out.reshape(B, T, D)


def kernel(x, positions, w_in, conv_w, cmp_pos_k, cmp_w1_k, cmp_w2_k, cmp_pos_v, cmp_w1_v, cmp_w2_v, w_up_conv, w_up_nsa, w_o, ln1_g, ln1_b, w_router, b_router, w_gate_up, b_gate_up, w_down, b_down, ln2_g, ln2_b):
    depth = w_in.shape[0]
    alpha = float((2 * depth) ** 0.25)
    h = x
    for l in range(depth):
        h = _layer(h, positions, w_in[l], conv_w[l], cmp_pos_k[l], cmp_w1_k[l], cmp_w2_k[l],
                   cmp_pos_v[l], cmp_w1_v[l], cmp_w2_v[l], w_up_conv[l], w_up_nsa[l], w_o[l],
                   ln1_g[l], ln1_b[l], w_router[l], b_router[l], w_gate_up[l], b_gate_up[l],
                   w_down[l], b_down[l], ln2_g[l], ln2_b[l], alpha)
    return h
```
